```python
import jax, jax.numpy as jnp
from jax import lax
import numpy as np

D_MODEL = 1024
BATCH = 8
SEQ = 8192
DEPTH = 4

GRID_W = 64
CTX_LEN = 256
A_WIDTH = 512
B_WIDTH = 512
A_CONV = 3
B_CONV = 31
AB_IN = 3 * A_WIDTH + 2 * B_WIDTH
AB_OUT = A_WIDTH + B_WIDTH
N_HEADS = 16
N_KV_HEADS = 4
HEAD_DIM = 64
Q_GROUP = N_HEADS // N_KV_HEADS
Q_W = N_HEADS * HEAD_DIM
KV_W = N_KV_HEADS * HEAD_DIM
WINDOW = 128
BLOCK = 128
ROPE_THETA = 10000.0
D_FF = 2816
FFN_CONV = 3
EPS = 1e-6
NEG_INF = -1e30
N_EVEN = (DEPTH + 1) // 2
N_ODD = DEPTH // 2

kernel_name = 'hybrid_conv_swa_diffusion_block'


def rmsnorm(x, g):
    xf = x.astype(jnp.float32)
    y = xf * lax.rsqrt(jnp.mean(xf * xf, axis=-1, keepdims=True) + EPS)
    return (y * g.astype(jnp.float32)).astype(x.dtype)


def layernorm(x, g, b):
    xf = x.astype(jnp.float32)
    mu = jnp.mean(xf, axis=-1, keepdims=True)
    var = jnp.mean(jnp.square(xf - mu), axis=-1, keepdims=True)
    y = (xf - mu) * lax.rsqrt(var + EPS)
    return (y * g.astype(jnp.float32) + b.astype(jnp.float32)).astype(x.dtype)


def modulate(x, shift, scale):
    return x * (1.0 + scale) + shift


def adaln(cond, w_mod, b_mod):
    m = jax.nn.silu(cond) @ w_mod + b_mod
    return jnp.split(m, 6, axis=-1)


def dwconv(x, w):
    k = w.shape[0]
    return lax.conv_general_dilated(
        x, w[:, None, :], window_strides=(1,), padding=[(k // 2, k // 2)],
        dimension_numbers=('NWC', 'WIO', 'NWC'), feature_group_count=x.shape[-1])


def axial_rope_tables(length):
    rows = length // GRID_W
    row = jnp.repeat(jnp.arange(rows), GRID_W).astype(jnp.float32)
    col = jnp.tile(jnp.arange(GRID_W), rows).astype(jnp.float32)
    n_freq = HEAD_DIM // 4
    inv_freq = ROPE_THETA ** (-jnp.arange(n_freq, dtype=jnp.float32) / n_freq)
    ang = jnp.concatenate([row[:, None] * inv_freq, col[:, None] * inv_freq], axis=-1)
    return jnp.cos(ang)[:, None, :], jnp.sin(ang)[:, None, :]


def apply_rope(x, cos, sin):
    xf = x.astype(jnp.float32)
    x1, x2 = jnp.split(xf, 2, axis=-1)
    return jnp.concatenate([x1 * cos - x2 * sin, x2 * cos + x1 * sin], axis=-1).astype(x.dtype)


def sink_softmax(logits, sink):
    full = jnp.concatenate([logits, jnp.broadcast_to(sink, logits.shape[:-1] + (1,))], axis=-1)
    return jax.nn.softmax(full, axis=-1)[..., :-1]


def conv_mixers(h, w_in, conv_a, conv_b, conv_b_bias, ln_g, ln_b, w_out):
    p = h @ w_in
    g_b, g_c, u_a, v_b, gate_b = jnp.split(
        p, [A_WIDTH, 2 * A_WIDTH, 3 * A_WIDTH, 3 * A_WIDTH + B_WIDTH], axis=-1)
    y_a = g_b * dwconv(g_c * u_a, conv_a)
    u = v_b * jax.nn.sigmoid(gate_b)
    u = dwconv(u, conv_b) + conv_b_bias
    y_b = jax.nn.silu(layernorm(u, ln_g, ln_b))
    return jnp.concatenate([y_a, y_b], axis=-1) @ w_out


def windowed_gqa(h, hc, w_qkv, w_o, sinks, need_ctx_out):
    bsz, length, _ = h.shape
    n_ctx = hc.shape[1]
    scale = HEAD_DIM ** -0.5
    q, k, v = jnp.split(h @ w_qkv, [Q_W, Q_W + KV_W], axis=-1)
    q = q.reshape(bsz, length, N_HEADS, HEAD_DIM)
    k = k.reshape(bsz, length, N_KV_HEADS, HEAD_DIM)
    v = v.reshape(bsz, length, N_KV_HEADS, HEAD_DIM)
    cos, sin = axial_rope_tables(length)
    q = apply_rope(q, cos, sin) * scale
    k = apply_rope(k, cos, sin)
    kc, vc = jnp.split(hc @ w_qkv[:, Q_W:], [KV_W], axis=-1)
    kc = kc.reshape(bsz, n_ctx, N_KV_HEADS, HEAD_DIM)
    vc = vc.reshape(bsz, n_ctx, N_KV_HEADS, HEAD_DIM)
    sink = sinks.astype(jnp.float32).reshape(1, N_KV_HEADS, Q_GROUP, 1, 1)

    nblk = length // BLOCK
    qb = q.reshape(bsz, nblk, BLOCK, N_KV_HEADS, Q_GROUP, HEAD_DIM)

    def band(t):
        tb = t.reshape(bsz, nblk, BLOCK, N_KV_HEADS, HEAD_DIM)
        tp = jnp.pad(tb, ((0, 0), (1, 1), (0, 0), (0, 0), (0, 0)))
        return jnp.concatenate([tp[:, :-2], tp[:, 1:-1], tp[:, 2:]], axis=2)

    k_band, v_band = band(k), band(v)
    blk = jnp.arange(nblk)[:, None, None]
    q_pos = blk * BLOCK + jnp.arange(BLOCK)[None, :, None]
    k_pos = (blk - 1) * BLOCK + jnp.arange(3 * BLOCK)[None, None, :]
    mask = (jnp.abs(q_pos - k_pos) <= WINDOW) & (k_pos >= 0) & (k_pos < length)

    def attend_block(args):
        q_blk, k_blk, v_blk, m = args
        s_loc = jnp.einsum('bqhgd,bshd->bhgqs', q_blk, k_blk).astype(jnp.float32)
        s_loc = jnp.where(m, s_loc, NEG_INF)
        s_ctx = jnp.einsum('bqhgd,bchd->bhgqc', q_blk, kc).astype(jnp.float32)
        p = sink_softmax(jnp.concatenate([s_loc, s_ctx], axis=-1), sink)
        p_loc = p[..., :3 * BLOCK].astype(v_blk.dtype)
        p_ctx = p[..., 3 * BLOCK:].astype(vc.dtype)
        return (jnp.einsum('bhgqs,bshd->bqhgd', p_loc, v_blk)
                + jnp.einsum('bhgqc,bchd->bqhgd', p_ctx, vc))

    out = lax.map(attend_block, (jnp.moveaxis(qb, 1, 0), jnp.moveaxis(k_band, 1, 0),
                                 jnp.moveaxis(v_band, 1, 0), mask))
    y = jnp.moveaxis(out, 0, 1).reshape(bsz, length, Q_W) @ w_o

    yc = None
    if need_ctx_out:
        qc = (hc @ w_qkv[:, :Q_W]).reshape(bsz, n_ctx, N_KV_HEADS, Q_GROUP, HEAD_DIM) * scale
        s = jnp.einsum('bqhgd,bchd->bhgqc', qc, kc).astype(jnp.float32)
        p = sink_softmax(s, sink).astype(vc.dtype)
        yc = jnp.einsum('bhgqc,bchd->bqhgd', p, vc).reshape(bsz, n_ctx, Q_W) @ w_o
    return y, yc


def conv_ffn(h, w_up, conv, w_down):
    u = dwconv(h @ w_up, conv)
    a, g = jnp.split(u, 2, axis=-1)
    return (jax.nn.silu(g) * a) @ w_down


def _fwd_setup_inputs(seed: int = 0) -> dict:
    key = jax.random.key(seed)
    ks = jax.random.split(key, 22)

    def nrm(k, shape, s):
        return jax.random.normal(k, shape, jnp.float32) * s

    return {
        'x': nrm(ks[0], (BATCH, SEQ, D_MODEL), 1.0),
        'c': nrm(ks[1], (BATCH, D_MODEL), 1.0),
        'ctx': nrm(ks[2], (BATCH, CTX_LEN, D_MODEL), 1.0),
        'c_ctx': nrm(ks[3], (D_MODEL,), 1.0),
        'w_mod': nrm(ks[4], (DEPTH, D_MODEL, 6 * D_MODEL), 0.5 * D_MODEL ** -0.5),
        'b_mod': nrm(ks[5], (DEPTH, 6 * D_MODEL), 0.02),
        'norm_mix': 1.0 + nrm(ks[6], (DEPTH, D_MODEL), 0.02),
        'norm_ffn': 1.0 + nrm(ks[7], (DEPTH, D_MODEL), 0.02),
        'w_in_ab': nrm(ks[8], (N_EVEN, D_MODEL, AB_IN), D_MODEL ** -0.5),
        'conv_a': nrm(ks[9], (N_EVEN, A_CONV, A_WIDTH), A_CONV ** -0.5),
        'conv_b': nrm(ks[10], (N_EVEN, B_CONV, B_WIDTH), B_CONV ** -0.5),
        'conv_b_bias': nrm(ks[11], (N_EVEN, B_WIDTH), 0.02),
        'ln_b_gain': 1.0 + nrm(ks[12], (N_EVEN, B_WIDTH), 0.02),
        'ln_b_bias': nrm(ks[13], (N_EVEN, B_WIDTH), 0.02),
        'w_out_ab': nrm(ks[14], (N_EVEN, AB_OUT, D_MODEL), AB_OUT ** -0.5),
        'w_qkv': nrm(ks[15], (N_ODD, D_MODEL, Q_W + 2 * KV_W), D_MODEL ** -0.5),
        'w_o': nrm(ks[16], (N_ODD, Q_W, D_MODEL), Q_W ** -0.5),
        'sinks': nrm(ks[17], (N_ODD, N_HEADS), 1.0),
        'w_up': nrm(ks[18], (DEPTH, D_MODEL, 2 * D_FF), D_MODEL ** -0.5),
        'w_conv_ffn': nrm(ks[19], (DEPTH, FFN_CONV, 2 * D_FF), FFN_CONV ** -0.5),
        'w_down': nrm(ks[20], (DEPTH, D_FF, D_MODEL), D_FF ** -0.5),
        'final_norm': 1.0 + nrm(ks[21], (D_MODEL,), 0.02),
    }


def _fwd_reference(x, c, ctx, c_ctx, w_mod, b_mod, norm_mix, norm_ffn, w_in_ab, conv_a, conv_b,
              conv_b_bias, ln_b_gain, ln_b_bias, w_out_ab, w_qkv, w_o, sinks, w_up, w_conv_ffn,
              w_down, final_norm):
    xc = ctx
    for l in range(DEPTH):
        last = l == DEPTH - 1
        sh1, sc1, g1, sh2, sc2, g2 = adaln(c[:, None, :], w_mod[l], b_mod[l])
        csh1, csc1, cg1, csh2, csc2, cg2 = adaln(c_ctx, w_mod[l], b_mod[l])
        h = modulate(rmsnorm(x, norm_mix[l]), sh1, sc1)
        hc = modulate(rmsnorm(xc, norm_mix[l]), csh1, csc1)
        if l % 2 == 0:
            e = l // 2
            y = conv_mixers(h, w_in_ab[e], conv_a[e], conv_b[e], conv_b_bias[e],
                            ln_b_gain[e], ln_b_bias[e], w_out_ab[e])
            yc = None if last else conv_mixers(hc, w_in_ab[e], conv_a[e], conv_b[e], conv_b_bias[e],
                                               ln_b_gain[e], ln_b_bias[e], w_out_ab[e])
        else:
            o = l // 2
            y, yc = windowed_gqa(h, hc, w_qkv[o], w_o[o], sinks[o], not last)
        x = x + g1 * y
        x = x + g2 * conv_ffn(modulate(rmsnorm(x, norm_ffn[l]), sh2, sc2), w_up[l], w_conv_ffn[l], w_down[l])
        if not last:
            xc = xc + cg1 * yc
            xc = xc + cg2 * conv_ffn(modulate(rmsnorm(xc, norm_ffn[l]), csh2, csc2),
                                     w_up[l], w_conv_ffn[l], w_down[l])
    return rmsnorm(x, final_norm)


import jax as _jax
import jax.numpy as _jnp

TWIN_FORMAT = 'train_step'
FWD_PARAMS = ['x', 'c', 'ctx', 'c_ctx', 'w_mod', 'b_mod', 'norm_mix', 'norm_ffn', 'w_in_ab', 'conv_a', 'conv_b', 'conv_b_bias', 'ln_b_gain', 'ln_b_bias', 'w_out_ab', 'w_qkv', 'w_o', 'sinks', 'w_up', 'w_conv_ffn', 'w_down', 'final_norm']
TWIN_WEIGHTS = ['c_ctx', 'w_mod', 'b_mod', 'norm_mix', 'norm_ffn', 'w_in_ab', 'conv_a', 'conv_b', 'conv_b_bias', 'ln_b_gain', 'ln_b_bias', 'w_out_ab', 'w_qkv', 'w_o', 'sinks', 'w_up', 'w_conv_ffn', 'w_down', 'final_norm']
TWIN_DIFF_INPUT = 'x'
TWIN_INPUTS = ['x', 'c', 'ctx', 'c_ctx', 'w_mod', 'b_mod', 'norm_mix', 'norm_ffn', 'w_in_ab', 'conv_a', 'conv_b', 'conv_b_bias', 'ln_b_gain', 'ln_b_bias', 'w_out_ab', 'w_qkv', 'w_o', 'sinks', 'w_up', 'w_conv_ffn', 'w_down', 'final_norm', 'loss_target', 'm_c_ctx', 'm_w_mod', 'm_b_mod', 'm_norm_mix', 'm_norm_ffn', 'm_w_in_ab', 'm_conv_a', 'm_conv_b', 'm_conv_b_bias', 'm_ln_b_gain', 'm_ln_b_bias', 'm_w_out_ab', 'm_w_qkv', 'm_w_o', 'm_sinks', 'm_w_up', 'm_w_conv_ffn', 'm_w_down', 'm_final_norm', 'v_c_ctx', 'v_w_mod', 'v_b_mod', 'v_norm_mix', 'v_norm_ffn', 'v_w_in_ab', 'v_conv_a', 'v_conv_b', 'v_conv_b_bias', 'v_ln_b_gain', 'v_ln_b_bias', 'v_w_out_ab', 'v_w_qkv', 'v_w_o', 'v_sinks', 'v_w_up', 'v_w_conv_ffn', 'v_w_down', 'v_final_norm']
TWIN_OUTPUTS = ['loss', 'grad_x', 'grad_c_ctx', 'grad_w_mod', 'grad_b_mod', 'grad_norm_mix', 'grad_norm_ffn', 'grad_w_in_ab', 'grad_conv_a', 'grad_conv_b', 'grad_conv_b_bias', 'grad_ln_b_gain', 'grad_ln_b_bias', 'grad_w_out_ab', 'grad_w_qkv', 'grad_w_o', 'grad_sinks', 'grad_w_up', 'grad_w_conv_ffn', 'grad_w_down', 'grad_final_norm', 'delta_c_ctx', 'delta_w_mod', 'delta_b_mod', 'delta_norm_mix', 'delta_norm_ffn', 'delta_w_in_ab', 'delta_conv_a', 'delta_conv_b', 'delta_conv_b_bias', 'delta_ln_b_gain', 'delta_ln_b_bias', 'delta_w_out_ab', 'delta_w_qkv', 'delta_w_o', 'delta_sinks', 'delta_w_up', 'delta_w_conv_ffn', 'delta_w_down', 'delta_final_norm', 'new_m_c_ctx', 'new_m_w_mod', 'new_m_b_mod', 'new_m_norm_mix', 'new_m_norm_ffn', 'new_m_w_in_ab', 'new_m_conv_a', 'new_m_conv_b', 'new_m_conv_b_bias', 'new_m_ln_b_gain', 'new_m_ln_b_bias', 'new_m_w_out_ab', 'new_m_w_qkv', 'new_m_w_o', 'new_m_sinks', 'new_m_w_up', 'new_m_w_conv_ffn', 'new_m_w_down', 'new_m_final_norm', 'new_v_c_ctx', 'new_v_w_mod', 'new_v_b_mod', 'new_v_norm_mix', 'new_v_norm_ffn', 'new_v_w_in_ab', 'new_v_conv_a', 'new_v_conv_b', 'new_v_conv_b_bias', 'new_v_ln_b_gain', 'new_v_ln_b_bias', 'new_v_w_out_ab', 'new_v_w_qkv', 'new_v_w_o', 'new_v_sinks', 'new_v_w_up', 'new_v_w_conv_ffn', 'new_v_w_down', 'new_v_final_norm']
TWIN_LEAF_KINDS = {'loss': 'loss', 'grad_x': 'grad_x', 'grad_c_ctx': 'grad_w', 'grad_w_mod': 'grad_w', 'grad_b_mod': 'grad_w', 'grad_norm_mix': 'grad_w', 'grad_norm_ffn': 'grad_w', 'grad_w_in_ab': 'grad_w', 'grad_conv_a': 'grad_w', 'grad_conv_b': 'grad_w', 'grad_conv_b_bias': 'grad_w', 'grad_ln_b_gain': 'grad_w', 'grad_ln_b_bias': 'grad_w', 'grad_w_out_ab': 'grad_w', 'grad_w_qkv': 'grad_w', 'grad_w_o': 'grad_w', 'grad_sinks': 'grad_w', 'grad_w_up': 'grad_w', 'grad_w_conv_ffn': 'grad_w', 'grad_w_down': 'grad_w', 'grad_final_norm': 'grad_w', 'delta_c_ctx': 'delta_w', 'delta_w_mod': 'delta_w', 'delta_b_mod': 'delta_w', 'delta_norm_mix': 'delta_w', 'delta_norm_ffn': 'delta_w', 'delta_w_in_ab': 'delta_w', 'delta_conv_a': 'delta_w', 'delta_conv_b': 'delta_w', 'delta_conv_b_bias': 'delta_w', 'delta_ln_b_gain': 'delta_w', 'delta_ln_b_bias': 'delta_w', 'delta_w_out_ab': 'delta_w', 'delta_w_qkv': 'delta_w', 'delta_w_o': 'delta_w', 'delta_sinks': 'delta_w', 'delta_w_up': 'delta_w', 'delta_w_conv_ffn': 'delta_w', 'delta_w_down': 'delta_w', 'delta_final_norm': 'delta_w', 'new_m_c_ctx': 'new_m', 'new_m_w_mod': 'new_m', 'new_m_b_mod': 'new_m', 'new_m_norm_mix': 'new_m', 'new_m_norm_ffn': 'new_m', 'new_m_w_in_ab': 'new_m', 'new_m_conv_a': 'new_m', 'new_m_conv_b': 'new_m', 'new_m_conv_b_bias': 'new_m', 'new_m_ln_b_gain': 'new_m', 'new_m_ln_b_bias': 'new_m', 'new_m_w_out_ab': 'new_m', 'new_m_w_qkv': 'new_m', 'new_m_w_o': 'new_m', 'new_m_sinks': 'new_m', 'new_m_w_up': 'new_m', 'new_m_w_conv_ffn': 'new_m', 'new_m_w_down': 'new_m', 'new_m_final_norm': 'new_m', 'new_v_c_ctx': 'new_v', 'new_v_w_mod': 'new_v', 'new_v_b_mod': 'new_v', 'new_v_norm_mix': 'new_v', 'new_v_norm_ffn': 'new_v', 'new_v_w_in_ab': 'new_v', 'new_v_conv_a': 'new_v', 'new_v_conv_b': 'new_v', 'new_v_conv_b_bias': 'new_v', 'new_v_ln_b_gain': 'new_v', 'new_v_ln_b_bias': 'new_v', 'new_v_w_out_ab': 'new_v', 'new_v_w_qkv': 'new_v', 'new_v_w_o': 'new_v', 'new_v_sinks': 'new_v', 'new_v_w_up': 'new_v', 'new_v_w_conv_ffn': 'new_v', 'new_v_w_down': 'new_v', 'new_v_final_norm': 'new_v'}


def _forward(args):
    return _fwd_reference(*[args[k] for k in FWD_PARAMS])


def _output_shape():
    def fwd():
        inp = _fwd_setup_inputs(0)
        return _fwd_reference(*[inp[k] for k in FWD_PARAMS])
    out = _jax.eval_shape(fwd)
    return out.shape, out.dtype

N_MICROBATCH = 1
ADAM_LR = 0.001
ADAM_B1 = 0.9
ADAM_B2 = 0.999
ADAM_EPS = 1e-08
ADAM_WD = 0.01
ADAM_STEP = 10
PER_EXAMPLE_BATCH_AXIS = {'x': 0, 'c': 0, 'ctx': 0, 'loss_target': 0}
SHARED_INPUTS = []
_WEIGHT_DTYPES = {'c_ctx': _jnp.float32, 'w_mod': _jnp.float32, 'b_mod': _jnp.float32, 'norm_mix': _jnp.float32, 'norm_ffn': _jnp.float32, 'w_in_ab': _jnp.float32, 'conv_a': _jnp.float32, 'conv_b': _jnp.float32, 'conv_b_bias': _jnp.float32, 'ln_b_gain': _jnp.float32, 'ln_b_bias': _jnp.float32, 'w_out_ab': _jnp.float32, 'w_qkv': _jnp.float32, 'w_o': _jnp.float32, 'sinks': _jnp.float32, 'w_up': _jnp.float32, 'w_conv_ffn': _jnp.float32, 'w_down': _jnp.float32, 'final_norm': _jnp.float32}
MOMENT_SCALE = {'c_ctx': 2.055792e-02, 'w_mod': 7.734315e-02, 'b_mod': 1.273928e-01, 'norm_mix': 8.807628e-02, 'norm_ffn': 7.412551e-02, 'w_in_ab': 8.015564e-02, 'conv_a': 9.819082e-02, 'conv_b': 4.740212e-02, 'conv_b_bias': 7.965766e-02, 'ln_b_gain': 5.358636e-02, 'ln_b_bias': 4.637024e-02, 'w_out_ab': 7.707050e-02, 'w_qkv': 1.837307e-02, 'w_o': 1.850847e-02, 'sinks': 5.719723e-04, 'w_up': 3.225547e-02, 'w_conv_ffn': 3.205607e-02, 'w_down': 5.264509e-02, 'final_norm': 6.407335e+01}


def _to_microbatches(a, axis):
    t = _jnp.moveaxis(a, axis, 0)
    t = t.reshape((N_MICROBATCH, t.shape[0] // N_MICROBATCH) + t.shape[1:])
    return _jnp.moveaxis(t, 1, axis + 1)


def setup_inputs(seed: int = 0) -> dict:
    inp = _fwd_setup_inputs(seed)
    key = _jax.random.fold_in(_jax.random.key(seed), 7919)
    shape, _ = _output_shape()
    out = dict(inp)
    out["loss_target"] = _jax.random.normal(_jax.random.fold_in(key, 0), shape, _jnp.float32)
    for i, name in enumerate(TWIN_WEIGHTS):
        w = inp[name].astype(_jnp.float32)
        if MOMENT_SCALE is None:
            s = _jnp.sqrt(_jnp.mean(_jnp.square(w)) + 1e-30)
        else:
            s = MOMENT_SCALE[name]
        km, kv = _jax.random.split(_jax.random.fold_in(key, i + 1))
        out[name] = w
        out["m_" + name] = s * _jax.random.normal(km, w.shape, _jnp.float32)
        out["v_" + name] = (s * s) * _jax.random.uniform(kv, w.shape, _jnp.float32, 0.5, 1.5)
    if N_MICROBATCH > 1:
        for name, axis in PER_EXAMPLE_BATCH_AXIS.items():
            out[name] = _to_microbatches(out[name], axis)
    return {'x': out['x'], 'c': out['c'], 'ctx': out['ctx'], 'c_ctx': out['c_ctx'], 'w_mod': out['w_mod'], 'b_mod': out['b_mod'], 'norm_mix': out['norm_mix'], 'norm_ffn': out['norm_ffn'], 'w_in_ab': out['w_in_ab'], 'conv_a': out['conv_a'], 'conv_b': out['conv_b'], 'conv_b_bias': out['conv_b_bias'], 'ln_b_gain': out['ln_b_gain'], 'ln_b_bias': out['ln_b_bias'], 'w_out_ab': out['w_out_ab'], 'w_qkv': out['w_qkv'], 'w_o': out['w_o'], 'sinks': out['sinks'], 'w_up': out['w_up'], 'w_conv_ffn': out['w_conv_ffn'], 'w_down': out['w_down'], 'final_norm': out['final_norm'], 'loss_target': out['loss_target'], 'm_c_ctx': out['m_c_ctx'], 'm_w_mod': out['m_w_mod'], 'm_b_mod': out['m_b_mod'], 'm_norm_mix': out['m_norm_mix'], 'm_norm_ffn': out['m_norm_ffn'], 'm_w_in_ab': out['m_w_in_ab'], 'm_conv_a': out['m_conv_a'], 'm_conv_b': out['m_conv_b'], 'm_conv_b_bias': out['m_conv_b_bias'], 'm_ln_b_gain': out['m_ln_b_gain'], 'm_ln_b_bias': out['m_ln_b_bias'], 'm_w_out_ab': out['m_w_out_ab'], 'm_w_qkv': out['m_w_qkv'], 'm_w_o': out['m_w_o'], 'm_sinks': out['m_sinks'], 'm_w_up': out['m_w_up'], 'm_w_conv_ffn': out['m_w_conv_ffn'], 'm_w_down': out['m_w_down'], 'm_final_norm': out['m_final_norm'], 'v_c_ctx': out['v_c_ctx'], 'v_w_mod': out['v_w_mod'], 'v_b_mod': out['v_b_mod'], 'v_norm_mix': out['v_norm_mix'], 'v_norm_ffn': out['v_norm_ffn'], 'v_w_in_ab': out['v_w_in_ab'], 'v_conv_a': out['v_conv_a'], 'v_conv_b': out['v_conv_b'], 'v_conv_b_bias': out['v_conv_b_bias'], 'v_ln_b_gain': out['v_ln_b_gain'], 'v_ln_b_bias': out['v_ln_b_bias'], 'v_w_out_ab': out['v_w_out_ab'], 'v_w_qkv': out['v_w_qkv'], 'v_w_o': out['v_w_o'], 'v_sinks': out['v_sinks'], 'v_w_up': out['v_w_up'], 'v_w_conv_ffn': out['v_w_conv_ffn'], 'v_w_down': out['v_w_down'], 'v_final_norm': out['v_final_norm']}


def _loss(weights, diff, rest, loss_target):
    with _jax.named_scope("forward"):
        args = {**rest, TWIN_DIFF_INPUT: diff, **{k: w.astype(_WEIGHT_DTYPES[k]) for k, w in weights.items()}}
        y = _forward(args)
    with _jax.named_scope("loss_head"):
        err = _jnp.square(y.astype(_jnp.float32) - loss_target)
        return 0.5 * _jnp.sum(_jnp.mean(err, axis=-1)) if err.ndim else 0.5 * err


def _adamw(w, g, m, v):
    m = ADAM_B1 * m + (1.0 - ADAM_B1) * g
    v = ADAM_B2 * v + (1.0 - ADAM_B2) * _jnp.square(g)
    m_hat = m / (1.0 - ADAM_B1 ** ADAM_STEP)
    v_hat = v / (1.0 - ADAM_B2 ** ADAM_STEP)
    delta = -ADAM_LR * (m_hat / (_jnp.sqrt(v_hat) + ADAM_EPS) + ADAM_WD * w)
    return delta, m, v


def reference(x, c, ctx, c_ctx, w_mod, b_mod, norm_mix, norm_ffn, w_in_ab, conv_a, conv_b, conv_b_bias, ln_b_gain, ln_b_bias, w_out_ab, w_qkv, w_o, sinks, w_up, w_conv_ffn, w_down, final_norm, loss_target, m_c_ctx, m_w_mod, m_b_mod, m_norm_mix, m_norm_ffn, m_w_in_ab, m_conv_a, m_conv_b, m_conv_b_bias, m_ln_b_gain, m_ln_b_bias, m_w_out_ab, m_w_qkv, m_w_o, m_sinks, m_w_up, m_w_conv_ffn, m_w_down, m_final_norm, v_c_ctx, v_w_mod, v_b_mod, v_norm_mix, v_norm_ffn, v_w_in_ab, v_conv_a, v_conv_b, v_conv_b_bias, v_ln_b_gain, v_ln_b_bias, v_w_out_ab, v_w_qkv, v_w_o, v_sinks, v_w_up, v_w_conv_ffn, v_w_down, v_final_norm):
    given = dict(x=x, c=c, ctx=ctx, c_ctx=c_ctx, w_mod=w_mod, b_mod=b_mod, norm_mix=norm_mix, norm_ffn=norm_ffn, w_in_ab=w_in_ab, conv_a=conv_a, conv_b=conv_b, conv_b_bias=conv_b_bias, ln_b_gain=ln_b_gain, ln_b_bias=ln_b_bias, w_out_ab=w_out_ab, w_qkv=w_qkv, w_o=w_o, sinks=sinks, w_up=w_up, w_conv_ffn=w_conv_ffn, w_down=w_down, final_norm=final_norm, loss_target=loss_target, m_c_ctx=m_c_ctx, m_w_mod=m_w_mod, m_b_mod=m_b_mod, m_norm_mix=m_norm_mix, m_norm_ffn=m_norm_ffn, m_w_in_ab=m_w_in_ab, m_conv_a=m_conv_a, m_conv_b=m_conv_b, m_conv_b_bias=m_conv_b_bias, m_ln_b_gain=m_ln_b_gain, m_ln_b_bias=m_ln_b_bias, m_w_out_ab=m_w_out_ab, m_w_qkv=m_w_qkv, m_w_o=m_w_o, m_sinks=m_sinks, m_w_up=m_w_up, m_w_conv_ffn=m_w_conv_ffn, m_w_down=m_w_down, m_final_norm=m_final_norm, v_c_ctx=v_c_ctx, v_w_mod=v_w_mod, v_b_mod=v_b_mod, v_norm_mix=v_norm_mix, v_norm_ffn=v_norm_ffn, v_w_in_ab=v_w_in_ab, v_conv_a=v_conv_a, v_conv_b=v_conv_b, v_conv_b_bias=v_conv_b_bias, v_ln_b_gain=v_ln_b_gain, v_ln_b_bias=v_ln_b_bias, v_w_out_ab=v_w_out_ab, v_w_qkv=v_w_qkv, v_w_o=v_w_o, v_sinks=v_sinks, v_w_up=v_w_up, v_w_conv_ffn=v_w_conv_ffn, v_w_down=v_w_down, v_final_norm=v_final_norm)
    weights = {n: given[n] for n in TWIN_WEIGHTS}
    shared = {n: given[n] for n in SHARED_INPUTS}
    per_example = {n: given[n] for n in ['x', 'c', 'ctx']}
    grad_fn = _jax.value_and_grad(_loss, argnums=(0, 1))

    def one_microbatch(ex, loss_target):
        ex = dict(ex)
        diff = ex.pop(TWIN_DIFF_INPUT)
        return grad_fn(weights, diff, {**shared, **ex}, loss_target)

    if N_MICROBATCH == 1:
        loss, (grad_w, grad_x) = one_microbatch(per_example, given["loss_target"])
    else:
        def body(carry, xs):
            loss_sum, grad_sum = carry
            l_k, (gw_k, gx_k) = one_microbatch(xs[0], xs[1])
            with _jax.named_scope("update"):
                return (loss_sum + l_k, _jax.tree.map(_jnp.add, grad_sum, gw_k)), gx_k

        init = (_jnp.zeros((), _jnp.float32), _jax.tree.map(_jnp.zeros_like, weights))
        (loss, grad_w), grad_x = _jax.lax.scan(body, init, (per_example, given["loss_target"]))
    with _jax.named_scope("update"):
        delta_w, new_m, new_v = {}, {}, {}
        for n in TWIN_WEIGHTS:
            delta_w[n], new_m[n], new_v[n] = _adamw(weights[n], grad_w[n], given["m_" + n], given["v_" + n])
    return (loss, grad_x, *[grad_w[n] for n in TWIN_WEIGHTS], *[delta_w[n] for n in TWIN_WEIGHTS],
            *[new_m[n] for n in TWIN_WEIGHTS], *[new_v[n] for n in TWIN_WEIGHTS])
```

```python
import functools

import jax
import jax.numpy as jnp
from jax import lax
from jax.experimental import pallas as pl
from jax.experimental.pallas import tpu as pltpu

_F32 = jnp.float32
_MM = jnp.bfloat16
_ACT = jnp.bfloat16
_TM = 256
_HALO = 16
_NDEV = 8
_HEAD_DIM = 64
_WINDOW = 128
_GRID_W = 64
_ROPE_THETA = 10000.0
_EPS = 1e-6
_NEG_INF = -1e30
_VMEM_LIMIT = 56 * 1024 * 1024
_ADAM = dict(lr=0.001, b1=0.9, b2=0.999, eps=1e-08, wd=0.01, step=10)
_MESH = pl.DeviceIdType.MESH


def _params(sem=None):
    return pltpu.CompilerParams(dimension_semantics=sem, vmem_limit_bytes=_VMEM_LIMIT)


def _divisor(n, cap, mult):
    if n <= cap:
        return n
    for d in range(cap - cap % mult, 0, -mult):
        if n % d == 0:
            return d
    raise ValueError(f"no tile for {n}")


def _my_coords():
    return lax.axis_index("x"), lax.axis_index("y"), lax.axis_index("c")


def _peer(k):
    x, y, c = _my_coords()
    px = 1 - x if k & 4 else x
    py = 1 - y if k & 2 else y
    pc = 1 - c if k & 1 else c
    return (px, py, pc), 4 * px + 2 * py + pc


def _all_gather_small(name, v):
    rows, cols = v.shape

    def body(v_ref, out_ref, send_sems, recv_sems):
        x, y, c = _my_coords()
        me = 4 * x + 2 * y + c
        out_ref[me] = v_ref[...]
        sends = []
        for k in range(1, _NDEV):
            peer, _ = _peer(k)
            cp = pltpu.make_async_remote_copy(
                src_ref=v_ref, dst_ref=out_ref.at[me], send_sem=send_sems.at[k - 1], recv_sem=recv_sems.at[k - 1],
                device_id=peer, device_id_type=_MESH)
            cp.start()
            sends.append(cp)
        for k in range(1, _NDEV):
            peer, pid = _peer(k)
            pltpu.make_async_remote_copy(
                src_ref=v_ref, dst_ref=out_ref.at[pid], send_sem=send_sems.at[k - 1], recv_sem=recv_sems.at[k - 1],
                device_id=peer, device_id_type=_MESH).wait_recv()
        for cp in sends:
            cp.wait_send()

    return pl.pallas_call(
        body, name=name,
        out_shape=jax.ShapeDtypeStruct((_NDEV, rows, cols), v.dtype),
        in_specs=[pl.BlockSpec(memory_space=pltpu.VMEM)],
        out_specs=pl.BlockSpec(memory_space=pltpu.VMEM),
        scratch_shapes=[pltpu.SemaphoreType.DMA((_NDEV - 1,)), pltpu.SemaphoreType.DMA((_NDEV - 1,))],
        compiler_params=pltpu.CompilerParams(vmem_limit_bytes=_VMEM_LIMIT),
    )(v)


def _all_gather_slab(name, slab):
    rows, cols = slab.shape

    def body(s_ref, out_ref, send_sems, recv_sems, local_sem):
        x, y, c = _my_coords()
        me = 4 * x + 2 * y + c
        local = pltpu.make_async_copy(s_ref, out_ref.at[me], local_sem)
        local.start()
        sends = []
        for k in range(1, _NDEV):
            peer, _ = _peer(k)
            cp = pltpu.make_async_remote_copy(
                src_ref=s_ref, dst_ref=out_ref.at[me], send_sem=send_sems.at[k - 1], recv_sem=recv_sems.at[k - 1],
                device_id=peer, device_id_type=_MESH)
            cp.start()
            sends.append(cp)
        for k in range(1, _NDEV):
            peer, pid = _peer(k)
            pltpu.make_async_remote_copy(
                src_ref=s_ref, dst_ref=out_ref.at[pid], send_sem=send_sems.at[k - 1], recv_sem=recv_sems.at[k - 1],
                device_id=peer, device_id_type=_MESH).wait_recv()
        for cp in sends:
            cp.wait_send()
        local.wait()

    return pl.pallas_call(
        body, name=name,
        out_shape=jax.ShapeDtypeStruct((_NDEV, rows, cols), slab.dtype),
        in_specs=[pl.BlockSpec(memory_space=pl.ANY)],
        out_specs=pl.BlockSpec(memory_space=pl.ANY),
        scratch_shapes=[pltpu.SemaphoreType.DMA((_NDEV - 1,)), pltpu.SemaphoreType.DMA((_NDEV - 1,)),
                        pltpu.SemaphoreType.DMA],
        compiler_params=pltpu.CompilerParams(vmem_limit_bytes=_VMEM_LIMIT),
    )(slab)


def _exchange_grads(name, parts, offsets, total_rows):
    n = len(parts)
    cols = parts[0].shape[2]
    dtype = parts[0].dtype

    def body(*refs):
        g_refs, out_ref = refs[:n], refs[n]
        send_sems, recv_sems, local_sems = refs[n + 1:]
        x, y, c = _my_coords()
        me = 4 * x + 2 * y + c

        def dst(slot, m):
            return out_ref.at[slot, pl.ds(offsets[m], parts[m].shape[1]), :]

        locals_ = []
        for m in range(n):
            cp = pltpu.make_async_copy(g_refs[m].at[me], dst(me, m), local_sems.at[m])
            cp.start()
            locals_.append(cp)
        sends = []
        for k in range(1, _NDEV):
            peer, pid = _peer(k)
            for m in range(n):
                cp = pltpu.make_async_remote_copy(
                    src_ref=g_refs[m].at[pid], dst_ref=dst(me, m), send_sem=send_sems.at[k - 1, m],
                    recv_sem=recv_sems.at[k - 1, m], device_id=peer, device_id_type=_MESH)
                cp.start()
                sends.append(cp)
        for k in range(1, _NDEV):
            peer, pid = _peer(k)
            for m in range(n):
                pltpu.make_async_remote_copy(
                    src_ref=g_refs[m].at[pid], dst_ref=dst(pid, m), send_sem=send_sems.at[k - 1, m],
                    recv_sem=recv_sems.at[k - 1, m], device_id=peer, device_id_type=_MESH).wait_recv()
        for cp in sends:
            cp.wait_send()
        for cp in locals_:
            cp.wait()

    return pl.pallas_call(
        body, name=name,
        out_shape=jax.ShapeDtypeStruct((_NDEV, total_rows, cols), dtype),
        in_specs=[pl.BlockSpec(memory_space=pl.ANY)] * n,
        out_specs=pl.BlockSpec(memory_space=pl.ANY),
        scratch_shapes=[pltpu.SemaphoreType.DMA((_NDEV - 1, n)), pltpu.SemaphoreType.DMA((_NDEV - 1, n)),
                        pltpu.SemaphoreType.DMA((n,))],
        compiler_params=pltpu.CompilerParams(vmem_limit_bytes=_VMEM_LIMIT),
    )(*parts)


def _sum_slots(name, v):
    _, rows, cols = v.shape
    tr = _divisor(rows, 512, 16)

    def body(v_ref, o_ref):
        acc = v_ref[0].astype(_F32)
        for e in range(1, _NDEV):
            acc = acc + v_ref[e].astype(_F32)
        o_ref[...] = acc

    return pl.pallas_call(
        body, name=name, grid=(rows // tr,),
        out_shape=jax.ShapeDtypeStruct((rows, cols), _F32),
        in_specs=[pl.BlockSpec((_NDEV, tr, cols), lambda i: (0, i, 0))],
        out_specs=pl.BlockSpec((tr, cols), lambda i: (i, 0)),
        compiler_params=_params(("arbitrary",)),
    )(v)


def _load_weight(wg_ref, wbuf, sems, off, r, step):
    @pl.when(step == 0)
    def _():
        copies = [pltpu.make_async_copy(wg_ref.at[e, pl.ds(off, r), :], wbuf.at[pl.ds(e * r, r), :], sems.at[e])
                  for e in range(_NDEV)]
        for cp in copies:
            cp.start()
        for cp in copies:
            cp.wait()


def _mm_nt(name, a, wg, off, r, out_dtype):
    rows, kdim = a.shape
    n = _NDEV * r
    chunk = _divisor(n, 512, 128)

    def body(a_ref, wg_ref, o_ref, wbuf, sems):
        _load_weight(wg_ref, wbuf, sems, off, r, pl.program_id(0))
        av = a_ref[...].astype(_MM)
        for j in range(n // chunk):
            o_ref[:, j * chunk:(j + 1) * chunk] = lax.dot_general(
                av, wbuf[j * chunk:(j + 1) * chunk, :], (((1,), (1,)), ((), ())),
                preferred_element_type=_F32).astype(out_dtype)

    return pl.pallas_call(
        body, name=name, grid=(rows // _TM,),
        out_shape=jax.ShapeDtypeStruct((rows, n), out_dtype),
        in_specs=[pl.BlockSpec((_TM, kdim), lambda i: (i, 0)), pl.BlockSpec(memory_space=pl.ANY)],
        out_specs=pl.BlockSpec((_TM, n), lambda i: (i, 0)),
        scratch_shapes=[pltpu.VMEM((n, kdim), wg.dtype), pltpu.SemaphoreType.DMA((_NDEV,))],
        compiler_params=_params(("arbitrary",)),
    )(a, wg)


def _mm_nn(name, a, wg, off, r, out_dtype):
    rows, kdim = a.shape
    assert kdim == _NDEV * r
    n = wg.shape[2]

    def body(a_ref, wg_ref, o_ref, wbuf, sems):
        _load_weight(wg_ref, wbuf, sems, off, r, pl.program_id(0))
        o_ref[...] = jnp.dot(a_ref[...].astype(_MM), wbuf[...], preferred_element_type=_F32).astype(out_dtype)

    return pl.pallas_call(
        body, name=name, grid=(rows // _TM,),
        out_shape=jax.ShapeDtypeStruct((rows, n), out_dtype),
        in_specs=[pl.BlockSpec((_TM, kdim), lambda i: (i, 0)), pl.BlockSpec(memory_space=pl.ANY)],
        out_specs=pl.BlockSpec((_TM, n), lambda i: (i, 0)),
        scratch_shapes=[pltpu.VMEM((kdim, n), wg.dtype), pltpu.SemaphoreType.DMA((_NDEV,))],
        compiler_params=_params(("arbitrary",)),
    )(a, wg)


def _mm_tn(name, a, b, out_dtype, silu_a=False):
    rows, na = a.shape
    nb = b.shape[1]
    tr = _divisor(rows, 1536, 16)
    tn = _divisor(na, 512, 128)
    steps = rows // tr

    def body(a_ref, b_ref, o_ref, acc):
        t = pl.program_id(1)

        @pl.when(t == 0)
        def _():
            acc[...] = jnp.zeros_like(acc)

        av = a_ref[...]
        if silu_a:
            av = av.astype(_F32)
            av = av * jax.nn.sigmoid(av)
        acc[...] += lax.dot_general(av.astype(_MM), b_ref[...].astype(_MM), (((0,), (0,)), ((), ())),
                                    preferred_element_type=_F32)

        @pl.when(t == steps - 1)
        def _():
            o_ref[...] = acc[...].astype(out_dtype)

    return pl.pallas_call(
        body, name=name, grid=(na // tn, steps),
        out_shape=jax.ShapeDtypeStruct((na, nb), out_dtype),
        in_specs=[pl.BlockSpec((tr, tn), lambda j, t: (t, j)), pl.BlockSpec((tr, nb), lambda j, t: (t, 0))],
        out_specs=pl.BlockSpec((tn, nb), lambda j, t: (j, 0)),
        scratch_shapes=[pltpu.VMEM((tn, nb), _F32)],
        compiler_params=_params(("arbitrary", "arbitrary")),
    )(a, b)


def _mod_forward(name, cond, w_mod):
    depth, d, n = w_mod.shape
    rows = cond.shape[0]

    def body(c_ref, w_ref, o_ref):
        cv = c_ref[...]
        a = (cv * jax.nn.sigmoid(cv)).astype(_MM)
        o_ref[...] = jnp.dot(a, w_ref[...].astype(_MM), preferred_element_type=_F32)

    return pl.pallas_call(
        body, name=name, grid=(depth,),
        out_shape=jax.ShapeDtypeStruct((depth, rows, n), _F32),
        in_specs=[pl.BlockSpec((rows, d), lambda l: (0, 0)), pl.BlockSpec((None, d, n), lambda l: (l, 0, 0))],
        out_specs=pl.BlockSpec((None, rows, n), lambda l: (l, 0, 0)),
        compiler_params=_params(("arbitrary",)),
    )(cond, w_mod)


def _mod_backward_cond(name, dm, w_mod):
    depth, d, n = w_mod.shape
    rows = dm.shape[1]

    def body(g_ref, w_ref, o_ref):
        @pl.when(pl.program_id(0) == 0)
        def _():
            o_ref[...] = jnp.zeros_like(o_ref)

        o_ref[...] += lax.dot_general(g_ref[...].astype(_MM), w_ref[...].astype(_MM), (((1,), (1,)), ((), ())),
                                      preferred_element_type=_F32)

    return pl.pallas_call(
        body, name=name, grid=(depth,),
        out_shape=jax.ShapeDtypeStruct((rows, d), _F32),
        in_specs=[pl.BlockSpec((None, rows, n), lambda l: (l, 0, 0)), pl.BlockSpec((None, d, n), lambda l: (l, 0, 0))],
        out_specs=pl.BlockSpec((rows, d), lambda l: (0, 0)),
        compiler_params=_params(("arbitrary",)),
    )(dm, w_mod)


def _param_spec(arr, kind, n_lat):
    if kind == "stream":
        return pl.BlockSpec((None,) + arr.shape[1:], lambda i: (i // n_lat, 0, 0))
    return pl.BlockSpec(arr.shape, lambda i: (0,) * arr.ndim)


def _rowfn(name, fn, params, xs, outs, n_lat):
    rows = xs[0][0].shape[0]
    np_, nx = len(params), len(xs)
    stored = [(dt, ws) for dt, ws in outs if dt is not None]

    def body(*refs):
        ps = [r[...].astype(_F32) for r in refs[:np_]]
        xv = [r[...].astype(_F32) for r in refs[np_:np_ + nx]]
        pieces = fn(ps, xv)
        o_refs = iter(refs[np_ + nx:])
        k = 0
        for dt, ws in outs:
            o_ref = next(o_refs) if dt is not None else None
            off = 0
            for w in ws:
                if o_ref is not None:
                    o_ref[:, off:off + w] = pieces[k].astype(dt)
                off += w
                k += 1

    return pl.pallas_call(
        body, name=name, grid=(rows // _TM,),
        out_shape=[jax.ShapeDtypeStruct((rows, sum(ws)), dt) for dt, ws in stored],
        in_specs=[_param_spec(a, kind, n_lat) for a, kind in params]
        + [pl.BlockSpec((_TM, w), lambda i, cb=cb: (i, cb)) for _, w, cb in xs],
        out_specs=[pl.BlockSpec((_TM, sum(ws)), lambda i: (i, 0)) for _, ws in stored],
        compiler_params=_params(("arbitrary",)),
    )(*[a for a, _ in params], *[a for a, _, _ in xs])


def _rowfn_bwd(name, fn, params, xs, diff, douts, dx_outs, n_lat, residual=None):
    rows = xs[0][0].shape[0]
    np_, nx, nd = len(params), len(xs), len(douts)
    nres = 0 if residual is None else 1
    nt = rows // _TM

    def body(*refs):
        i = pl.program_id(0)
        ps = [r[...].astype(_F32) for r in refs[:np_]]
        xv = [r[...].astype(_F32) for r in refs[np_:np_ + nx]]
        d_refs = refs[np_ + nx:np_ + nx + nd]
        res_ref = refs[np_ + nx + nd] if nres else None
        dp_refs = refs[np_ + nx + nd + nres:np_ + nx + nd + nres + np_]
        dx_refs = refs[np_ + nx + nd + nres + np_:]

        def f(ps_, xd):
            full = list(xv)
            for j, v in zip(diff, xd):
                full[j] = v
            return fn(ps_, full)

        _, vjp = jax.vjp(f, ps, [xv[j] for j in diff])
        cts = []
        for d_ref, (_, ws) in zip(d_refs, douts):
            off = 0
            for w in ws:
                cts.append(d_ref[:, off:off + w].astype(_F32))
                off += w
        dps, dxd = vjp(cts)
        grads = dict(zip(diff, dxd))
        for (dp_ref, (_, kind)), dp in zip(zip(dp_refs, params), dps):
            first = (i == 0) | (i == n_lat) if kind == "stream" else i == 0

            @pl.when(first)
            def _(dp_ref=dp_ref):
                dp_ref[...] = jnp.zeros_like(dp_ref)

            dp_ref[...] += dp
        for n_out, (dx_ref, (dt, idxs)) in enumerate(zip(dx_refs, dx_outs)):
            off = 0
            for j in idxs:
                w = xs[j][1]
                g = grads[j]
                if res_ref is not None and n_out == 0 and off == 0:
                    g = g + res_ref[...].astype(_F32)
                dx_ref[:, off:off + w] = g.astype(dt)
                off += w

    dp_shapes = [jax.ShapeDtypeStruct(a.shape, _F32) for a, _ in params]
    dx_shapes = [jax.ShapeDtypeStruct((rows, sum(xs[j][1] for j in idxs)), dt) for dt, idxs in dx_outs]
    in_specs = ([_param_spec(a, kind, n_lat) for a, kind in params]
                + [pl.BlockSpec((_TM, w), lambda i, cb=cb: (i, cb)) for _, w, cb in xs]
                + [pl.BlockSpec((_TM, sum(ws)), lambda i: (i, 0)) for _, ws in douts])
    operands = [a for a, _ in params] + [a for a, _, _ in xs] + [a for a, _ in douts]
    if nres:
        in_specs.append(pl.BlockSpec((_TM, residual.shape[1]), lambda i: (i, 0)))
        operands.append(residual)
    res = pl.pallas_call(
        body, name=name, grid=(nt,),
        out_shape=dp_shapes + dx_shapes,
        in_specs=in_specs,
        out_specs=[_param_spec(a, kind, n_lat) for a, kind in params]
        + [pl.BlockSpec((_TM, s.shape[1]), lambda i: (i, 0)) for s in dx_shapes],
        compiler_params=_params(("arbitrary",)),
    )(*operands)
    return list(res[:np_]), list(res[np_:])


def _f_norm_mod(ps, xs):
    gain, shift, scale = ps
    (x,) = xs
    y = x * lax.rsqrt(jnp.mean(x * x, axis=-1, keepdims=True) + _EPS) * gain
    return [y * (1.0 + scale) + shift]


def _f_gate(ps, xs):
    return [ps[0] * xs[0]]


def _f_residual(ps, xs):
    return [xs[0] + ps[0] * xs[1]]


def _f_premix(ps, xs):
    g_b, g_c, u_a, v_b, gate_b = xs
    return [g_b, g_c * u_a, v_b * jax.nn.sigmoid(gate_b)]


def _f_postmix(ps, xs):
    bias, ln_g, ln_b = ps
    g_b, cv_a, cv_b = xs
    u = cv_b + bias
    mu = jnp.mean(u, axis=-1, keepdims=True)
    var = jnp.mean(jnp.square(u - mu), axis=-1, keepdims=True)
    y = (u - mu) * lax.rsqrt(var + _EPS) * ln_g + ln_b
    return [g_b * cv_a, y * jax.nn.sigmoid(y)]


def _f_glu(ps, xs):
    a, g = xs
    return [g * jax.nn.sigmoid(g) * a]


def _conv_halo_specs(width, cb0, n_rows):
    per = _TM // _HALO
    last = n_rows // _HALO - 1
    return [
        pl.BlockSpec((_TM, width), lambda i, j: (i, cb0 + j)),
        pl.BlockSpec((_HALO, width), lambda i, j: (jnp.maximum(i * per - 1, 0), cb0 + j)),
        pl.BlockSpec((_HALO, width), lambda i, j: (jnp.minimum((i + 1) * per, last), cb0 + j)),
    ]


def _fill_ext(ext, main_ref, prev_ref, next_ref, i, n_lat, nt):
    has_prev = (i != 0) & (i != n_lat)
    has_next = (i != n_lat - 1) & (i != nt - 1)
    ext[0:_HALO, :] = jnp.where(has_prev, prev_ref[...].astype(_F32), 0.0)
    ext[_HALO:_HALO + _TM, :] = main_ref[...].astype(_F32)
    ext[_HALO + _TM:, :] = jnp.where(has_next, next_ref[...].astype(_F32), 0.0)


def _dwconv(name, x, cb0, channels, taps, out_dtype, n_lat):
    rows = x.shape[0]
    ktaps = taps.shape[0]
    half = ktaps // 2
    width = _divisor(channels, 1536, 128)
    assert (cb0 * channels) % width == 0
    cb0 = cb0 * channels // width
    nt = rows // _TM

    def body(main_ref, prev_ref, next_ref, taps_ref, o_ref, ext):
        _fill_ext(ext, main_ref, prev_ref, next_ref, pl.program_id(0), n_lat, nt)
        acc = jnp.zeros((_TM, width), _F32)
        for k in range(ktaps):
            acc = acc + taps_ref[k:k + 1, :] * ext[_HALO - half + k:_HALO - half + k + _TM, :]
        o_ref[...] = acc.astype(out_dtype)

    return pl.pallas_call(
        body, name=name, grid=(nt, channels // width),
        out_shape=jax.ShapeDtypeStruct((rows, channels), out_dtype),
        in_specs=_conv_halo_specs(width, cb0, rows) + [pl.BlockSpec((ktaps, width), lambda i, j: (0, j))],
        out_specs=pl.BlockSpec((_TM, width), lambda i, j: (i, j)),
        scratch_shapes=[pltpu.VMEM((_TM + 2 * _HALO, width), _F32)],
        compiler_params=_params(("arbitrary", "arbitrary")),
    )(x, x, x, taps)


def _dwconv_wgrad(name, dy, x, cb0, channels, ktaps, n_lat):
    rows = x.shape[0]
    half = ktaps // 2
    width = _divisor(channels, 1536, 128)
    cb0 = cb0 * channels // width
    nt = rows // _TM

    def body(dy_ref, main_ref, prev_ref, next_ref, o_ref, ext):
        i = pl.program_id(1)
        _fill_ext(ext, main_ref, prev_ref, next_ref, i, n_lat, nt)

        @pl.when(i == 0)
        def _():
            o_ref[...] = jnp.zeros_like(o_ref)

        dyv = dy_ref[...].astype(_F32)
        for k in range(ktaps):
            o_ref[k:k + 1, :] += jnp.sum(dyv * ext[_HALO - half + k:_HALO - half + k + _TM, :], axis=0,
                                         keepdims=True)

    per = _TM // _HALO
    last = rows // _HALO - 1
    return pl.pallas_call(
        body, name=name, grid=(channels // width, nt),
        out_shape=jax.ShapeDtypeStruct((ktaps, channels), _F32),
        in_specs=[
            pl.BlockSpec((_TM, width), lambda j, i: (i, j)),
            pl.BlockSpec((_TM, width), lambda j, i: (i, cb0 + j)),
            pl.BlockSpec((_HALO, width), lambda j, i: (jnp.maximum(i * per - 1, 0), cb0 + j)),
            pl.BlockSpec((_HALO, width), lambda j, i: (jnp.minimum((i + 1) * per, last), cb0 + j)),
        ],
        out_specs=pl.BlockSpec((ktaps, width), lambda j, i: (0, j)),
        scratch_shapes=[pltpu.VMEM((_TM + 2 * _HALO, width), _F32)],
        compiler_params=_params(("arbitrary", "arbitrary")),
    )(dy, x, x, x)


def _rope_tables(length, ctx_len):
    t = jnp.arange(length)
    row = (t // _GRID_W).astype(_F32)
    col = (t % _GRID_W).astype(_F32)
    n_freq = _HEAD_DIM // 4
    inv_freq = _ROPE_THETA ** (-jnp.arange(n_freq, dtype=_F32) / n_freq)
    ang = jnp.concatenate([row[:, None] * inv_freq, col[:, None] * inv_freq], axis=-1)
    cos, sin = jnp.cos(ang), jnp.sin(ang)
    cos = jnp.concatenate([cos, jnp.ones((ctx_len, _HEAD_DIM // 2), _F32)], axis=0)
    sin = jnp.concatenate([sin, jnp.zeros((ctx_len, _HEAD_DIM // 2), _F32)], axis=0)
    return jnp.tile(cos, (1, 4)), jnp.tile(jnp.concatenate([-sin, sin], axis=-1), (1, 2))


def _rotate(v, cos_ref, sin_ref):
    width = v.shape[1]
    reps = width // 128
    cos = jnp.tile(cos_ref[...], (1, reps))
    sin = jnp.tile(sin_ref[...], (1, reps))
    return v * cos, sin, width


def _partner(v):
    width = v.shape[1]
    half = _HEAD_DIM // 2
    lane = lax.broadcasted_iota(jnp.int32, v.shape, 1)
    return jnp.where(lane % _HEAD_DIM < half, pltpu.roll(v, width - half, 1), pltpu.roll(v, half, 1))


def _rope_fwd(name, p, cos, sin, q_w, kv_w):
    rows, width = p.shape
    scale = _HEAD_DIM ** -0.5

    def body(p_ref, cos_ref, sin_ref, q_ref, k_ref):
        v = p_ref[:, :q_w + kv_w].astype(_F32)
        vc, s, _ = _rotate(v, cos_ref, sin_ref)
        y = vc + _partner(v) * s
        q_ref[...] = (y[:, :q_w] * scale).astype(q_ref.dtype)
        k_ref[...] = y[:, q_w:].astype(k_ref.dtype)

    return pl.pallas_call(
        body, name=name, grid=(rows // _TM,),
        out_shape=[jax.ShapeDtypeStruct((rows, q_w), _ACT), jax.ShapeDtypeStruct((rows, kv_w), _ACT)],
        in_specs=[pl.BlockSpec((_TM, width), lambda i: (i, 0)), pl.BlockSpec((_TM, 128), lambda i: (i, 0)),
                  pl.BlockSpec((_TM, 128), lambda i: (i, 0))],
        out_specs=[pl.BlockSpec((_TM, q_w), lambda i: (i, 0)), pl.BlockSpec((_TM, kv_w), lambda i: (i, 0))],
        compiler_params=_params(("arbitrary",)),
    )(p, cos, sin)


def _rope_bwd(name, dq, dk, dv, cos, sin):
    rows, q_w = dq.shape
    kv_w = dk.shape[1]
    scale = _HEAD_DIM ** -0.5

    def body(dq_ref, dk_ref, dv_ref, cos_ref, sin_ref, o_ref):
        dy = jnp.concatenate([dq_ref[...].astype(_F32) * scale, dk_ref[...].astype(_F32)], axis=1)
        dyc, s, _ = _rotate(dy, cos_ref, sin_ref)
        o_ref[:, :q_w + kv_w] = (dyc + _partner(dy * s)).astype(o_ref.dtype)
        o_ref[:, q_w + kv_w:] = dv_ref[...].astype(o_ref.dtype)

    return pl.pallas_call(
        body, name=name, grid=(rows // _TM,),
        out_shape=jax.ShapeDtypeStruct((rows, q_w + 2 * kv_w), _ACT),
        in_specs=[pl.BlockSpec((_TM, q_w), lambda i: (i, 0)), pl.BlockSpec((_TM, kv_w), lambda i: (i, 0)),
                  pl.BlockSpec((_TM, kv_w), lambda i: (i, 0)), pl.BlockSpec((_TM, 128), lambda i: (i, 0)),
                  pl.BlockSpec((_TM, 128), lambda i: (i, 0))],
        out_specs=pl.BlockSpec((_TM, q_w + 2 * kv_w), lambda i: (i, 0)),
        compiler_params=_params(("arbitrary",)),
    )(dq, dk, dv, cos, sin)


def _attn_window(i, n_lat, length):
    wk = _TM + 2 * _WINDOW
    start = pl.multiple_of(jnp.clip(i * _TM - _WINDOW, 0, length - wk), _WINDOW)
    q_pos = i * _TM + lax.broadcasted_iota(jnp.int32, (_TM, wk), 0)
    k_pos = start + lax.broadcasted_iota(jnp.int32, (_TM, wk), 1)
    mask = (jnp.abs(q_pos - k_pos) <= _WINDOW) & (i < n_lat)
    return start, wk, mask


def _softmax_parts(q, k_loc, k_ctx, mask, sink):
    nt = (((1,), (1,)), ((), ()))
    s_loc = jnp.where(mask, lax.dot_general(q, k_loc, nt, preferred_element_type=_F32), _NEG_INF)
    s_ctx = lax.dot_general(q, k_ctx, nt, preferred_element_type=_F32)
    m = jnp.maximum(jnp.maximum(jnp.max(s_loc, axis=-1, keepdims=True), jnp.max(s_ctx, axis=-1, keepdims=True)),
                    sink)
    e_loc = jnp.exp(s_loc - m)
    e_ctx = jnp.exp(s_ctx - m)
    e_sink = jnp.exp(sink - m)
    inv = 1.0 / (jnp.sum(e_loc, axis=-1, keepdims=True) + jnp.sum(e_ctx, axis=-1, keepdims=True) + e_sink)
    return e_loc * inv, e_ctx * inv, e_sink * inv


def _attn_fwd(name, q, k, p, sinks, n_lat, length, kv_w):
    rows, q_w = q.shape
    ctx_len = rows - length
    n_heads = q_w // _HEAD_DIM
    n_kv = kv_w // _HEAD_DIM
    group = n_heads // n_kv
    v_cb = p.shape[1] // kv_w - 1
    hd = _HEAD_DIM

    def body(q_ref, k_ref, v_ref, sink_ref, o_ref):
        i = pl.program_id(0)
        start, wk, mask = _attn_window(i, n_lat, length)
        for h in range(n_kv):
            k_loc = k_ref[pl.ds(start, wk), h * hd:(h + 1) * hd]
            v_loc = v_ref[pl.ds(start, wk), h * hd:(h + 1) * hd]
            k_ctx = k_ref[length:length + ctx_len, h * hd:(h + 1) * hd]
            v_ctx = v_ref[length:length + ctx_len, h * hd:(h + 1) * hd]
            for g in range(group):
                n = h * group + g
                p_loc, p_ctx, _ = _softmax_parts(q_ref[:, n * hd:(n + 1) * hd], k_loc, k_ctx, mask,
                                                 sink_ref[:, n:n + 1])
                o = (jnp.dot(p_loc.astype(_MM), v_loc, preferred_element_type=_F32)
                     + jnp.dot(p_ctx.astype(_MM), v_ctx, preferred_element_type=_F32))
                o_ref[:, n * hd:(n + 1) * hd] = o.astype(o_ref.dtype)

    return pl.pallas_call(
        body, name=name, grid=(rows // _TM,),
        out_shape=jax.ShapeDtypeStruct((rows, q_w), _ACT),
        in_specs=[pl.BlockSpec((_TM, q_w), lambda i: (i, 0)), pl.BlockSpec((rows, kv_w), lambda i: (0, 0)),
                  pl.BlockSpec((rows, kv_w), lambda i: (0, v_cb)), pl.BlockSpec((1, n_heads), lambda i: (0, 0))],
        out_specs=pl.BlockSpec((_TM, q_w), lambda i: (i, 0)),
        compiler_params=_params(("arbitrary",)),
    )(q, k, p, sinks)


def _attn_bwd(name, q, k, p, sinks, do, n_lat, length, kv_w):
    rows, q_w = q.shape
    ctx_len = rows - length
    n_heads = q_w // _HEAD_DIM
    n_kv = kv_w // _HEAD_DIM
    group = n_heads // n_kv
    v_cb = p.shape[1] // kv_w - 1
    hd = _HEAD_DIM
    nt_dims = (((1,), (1,)), ((), ()))
    tn_dims = (((0,), (0,)), ((), ()))

    def body(q_ref, k_ref, v_ref, sink_ref, do_ref, dq_ref, dk_out, dv_out, ds_ref, dk_ref, dv_ref, out_sems):
        i = pl.program_id(0)

        @pl.when(i == 0)
        def _():
            dk_ref[...] = jnp.zeros_like(dk_ref)
            dv_ref[...] = jnp.zeros_like(dv_ref)
            ds_ref[...] = jnp.zeros_like(ds_ref)

        start, wk, mask = _attn_window(i, n_lat, length)
        head_lane = lax.broadcasted_iota(jnp.int32, (1, n_heads), 1)
        dsink = jnp.zeros((1, n_heads), _F32)
        for h in range(n_kv):
            cols = slice(h * hd, (h + 1) * hd)
            k_loc = k_ref[pl.ds(start, wk), cols]
            v_loc = v_ref[pl.ds(start, wk), cols]
            k_ctx = k_ref[length:length + ctx_len, cols]
            v_ctx = v_ref[length:length + ctx_len, cols]
            dk_loc = jnp.zeros((wk, hd), _F32)
            dv_loc = jnp.zeros((wk, hd), _F32)
            dk_ctx = jnp.zeros((ctx_len, hd), _F32)
            dv_ctx = jnp.zeros((ctx_len, hd), _F32)
            for g in range(group):
                n = h * group + g
                qh = q_ref[:, n * hd:(n + 1) * hd]
                doh = do_ref[:, n * hd:(n + 1) * hd].astype(_MM)
                p_loc, p_ctx, p_sink = _softmax_parts(qh, k_loc, k_ctx, mask, sink_ref[:, n:n + 1])
                dp_loc = lax.dot_general(doh, v_loc, nt_dims, preferred_element_type=_F32)
                dp_ctx = lax.dot_general(doh, v_ctx, nt_dims, preferred_element_type=_F32)
                dsum = (jnp.sum(p_loc * dp_loc, axis=-1, keepdims=True)
                        + jnp.sum(p_ctx * dp_ctx, axis=-1, keepdims=True))
                ds_loc = (p_loc * (dp_loc - dsum)).astype(_MM)
                ds_ctx = (p_ctx * (dp_ctx - dsum)).astype(_MM)
                dsink = dsink + jnp.where(head_lane == n, -jnp.sum(p_sink * dsum), 0.0)
                dq = (jnp.dot(ds_loc, k_loc, preferred_element_type=_F32)
                      + jnp.dot(ds_ctx, k_ctx, preferred_element_type=_F32))
                dq_ref[:, n * hd:(n + 1) * hd] = dq.astype(dq_ref.dtype)
                dk_loc += lax.dot_general(ds_loc, qh, tn_dims, preferred_element_type=_F32)
                dk_ctx += lax.dot_general(ds_ctx, qh, tn_dims, preferred_element_type=_F32)
                dv_loc += lax.dot_general(p_loc.astype(_MM), doh, tn_dims, preferred_element_type=_F32)
                dv_ctx += lax.dot_general(p_ctx.astype(_MM), doh, tn_dims, preferred_element_type=_F32)
            dk_ref[pl.ds(start, wk), cols] += dk_loc
            dv_ref[pl.ds(start, wk), cols] += dv_loc
            dk_ref[length:length + ctx_len, cols] += dk_ctx
            dv_ref[length:length + ctx_len, cols] += dv_ctx
        ds_ref[...] += dsink

        @pl.when(i == rows // _TM - 1)
        def _():
            copies = [pltpu.make_async_copy(dk_ref, dk_out, out_sems.at[0]),
                      pltpu.make_async_copy(dv_ref, dv_out, out_sems.at[1])]
            for cp in copies:
                cp.start()
            for cp in copies:
                cp.wait()

    return pl.pallas_call(
        body, name=name, grid=(rows // _TM,),
        out_shape=[jax.ShapeDtypeStruct((rows, q_w), _ACT), jax.ShapeDtypeStruct((rows, kv_w), _F32),
                   jax.ShapeDtypeStruct((rows, kv_w), _F32), jax.ShapeDtypeStruct((1, n_heads), _F32)],
        in_specs=[pl.BlockSpec((_TM, q_w), lambda i: (i, 0)), pl.BlockSpec((rows, kv_w), lambda i: (0, 0)),
                  pl.BlockSpec((rows, kv_w), lambda i: (0, v_cb)), pl.BlockSpec((1, n_heads), lambda i: (0, 0)),
                  pl.BlockSpec((_TM, q_w), lambda i: (i, 0))],
        out_specs=[pl.BlockSpec((_TM, q_w), lambda i: (i, 0)), pl.BlockSpec(memory_space=pl.ANY),
                   pl.BlockSpec(memory_space=pl.ANY), pl.BlockSpec((1, n_heads), lambda i: (0, 0))],
        scratch_shapes=[pltpu.VMEM((rows, kv_w), _F32), pltpu.VMEM((rows, kv_w), _F32),
                        pltpu.SemaphoreType.DMA((2,))],
        compiler_params=_params(("arbitrary",)),
    )(q, k, p, sinks, do)


def _loss_head(name, xs, gain, target, n_lat):
    rows, d = xs.shape

    def body(x_ref, g_ref, t_ref, loss_ref, dg_ref, dx_ref):
        i = pl.program_id(0)

        @pl.when(i == 0)
        def _():
            loss_ref[...] = jnp.zeros_like(loss_ref)
            dg_ref[...] = jnp.zeros_like(dg_ref)

        @pl.when(i < n_lat)
        def _():
            tv = t_ref[...]

            def f(gain_, x):
                y = x * lax.rsqrt(jnp.mean(x * x, axis=-1, keepdims=True) + _EPS) * gain_
                return 0.5 * jnp.sum(jnp.mean(jnp.square(y - tv), axis=-1))

            val, (dg, dx) = jax.value_and_grad(f, argnums=(0, 1))(g_ref[...], x_ref[...])
            loss_ref[...] += val
            dg_ref[...] += dg
            dx_ref[...] = dx

        @pl.when(i >= n_lat)
        def _():
            dx_ref[...] = jnp.zeros_like(dx_ref)

    return pl.pallas_call(
        body, name=name, grid=(rows // _TM,),
        out_shape=[jax.ShapeDtypeStruct((1, 128), _F32), jax.ShapeDtypeStruct((1, d), _F32),
                   jax.ShapeDtypeStruct((rows, d), _F32)],
        in_specs=[pl.BlockSpec((_TM, d), lambda i: (i, 0)), pl.BlockSpec((1, d), lambda i: (0, 0)),
                  pl.BlockSpec((_TM, d), lambda i: (jnp.minimum(i, n_lat - 1), 0))],
        out_specs=[pl.BlockSpec((1, 128), lambda i: (0, 0)), pl.BlockSpec((1, d), lambda i: (0, 0)),
                   pl.BlockSpec((_TM, d), lambda i: (i, 0))],
        compiler_params=_params(("arbitrary",)),
    )(xs, gain, target)


def _adamw(name, w, g, m, v):
    rows, cols = w.shape
    tr = _divisor(rows, 512, 8)
    b1, b2 = _ADAM["b1"], _ADAM["b2"]
    c1 = 1.0 - b1 ** _ADAM["step"]
    c2 = 1.0 - b2 ** _ADAM["step"]

    def body(w_ref, g_ref, m_ref, v_ref, d_ref, nm_ref, nv_ref):
        gv = g_ref[...]
        nm = b1 * m_ref[...] + (1.0 - b1) * gv
        nv = b2 * v_ref[...] + (1.0 - b2) * jnp.square(gv)
        d_ref[...] = -_ADAM["lr"] * ((nm / c1) / (jnp.sqrt(nv / c2) + _ADAM["eps"]) + _ADAM["wd"] * w_ref[...])
        nm_ref[...] = nm
        nv_ref[...] = nv

    spec = pl.BlockSpec((tr, cols), lambda i: (i, 0))
    return pl.pallas_call(
        body, name=name, grid=(rows // tr,),
        out_shape=[jax.ShapeDtypeStruct((rows, cols), _F32)] * 3,
        in_specs=[spec] * 4, out_specs=[spec] * 3,
        compiler_params=_params(("arbitrary",)),
    )(w, g, m, v)


def _pack(arrays, cols=128):
    flat = jnp.concatenate([a.reshape(-1).astype(_F32) for a in arrays])
    pad = (-flat.shape[0]) % (64 * cols)
    return jnp.pad(flat, (0, pad)).reshape(-1, cols)


def _unpack(flat, shapes):
    out, off = [], 0
    for s in shapes:
        n = 1
        for d in s:
            n *= d
        out.append(flat[..., off:off + n].reshape(flat.shape[:-1] + tuple(s)))
        off += n
    return out


def _gather_channels(parts):
    moved = jnp.moveaxis(parts, 0, -2)
    return moved.reshape(moved.shape[:-2] + (moved.shape[-2] * moved.shape[-1],))


def kernel(x, c, ctx, c_ctx, w_mod, b_mod, norm_mix, norm_ffn, w_in_ab, conv_a, conv_b, conv_b_bias, ln_b_gain, ln_b_bias, w_out_ab, w_qkv, w_o, sinks, w_up, w_conv_ffn, w_down, final_norm, loss_target, m_c_ctx, m_w_mod, m_b_mod, m_norm_mix, m_norm_ffn, m_w_in_ab, m_conv_a, m_conv_b, m_conv_b_bias, m_ln_b_gain, m_ln_b_bias, m_w_out_ab, m_w_qkv, m_w_o, m_sinks, m_w_up, m_w_conv_ffn, m_w_down, m_final_norm, v_c_ctx, v_w_mod, v_b_mod, v_norm_mix, v_norm_ffn, v_w_in_ab, v_conv_a, v_conv_b, v_conv_b_bias, v_ln_b_gain, v_ln_b_bias, v_w_out_ab, v_w_qkv, v_w_o, v_sinks, v_w_up, v_w_conv_ffn, v_w_down, v_final_norm):
    args = dict(locals())
    weight_names = ["c_ctx", "w_mod", "b_mod", "norm_mix", "norm_ffn", "w_in_ab", "conv_a", "conv_b", "conv_b_bias",
                    "ln_b_gain", "ln_b_bias", "w_out_ab", "w_qkv", "w_o", "sinks", "w_up", "w_conv_ffn", "w_down",
                    "final_norm"]
    length, d = x.shape[1], x.shape[2]
    ctx_len = ctx.shape[1]
    assert ctx_len == _TM and length % _TM == 0 and x.shape[0] == 1
    n_lat = length // _TM
    depth = w_mod.shape[0]
    n_even, n_odd = w_in_ab.shape[0], w_qkv.shape[0]
    a_w = conv_a.shape[2] * _NDEV
    b_w = conv_b.shape[2] * _NDEV
    assert a_w == b_w
    q_w = w_o.shape[1] * _NDEV
    kv_w = (w_qkv.shape[2] * _NDEV - q_w) // 2
    d_ff = w_down.shape[1] * _NDEV
    dev = 4 * lax.axis_index("x") + 2 * lax.axis_index("y") + lax.axis_index("c")

    small_shapes = [c.shape[1:], conv_a.shape, conv_b.shape, w_conv_ffn.shape]
    g0 = _all_gather_small("gather_small_params", _pack([c, conv_a, conv_b, w_conv_ffn]))
    c_parts, ca_parts, cb_parts, cf_parts = _unpack(g0.reshape(_NDEV, -1), small_shapes)
    conv_a_full = _gather_channels(ca_parts)
    conv_b_full = _gather_channels(cb_parts)
    conv_f_full = _gather_channels(cf_parts)

    cond = jnp.concatenate([c_parts, c_ctx[None], jnp.zeros((16 - _NDEV - 1, d), _F32)], axis=0)
    mod_cols = w_mod.shape[2]
    m_shard = _mod_forward("mod_forward", cond, w_mod)
    m_all = _all_gather_small("gather_mod", m_shard.reshape(depth * 16, mod_cols))
    m_all = jnp.moveaxis(m_all.reshape(_NDEV, depth, 16, mod_cols), 0, 2).reshape(depth, 16, _NDEV * mod_cols)
    m_all = m_all + b_mod[:, None, :]
    m_lat = lax.dynamic_index_in_dim(m_all, dev, axis=1, keepdims=False)
    m_ctx = m_all[:, _NDEV]

    def mod_vec(l, j):
        return jnp.stack([m_lat[l, j * d:(j + 1) * d], m_ctx[l, j * d:(j + 1) * d]])[:, None, :]

    fams = [("in", [w_in_ab[e].T for e in range(n_even)]), ("qkv", [w_qkv[o].T for o in range(n_odd)]),
            ("up", [w_up[l].T for l in range(depth)]), ("out", [w_out_ab[e] for e in range(n_even)]),
            ("o", [w_o[o] for o in range(n_odd)]), ("down", [w_down[l] for l in range(depth)])]
    slab_off, slab_r, pieces, off = {}, {}, [], 0
    for fam, mats in fams:
        slab_r[fam] = mats[0].shape[0]
        for idx, mat in enumerate(mats):
            slab_off[fam, idx] = off
            off += mat.shape[0]
            pieces.append(mat.astype(_MM))
    slab_rows = off
    wg = _all_gather_slab("gather_weights", jnp.concatenate(pieces, axis=0))

    cos, sin = _rope_tables(length, ctx_len)
    xs = jnp.concatenate([x[0], ctx[0]], axis=0)

    def full(a):
        return (a.reshape(1, -1), "full")

    saved = []
    for l in range(depth):
        sv = {"x_in": xs}
        gain1 = full(norm_mix[l])
        (h1,) = _rowfn(f"norm_mix_{l}", _f_norm_mod, [gain1, (mod_vec(l, 0), "stream"), (mod_vec(l, 1), "stream")],
                       [(xs, d, 0)], [(_ACT, [d])], n_lat)
        sv["h1"] = h1
        if l % 2 == 0:
            e = l // 2
            p = _mm_nt(f"proj_in_{l}", h1, wg, slab_off["in", e], slab_r["in"], _ACT)
            (qm,) = _rowfn(f"premix_{l}", _f_premix, [], [(p, a_w, j) for j in range(5)],
                           [(None, [a_w]), (_ACT, [a_w, b_w])], n_lat)
            cv_a = _dwconv(f"conv_a_{l}", qm, 0, a_w, conv_a_full[e], _ACT, n_lat)
            cv_b = _dwconv(f"conv_b_{l}", qm, 1, b_w, conv_b_full[e], _ACT, n_lat)
            post_params = [full(conv_b_bias[e]), full(ln_b_gain[e]), full(ln_b_bias[e])]
            (z,) = _rowfn(f"postmix_{l}", _f_postmix, post_params, [(p, a_w, 0), (cv_a, a_w, 0), (cv_b, b_w, 0)],
                          [(_ACT, [a_w, b_w])], n_lat)
            y1 = _mm_nn(f"proj_out_{l}", z, wg, slab_off["out", e], slab_r["out"], _ACT)
            sv.update(p=p, qm=qm, cv_a=cv_a, cv_b=cv_b, z=z)
        else:
            o = l // 2
            p = _mm_nt(f"proj_qkv_{l}", h1, wg, slab_off["qkv", o], slab_r["qkv"], _ACT)
            qr, kr = _rope_fwd(f"rope_{l}", p, cos, sin, q_w, kv_w)
            sk = sinks[o].reshape(1, -1)
            z = _attn_fwd(f"attn_{l}", qr, kr, p, sk, n_lat, length, kv_w)
            y1 = _mm_nn(f"proj_o_{l}", z, wg, slab_off["o", o], slab_r["o"], _ACT)
            sv.update(p=p, qr=qr, kr=kr, z=z)
        (xs,) = _rowfn(f"residual_mix_{l}", _f_residual, [(mod_vec(l, 2), "stream")], [(xs, d, 0), (y1, d, 0)],
                       [(_F32, [d])], n_lat)
        sv.update(y1=y1, x_mid=xs)
        (h2,) = _rowfn(f"norm_ffn_{l}", _f_norm_mod,
                       [full(norm_ffn[l]), (mod_vec(l, 3), "stream"), (mod_vec(l, 4), "stream")],
                       [(xs, d, 0)], [(_ACT, [d])], n_lat)
        pu = _mm_nt(f"proj_up_{l}", h2, wg, slab_off["up", l], slab_r["up"], _ACT)
        u = _dwconv(f"conv_ffn_{l}", pu, 0, 2 * d_ff, conv_f_full[l], _ACT, n_lat)
        (f,) = _rowfn(f"glu_{l}", _f_glu, [], [(u, d_ff, 0), (u, d_ff, 1)], [(_ACT, [d_ff])], n_lat)
        y2 = _mm_nn(f"proj_down_{l}", f, wg, slab_off["down", l], slab_r["down"], _ACT)
        (xs,) = _rowfn(f"residual_ffn_{l}", _f_residual, [(mod_vec(l, 5), "stream")], [(xs, d, 0), (y2, d, 0)],
                       [(_F32, [d])], n_lat)
        sv.update(h2=h2, pu=pu, u=u, f=f, y2=y2)
        saved.append(sv)

    loss_part, d_final_norm, dxs = _loss_head("loss_head", xs, final_norm.reshape(1, -1), loss_target[0], n_lat)
    loss = lax.psum(loss_part[0, 0], ("x", "y", "c"))

    wgrads = {}
    d_mod = [[None] * 6 for _ in range(depth)]
    d_norm_mix, d_norm_ffn = [None] * depth, [None] * depth
    d_conv_a, d_conv_b = [None] * n_even, [None] * n_even
    d_bias, d_ln_g, d_ln_b = [None] * n_even, [None] * n_even, [None] * n_even
    d_sinks = [None] * n_odd
    d_conv_f = [None] * depth
    for l in reversed(range(depth)):
        sv = saved[l]
        (dg,), (dy2,) = _rowfn_bwd(f"residual_ffn_bwd_{l}", _f_gate, [(mod_vec(l, 5), "stream")], [(sv["y2"], d, 0)],
                                   [0], [(dxs, [d])], [(_ACT, [0])], n_lat)
        d_mod[l][5] = dg
        wgrads["down", l] = _mm_tn(f"wgrad_down_{l}", sv["f"], dy2, _ACT)
        df = _mm_nt(f"bwd_down_{l}", dy2, wg, slab_off["down", l], slab_r["down"], _ACT)
        _, (du,) = _rowfn_bwd(f"glu_bwd_{l}", _f_glu, [], [(sv["u"], d_ff, 0), (sv["u"], d_ff, 1)], [0, 1],
                              [(df, [d_ff])], [(_ACT, [0, 1])], n_lat)
        d_conv_f[l] = _dwconv_wgrad(f"conv_ffn_wgrad_{l}", du, sv["pu"], 0, 2 * d_ff, 3, n_lat)
        dpu = _dwconv(f"conv_ffn_bwd_{l}", du, 0, 2 * d_ff, conv_f_full[l][::-1], _ACT, n_lat)
        wgrads["up", l] = _mm_tn(f"wgrad_up_{l}", dpu, sv["h2"], _ACT)
        dh2 = _mm_nn(f"bwd_up_{l}", dpu, wg, slab_off["up", l], slab_r["up"], _ACT)
        (dgain, dsh, dsc), (dxs,) = _rowfn_bwd(
            f"norm_ffn_bwd_{l}", _f_norm_mod,
            [full(norm_ffn[l]), (mod_vec(l, 3), "stream"), (mod_vec(l, 4), "stream")], [(sv["x_mid"], d, 0)], [0],
            [(dh2, [d])], [(_F32, [0])], n_lat, residual=dxs)
        d_norm_ffn[l], d_mod[l][3], d_mod[l][4] = dgain, dsh, dsc
        (dg,), (dy1,) = _rowfn_bwd(f"residual_mix_bwd_{l}", _f_gate, [(mod_vec(l, 2), "stream")], [(sv["y1"], d, 0)],
                                   [0], [(dxs, [d])], [(_ACT, [0])], n_lat)
        d_mod[l][2] = dg
        if l % 2 == 0:
            e = l // 2
            wgrads["out", e] = _mm_tn(f"wgrad_out_{l}", sv["z"], dy1, _ACT)
            dz = _mm_nt(f"bwd_out_{l}", dy1, wg, slab_off["out", e], slab_r["out"], _ACT)
            post_params = [full(conv_b_bias[e]), full(ln_b_gain[e]), full(ln_b_bias[e])]
            (dbias, dlg, dlb), (dgb, dcv_a, dcv_b) = _rowfn_bwd(
                f"postmix_bwd_{l}", _f_postmix, post_params,
                [(sv["p"], a_w, 0), (sv["cv_a"], a_w, 0), (sv["cv_b"], b_w, 0)], [0, 1, 2], [(dz, [a_w, b_w])],
                [(_ACT, [0]), (_ACT, [1]), (_ACT, [2])], n_lat)
            d_bias[e], d_ln_g[e], d_ln_b[e] = dbias, dlg, dlb
            d_conv_a[e] = _dwconv_wgrad(f"conv_a_wgrad_{l}", dcv_a, sv["qm"], 0, a_w, conv_a_full.shape[1], n_lat)
            d_conv_b[e] = _dwconv_wgrad(f"conv_b_wgrad_{l}", dcv_b, sv["qm"], 1, b_w, conv_b_full.shape[1], n_lat)
            dq_a = _dwconv(f"conv_a_bwd_{l}", dcv_a, 0, a_w, conv_a_full[e][::-1], _ACT, n_lat)
            dq_b = _dwconv(f"conv_b_bwd_{l}", dcv_b, 0, b_w, conv_b_full[e][::-1], _ACT, n_lat)
            _, (dp,) = _rowfn_bwd(f"premix_bwd_{l}", _f_premix, [], [(sv["p"], a_w, j) for j in range(5)],
                                  [0, 1, 2, 3, 4], [(dgb, [a_w]), (dq_a, [a_w]), (dq_b, [b_w])],
                                  [(_ACT, [0, 1, 2, 3, 4])], n_lat)
            wgrads["in", e] = _mm_tn(f"wgrad_in_{l}", dp, sv["h1"], _ACT)
            dh1 = _mm_nn(f"bwd_in_{l}", dp, wg, slab_off["in", e], slab_r["in"], _ACT)
        else:
            o = l // 2
            wgrads["o", o] = _mm_tn(f"wgrad_o_{l}", sv["z"], dy1, _ACT)
            dz = _mm_nt(f"bwd_o_{l}", dy1, wg, slab_off["o", o], slab_r["o"], _ACT)
            sk = sinks[o].reshape(1, -1)
            dqr, dkr, dv, dsk = _attn_bwd(f"attn_bwd_{l}", sv["qr"], sv["kr"], sv["p"], sk, dz, n_lat, length, kv_w)
            d_sinks[o] = dsk
            dp = _rope_bwd(f"rope_bwd_{l}", dqr, dkr, dv, cos, sin)
            wgrads["qkv", o] = _mm_tn(f"wgrad_qkv_{l}", dp, sv["h1"], _ACT)
            dh1 = _mm_nn(f"bwd_qkv_{l}", dp, wg, slab_off["qkv", o], slab_r["qkv"], _ACT)
        (dgain, dsh, dsc), (dxs,) = _rowfn_bwd(
            f"norm_mix_bwd_{l}", _f_norm_mod,
            [full(norm_mix[l]), (mod_vec(l, 0), "stream"), (mod_vec(l, 1), "stream")], [(sv["x_in"], d, 0)], [0],
            [(dh1, [d])], [(_F32, [0])], n_lat, residual=dxs)
        d_norm_mix[l], d_mod[l][0], d_mod[l][1] = dgain, dsh, dsc
    grad_x = dxs[:length][None]

    order = [(fam, idx) for fam, mats in fams for idx in range(len(mats))]
    parts = [wgrads[key].reshape(_NDEV, slab_r[key[0]], d) for key in order]
    recv = _exchange_grads("exchange_weight_grads", parts, [slab_off[key] for key in order], slab_rows)
    gsum = _sum_slots("sum_weight_grads", recv)

    def slab_grad(fam, count, transposed):
        mats = [gsum[slab_off[fam, i]:slab_off[fam, i] + slab_r[fam]] for i in range(count)]
        return jnp.stack([m_.T if transposed else m_ for m_ in mats])

    grads = {
        "w_in_ab": slab_grad("in", n_even, True), "w_qkv": slab_grad("qkv", n_odd, True),
        "w_up": slab_grad("up", depth, True), "w_out_ab": slab_grad("out", n_even, False),
        "w_o": slab_grad("o", n_odd, False), "w_down": slab_grad("down", depth, False),
    }

    dm_dev = jnp.stack([jnp.concatenate([d_mod[l][j][:, 0, :] for j in range(6)], axis=-1)
                        for l in range(depth)])
    small_grads = [dm_dev, jnp.stack(d_norm_mix), jnp.stack(d_norm_ffn), jnp.stack(d_conv_a), jnp.stack(d_conv_b),
                   jnp.stack(d_bias), jnp.stack(d_ln_g), jnp.stack(d_ln_b), jnp.stack(d_sinks), jnp.stack(d_conv_f),
                   d_final_norm]
    sg_shapes = [a.shape for a in small_grads]
    sg_all = _all_gather_small("gather_small_grads", _pack(small_grads))
    sg_sum = _sum_slots("sum_small_grads", sg_all)
    (dm_sum, g_norm_mix, g_norm_ffn, g_conv_a, g_conv_b, g_bias, g_ln_g, g_ln_b, g_sinks, g_conv_f,
     g_final_norm) = _unpack(sg_sum.reshape(-1), sg_shapes)
    dm_each = _unpack(sg_all.reshape(_NDEV, -1), sg_shapes[:1])[0]

    def my_channels(a):
        width = a.shape[-1] // _NDEV
        return lax.dynamic_slice_in_dim(a, dev * width, width, axis=a.ndim - 1)

    grads["b_mod"] = dm_sum[:, 0] + dm_sum[:, 1]
    grads["norm_mix"] = g_norm_mix.reshape(depth, d)
    grads["norm_ffn"] = g_norm_ffn.reshape(depth, d)
    grads["conv_a"] = my_channels(g_conv_a)
    grads["conv_b"] = my_channels(g_conv_b)
    grads["conv_b_bias"] = g_bias.reshape(n_even, b_w)
    grads["ln_b_gain"] = g_ln_g.reshape(n_even, b_w)
    grads["ln_b_bias"] = g_ln_b.reshape(n_even, b_w)
    grads["sinks"] = g_sinks.reshape(n_odd, -1)
    grads["w_conv_ffn"] = my_channels(g_conv_f)
    grads["final_norm"] = g_final_norm.reshape(d)

    dm_rows = jnp.concatenate([jnp.moveaxis(dm_each[:, :, 0], 0, 1), dm_sum[:, 1:2],
                               jnp.zeros((depth, 16 - _NDEV - 1, 6 * d), _F32)], axis=1)
    dm_mine = my_channels(dm_rows)
    grads["w_mod"] = jnp.stack([_mm_tn(f"wgrad_mod_{l}", cond, dm_mine[l], _F32, silu_a=True) for l in range(depth)])
    dcond = _mod_backward_cond("mod_backward_cond", dm_mine, w_mod)
    dcond_all = _all_gather_small("gather_dcond", dcond)
    dcond_sum = _sum_slots("sum_dcond", dcond_all)[_NDEV]
    sg = jax.nn.sigmoid(c_ctx)
    grads["c_ctx"] = dcond_sum * (sg * (1.0 + c_ctx * (1.0 - sg)))

    big = ["w_mod", "w_in_ab", "w_out_ab", "w_qkv", "w_o", "w_up", "w_down"]
    small = [n for n in weight_names if n not in big]
    delta, new_m, new_v = {}, {}, {}
    for n in big:
        w = args[n]
        two_d = lambda a: a.reshape(-1, w.shape[-1])
        dl, nm, nv = _adamw(f"adamw_{n}", two_d(w), two_d(grads[n]), two_d(args["m_" + n]), two_d(args["v_" + n]))
        delta[n], new_m[n], new_v[n] = dl.reshape(w.shape), nm.reshape(w.shape), nv.reshape(w.shape)
    shapes = [args[n].shape for n in small]
    grads = {n: grads[n].reshape(args[n].shape) for n in grads}
    dl, nm, nv = _adamw("adamw_small", _pack([args[n] for n in small]), _pack([grads[n] for n in small]),
                        _pack([args["m_" + n] for n in small]), _pack([args["v_" + n] for n in small]))
    for res, packed in ((delta, dl), (new_m, nm), (new_v, nv)):
        for n, a in zip(small, _unpack(packed.reshape(-1), shapes)):
            res[n] = a

    return (loss, grad_x, *[grads[n] for n in weight_names], *[delta[n] for n in weight_names],
            *[new_m[n] for n in weight_names], *[new_v[n] for n in weight_names])
```

```python
import functools

import jax
import jax.numpy as jnp
from jax import lax
from jax.experimental import pallas as pl
from jax.experimental.pallas import tpu as pltpu

_F32 = jnp.float32
_MM = jnp.bfloat16
_ACT = jnp.bfloat16
_TM = 256
_HALO = 16
_LANES = 128
_CONV_ROWS = 128
_MM_ROWS = 1024
_NDEV = 8
_HEAD_DIM = 64
_WINDOW = 128
_GRID_W = 64
_ROPE_THETA = 10000.0
_EPS = 1e-6
_NEG_INF = -1e30
_VMEM_LIMIT = 56 * 1024 * 1024
_ADAM = dict(lr=0.001, b1=0.9, b2=0.999, eps=1e-08, wd=0.01, step=10)
_MESH = pl.DeviceIdType.MESH


def _params(sem=None):
    return pltpu.CompilerParams(dimension_semantics=sem, vmem_limit_bytes=_VMEM_LIMIT)


def _divisor(n, cap, mult):
    if n <= cap:
        return n
    for d in range(cap - cap % mult, 0, -mult):
        if n % d == 0:
            return d
    raise ValueError(f"no tile for {n}")


def _my_coords():
    return lax.axis_index("x"), lax.axis_index("y"), lax.axis_index("c")


def _peer(k):
    x, y, c = _my_coords()
    px = 1 - x if k & 4 else x
    py = 1 - y if k & 2 else y
    pc = 1 - c if k & 1 else c
    return (px, py, pc), 4 * px + 2 * py + pc


def _all_gather_small(name, v):
    rows, cols = v.shape

    def body(v_ref, out_ref, send_sems, recv_sems):
        x, y, c = _my_coords()
        me = 4 * x + 2 * y + c
        out_ref[me] = v_ref[...]
        sends = []
        for k in range(1, _NDEV):
            peer, _ = _peer(k)
            cp = pltpu.make_async_remote_copy(
                src_ref=v_ref, dst_ref=out_ref.at[me], send_sem=send_sems.at[k - 1], recv_sem=recv_sems.at[k - 1],
                device_id=peer, device_id_type=_MESH)
            cp.start()
            sends.append(cp)
        for k in range(1, _NDEV):
            peer, pid = _peer(k)
            pltpu.make_async_remote_copy(
                src_ref=v_ref, dst_ref=out_ref.at[pid], send_sem=send_sems.at[k - 1], recv_sem=recv_sems.at[k - 1],
                device_id=peer, device_id_type=_MESH).wait_recv()
        for cp in sends:
            cp.wait_send()

    return pl.pallas_call(
        body, name=name,
        out_shape=jax.ShapeDtypeStruct((_NDEV, rows, cols), v.dtype),
        in_specs=[pl.BlockSpec(memory_space=pltpu.VMEM)],
        out_specs=pl.BlockSpec(memory_space=pltpu.VMEM),
        scratch_shapes=[pltpu.SemaphoreType.DMA((_NDEV - 1,)), pltpu.SemaphoreType.DMA((_NDEV - 1,))],
        compiler_params=pltpu.CompilerParams(vmem_limit_bytes=_VMEM_LIMIT),
    )(v)


def _all_gather_slab(name, slab):
    rows, cols = slab.shape

    def body(s_ref, out_ref, send_sems, recv_sems, local_sem):
        x, y, c = _my_coords()
        me = 4 * x + 2 * y + c
        local = pltpu.make_async_copy(s_ref, out_ref.at[me], local_sem)
        local.start()
        sends = []
        for k in range(1, _NDEV):
            peer, _ = _peer(k)
            cp = pltpu.make_async_remote_copy(
                src_ref=s_ref, dst_ref=out_ref.at[me], send_sem=send_sems.at[k - 1], recv_sem=recv_sems.at[k - 1],
                device_id=peer, device_id_type=_MESH)
            cp.start()
            sends.append(cp)
        for k in range(1, _NDEV):
            peer, pid = _peer(k)
            pltpu.make_async_remote_copy(
                src_ref=s_ref, dst_ref=out_ref.at[pid], send_sem=send_sems.at[k - 1], recv_sem=recv_sems.at[k - 1],
                device_id=peer, device_id_type=_MESH).wait_recv()
        for cp in sends:
            cp.wait_send()
        local.wait()

    return pl.pallas_call(
        body, name=name,
        out_shape=jax.ShapeDtypeStruct((_NDEV, rows, cols), slab.dtype),
        in_specs=[pl.BlockSpec(memory_space=pl.ANY)],
        out_specs=pl.BlockSpec(memory_space=pl.ANY),
        scratch_shapes=[pltpu.SemaphoreType.DMA((_NDEV - 1,)), pltpu.SemaphoreType.DMA((_NDEV - 1,)),
                        pltpu.SemaphoreType.DMA],
        compiler_params=pltpu.CompilerParams(vmem_limit_bytes=_VMEM_LIMIT),
    )(slab)


def _exchange_grads(name, parts, offsets, total_rows):
    n = len(parts)
    cols = parts[0].shape[2]
    dtype = parts[0].dtype

    def body(*refs):
        g_refs, out_ref = refs[:n], refs[n]
        send_sems, recv_sems, local_sems = refs[n + 1:]
        x, y, c = _my_coords()
        me = 4 * x + 2 * y + c

        def dst(slot, m):
            return out_ref.at[slot, pl.ds(offsets[m], parts[m].shape[1]), :]

        locals_ = []
        for m in range(n):
            cp = pltpu.make_async_copy(g_refs[m].at[me], dst(me, m), local_sems.at[m])
            cp.start()
            locals_.append(cp)
        sends = []
        for k in range(1, _NDEV):
            peer, pid = _peer(k)
            for m in range(n):
                cp = pltpu.make_async_remote_copy(
                    src_ref=g_refs[m].at[pid], dst_ref=dst(me, m), send_sem=send_sems.at[k - 1, m],
                    recv_sem=recv_sems.at[k - 1, m], device_id=peer, device_id_type=_MESH)
                cp.start()
                sends.append(cp)
        for k in range(1, _NDEV):
            peer, pid = _peer(k)
            for m in range(n):
                pltpu.make_async_remote_copy(
                    src_ref=g_refs[m].at[pid], dst_ref=dst(pid, m), send_sem=send_sems.at[k - 1, m],
                    recv_sem=recv_sems.at[k - 1, m], device_id=peer, device_id_type=_MESH).wait_recv()
        for cp in sends:
            cp.wait_send()
        for cp in locals_:
            cp.wait()

    return pl.pallas_call(
        body, name=name,
        out_shape=jax.ShapeDtypeStruct((_NDEV, total_rows, cols), dtype),
        in_specs=[pl.BlockSpec(memory_space=pl.ANY)] * n,
        out_specs=pl.BlockSpec(memory_space=pl.ANY),
        scratch_shapes=[pltpu.SemaphoreType.DMA((_NDEV - 1, n)), pltpu.SemaphoreType.DMA((_NDEV - 1, n)),
                        pltpu.SemaphoreType.DMA((n,))],
        compiler_params=pltpu.CompilerParams(vmem_limit_bytes=_VMEM_LIMIT),
    )(*parts)


def _sum_slots(name, v):
    _, rows, cols = v.shape
    tr = _divisor(rows, 512, 16)

    def body(v_ref, o_ref):
        acc = v_ref[0].astype(_F32)
        for e in range(1, _NDEV):
            acc = acc + v_ref[e].astype(_F32)
        o_ref[...] = acc

    return pl.pallas_call(
        body, name=name, grid=(rows // tr,),
        out_shape=jax.ShapeDtypeStruct((rows, cols), _F32),
        in_specs=[pl.BlockSpec((_NDEV, tr, cols), lambda i: (0, i, 0))],
        out_specs=pl.BlockSpec((tr, cols), lambda i: (i, 0)),
        compiler_params=_params(("arbitrary",)),
    )(v)


_HBM_SPEC = pl.BlockSpec(memory_space=pltpu.HBM)
_SEM_SPEC = pl.BlockSpec(memory_space=pltpu.SEMAPHORE)
_EFFECT = pltpu.SideEffectType.DATAFLOW_SIDE_EFFECTING


def _in_hbm(a):
    return pltpu.with_memory_space_constraint(a, pltpu.HBM)


def _block_for(ref, device):
    return ref if len(ref.shape) == 2 else ref.at[device]


def _fill_own_slot(name, srcs, offsets, total_rows):
    n = len(srcs)
    cols = srcs[0].shape[-1]

    def body(*refs):
        src_refs, out_ref, sems = refs[:n], refs[n], refs[n + 1]
        x, y, c = _my_coords()
        me = 4 * x + 2 * y + c
        copies = [pltpu.make_async_copy(_block_for(src_refs[m], me),
                                        out_ref.at[me, pl.ds(offsets[m], srcs[m].shape[-2]), :], sems.at[m])
                  for m in range(n)]
        for cp in copies:
            cp.start()
        for cp in copies:
            cp.wait()

    return pl.pallas_call(
        body, name=name,
        out_shape=jax.ShapeDtypeStruct((_NDEV, total_rows, cols), srcs[0].dtype),
        in_specs=[pl.BlockSpec(memory_space=pl.ANY)] * n,
        out_specs=pl.BlockSpec(memory_space=pl.ANY),
        scratch_shapes=[pltpu.SemaphoreType.DMA((n,))],
        compiler_params=pltpu.CompilerParams(vmem_limit_bytes=_VMEM_LIMIT),
    )(*srcs)


def _exchange_start(name, srcs, land, offsets):
    n = len(srcs)

    def body(*refs):
        src_refs, land_ref = refs[:n], refs[n]
        send_sems, recv_sems, token = refs[n + 1], refs[n + 2], refs[-1]
        x, y, c = _my_coords()
        me = 4 * x + 2 * y + c
        for k in range(1, _NDEV):
            peer, pid = _peer(k)
            for m in range(n):
                pltpu.make_async_remote_copy(
                    src_ref=_block_for(src_refs[m], pid),
                    dst_ref=land_ref.at[me, pl.ds(offsets[m], srcs[m].shape[-2]), :],
                    send_sem=send_sems, recv_sem=recv_sems, device_id=peer, device_id_type=_MESH).start()
        token[...] = jnp.zeros_like(token)

    sems = pltpu.SemaphoreType.DMA(())
    return pl.pallas_call(
        body, name=name,
        out_shape=(sems, sems, *[pltpu.HBM(s.shape, s.dtype) for s in srcs], pltpu.HBM(land.shape, land.dtype),
                   jax.ShapeDtypeStruct((8, 128), _F32)),
        in_specs=[_HBM_SPEC] * (n + 1),
        out_specs=(_SEM_SPEC, _SEM_SPEC, *[_HBM_SPEC] * (n + 1), pl.BlockSpec(memory_space=pltpu.VMEM)),
        input_output_aliases={i: 2 + i for i in range(n + 1)},
        compiler_params=pltpu.CompilerParams(has_side_effects=_EFFECT),
    )(*[_in_hbm(s) for s in srcs], _in_hbm(land))


def _exchange_wait(name, started, offsets, after):
    send_sems, recv_sems = started[0], started[1]
    srcs, land = list(started[2:-2]), started[-2]
    n = len(srcs)

    def body(*refs):
        src_refs, land_ref = refs[:n], refs[n]
        send_sems_, recv_sems_ = refs[n + 1], refs[n + 2]
        others = land_ref.at[pl.ds(0, _NDEV - 1)]
        cp = pltpu.make_async_remote_copy(src_ref=others, dst_ref=others, send_sem=send_sems_, recv_sem=recv_sems_,
                                          device_id=_peer(1)[0], device_id_type=_MESH)
        cp.wait_send()
        cp.wait_recv()

    res = pl.pallas_call(
        body, name=name,
        out_shape=(*[pltpu.HBM(s.shape, s.dtype) for s in srcs], pltpu.HBM(land.shape, land.dtype)),
        in_specs=[_HBM_SPEC] * (n + 1) + [_SEM_SPEC, _SEM_SPEC, pl.BlockSpec(memory_space=pl.ANY)],
        out_specs=tuple([_HBM_SPEC] * (n + 1)),
        input_output_aliases={i: i for i in range(n + 1)},
        compiler_params=pltpu.CompilerParams(has_side_effects=_EFFECT),
    )(*srcs, land, send_sems, recv_sems, after)
    return res[n]


def _load_weight(wg_ref, wbuf, sems, off, r, step):
    @pl.when(step == 0)
    def _():
        copies = [pltpu.make_async_copy(wg_ref.at[e, pl.ds(off, r), :], wbuf.at[pl.ds(e * r, r), :], sems.at[e])
                  for e in range(_NDEV)]
        for cp in copies:
            cp.start()
        for cp in copies:
            cp.wait()


def _mm_nt(name, a, wg, off, r, out_dtype):
    rows, kdim = a.shape
    n = _NDEV * r
    chunk = _divisor(n, 512, 128)

    def body(a_ref, wg_ref, o_ref, wbuf, sems):
        _load_weight(wg_ref, wbuf, sems, off, r, pl.program_id(0))
        av = a_ref[...].astype(_MM)
        for j in range(n // chunk):
            o_ref[:, j * chunk:(j + 1) * chunk] = lax.dot_general(
                av, wbuf[j * chunk:(j + 1) * chunk, :], (((1,), (1,)), ((), ())),
                preferred_element_type=_F32).astype(out_dtype)

    tm = _divisor(rows, _MM_ROWS, _TM)
    return pl.pallas_call(
        body, name=name, grid=(rows // tm,),
        out_shape=jax.ShapeDtypeStruct((rows, n), out_dtype),
        in_specs=[pl.BlockSpec((tm, kdim), lambda i: (i, 0)), pl.BlockSpec(memory_space=pl.ANY)],
        out_specs=pl.BlockSpec((tm, n), lambda i: (i, 0)),
        scratch_shapes=[pltpu.VMEM((n, kdim), wg.dtype), pltpu.SemaphoreType.DMA((_NDEV,))],
        compiler_params=_params(("arbitrary",)),
    )(a, wg)


def _mm_nn(name, a, wg, off, r, out_dtype):
    rows, kdim = a.shape
    assert kdim == _NDEV * r
    n = wg.shape[2]

    def body(a_ref, wg_ref, o_ref, wbuf, sems):
        _load_weight(wg_ref, wbuf, sems, off, r, pl.program_id(0))
        o_ref[...] = jnp.dot(a_ref[...].astype(_MM), wbuf[...], preferred_element_type=_F32).astype(out_dtype)

    tm = _divisor(rows, _MM_ROWS, _TM)
    return pl.pallas_call(
        body, name=name, grid=(rows // tm,),
        out_shape=jax.ShapeDtypeStruct((rows, n), out_dtype),
        in_specs=[pl.BlockSpec((tm, kdim), lambda i: (i, 0)), pl.BlockSpec(memory_space=pl.ANY)],
        out_specs=pl.BlockSpec((tm, n), lambda i: (i, 0)),
        scratch_shapes=[pltpu.VMEM((kdim, n), wg.dtype), pltpu.SemaphoreType.DMA((_NDEV,))],
        compiler_params=_params(("arbitrary",)),
    )(a, wg)


def _mm_tn(name, a, b, out_dtype, silu_a=False):
    rows, na = a.shape
    nb = b.shape[1]
    tr = _divisor(rows, 1536, 16)
    tn = _divisor(na, 512, 128)
    steps = rows // tr

    def body(a_ref, b_ref, o_ref, acc):
        t = pl.program_id(1)

        @pl.when(t == 0)
        def _():
            acc[...] = jnp.zeros_like(acc)

        av = a_ref[...]
        if silu_a:
            av = av.astype(_F32)
            av = av * jax.nn.sigmoid(av)
        acc[...] += lax.dot_general(av.astype(_MM), b_ref[...].astype(_MM), (((0,), (0,)), ((), ())),
                                    preferred_element_type=_F32)

        @pl.when(t == steps - 1)
        def _():
            o_ref[...] = acc[...].astype(out_dtype)

    return pl.pallas_call(
        body, name=name, grid=(na // tn, steps),
        out_shape=jax.ShapeDtypeStruct((na, nb), out_dtype),
        in_specs=[pl.BlockSpec((tr, tn), lambda j, t: (t, j)), pl.BlockSpec((tr, nb), lambda j, t: (t, 0))],
        out_specs=pl.BlockSpec((tn, nb), lambda j, t: (j, 0)),
        scratch_shapes=[pltpu.VMEM((tn, nb), _F32)],
        compiler_params=_params(("arbitrary", "arbitrary")),
    )(a, b)


def _mod_forward(name, cond, w_mod):
    depth, d, n = w_mod.shape
    rows = cond.shape[0]

    def body(c_ref, w_ref, o_ref):
        cv = c_ref[...]
        a = (cv * jax.nn.sigmoid(cv)).astype(_MM)
        o_ref[...] = jnp.dot(a, w_ref[...].astype(_MM), preferred_element_type=_F32)

    return pl.pallas_call(
        body, name=name, grid=(depth,),
        out_shape=jax.ShapeDtypeStruct((depth, rows, n), _F32),
        in_specs=[pl.BlockSpec((rows, d), lambda l: (0, 0)), pl.BlockSpec((None, d, n), lambda l: (l, 0, 0))],
        out_specs=pl.BlockSpec((None, rows, n), lambda l: (l, 0, 0)),
        compiler_params=_params(("arbitrary",)),
    )(cond, w_mod)


def _mod_backward_cond(name, dm, w_mod):
    depth, d, n = w_mod.shape
    rows = dm.shape[1]

    def body(g_ref, w_ref, o_ref):
        @pl.when(pl.program_id(0) == 0)
        def _():
            o_ref[...] = jnp.zeros_like(o_ref)

        o_ref[...] += lax.dot_general(g_ref[...].astype(_MM), w_ref[...].astype(_MM), (((1,), (1,)), ((), ())),
                                      preferred_element_type=_F32)

    return pl.pallas_call(
        body, name=name, grid=(depth,),
        out_shape=jax.ShapeDtypeStruct((rows, d), _F32),
        in_specs=[pl.BlockSpec((None, rows, n), lambda l: (l, 0, 0)), pl.BlockSpec((None, d, n), lambda l: (l, 0, 0))],
        out_specs=pl.BlockSpec((rows, d), lambda l: (0, 0)),
        compiler_params=_params(("arbitrary",)),
    )(dm, w_mod)


def _param_spec(arr, kind, n_lat):
    if kind == "stream":
        return pl.BlockSpec((None,) + arr.shape[1:], lambda i: (i // n_lat, 0, 0))
    return pl.BlockSpec(arr.shape, lambda i: (0,) * arr.ndim)


def _rowfn(name, fn, params, xs, outs, n_lat):
    rows = xs[0][0].shape[0]
    np_, nx = len(params), len(xs)
    stored = [(dt, ws) for dt, ws in outs if dt is not None]

    def body(*refs):
        ps = [r[...].astype(_F32) for r in refs[:np_]]
        xv = [r[...].astype(_F32) for r in refs[np_:np_ + nx]]
        pieces = fn(ps, xv)
        o_refs = iter(refs[np_ + nx:])
        k = 0
        for dt, ws in outs:
            o_ref = next(o_refs) if dt is not None else None
            off = 0
            for w in ws:
                if o_ref is not None:
                    o_ref[:, off:off + w] = pieces[k].astype(dt)
                off += w
                k += 1

    return pl.pallas_call(
        body, name=name, grid=(rows // _TM,),
        out_shape=[jax.ShapeDtypeStruct((rows, sum(ws)), dt) for dt, ws in stored],
        in_specs=[_param_spec(a, kind, n_lat) for a, kind in params]
        + [pl.BlockSpec((_TM, w), lambda i, cb=cb: (i, cb)) for _, w, cb in xs],
        out_specs=[pl.BlockSpec((_TM, sum(ws)), lambda i: (i, 0)) for _, ws in stored],
        compiler_params=_params(("arbitrary",)),
    )(*[a for a, _ in params], *[a for a, _, _ in xs])


def _rowfn_bwd(name, fn, params, xs, diff, douts, dx_outs, n_lat, residual=None):
    rows = xs[0][0].shape[0]
    np_, nx, nd = len(params), len(xs), len(douts)
    nres = 0 if residual is None else 1
    nt = rows // _TM

    def body(*refs):
        i = pl.program_id(0)
        ps = [r[...].astype(_F32) for r in refs[:np_]]
        xv = [r[...].astype(_F32) for r in refs[np_:np_ + nx]]
        d_refs = refs[np_ + nx:np_ + nx + nd]
        res_ref = refs[np_ + nx + nd] if nres else None
        dp_refs = refs[np_ + nx + nd + nres:np_ + nx + nd + nres + np_]
        dx_refs = refs[np_ + nx + nd + nres + np_:]

        def f(ps_, xd):
            full = list(xv)
            for j, v in zip(diff, xd):
                full[j] = v
            return fn(ps_, full)

        _, vjp = jax.vjp(f, ps, [xv[j] for j in diff])
        cts = []
        for d_ref, (_, ws) in zip(d_refs, douts):
            off = 0
            for w in ws:
                cts.append(d_ref[:, off:off + w].astype(_F32))
                off += w
        dps, dxd = vjp(cts)
        grads = dict(zip(diff, dxd))
        for (dp_ref, (_, kind)), dp in zip(zip(dp_refs, params), dps):
            first = (i == 0) | (i == n_lat) if kind == "stream" else i == 0

            @pl.when(first)
            def _(dp_ref=dp_ref):
                dp_ref[...] = jnp.zeros_like(dp_ref)

            dp_ref[...] += dp
        for n_out, (dx_ref, (dt, idxs)) in enumerate(zip(dx_refs, dx_outs)):
            off = 0
            for j in idxs:
                w = xs[j][1]
                g = grads[j]
                if res_ref is not None and n_out == 0 and off == 0:
                    g = g + res_ref[...].astype(_F32)
                dx_ref[:, off:off + w] = g.astype(dt)
                off += w

    dp_shapes = [jax.ShapeDtypeStruct(a.shape, _F32) for a, _ in params]
    dx_shapes = [jax.ShapeDtypeStruct((rows, sum(xs[j][1] for j in idxs)), dt) for dt, idxs in dx_outs]
    in_specs = ([_param_spec(a, kind, n_lat) for a, kind in params]
                + [pl.BlockSpec((_TM, w), lambda i, cb=cb: (i, cb)) for _, w, cb in xs]
                + [pl.BlockSpec((_TM, sum(ws)), lambda i: (i, 0)) for _, ws in douts])
    operands = [a for a, _ in params] + [a for a, _, _ in xs] + [a for a, _ in douts]
    if nres:
        in_specs.append(pl.BlockSpec((_TM, residual.shape[1]), lambda i: (i, 0)))
        operands.append(residual)
    res = pl.pallas_call(
        body, name=name, grid=(nt,),
        out_shape=dp_shapes + dx_shapes,
        in_specs=in_specs,
        out_specs=[_param_spec(a, kind, n_lat) for a, kind in params]
        + [pl.BlockSpec((_TM, s.shape[1]), lambda i: (i, 0)) for s in dx_shapes],
        compiler_params=_params(("arbitrary",)),
    )(*operands)
    return list(res[:np_]), list(res[np_:])


def _f_norm_mod(ps, xs):
    gain, shift, scale = ps
    (x,) = xs
    y = x * lax.rsqrt(jnp.mean(x * x, axis=-1, keepdims=True) + _EPS) * gain
    return [y * (1.0 + scale) + shift]


def _f_gate(ps, xs):
    return [ps[0] * xs[0]]


def _f_residual(ps, xs):
    return [xs[0] + ps[0] * xs[1]]


def _f_premix(ps, xs):
    g_b, g_c, u_a, v_b, gate_b = xs
    return [g_b, g_c * u_a, v_b * jax.nn.sigmoid(gate_b)]


def _f_postmix(ps, xs):
    bias, ln_g, ln_b = ps
    g_b, cv_a, cv_b = xs
    u = cv_b + bias
    mu = jnp.mean(u, axis=-1, keepdims=True)
    var = jnp.mean(jnp.square(u - mu), axis=-1, keepdims=True)
    y = (u - mu) * lax.rsqrt(var + _EPS) * ln_g + ln_b
    return [g_b * cv_a, y * jax.nn.sigmoid(y)]


def _f_glu(ps, xs):
    a, g = xs
    return [g * jax.nn.sigmoid(g) * a]


def _conv_halo_specs(width, cb0, n_rows):
    per = _TM // _HALO
    last = n_rows // _HALO - 1
    return [
        pl.BlockSpec((_TM, width), lambda i, j: (i, cb0 + j)),
        pl.BlockSpec((_HALO, width), lambda i, j: (jnp.maximum(i * per - 1, 0), cb0 + j)),
        pl.BlockSpec((_HALO, width), lambda i, j: (jnp.minimum((i + 1) * per, last), cb0 + j)),
    ]


def _fill_ext(ext, main_ref, prev_ref, next_ref, i, n_lat, nt):
    has_prev = (i != 0) & (i != n_lat)
    has_next = (i != n_lat - 1) & (i != nt - 1)
    ext[0:_HALO, :] = jnp.where(has_prev, prev_ref[...].astype(_F32), 0.0)
    ext[_HALO:_HALO + _TM, :] = main_ref[...].astype(_F32)
    ext[_HALO + _TM:, :] = jnp.where(has_next, next_ref[...].astype(_F32), 0.0)


def _dwconv(name, x, cb0, channels, taps, out_dtype, n_lat):
    rows = x.shape[0]
    ktaps = taps.shape[0]
    half = ktaps // 2
    width = _divisor(channels, 1536, 128)
    assert (cb0 * channels) % width == 0
    cb0 = cb0 * channels // width
    nt = rows // _TM

    def body(main_ref, prev_ref, next_ref, taps_ref, o_ref, ext):
        _fill_ext(ext, main_ref, prev_ref, next_ref, pl.program_id(0), n_lat, nt)

        def chunk(j, carry):
            cols = pl.ds(pl.multiple_of(j * _LANES, _LANES), _LANES)
            for r0 in range(0, _TM, _CONV_ROWS):
                base = _HALO - half + r0
                acc = taps_ref[0:1, cols] * ext[base:base + _CONV_ROWS, cols]
                for k in range(1, ktaps):
                    acc = acc + taps_ref[k:k + 1, cols] * ext[base + k:base + k + _CONV_ROWS, cols]
                o_ref[r0:r0 + _CONV_ROWS, cols] = acc.astype(out_dtype)
            return carry

        lax.fori_loop(0, width // _LANES, chunk, 0)

    return pl.pallas_call(
        body, name=name, grid=(nt, channels // width),
        out_shape=jax.ShapeDtypeStruct((rows, channels), out_dtype),
        in_specs=_conv_halo_specs(width, cb0, rows) + [pl.BlockSpec((ktaps, width), lambda i, j: (0, j))],
        out_specs=pl.BlockSpec((_TM, width), lambda i, j: (i, j)),
        scratch_shapes=[pltpu.VMEM((_TM + 2 * _HALO, width), _F32)],
        compiler_params=_params(("arbitrary", "arbitrary")),
    )(x, x, x, taps)


def _dwconv_wgrad(name, dy, x, cb0, channels, ktaps, n_lat):
    rows = x.shape[0]
    half = ktaps // 2
    width = _divisor(channels, 1536, 128)
    cb0 = cb0 * channels // width
    nt = rows // _TM

    def body(dy_ref, main_ref, prev_ref, next_ref, o_ref, ext):
        i = pl.program_id(1)
        _fill_ext(ext, main_ref, prev_ref, next_ref, i, n_lat, nt)

        @pl.when(i == 0)
        def _():
            o_ref[...] = jnp.zeros_like(o_ref)

        def chunk(j, carry):
            cols = pl.ds(pl.multiple_of(j * _LANES, _LANES), _LANES)
            for r0 in range(0, _TM, _CONV_ROWS):
                dyv = dy_ref[r0:r0 + _CONV_ROWS, cols].astype(_F32)
                base = _HALO - half + r0
                for k in range(ktaps):
                    o_ref[k:k + 1, cols] += jnp.sum(dyv * ext[base + k:base + k + _CONV_ROWS, cols], axis=0,
                                                    keepdims=True)
            return carry

        lax.fori_loop(0, width // _LANES, chunk, 0)

    per = _TM // _HALO
    last = rows // _HALO - 1
    return pl.pallas_call(
        body, name=name, grid=(channels // width, nt),
        out_shape=jax.ShapeDtypeStruct((ktaps, channels), _F32),
        in_specs=[
            pl.BlockSpec((_TM, width), lambda j, i: (i, j)),
            pl.BlockSpec((_TM, width), lambda j, i: (i, cb0 + j)),
            pl.BlockSpec((_HALO, width), lambda j, i: (jnp.maximum(i * per - 1, 0), cb0 + j)),
            pl.BlockSpec((_HALO, width), lambda j, i: (jnp.minimum((i + 1) * per, last), cb0 + j)),
        ],
        out_specs=pl.BlockSpec((ktaps, width), lambda j, i: (0, j)),
        scratch_shapes=[pltpu.VMEM((_TM + 2 * _HALO, width), _F32)],
        compiler_params=_params(("arbitrary", "arbitrary")),
    )(dy, x, x, x)


def _rope_tables(length, ctx_len):
    t = jnp.arange(length)
    row = (t // _GRID_W).astype(_F32)
    col = (t % _GRID_W).astype(_F32)
    n_freq = _HEAD_DIM // 4
    inv_freq = _ROPE_THETA ** (-jnp.arange(n_freq, dtype=_F32) / n_freq)
    ang = jnp.concatenate([row[:, None] * inv_freq, col[:, None] * inv_freq], axis=-1)
    cos, sin = jnp.cos(ang), jnp.sin(ang)
    cos = jnp.concatenate([cos, jnp.ones((ctx_len, _HEAD_DIM // 2), _F32)], axis=0)
    sin = jnp.concatenate([sin, jnp.zeros((ctx_len, _HEAD_DIM // 2), _F32)], axis=0)
    return jnp.tile(cos, (1, 4)), jnp.tile(jnp.concatenate([-sin, sin], axis=-1), (1, 2))


def _rotate(v, cos_ref, sin_ref):
    width = v.shape[1]
    reps = width // 128
    cos = jnp.tile(cos_ref[...], (1, reps))
    sin = jnp.tile(sin_ref[...], (1, reps))
    return v * cos, sin, width


def _partner(v):
    width = v.shape[1]
    half = _HEAD_DIM // 2
    lane = lax.broadcasted_iota(jnp.int32, v.shape, 1)
    return jnp.where(lane % _HEAD_DIM < half, pltpu.roll(v, width - half, 1), pltpu.roll(v, half, 1))


def _rope_fwd(name, p, cos, sin, q_w, kv_w):
    rows, width = p.shape
    scale = _HEAD_DIM ** -0.5

    def body(p_ref, cos_ref, sin_ref, q_ref, k_ref):
        v = p_ref[:, :q_w + kv_w].astype(_F32)
        vc, s, _ = _rotate(v, cos_ref, sin_ref)
        y = vc + _partner(v) * s
        q_ref[...] = (y[:, :q_w] * scale).astype(q_ref.dtype)
        k_ref[...] = y[:, q_w:].astype(k_ref.dtype)

    return pl.pallas_call(
        body, name=name, grid=(rows // _TM,),
        out_shape=[jax.ShapeDtypeStruct((rows, q_w), _ACT), jax.ShapeDtypeStruct((rows, kv_w), _ACT)],
        in_specs=[pl.BlockSpec((_TM, width), lambda i: (i, 0)), pl.BlockSpec((_TM, 128), lambda i: (i, 0)),
                  pl.BlockSpec((_TM, 128), lambda i: (i, 0))],
        out_specs=[pl.BlockSpec((_TM, q_w), lambda i: (i, 0)), pl.BlockSpec((_TM, kv_w), lambda i: (i, 0))],
        compiler_params=_params(("arbitrary",)),
    )(p, cos, sin)


def _rope_bwd(name, dq, dk, dv, cos, sin):
    rows, q_w = dq.shape
    kv_w = dk.shape[1]
    scale = _HEAD_DIM ** -0.5

    def body(dq_ref, dk_ref, dv_ref, cos_ref, sin_ref, o_ref):
        dy = jnp.concatenate([dq_ref[...].astype(_F32) * scale, dk_ref[...].astype(_F32)], axis=1)
        dyc, s, _ = _rotate(dy, cos_ref, sin_ref)
        o_ref[:, :q_w + kv_w] = (dyc + _partner(dy * s)).astype(o_ref.dtype)
        o_ref[:, q_w + kv_w:] = dv_ref[...].astype(o_ref.dtype)

    return pl.pallas_call(
        body, name=name, grid=(rows // _TM,),
        out_shape=jax.ShapeDtypeStruct((rows, q_w + 2 * kv_w), _ACT),
        in_specs=[pl.BlockSpec((_TM, q_w), lambda i: (i, 0)), pl.BlockSpec((_TM, kv_w), lambda i: (i, 0)),
                  pl.BlockSpec((_TM, kv_w), lambda i: (i, 0)), pl.BlockSpec((_TM, 128), lambda i: (i, 0)),
                  pl.BlockSpec((_TM, 128), lambda i: (i, 0))],
        out_specs=pl.BlockSpec((_TM, q_w + 2 * kv_w), lambda i: (i, 0)),
        compiler_params=_params(("arbitrary",)),
    )(dq, dk, dv, cos, sin)


def _attn_window(i, n_lat, length):
    wk = _TM + 2 * _WINDOW
    start = pl.multiple_of(jnp.clip(i * _TM - _WINDOW, 0, length - wk), _WINDOW)
    q_pos = i * _TM + lax.broadcasted_iota(jnp.int32, (_TM, wk), 0)
    k_pos = start + lax.broadcasted_iota(jnp.int32, (_TM, wk), 1)
    mask = (jnp.abs(q_pos - k_pos) <= _WINDOW) & (i < n_lat)
    return start, wk, mask


def _softmax_parts(q, k_loc, k_ctx, mask, sink):
    nt = (((1,), (1,)), ((), ()))
    s_loc = jnp.where(mask, lax.dot_general(q, k_loc, nt, preferred_element_type=_F32), _NEG_INF)
    s_ctx = lax.dot_general(q, k_ctx, nt, preferred_element_type=_F32)
    m = jnp.maximum(jnp.maximum(jnp.max(s_loc, axis=-1, keepdims=True), jnp.max(s_ctx, axis=-1, keepdims=True)),
                    sink)
    e_loc = jnp.exp(s_loc - m)
    e_ctx = jnp.exp(s_ctx - m)
    e_sink = jnp.exp(sink - m)
    inv = 1.0 / (jnp.sum(e_loc, axis=-1, keepdims=True) + jnp.sum(e_ctx, axis=-1, keepdims=True) + e_sink)
    return e_loc * inv, e_ctx * inv, e_sink * inv


def _attn_fwd(name, q, k, p, sinks, n_lat, length, kv_w):
    rows, q_w = q.shape
    ctx_len = rows - length
    n_heads = q_w // _HEAD_DIM
    n_kv = kv_w // _HEAD_DIM
    group = n_heads // n_kv
    v_cb = p.shape[1] // kv_w - 1
    hd = _HEAD_DIM

    def body(q_ref, k_ref, v_ref, sink_ref, o_ref):
        i = pl.program_id(0)
        start, wk, mask = _attn_window(i, n_lat, length)
        for h in range(n_kv):
            k_loc = k_ref[pl.ds(start, wk), h * hd:(h + 1) * hd]
            v_loc = v_ref[pl.ds(start, wk), h * hd:(h + 1) * hd]
            k_ctx = k_ref[length:length + ctx_len, h * hd:(h + 1) * hd]
            v_ctx = v_ref[length:length + ctx_len, h * hd:(h + 1) * hd]
            for g in range(group):
                n = h * group + g
                p_loc, p_ctx, _ = _softmax_parts(q_ref[:, n * hd:(n + 1) * hd], k_loc, k_ctx, mask,
                                                 sink_ref[:, n:n + 1])
                o = (jnp.dot(p_loc.astype(_MM), v_loc, preferred_element_type=_F32)
                     + jnp.dot(p_ctx.astype(_MM), v_ctx, preferred_element_type=_F32))
                o_ref[:, n * hd:(n + 1) * hd] = o.astype(o_ref.dtype)

    return pl.pallas_call(
        body, name=name, grid=(rows // _TM,),
        out_shape=jax.ShapeDtypeStruct((rows, q_w), _ACT),
        in_specs=[pl.BlockSpec((_TM, q_w), lambda i: (i, 0)), pl.BlockSpec((rows, kv_w), lambda i: (0, 0)),
                  pl.BlockSpec((rows, kv_w), lambda i: (0, v_cb)), pl.BlockSpec((1, n_heads), lambda i: (0, 0))],
        out_specs=pl.BlockSpec((_TM, q_w), lambda i: (i, 0)),
        compiler_params=_params(("arbitrary",)),
    )(q, k, p, sinks)


def _attn_bwd(name, q, k, p, sinks, do, n_lat, length, kv_w):
    rows, q_w = q.shape
    ctx_len = rows - length
    n_heads = q_w // _HEAD_DIM
    n_kv = kv_w // _HEAD_DIM
    group = n_heads // n_kv
    v_cb = p.shape[1] // kv_w - 1
    hd = _HEAD_DIM
    nt_dims = (((1,), (1,)), ((), ()))
    tn_dims = (((0,), (0,)), ((), ()))

    def body(q_ref, k_ref, v_ref, sink_ref, do_ref, dq_ref, dk_out, dv_out, ds_ref, dk_ref, dv_ref, out_sems):
        i = pl.program_id(0)

        @pl.when(i == 0)
        def _():
            dk_ref[...] = jnp.zeros_like(dk_ref)
            dv_ref[...] = jnp.zeros_like(dv_ref)
            ds_ref[...] = jnp.zeros_like(ds_ref)

        start, wk, mask = _attn_window(i, n_lat, length)
        head_lane = lax.broadcasted_iota(jnp.int32, (1, n_heads), 1)
        dsink = jnp.zeros((1, n_heads), _F32)
        for h in range(n_kv):
            cols = slice(h * hd, (h + 1) * hd)
            k_loc = k_ref[pl.ds(start, wk), cols]
            v_loc = v_ref[pl.ds(start, wk), cols]
            k_ctx = k_ref[length:length + ctx_len, cols]
            v_ctx = v_ref[length:length + ctx_len, cols]
            dk_loc = jnp.zeros((wk, hd), _F32)
            dv_loc = jnp.zeros((wk, hd), _F32)
            dk_ctx = jnp.zeros((ctx_len, hd), _F32)
            dv_ctx = jnp.zeros((ctx_len, hd), _F32)
            for g in range(group):
                n = h * group + g
                qh = q_ref[:, n * hd:(n + 1) * hd]
                doh = do_ref[:, n * hd:(n + 1) * hd].astype(_MM)
                p_loc, p_ctx, p_sink = _softmax_parts(qh, k_loc, k_ctx, mask, sink_ref[:, n:n + 1])
                dp_loc = lax.dot_general(doh, v_loc, nt_dims, preferred_element_type=_F32)
                dp_ctx = lax.dot_general(doh, v_ctx, nt_dims, preferred_element_type=_F32)
                dsum = (jnp.sum(p_loc * dp_loc, axis=-1, keepdims=True)
                        + jnp.sum(p_ctx * dp_ctx, axis=-1, keepdims=True))
                ds_loc = (p_loc * (dp_loc - dsum)).astype(_MM)
                ds_ctx = (p_ctx * (dp_ctx - dsum)).astype(_MM)
                dsink = dsink + jnp.where(head_lane == n, -jnp.sum(p_sink * dsum), 0.0)
                dq = (jnp.dot(ds_loc, k_loc, preferred_element_type=_F32)
                      + jnp.dot(ds_ctx, k_ctx, preferred_element_type=_F32))
                dq_ref[:, n * hd:(n + 1) * hd] = dq.astype(dq_ref.dtype)
                dk_loc += lax.dot_general(ds_loc, qh, tn_dims, preferred_element_type=_F32)
                dk_ctx += lax.dot_general(ds_ctx, qh, tn_dims, preferred_element_type=_F32)
                dv_loc += lax.dot_general(p_loc.astype(_MM), doh, tn_dims, preferred_element_type=_F32)
                dv_ctx += lax.dot_general(p_ctx.astype(_MM), doh, tn_dims, preferred_element_type=_F32)
            dk_ref[pl.ds(start, wk), cols] += dk_loc
            dv_ref[pl.ds(start, wk), cols] += dv_loc
            dk_ref[length:length + ctx_len, cols] += dk_ctx
            dv_ref[length:length + ctx_len, cols] += dv_ctx
        ds_ref[...] += dsink

        @pl.when(i == rows // _TM - 1)
        def _():
            copies = [pltpu.make_async_copy(dk_ref, dk_out, out_sems.at[0]),
                      pltpu.make_async_copy(dv_ref, dv_out, out_sems.at[1])]
            for cp in copies:
                cp.start()
            for cp in copies:
                cp.wait()

    return pl.pallas_call(
        body, name=name, grid=(rows // _TM,),
        out_shape=[jax.ShapeDtypeStruct((rows, q_w), _ACT), jax.ShapeDtypeStruct((rows, kv_w), _F32),
                   jax.ShapeDtypeStruct((rows, kv_w), _F32), jax.ShapeDtypeStruct((1, n_heads), _F32)],
        in_specs=[pl.BlockSpec((_TM, q_w), lambda i: (i, 0)), pl.BlockSpec((rows, kv_w), lambda i: (0, 0)),
                  pl.BlockSpec((rows, kv_w), lambda i: (0, v_cb)), pl.BlockSpec((1, n_heads), lambda i: (0, 0)),
                  pl.BlockSpec((_TM, q_w), lambda i: (i, 0))],
        out_specs=[pl.BlockSpec((_TM, q_w), lambda i: (i, 0)), pl.BlockSpec(memory_space=pl.ANY),
                   pl.BlockSpec(memory_space=pl.ANY), pl.BlockSpec((1, n_heads), lambda i: (0, 0))],
        scratch_shapes=[pltpu.VMEM((rows, kv_w), _F32), pltpu.VMEM((rows, kv_w), _F32),
                        pltpu.SemaphoreType.DMA((2,))],
        compiler_params=_params(("arbitrary",)),
    )(q, k, p, sinks, do)


def _loss_head(name, xs, gain, target, n_lat):
    rows, d = xs.shape

    def body(x_ref, g_ref, t_ref, loss_ref, dg_ref, dx_ref):
        i = pl.program_id(0)

        @pl.when(i == 0)
        def _():
            loss_ref[...] = jnp.zeros_like(loss_ref)
            dg_ref[...] = jnp.zeros_like(dg_ref)

        @pl.when(i < n_lat)
        def _():
            tv = t_ref[...]

            def f(gain_, x):
                y = x * lax.rsqrt(jnp.mean(x * x, axis=-1, keepdims=True) + _EPS) * gain_
                return 0.5 * jnp.sum(jnp.mean(jnp.square(y - tv), axis=-1))

            val, (dg, dx) = jax.value_and_grad(f, argnums=(0, 1))(g_ref[...], x_ref[...])
            loss_ref[...] += val
            dg_ref[...] += dg
            dx_ref[...] = dx

        @pl.when(i >= n_lat)
        def _():
            dx_ref[...] = jnp.zeros_like(dx_ref)

    return pl.pallas_call(
        body, name=name, grid=(rows // _TM,),
        out_shape=[jax.ShapeDtypeStruct((1, 128), _F32), jax.ShapeDtypeStruct((1, d), _F32),
                   jax.ShapeDtypeStruct((rows, d), _F32)],
        in_specs=[pl.BlockSpec((_TM, d), lambda i: (i, 0)), pl.BlockSpec((1, d), lambda i: (0, 0)),
                  pl.BlockSpec((_TM, d), lambda i: (jnp.minimum(i, n_lat - 1), 0))],
        out_specs=[pl.BlockSpec((1, 128), lambda i: (0, 0)), pl.BlockSpec((1, d), lambda i: (0, 0)),
                   pl.BlockSpec((_TM, d), lambda i: (i, 0))],
        compiler_params=_params(("arbitrary",)),
    )(xs, gain, target)


def _adamw(name, w, g, m, v):
    rows, cols = w.shape
    tr = _divisor(rows, 512, 8)
    b1, b2 = _ADAM["b1"], _ADAM["b2"]
    c1 = 1.0 - b1 ** _ADAM["step"]
    c2 = 1.0 - b2 ** _ADAM["step"]

    def body(w_ref, g_ref, m_ref, v_ref, d_ref, nm_ref, nv_ref):
        gv = g_ref[...]
        nm = b1 * m_ref[...] + (1.0 - b1) * gv
        nv = b2 * v_ref[...] + (1.0 - b2) * jnp.square(gv)
        d_ref[...] = -_ADAM["lr"] * ((nm / c1) / (jnp.sqrt(nv / c2) + _ADAM["eps"]) + _ADAM["wd"] * w_ref[...])
        nm_ref[...] = nm
        nv_ref[...] = nv

    spec = pl.BlockSpec((tr, cols), lambda i: (i, 0))
    return pl.pallas_call(
        body, name=name, grid=(rows // tr,),
        out_shape=[jax.ShapeDtypeStruct((rows, cols), _F32)] * 3,
        in_specs=[spec] * 4, out_specs=[spec] * 3,
        compiler_params=_params(("arbitrary",)),
    )(w, g, m, v)


def _pack(arrays, cols=128):
    flat = jnp.concatenate([a.reshape(-1).astype(_F32) for a in arrays])
    pad = (-flat.shape[0]) % (64 * cols)
    return jnp.pad(flat, (0, pad)).reshape(-1, cols)


def _unpack(flat, shapes):
    out, off = [], 0
    for s in shapes:
        n = 1
        for d in s:
            n *= d
        out.append(flat[..., off:off + n].reshape(flat.shape[:-1] + tuple(s)))
        off += n
    return out


def _gather_channels(parts):
    moved = jnp.moveaxis(parts, 0, -2)
    return moved.reshape(moved.shape[:-2] + (moved.shape[-2] * moved.shape[-1],))


def kernel(x, c, ctx, c_ctx, w_mod, b_mod, norm_mix, norm_ffn, w_in_ab, conv_a, conv_b, conv_b_bias, ln_b_gain, ln_b_bias, w_out_ab, w_qkv, w_o, sinks, w_up, w_conv_ffn, w_down, final_norm, loss_target, m_c_ctx, m_w_mod, m_b_mod, m_norm_mix, m_norm_ffn, m_w_in_ab, m_conv_a, m_conv_b, m_conv_b_bias, m_ln_b_gain, m_ln_b_bias, m_w_out_ab, m_w_qkv, m_w_o, m_sinks, m_w_up, m_w_conv_ffn, m_w_down, m_final_norm, v_c_ctx, v_w_mod, v_b_mod, v_norm_mix, v_norm_ffn, v_w_in_ab, v_conv_a, v_conv_b, v_conv_b_bias, v_ln_b_gain, v_ln_b_bias, v_w_out_ab, v_w_qkv, v_w_o, v_sinks, v_w_up, v_w_conv_ffn, v_w_down, v_final_norm):
    args = dict(locals())
    weight_names = ["c_ctx", "w_mod", "b_mod", "norm_mix", "norm_ffn", "w_in_ab", "conv_a", "conv_b", "conv_b_bias",
                    "ln_b_gain", "ln_b_bias", "w_out_ab", "w_qkv", "w_o", "sinks", "w_up", "w_conv_ffn", "w_down",
                    "final_norm"]
    length, d = x.shape[1], x.shape[2]
    ctx_len = ctx.shape[1]
    assert ctx_len == _TM and length % _TM == 0 and x.shape[0] == 1
    n_lat = length // _TM
    depth = w_mod.shape[0]
    n_even, n_odd = w_in_ab.shape[0], w_qkv.shape[0]
    a_w = conv_a.shape[2] * _NDEV
    b_w = conv_b.shape[2] * _NDEV
    assert a_w == b_w
    q_w = w_o.shape[1] * _NDEV
    kv_w = (w_qkv.shape[2] * _NDEV - q_w) // 2
    d_ff = w_down.shape[1] * _NDEV
    dev = 4 * lax.axis_index("x") + 2 * lax.axis_index("y") + lax.axis_index("c")

    small_shapes = [c.shape[1:], conv_a.shape, conv_b.shape, w_conv_ffn.shape]
    g0 = _all_gather_small("gather_small_params", _pack([c, conv_a, conv_b, w_conv_ffn]))
    c_parts, ca_parts, cb_parts, cf_parts = _unpack(g0.reshape(_NDEV, -1), small_shapes)
    conv_a_full = _gather_channels(ca_parts)
    conv_b_full = _gather_channels(cb_parts)
    conv_f_full = _gather_channels(cf_parts)

    cond = jnp.concatenate([c_parts, c_ctx[None], jnp.zeros((16 - _NDEV - 1, d), _F32)], axis=0)
    mod_cols = w_mod.shape[2]
    m_shard = _mod_forward("mod_forward", cond, w_mod)
    m_all = _all_gather_small("gather_mod", m_shard.reshape(depth * 16, mod_cols))
    m_all = jnp.moveaxis(m_all.reshape(_NDEV, depth, 16, mod_cols), 0, 2).reshape(depth, 16, _NDEV * mod_cols)
    m_all = m_all + b_mod[:, None, :]
    m_lat = lax.dynamic_index_in_dim(m_all, dev, axis=1, keepdims=False)
    m_ctx = m_all[:, _NDEV]

    def mod_vec(l, j):
        return jnp.stack([m_lat[l, j * d:(j + 1) * d], m_ctx[l, j * d:(j + 1) * d]])[:, None, :]

    def layer_mats(l):
        if l % 2 == 0:
            first = [("in", l // 2, w_in_ab[l // 2].T), ("out", l // 2, w_out_ab[l // 2])]
        else:
            first = [("qkv", l // 2, w_qkv[l // 2].T), ("o", l // 2, w_o[l // 2])]
        return first + [("up", l, w_up[l].T), ("down", l, w_down[l])]

    slab_off, slab_r, layer_keys, layer_rows, slabs = {}, {}, [], [], []
    for l in range(depth):
        off, keys = 0, []
        for fam, idx, mat in layer_mats(l):
            slab_off[fam, idx], slab_r[fam] = off, mat.shape[0]
            off += mat.shape[0]
            keys.append((fam, idx))
        layer_keys.append(keys)
        layer_rows.append(off)
        slabs.append(jnp.concatenate([mat.astype(_MM) for _, _, mat in layer_mats(l)], axis=0))
    wgs = [_all_gather_slab("gather_weights_0", slabs[0])] + [None] * (depth - 1)
    gathers, start_token = [None] * depth, jnp.zeros((), _F32)
    for l in range(1, depth):
        land = _fill_own_slot(f"gather_fill_{l}", [slabs[l]], [0], layer_rows[l])
        gathers[l] = _exchange_start(f"gather_start_{l}", [slabs[l]], land, [0])
        start_token = start_token + gathers[l][-1][0, 0]

    cos, sin = _rope_tables(length, ctx_len)
    xs = jnp.concatenate([x[0], ctx[0]], axis=0)

    def full(a):
        return (a.reshape(1, -1), "full")

    saved = []
    for l in range(depth):
        sv = {"x_in": xs}
        if l > 0:
            wgs[l] = _exchange_wait(f"gather_wait_{l}", gathers[l], [0], xs)
        wg = wgs[l]
        gain1 = full(norm_mix[l] + start_token) if l == 0 else full(norm_mix[l])
        (h1,) = _rowfn(f"norm_mix_{l}", _f_norm_mod, [gain1, (mod_vec(l, 0), "stream"), (mod_vec(l, 1), "stream")],
                       [(xs, d, 0)], [(_ACT, [d])], n_lat)
        sv["h1"] = h1
        if l % 2 == 0:
            e = l // 2
            p = _mm_nt(f"proj_in_{l}", h1, wg, slab_off["in", e], slab_r["in"], _ACT)
            (qm,) = _rowfn(f"premix_{l}", _f_premix, [], [(p, a_w, j) for j in range(5)],
                           [(None, [a_w]), (_ACT, [a_w, b_w])], n_lat)
            cv_a = _dwconv(f"conv_a_{l}", qm, 0, a_w, conv_a_full[e], _ACT, n_lat)
            cv_b = _dwconv(f"conv_b_{l}", qm, 1, b_w, conv_b_full[e], _ACT, n_lat)
            post_params = [full(conv_b_bias[e]), full(ln_b_gain[e]), full(ln_b_bias[e])]
            (z,) = _rowfn(f"postmix_{l}", _f_postmix, post_params, [(p, a_w, 0), (cv_a, a_w, 0), (cv_b, b_w, 0)],
                          [(_ACT, [a_w, b_w])], n_lat)
            y1 = _mm_nn(f"proj_out_{l}", z, wg, slab_off["out", e], slab_r["out"], _ACT)
            sv.update(p=p, qm=qm, cv_a=cv_a, cv_b=cv_b, z=z)
        else:
            o = l // 2
            p = _mm_nt(f"proj_qkv_{l}", h1, wg, slab_off["qkv", o], slab_r["qkv"], _ACT)
            qr, kr = _rope_fwd(f"rope_{l}", p, cos, sin, q_w, kv_w)
            sk = sinks[o].reshape(1, -1)
            z = _attn_fwd(f"attn_{l}", qr, kr, p, sk, n_lat, length, kv_w)
            y1 = _mm_nn(f"proj_o_{l}", z, wg, slab_off["o", o], slab_r["o"], _ACT)
            sv.update(p=p, qr=qr, kr=kr, z=z)
        (xs,) = _rowfn(f"residual_mix_{l}", _f_residual, [(mod_vec(l, 2), "stream")], [(xs, d, 0), (y1, d, 0)],
                       [(_F32, [d])], n_lat)
        sv.update(y1=y1, x_mid=xs)
        (h2,) = _rowfn(f"norm_ffn_{l}", _f_norm_mod,
                       [full(norm_ffn[l]), (mod_vec(l, 3), "stream"), (mod_vec(l, 4), "stream")],
                       [(xs, d, 0)], [(_ACT, [d])], n_lat)
        pu = _mm_nt(f"proj_up_{l}", h2, wg, slab_off["up", l], slab_r["up"], _ACT)
        u = _dwconv(f"conv_ffn_{l}", pu, 0, 2 * d_ff, conv_f_full[l], _ACT, n_lat)
        (f,) = _rowfn(f"glu_{l}", _f_glu, [], [(u, d_ff, 0), (u, d_ff, 1)], [(_ACT, [d_ff])], n_lat)
        y2 = _mm_nn(f"proj_down_{l}", f, wg, slab_off["down", l], slab_r["down"], _ACT)
        (xs,) = _rowfn(f"residual_ffn_{l}", _f_residual, [(mod_vec(l, 5), "stream")], [(xs, d, 0), (y2, d, 0)],
                       [(_F32, [d])], n_lat)
        sv.update(h2=h2, pu=pu, u=u, f=f, y2=y2)
        saved.append(sv)

    loss_part, d_final_norm, dxs = _loss_head("loss_head", xs, final_norm.reshape(1, -1), loss_target[0], n_lat)
    loss = lax.psum(loss_part[0, 0], ("x", "y", "c"))

    wgrads = {}
    d_mod = [[None] * 6 for _ in range(depth)]
    d_norm_mix, d_norm_ffn = [None] * depth, [None] * depth
    d_conv_a, d_conv_b = [None] * n_even, [None] * n_even
    d_bias, d_ln_g, d_ln_b = [None] * n_even, [None] * n_even, [None] * n_even
    d_sinks = [None] * n_odd
    d_conv_f = [None] * depth
    exchanges, recvs, exchange_token = [None] * depth, [None] * depth, jnp.zeros((), _F32)
    for l in reversed(range(depth)):
        sv = saved[l]
        wg = wgs[l]
        (dg,), (dy2,) = _rowfn_bwd(f"residual_ffn_bwd_{l}", _f_gate, [(mod_vec(l, 5) + exchange_token, "stream")],
                                   [(sv["y2"], d, 0)], [0], [(dxs, [d])], [(_ACT, [0])], n_lat)
        d_mod[l][5] = dg
        wgrads["down", l] = _mm_tn(f"wgrad_down_{l}", sv["f"], dy2, _ACT)
        df = _mm_nt(f"bwd_down_{l}", dy2, wg, slab_off["down", l], slab_r["down"], _ACT)
        _, (du,) = _rowfn_bwd(f"glu_bwd_{l}", _f_glu, [], [(sv["u"], d_ff, 0), (sv["u"], d_ff, 1)], [0, 1],
                              [(df, [d_ff])], [(_ACT, [0, 1])], n_lat)
        d_conv_f[l] = _dwconv_wgrad(f"conv_ffn_wgrad_{l}", du, sv["pu"], 0, 2 * d_ff, 3, n_lat)
        dpu = _dwconv(f"conv_ffn_bwd_{l}", du, 0, 2 * d_ff, conv_f_full[l][::-1], _ACT, n_lat)
        wgrads["up", l] = _mm_tn(f"wgrad_up_{l}", dpu, sv["h2"], _ACT)
        dh2 = _mm_nn(f"bwd_up_{l}", dpu, wg, slab_off["up", l], slab_r["up"], _ACT)
        (dgain, dsh, dsc), (dxs,) = _rowfn_bwd(
            f"norm_ffn_bwd_{l}", _f_norm_mod,
            [full(norm_ffn[l]), (mod_vec(l, 3), "stream"), (mod_vec(l, 4), "stream")], [(sv["x_mid"], d, 0)], [0],
            [(dh2, [d])], [(_F32, [0])], n_lat, residual=dxs)
        d_norm_ffn[l], d_mod[l][3], d_mod[l][4] = dgain, dsh, dsc
        (dg,), (dy1,) = _rowfn_bwd(f"residual_mix_bwd_{l}", _f_gate, [(mod_vec(l, 2), "stream")], [(sv["y1"], d, 0)],
                                   [0], [(dxs, [d])], [(_ACT, [0])], n_lat)
        d_mod[l][2] = dg
        if l % 2 == 0:
            e = l // 2
            wgrads["out", e] = _mm_tn(f"wgrad_out_{l}", sv["z"], dy1, _ACT)
            dz = _mm_nt(f"bwd_out_{l}", dy1, wg, slab_off["out", e], slab_r["out"], _ACT)
            post_params = [full(conv_b_bias[e]), full(ln_b_gain[e]), full(ln_b_bias[e])]
            (dbias, dlg, dlb), (dgb, dcv_a, dcv_b) = _rowfn_bwd(
                f"postmix_bwd_{l}", _f_postmix, post_params,
                [(sv["p"], a_w, 0), (sv["cv_a"], a_w, 0), (sv["cv_b"], b_w, 0)], [0, 1, 2], [(dz, [a_w, b_w])],
                [(_ACT, [0]), (_ACT, [1]), (_ACT, [2])], n_lat)
            d_bias[e], d_ln_g[e], d_ln_b[e] = dbias, dlg, dlb
            d_conv_a[e] = _dwconv_wgrad(f"conv_a_wgrad_{l}", dcv_a, sv["qm"], 0, a_w, conv_a_full.shape[1], n_lat)
            d_conv_b[e] = _dwconv_wgrad(f"conv_b_wgrad_{l}", dcv_b, sv["qm"], 1, b_w, conv_b_full.shape[1], n_lat)
            dq_a = _dwconv(f"conv_a_bwd_{l}", dcv_a, 0, a_w, conv_a_full[e][::-1], _ACT, n_lat)
            dq_b = _dwconv(f"conv_b_bwd_{l}", dcv_b, 0, b_w, conv_b_full[e][::-1], _ACT, n_lat)
            _, (dp,) = _rowfn_bwd(f"premix_bwd_{l}", _f_premix, [], [(sv["p"], a_w, j) for j in range(5)],
                                  [0, 1, 2, 3, 4], [(dgb, [a_w]), (dq_a, [a_w]), (dq_b, [b_w])],
                                  [(_ACT, [0, 1, 2, 3, 4])], n_lat)
            wgrads["in", e] = _mm_tn(f"wgrad_in_{l}", dp, sv["h1"], _ACT)
            dh1 = _mm_nn(f"bwd_in_{l}", dp, wg, slab_off["in", e], slab_r["in"], _ACT)
        else:
            o = l // 2
            wgrads["o", o] = _mm_tn(f"wgrad_o_{l}", sv["z"], dy1, _ACT)
            dz = _mm_nt(f"bwd_o_{l}", dy1, wg, slab_off["o", o], slab_r["o"], _ACT)
            sk = sinks[o].reshape(1, -1)
            dqr, dkr, dv, dsk = _attn_bwd(f"attn_bwd_{l}", sv["qr"], sv["kr"], sv["p"], sk, dz, n_lat, length, kv_w)
            d_sinks[o] = dsk
            dp = _rope_bwd(f"rope_bwd_{l}", dqr, dkr, dv, cos, sin)
            wgrads["qkv", o] = _mm_tn(f"wgrad_qkv_{l}", dp, sv["h1"], _ACT)
            dh1 = _mm_nn(f"bwd_qkv_{l}", dp, wg, slab_off["qkv", o], slab_r["qkv"], _ACT)
        (dgain, dsh, dsc), (dxs,) = _rowfn_bwd(
            f"norm_mix_bwd_{l}", _f_norm_mod,
            [full(norm_mix[l]), (mod_vec(l, 0), "stream"), (mod_vec(l, 1), "stream")], [(sv["x_in"], d, 0)], [0],
            [(dh1, [d])], [(_F32, [0])], n_lat, residual=dxs)
        d_norm_mix[l], d_mod[l][0], d_mod[l][1] = dgain, dsh, dsc
        parts = [wgrads[key].reshape(_NDEV, slab_r[key[0]], d) for key in layer_keys[l]]
        offsets = [slab_off[key] for key in layer_keys[l]]
        if l > 0:
            land = _fill_own_slot(f"exchange_fill_{l}", parts, offsets, layer_rows[l])
            exchanges[l] = _exchange_start(f"exchange_start_{l}", parts, land, offsets)
            exchange_token = exchanges[l][-1][0, 0]
        else:
            recvs[0] = _exchange_grads("exchange_weight_grads_0", parts, offsets, layer_rows[0])
    grad_x = dxs[:length][None]

    gsums = []
    for l in range(depth):
        if l > 0:
            recvs[l] = _exchange_wait(f"exchange_wait_{l}", exchanges[l], [slab_off[key] for key in layer_keys[l]], dxs)
        gsums.append(_sum_slots(f"sum_weight_grads_{l}", recvs[l]))
    layer_of = {key: l for l in range(depth) for key in layer_keys[l]}

    def slab_grad(fam, count, transposed):
        mats = [gsums[layer_of[fam, i]][slab_off[fam, i]:slab_off[fam, i] + slab_r[fam]] for i in range(count)]
        return jnp.stack([m_.T if transposed else m_ for m_ in mats])

    grads = {
        "w_in_ab": slab_grad("in", n_even, True), "w_qkv": slab_grad("qkv", n_odd, True),
        "w_up": slab_grad("up", depth, True), "w_out_ab": slab_grad("out", n_even, False),
        "w_o": slab_grad("o", n_odd, False), "w_down": slab_grad("down", depth, False),
    }

    dm_dev = jnp.stack([jnp.concatenate([d_mod[l][j][:, 0, :] for j in range(6)], axis=-1)
                        for l in range(depth)])
    small_grads = [dm_dev, jnp.stack(d_norm_mix), jnp.stack(d_norm_ffn), jnp.stack(d_conv_a), jnp.stack(d_conv_b),
                   jnp.stack(d_bias), jnp.stack(d_ln_g), jnp.stack(d_ln_b), jnp.stack(d_sinks), jnp.stack(d_conv_f),
                   d_final_norm]
    sg_shapes = [a.shape for a in small_grads]
    sg_all = _all_gather_small("gather_small_grads", _pack(small_grads))
    sg_sum = _sum_slots("sum_small_grads", sg_all)
    (dm_sum, g_norm_mix, g_norm_ffn, g_conv_a, g_conv_b, g_bias, g_ln_g, g_ln_b, g_sinks, g_conv_f,
     g_final_norm) = _unpack(sg_sum.reshape(-1), sg_shapes)
    dm_each = _unpack(sg_all.reshape(_NDEV, -1), sg_shapes[:1])[0]

    def my_channels(a):
        width = a.shape[-1] // _NDEV
        return lax.dynamic_slice_in_dim(a, dev * width, width, axis=a.ndim - 1)

    grads["b_mod"] = dm_sum[:, 0] + dm_sum[:, 1]
    grads["norm_mix"] = g_norm_mix.reshape(depth, d)
    grads["norm_ffn"] = g_norm_ffn.reshape(depth, d)
    grads["conv_a"] = my_channels(g_conv_a)
    grads["conv_b"] = my_channels(g_conv_b)
    grads["conv_b_bias"] = g_bias.reshape(n_even, b_w)
    grads["ln_b_gain"] = g_ln_g.reshape(n_even, b_w)
    grads["ln_b_bias"] = g_ln_b.reshape(n_even, b_w)
    grads["sinks"] = g_sinks.reshape(n_odd, -1)
    grads["w_conv_ffn"] = my_channels(g_conv_f)
    grads["final_norm"] = g_final_norm.reshape(d)

    dm_rows = jnp.concatenate([jnp.moveaxis(dm_each[:, :, 0], 0, 1), dm_sum[:, 1:2],
                               jnp.zeros((depth, 16 - _NDEV - 1, 6 * d), _F32)], axis=1)
    dm_mine = my_channels(dm_rows)
    grads["w_mod"] = jnp.stack([_mm_tn(f"wgrad_mod_{l}", cond, dm_mine[l], _F32, silu_a=True) for l in range(depth)])
    dcond = _mod_backward_cond("mod_backward_cond", dm_mine, w_mod)
    dcond_all = _all_gather_small("gather_dcond", dcond)
    dcond_sum = _sum_slots("sum_dcond", dcond_all)[_NDEV]
    sg = jax.nn.sigmoid(c_ctx)
    grads["c_ctx"] = dcond_sum * (sg * (1.0 + c_ctx * (1.0 - sg)))

    big = ["w_mod", "w_in_ab", "w_out_ab", "w_qkv", "w_o", "w_up", "w_down"]
    small = [n for n in weight_names if n not in big]
    delta, new_m, new_v = {}, {}, {}
    for n in big:
        w = args[n]
        two_d = lambda a: a.reshape(-1, w.shape[-1])
        dl, nm, nv = _adamw(f"adamw_{n}", two_d(w), two_d(grads[n]), two_d(args["m_" + n]), two_d(args["v_" + n]))
        delta[n], new_m[n], new_v[n] = dl.reshape(w.shape), nm.reshape(w.shape), nv.reshape(w.shape)
    shapes = [args[n].shape for n in small]
    grads = {n: grads[n].reshape(args[n].shape) for n in grads}
    dl, nm, nv = _adamw("adamw_small", _pack([args[n] for n in small]), _pack([grads[n] for n in small]),
                        _pack([args["m_" + n] for n in small]), _pack([args["v_" + n] for n in small]))
    for res, packed in ((delta, dl), (new_m, nm), (new_v, nv)):
        for n, a in zip(small, _unpack(packed.reshape(-1), shapes)):
            res[n] = a

    return (loss, grad_x, *[grads[n] for n in weight_names], *[delta[n] for n in weight_names],
            *[new_m[n] for n in weight_names], *[new_v[n] for n in weight_names])
```

```python
import functools

import jax
import jax.numpy as jnp
from jax import lax
from jax.experimental import pallas as pl
from jax.experimental.pallas import tpu as pltpu

_F32 = jnp.float32
_MM = jnp.bfloat16
_ACT = jnp.bfloat16
_TM = 256
_HALO = 16
_LANES = 128
_CONV_ROWS = 128
_MM_ROWS = 1024
_NDEV = 8
_HEAD_DIM = 64
_WINDOW = 128
_GRID_W = 64
_ROPE_THETA = 10000.0
_EPS = 1e-6
_NEG_INF = -1e30
_VMEM_LIMIT = 56 * 1024 * 1024
_ADAM = dict(lr=0.001, b1=0.9, b2=0.999, eps=1e-08, wd=0.01, step=10)
_MESH = pl.DeviceIdType.MESH


def _params(sem=None):
    return pltpu.CompilerParams(dimension_semantics=sem, vmem_limit_bytes=_VMEM_LIMIT)


def _divisor(n, cap, mult):
    if n <= cap:
        return n
    for d in range(cap - cap % mult, 0, -mult):
        if n % d == 0:
            return d
    raise ValueError(f"no tile for {n}")


def _my_coords():
    return lax.axis_index("x"), lax.axis_index("y"), lax.axis_index("c")


def _peer(k):
    x, y, c = _my_coords()
    px = 1 - x if k & 4 else x
    py = 1 - y if k & 2 else y
    pc = 1 - c if k & 1 else c
    return (px, py, pc), 4 * px + 2 * py + pc


def _all_gather_small(name, v):
    rows, cols = v.shape

    def body(v_ref, out_ref, send_sems, recv_sems):
        x, y, c = _my_coords()
        me = 4 * x + 2 * y + c
        out_ref[me] = v_ref[...]
        sends = []
        for k in range(1, _NDEV):
            peer, _ = _peer(k)
            cp = pltpu.make_async_remote_copy(
                src_ref=v_ref, dst_ref=out_ref.at[me], send_sem=send_sems.at[k - 1], recv_sem=recv_sems.at[k - 1],
                device_id=peer, device_id_type=_MESH)
            cp.start()
            sends.append(cp)
        for k in range(1, _NDEV):
            peer, pid = _peer(k)
            pltpu.make_async_remote_copy(
                src_ref=v_ref, dst_ref=out_ref.at[pid], send_sem=send_sems.at[k - 1], recv_sem=recv_sems.at[k - 1],
                device_id=peer, device_id_type=_MESH).wait_recv()
        for cp in sends:
            cp.wait_send()

    return pl.pallas_call(
        body, name=name,
        out_shape=jax.ShapeDtypeStruct((_NDEV, rows, cols), v.dtype),
        in_specs=[pl.BlockSpec(memory_space=pltpu.VMEM)],
        out_specs=pl.BlockSpec(memory_space=pltpu.VMEM),
        scratch_shapes=[pltpu.SemaphoreType.DMA((_NDEV - 1,)), pltpu.SemaphoreType.DMA((_NDEV - 1,))],
        compiler_params=pltpu.CompilerParams(vmem_limit_bytes=_VMEM_LIMIT),
    )(v)


def _all_gather_slab(name, slab):
    rows, cols = slab.shape

    def body(s_ref, out_ref, send_sems, recv_sems, local_sem):
        x, y, c = _my_coords()
        me = 4 * x + 2 * y + c
        local = pltpu.make_async_copy(s_ref, out_ref.at[me], local_sem)
        local.start()
        sends = []
        for k in range(1, _NDEV):
            peer, _ = _peer(k)
            cp = pltpu.make_async_remote_copy(
                src_ref=s_ref, dst_ref=out_ref.at[me], send_sem=send_sems.at[k - 1], recv_sem=recv_sems.at[k - 1],
                device_id=peer, device_id_type=_MESH)
            cp.start()
            sends.append(cp)
        for k in range(1, _NDEV):
            peer, pid = _peer(k)
            pltpu.make_async_remote_copy(
                src_ref=s_ref, dst_ref=out_ref.at[pid], send_sem=send_sems.at[k - 1], recv_sem=recv_sems.at[k - 1],
                device_id=peer, device_id_type=_MESH).wait_recv()
        for cp in sends:
            cp.wait_send()
        local.wait()

    return pl.pallas_call(
        body, name=name,
        out_shape=jax.ShapeDtypeStruct((_NDEV, rows, cols), slab.dtype),
        in_specs=[pl.BlockSpec(memory_space=pl.ANY)],
        out_specs=pl.BlockSpec(memory_space=pl.ANY),
        scratch_shapes=[pltpu.SemaphoreType.DMA((_NDEV - 1,)), pltpu.SemaphoreType.DMA((_NDEV - 1,)),
                        pltpu.SemaphoreType.DMA],
        compiler_params=pltpu.CompilerParams(vmem_limit_bytes=_VMEM_LIMIT),
    )(slab)


def _exchange_grads(name, parts, offsets, total_rows):
    n = len(parts)
    cols = parts[0].shape[2]
    dtype = parts[0].dtype

    def body(*refs):
        g_refs, out_ref = refs[:n], refs[n]
        send_sems, recv_sems, local_sems = refs[n + 1:]
        x, y, c = _my_coords()
        me = 4 * x + 2 * y + c

        def dst(slot, m):
            return out_ref.at[slot, pl.ds(offsets[m], parts[m].shape[1]), :]

        locals_ = []
        for m in range(n):
            cp = pltpu.make_async_copy(g_refs[m].at[me], dst(me, m), local_sems.at[m])
            cp.start()
            locals_.append(cp)
        sends = []
        for k in range(1, _NDEV):
            peer, pid = _peer(k)
            for m in range(n):
                cp = pltpu.make_async_remote_copy(
                    src_ref=g_refs[m].at[pid], dst_ref=dst(me, m), send_sem=send_sems.at[k - 1, m],
                    recv_sem=recv_sems.at[k - 1, m], device_id=peer, device_id_type=_MESH)
                cp.start()
                sends.append(cp)
        for k in range(1, _NDEV):
            peer, pid = _peer(k)
            for m in range(n):
                pltpu.make_async_remote_copy(
                    src_ref=g_refs[m].at[pid], dst_ref=dst(pid, m), send_sem=send_sems.at[k - 1, m],
                    recv_sem=recv_sems.at[k - 1, m], device_id=peer, device_id_type=_MESH).wait_recv()
        for cp in sends:
            cp.wait_send()
        for cp in locals_:
            cp.wait()

    return pl.pallas_call(
        body, name=name,
        out_shape=jax.ShapeDtypeStruct((_NDEV, total_rows, cols), dtype),
        in_specs=[pl.BlockSpec(memory_space=pl.ANY)] * n,
        out_specs=pl.BlockSpec(memory_space=pl.ANY),
        scratch_shapes=[pltpu.SemaphoreType.DMA((_NDEV - 1, n)), pltpu.SemaphoreType.DMA((_NDEV - 1, n)),
                        pltpu.SemaphoreType.DMA((n,))],
        compiler_params=pltpu.CompilerParams(vmem_limit_bytes=_VMEM_LIMIT),
    )(*parts)


def _sum_slots(name, v):
    _, rows, cols = v.shape
    tr = _divisor(rows, 512, 16)

    def body(v_ref, o_ref):
        acc = v_ref[0].astype(_F32)
        for e in range(1, _NDEV):
            acc = acc + v_ref[e].astype(_F32)
        o_ref[...] = acc

    return pl.pallas_call(
        body, name=name, grid=(rows // tr,),
        out_shape=jax.ShapeDtypeStruct((rows, cols), _F32),
        in_specs=[pl.BlockSpec((_NDEV, tr, cols), lambda i: (0, i, 0))],
        out_specs=pl.BlockSpec((tr, cols), lambda i: (i, 0)),
        compiler_params=_params(("arbitrary",)),
    )(v)


_HBM_SPEC = pl.BlockSpec(memory_space=pltpu.HBM)
_SEM_SPEC = pl.BlockSpec(memory_space=pltpu.SEMAPHORE)
_EFFECT = pltpu.SideEffectType.DATAFLOW_SIDE_EFFECTING


def _in_hbm(a):
    return pltpu.with_memory_space_constraint(a, pltpu.HBM)


def _block_for(ref, device):
    return ref if len(ref.shape) == 2 else ref.at[device]


def _fill_own_slot(name, srcs, offsets, total_rows):
    n = len(srcs)
    cols = srcs[0].shape[-1]

    def body(*refs):
        src_refs, out_ref, bufs, sems = refs[:n], refs[n], refs[n + 1:2 * n + 1], refs[2 * n + 1]
        x, y, c = _my_coords()
        me = 4 * x + 2 * y + c
        loads = [pltpu.make_async_copy(_block_for(src_refs[m], me), bufs[m], sems.at[0, m]) for m in range(n)]
        stores = [pltpu.make_async_copy(bufs[m], out_ref.at[me, pl.ds(offsets[m], srcs[m].shape[-2]), :],
                                        sems.at[1, m]) for m in range(n)]
        for copies in (loads, stores):
            for cp in copies:
                cp.start()
            for cp in copies:
                cp.wait()

    return pl.pallas_call(
        body, name=name,
        out_shape=jax.ShapeDtypeStruct((_NDEV, total_rows, cols), srcs[0].dtype),
        in_specs=[pl.BlockSpec(memory_space=pl.ANY)] * n,
        out_specs=pl.BlockSpec(memory_space=pl.ANY),
        scratch_shapes=[pltpu.VMEM(s.shape[-2:], s.dtype) for s in srcs] + [pltpu.SemaphoreType.DMA((2, n))],
        compiler_params=pltpu.CompilerParams(vmem_limit_bytes=_VMEM_LIMIT),
    )(*srcs)


def _exchange_start(name, srcs, land, offsets):
    n = len(srcs)

    def body(*refs):
        src_refs, land_ref = refs[:n], refs[n]
        send_sems, recv_sems, token = refs[n + 1], refs[n + 2], refs[-1]
        x, y, c = _my_coords()
        me = 4 * x + 2 * y + c
        for k in range(1, _NDEV):
            peer, pid = _peer(k)
            for m in range(n):
                pltpu.make_async_remote_copy(
                    src_ref=_block_for(src_refs[m], pid),
                    dst_ref=land_ref.at[me, pl.ds(offsets[m], srcs[m].shape[-2]), :],
                    send_sem=send_sems, recv_sem=recv_sems, device_id=peer, device_id_type=_MESH).start()
        token[...] = jnp.zeros_like(token)

    sems = pltpu.SemaphoreType.DMA(())
    return pl.pallas_call(
        body, name=name,
        out_shape=(sems, sems, *[pltpu.HBM(s.shape, s.dtype) for s in srcs], pltpu.HBM(land.shape, land.dtype),
                   jax.ShapeDtypeStruct((8, 128), _F32)),
        in_specs=[_HBM_SPEC] * (n + 1),
        out_specs=(_SEM_SPEC, _SEM_SPEC, *[_HBM_SPEC] * (n + 1), pl.BlockSpec(memory_space=pltpu.VMEM)),
        input_output_aliases={i: 2 + i for i in range(n + 1)},
        compiler_params=pltpu.CompilerParams(has_side_effects=_EFFECT),
    )(*[_in_hbm(s) for s in srcs], _in_hbm(land))


def _exchange_wait(name, started, offsets, after):
    send_sems, recv_sems = started[0], started[1]
    srcs, land = list(started[2:-2]), started[-2]
    n = len(srcs)

    def body(*refs):
        src_refs, land_ref = refs[:n], refs[n]
        send_sems_, recv_sems_ = refs[n + 1], refs[n + 2]
        others = land_ref.at[pl.ds(0, _NDEV - 1)]
        cp = pltpu.make_async_remote_copy(src_ref=others, dst_ref=others, send_sem=send_sems_, recv_sem=recv_sems_,
                                          device_id=_peer(1)[0], device_id_type=_MESH)
        cp.wait_send()
        cp.wait_recv()

    res = pl.pallas_call(
        body, name=name,
        out_shape=(*[pltpu.HBM(s.shape, s.dtype) for s in srcs], pltpu.HBM(land.shape, land.dtype)),
        in_specs=[_HBM_SPEC] * (n + 1) + [_SEM_SPEC, _SEM_SPEC, pl.BlockSpec(memory_space=pl.ANY)],
        out_specs=tuple([_HBM_SPEC] * (n + 1)),
        input_output_aliases={i: i for i in range(n + 1)},
        compiler_params=pltpu.CompilerParams(has_side_effects=_EFFECT),
    )(*srcs, land, send_sems, recv_sems, after)
    return res[n]


def _load_weight(wg_ref, wbuf, sems, off, r, step):
    @pl.when(step == 0)
    def _():
        copies = [pltpu.make_async_copy(wg_ref.at[e, pl.ds(off, r), :], wbuf.at[pl.ds(e * r, r), :], sems.at[e])
                  for e in range(_NDEV)]
        for cp in copies:
            cp.start()
        for cp in copies:
            cp.wait()


def _mm_nt(name, a, wg, off, r, out_dtype):
    rows, kdim = a.shape
    n = _NDEV * r
    chunk = _divisor(n, 512, 128)

    def body(a_ref, wg_ref, o_ref, wbuf, sems):
        _load_weight(wg_ref, wbuf, sems, off, r, pl.program_id(0))
        av = a_ref[...].astype(_MM)
        for j in range(n // chunk):
            o_ref[:, j * chunk:(j + 1) * chunk] = lax.dot_general(
                av, wbuf[j * chunk:(j + 1) * chunk, :], (((1,), (1,)), ((), ())),
                preferred_element_type=_F32).astype(out_dtype)

    tm = _divisor(rows, _MM_ROWS, _TM)
    return pl.pallas_call(
        body, name=name, grid=(rows // tm,),
        out_shape=jax.ShapeDtypeStruct((rows, n), out_dtype),
        in_specs=[pl.BlockSpec((tm, kdim), lambda i: (i, 0)), pl.BlockSpec(memory_space=pl.ANY)],
        out_specs=pl.BlockSpec((tm, n), lambda i: (i, 0)),
        scratch_shapes=[pltpu.VMEM((n, kdim), wg.dtype), pltpu.SemaphoreType.DMA((_NDEV,))],
        compiler_params=_params(("arbitrary",)),
    )(a, wg)


def _mm_nn(name, a, wg, off, r, out_dtype):
    rows, kdim = a.shape
    assert kdim == _NDEV * r
    n = wg.shape[2]

    def body(a_ref, wg_ref, o_ref, wbuf, sems):
        _load_weight(wg_ref, wbuf, sems, off, r, pl.program_id(0))
        o_ref[...] = jnp.dot(a_ref[...].astype(_MM), wbuf[...], preferred_element_type=_F32).astype(out_dtype)

    tm = _divisor(rows, _MM_ROWS, _TM)
    return pl.pallas_call(
        body, name=name, grid=(rows // tm,),
        out_shape=jax.ShapeDtypeStruct((rows, n), out_dtype),
        in_specs=[pl.BlockSpec((tm, kdim), lambda i: (i, 0)), pl.BlockSpec(memory_space=pl.ANY)],
        out_specs=pl.BlockSpec((tm, n), lambda i: (i, 0)),
        scratch_shapes=[pltpu.VMEM((kdim, n), wg.dtype), pltpu.SemaphoreType.DMA((_NDEV,))],
        compiler_params=_params(("arbitrary",)),
    )(a, wg)


def _mm_tn(name, a, b, out_dtype, silu_a=False):
    rows, na = a.shape
    nb = b.shape[1]
    tr = _divisor(rows, 1536, 16)
    tn = _divisor(na, 512, 128)
    steps = rows // tr

    def body(a_ref, b_ref, o_ref, acc):
        t = pl.program_id(1)

        @pl.when(t == 0)
        def _():
            acc[...] = jnp.zeros_like(acc)

        av = a_ref[...]
        if silu_a:
            av = av.astype(_F32)
            av = av * jax.nn.sigmoid(av)
        acc[...] += lax.dot_general(av.astype(_MM), b_ref[...].astype(_MM), (((0,), (0,)), ((), ())),
                                    preferred_element_type=_F32)

        @pl.when(t == steps - 1)
        def _():
            o_ref[...] = acc[...].astype(out_dtype)

    return pl.pallas_call(
        body, name=name, grid=(na // tn, steps),
        out_shape=jax.ShapeDtypeStruct((na, nb), out_dtype),
        in_specs=[pl.BlockSpec((tr, tn), lambda j, t: (t, j)), pl.BlockSpec((tr, nb), lambda j, t: (t, 0))],
        out_specs=pl.BlockSpec((tn, nb), lambda j, t: (j, 0)),
        scratch_shapes=[pltpu.VMEM((tn, nb), _F32)],
        compiler_params=_params(("arbitrary", "arbitrary")),
    )(a, b)


def _mod_forward(name, cond, w_mod):
    depth, d, n = w_mod.shape
    rows = cond.shape[0]

    def body(c_ref, w_ref, o_ref):
        cv = c_ref[...]
        a = (cv * jax.nn.sigmoid(cv)).astype(_MM)
        o_ref[...] = jnp.dot(a, w_ref[...].astype(_MM), preferred_element_type=_F32)

    return pl.pallas_call(
        body, name=name, grid=(depth,),
        out_shape=jax.ShapeDtypeStruct((depth, rows, n), _F32),
        in_specs=[pl.BlockSpec((rows, d), lambda l: (0, 0)), pl.BlockSpec((None, d, n), lambda l: (l, 0, 0))],
        out_specs=pl.BlockSpec((None, rows, n), lambda l: (l, 0, 0)),
        compiler_params=_params(("arbitrary",)),
    )(cond, w_mod)


def _mod_backward_cond(name, dm, w_mod):
    depth, d, n = w_mod.shape
    rows = dm.shape[1]

    def body(g_ref, w_ref, o_ref):
        @pl.when(pl.program_id(0) == 0)
        def _():
            o_ref[...] = jnp.zeros_like(o_ref)

        o_ref[...] += lax.dot_general(g_ref[...].astype(_MM), w_ref[...].astype(_MM), (((1,), (1,)), ((), ())),
                                      preferred_element_type=_F32)

    return pl.pallas_call(
        body, name=name, grid=(depth,),
        out_shape=jax.ShapeDtypeStruct((rows, d), _F32),
        in_specs=[pl.BlockSpec((None, rows, n), lambda l: (l, 0, 0)), pl.BlockSpec((None, d, n), lambda l: (l, 0, 0))],
        out_specs=pl.BlockSpec((rows, d), lambda l: (0, 0)),
        compiler_params=_params(("arbitrary",)),
    )(dm, w_mod)


def _param_spec(arr, kind, n_lat):
    if kind == "stream":
        return pl.BlockSpec((None,) + arr.shape[1:], lambda i: (i // n_lat, 0, 0))
    return pl.BlockSpec(arr.shape, lambda i: (0,) * arr.ndim)


def _rowfn(name, fn, params, xs, outs, n_lat):
    rows = xs[0][0].shape[0]
    np_, nx = len(params), len(xs)
    stored = [(dt, ws) for dt, ws in outs if dt is not None]

    def body(*refs):
        ps = [r[...].astype(_F32) for r in refs[:np_]]
        xv = [r[...].astype(_F32) for r in refs[np_:np_ + nx]]
        pieces = fn(ps, xv)
        o_refs = iter(refs[np_ + nx:])
        k = 0
        for dt, ws in outs:
            o_ref = next(o_refs) if dt is not None else None
            off = 0
            for w in ws:
                if o_ref is not None:
                    o_ref[:, off:off + w] = pieces[k].astype(dt)
                off += w
                k += 1

    return pl.pallas_call(
        body, name=name, grid=(rows // _TM,),
        out_shape=[jax.ShapeDtypeStruct((rows, sum(ws)), dt) for dt, ws in stored],
        in_specs=[_param_spec(a, kind, n_lat) for a, kind in params]
        + [pl.BlockSpec((_TM, w), lambda i, cb=cb: (i, cb)) for _, w, cb in xs],
        out_specs=[pl.BlockSpec((_TM, sum(ws)), lambda i: (i, 0)) for _, ws in stored],
        compiler_params=_params(("arbitrary",)),
    )(*[a for a, _ in params], *[a for a, _, _ in xs])


def _rowfn_bwd(name, fn, params, xs, diff, douts, dx_outs, n_lat, residual=None):
    rows = xs[0][0].shape[0]
    np_, nx, nd = len(params), len(xs), len(douts)
    nres = 0 if residual is None else 1
    nt = rows // _TM

    def body(*refs):
        i = pl.program_id(0)
        ps = [r[...].astype(_F32) for r in refs[:np_]]
        xv = [r[...].astype(_F32) for r in refs[np_:np_ + nx]]
        d_refs = refs[np_ + nx:np_ + nx + nd]
        res_ref = refs[np_ + nx + nd] if nres else None
        dp_refs = refs[np_ + nx + nd + nres:np_ + nx + nd + nres + np_]
        dx_refs = refs[np_ + nx + nd + nres + np_:]

        def f(ps_, xd):
            full = list(xv)
            for j, v in zip(diff, xd):
                full[j] = v
            return fn(ps_, full)

        _, vjp = jax.vjp(f, ps, [xv[j] for j in diff])
        cts = []
        for d_ref, (_, ws) in zip(d_refs, douts):
            off = 0
            for w in ws:
                cts.append(d_ref[:, off:off + w].astype(_F32))
                off += w
        dps, dxd = vjp(cts)
        grads = dict(zip(diff, dxd))
        for (dp_ref, (_, kind)), dp in zip(zip(dp_refs, params), dps):
            first = (i == 0) | (i == n_lat) if kind == "stream" else i == 0

            @pl.when(first)
            def _(dp_ref=dp_ref):
                dp_ref[...] = jnp.zeros_like(dp_ref)

            dp_ref[...] += dp
        for n_out, (dx_ref, (dt, idxs)) in enumerate(zip(dx_refs, dx_outs)):
            off = 0
            for j in idxs:
                w = xs[j][1]
                g = grads[j]
                if res_ref is not None and n_out == 0 and off == 0:
                    g = g + res_ref[...].astype(_F32)
                dx_ref[:, off:off + w] = g.astype(dt)
                off += w

    dp_shapes = [jax.ShapeDtypeStruct(a.shape, _F32) for a, _ in params]
    dx_shapes = [jax.ShapeDtypeStruct((rows, sum(xs[j][1] for j in idxs)), dt) for dt, idxs in dx_outs]
    in_specs = ([_param_spec(a, kind, n_lat) for a, kind in params]
                + [pl.BlockSpec((_TM, w), lambda i, cb=cb: (i, cb)) for _, w, cb in xs]
                + [pl.BlockSpec((_TM, sum(ws)), lambda i: (i, 0)) for _, ws in douts])
    operands = [a for a, _ in params] + [a for a, _, _ in xs] + [a for a, _ in douts]
    if nres:
        in_specs.append(pl.BlockSpec((_TM, residual.shape[1]), lambda i: (i, 0)))
        operands.append(residual)
    res = pl.pallas_call(
        body, name=name, grid=(nt,),
        out_shape=dp_shapes + dx_shapes,
        in_specs=in_specs,
        out_specs=[_param_spec(a, kind, n_lat) for a, kind in params]
        + [pl.BlockSpec((_TM, s.shape[1]), lambda i: (i, 0)) for s in dx_shapes],
        compiler_params=_params(("arbitrary",)),
    )(*operands)
    return list(res[:np_]), list(res[np_:])


def _f_norm_mod(ps, xs):
    gain, shift, scale = ps
    (x,) = xs
    y = x * lax.rsqrt(jnp.mean(x * x, axis=-1, keepdims=True) + _EPS) * gain
    return [y * (1.0 + scale) + shift]


def _f_gate(ps, xs):
    return [ps[0] * xs[0]]


def _f_residual(ps, xs):
    return [xs[0] + ps[0] * xs[1]]


def _f_premix(ps, xs):
    g_b, g_c, u_a, v_b, gate_b = xs
    return [g_b, g_c * u_a, v_b * jax.nn.sigmoid(gate_b)]


def _f_postmix(ps, xs):
    bias, ln_g, ln_b = ps
    g_b, cv_a, cv_b = xs
    u = cv_b + bias
    mu = jnp.mean(u, axis=-1, keepdims=True)
    var = jnp.mean(jnp.square(u - mu), axis=-1, keepdims=True)
    y = (u - mu) * lax.rsqrt(var + _EPS) * ln_g + ln_b
    return [g_b * cv_a, y * jax.nn.sigmoid(y)]


def _f_glu(ps, xs):
    a, g = xs
    return [g * jax.nn.sigmoid(g) * a]


def _conv_halo_specs(width, cb0, n_rows):
    per = _TM // _HALO
    last = n_rows // _HALO - 1
    return [
        pl.BlockSpec((_TM, width), lambda i, j: (i, cb0 + j)),
        pl.BlockSpec((_HALO, width), lambda i, j: (jnp.maximum(i * per - 1, 0), cb0 + j)),
        pl.BlockSpec((_HALO, width), lambda i, j: (jnp.minimum((i + 1) * per, last), cb0 + j)),
    ]


def _conv_window(main_ref, prev_ref, next_ref, r0, cols, i, n_lat, nt):
    if r0 == 0:
        has_prev = (i != 0) & (i != n_lat)
        head = jnp.where(has_prev, prev_ref[:, cols].astype(_F32), 0.0)
    else:
        head = main_ref[r0 - _HALO:r0, cols].astype(_F32)
    if r0 + _CONV_ROWS == _TM:
        has_next = (i != n_lat - 1) & (i != nt - 1)
        tail = jnp.where(has_next, next_ref[:, cols].astype(_F32), 0.0)
    else:
        tail = main_ref[r0 + _CONV_ROWS:r0 + _CONV_ROWS + _HALO, cols].astype(_F32)
    return jnp.concatenate([head, main_ref[r0:r0 + _CONV_ROWS, cols].astype(_F32), tail], axis=0)


def _shifted(win, offset):
    n = win.shape[0]
    rolled = win if offset == 0 else pltpu.roll(win, (-offset) % n, 0)
    return rolled[_HALO:_HALO + _CONV_ROWS]


def _dwconv(name, x, cb0, channels, taps, out_dtype, n_lat):
    rows = x.shape[0]
    ktaps = taps.shape[0]
    half = ktaps // 2
    width = _divisor(channels, 1536, 128)
    assert (cb0 * channels) % width == 0
    cb0 = cb0 * channels // width
    nt = rows // _TM

    def body(main_ref, prev_ref, next_ref, taps_ref, o_ref):
        i = pl.program_id(0)

        def chunk(j, carry):
            cols = pl.ds(pl.multiple_of(j * _LANES, _LANES), _LANES)
            for r0 in range(0, _TM, _CONV_ROWS):
                win = _conv_window(main_ref, prev_ref, next_ref, r0, cols, i, n_lat, nt)
                acc = taps_ref[0:1, cols] * _shifted(win, -half)
                for k in range(1, ktaps):
                    acc = acc + taps_ref[k:k + 1, cols] * _shifted(win, k - half)
                o_ref[r0:r0 + _CONV_ROWS, cols] = acc.astype(out_dtype)
            return carry

        lax.fori_loop(0, width // _LANES, chunk, 0)

    return pl.pallas_call(
        body, name=name, grid=(nt, channels // width),
        out_shape=jax.ShapeDtypeStruct((rows, channels), out_dtype),
        in_specs=_conv_halo_specs(width, cb0, rows) + [pl.BlockSpec((ktaps, width), lambda i, j: (0, j))],
        out_specs=pl.BlockSpec((_TM, width), lambda i, j: (i, j)),
        compiler_params=_params(("arbitrary", "arbitrary")),
    )(x, x, x, taps)


def _dwconv_wgrad(name, dy, x, cb0, channels, ktaps, n_lat):
    rows = x.shape[0]
    half = ktaps // 2
    width = _divisor(channels, 1536, 128)
    cb0 = cb0 * channels // width
    nt = rows // _TM

    def body(dy_ref, main_ref, prev_ref, next_ref, o_ref):
        i = pl.program_id(1)

        @pl.when(i == 0)
        def _():
            o_ref[...] = jnp.zeros_like(o_ref)

        def chunk(j, carry):
            cols = pl.ds(pl.multiple_of(j * _LANES, _LANES), _LANES)
            for r0 in range(0, _TM, _CONV_ROWS):
                dyv = dy_ref[r0:r0 + _CONV_ROWS, cols].astype(_F32)
                win = _conv_window(main_ref, prev_ref, next_ref, r0, cols, i, n_lat, nt)
                for k in range(ktaps):
                    o_ref[k:k + 1, cols] += jnp.sum(dyv * _shifted(win, k - half), axis=0, keepdims=True)
            return carry

        lax.fori_loop(0, width // _LANES, chunk, 0)

    per = _TM // _HALO
    last = rows // _HALO - 1
    return pl.pallas_call(
        body, name=name, grid=(channels // width, nt),
        out_shape=jax.ShapeDtypeStruct((ktaps, channels), _F32),
        in_specs=[
            pl.BlockSpec((_TM, width), lambda j, i: (i, j)),
            pl.BlockSpec((_TM, width), lambda j, i: (i, cb0 + j)),
            pl.BlockSpec((_HALO, width), lambda j, i: (jnp.maximum(i * per - 1, 0), cb0 + j)),
            pl.BlockSpec((_HALO, width), lambda j, i: (jnp.minimum((i + 1) * per, last), cb0 + j)),
        ],
        out_specs=pl.BlockSpec((ktaps, width), lambda j, i: (0, j)),
        compiler_params=_params(("arbitrary", "arbitrary")),
    )(dy, x, x, x)


def _rope_tables(length, ctx_len):
    t = jnp.arange(length)
    row = (t // _GRID_W).astype(_F32)
    col = (t % _GRID_W).astype(_F32)
    n_freq = _HEAD_DIM // 4
    inv_freq = _ROPE_THETA ** (-jnp.arange(n_freq, dtype=_F32) / n_freq)
    ang = jnp.concatenate([row[:, None] * inv_freq, col[:, None] * inv_freq], axis=-1)
    cos, sin = jnp.cos(ang), jnp.sin(ang)
    cos = jnp.concatenate([cos, jnp.ones((ctx_len, _HEAD_DIM // 2), _F32)], axis=0)
    sin = jnp.concatenate([sin, jnp.zeros((ctx_len, _HEAD_DIM // 2), _F32)], axis=0)
    return jnp.tile(cos, (1, 4)), jnp.tile(jnp.concatenate([-sin, sin], axis=-1), (1, 2))


def _rotate(v, cos_ref, sin_ref):
    width = v.shape[1]
    reps = width // 128
    cos = jnp.tile(cos_ref[...], (1, reps))
    sin = jnp.tile(sin_ref[...], (1, reps))
    return v * cos, sin, width


def _partner(v):
    width = v.shape[1]
    half = _HEAD_DIM // 2
    lane = lax.broadcasted_iota(jnp.int32, v.shape, 1)
    return jnp.where(lane % _HEAD_DIM < half, pltpu.roll(v, width - half, 1), pltpu.roll(v, half, 1))


def _rope_fwd(name, p, cos, sin, q_w, kv_w):
    rows, width = p.shape
    scale = _HEAD_DIM ** -0.5

    def body(p_ref, cos_ref, sin_ref, q_ref, k_ref):
        v = p_ref[:, :q_w + kv_w].astype(_F32)
        vc, s, _ = _rotate(v, cos_ref, sin_ref)
        y = vc + _partner(v) * s
        q_ref[...] = (y[:, :q_w] * scale).astype(q_ref.dtype)
        k_ref[...] = y[:, q_w:].astype(k_ref.dtype)

    return pl.pallas_call(
        body, name=name, grid=(rows // _TM,),
        out_shape=[jax.ShapeDtypeStruct((rows, q_w), _ACT), jax.ShapeDtypeStruct((rows, kv_w), _ACT)],
        in_specs=[pl.BlockSpec((_TM, width), lambda i: (i, 0)), pl.BlockSpec((_TM, 128), lambda i: (i, 0)),
                  pl.BlockSpec((_TM, 128), lambda i: (i, 0))],
        out_specs=[pl.BlockSpec((_TM, q_w), lambda i: (i, 0)), pl.BlockSpec((_TM, kv_w), lambda i: (i, 0))],
        compiler_params=_params(("arbitrary",)),
    )(p, cos, sin)


def _rope_bwd(name, dq, dk, dv, cos, sin):
    rows, q_w = dq.shape
    kv_w = dk.shape[1]
    scale = _HEAD_DIM ** -0.5

    def body(dq_ref, dk_ref, dv_ref, cos_ref, sin_ref, o_ref):
        dy = jnp.concatenate([dq_ref[...].astype(_F32) * scale, dk_ref[...].astype(_F32)], axis=1)
        dyc, s, _ = _rotate(dy, cos_ref, sin_ref)
        o_ref[:, :q_w + kv_w] = (dyc + _partner(dy * s)).astype(o_ref.dtype)
        o_ref[:, q_w + kv_w:] = dv_ref[...].astype(o_ref.dtype)

    return pl.pallas_call(
        body, name=name, grid=(rows // _TM,),
        out_shape=jax.ShapeDtypeStruct((rows, q_w + 2 * kv_w), _ACT),
        in_specs=[pl.BlockSpec((_TM, q_w), lambda i: (i, 0)), pl.BlockSpec((_TM, kv_w), lambda i: (i, 0)),
                  pl.BlockSpec((_TM, kv_w), lambda i: (i, 0)), pl.BlockSpec((_TM, 128), lambda i: (i, 0)),
                  pl.BlockSpec((_TM, 128), lambda i: (i, 0))],
        out_specs=pl.BlockSpec((_TM, q_w + 2 * kv_w), lambda i: (i, 0)),
        compiler_params=_params(("arbitrary",)),
    )(dq, dk, dv, cos, sin)


def _attn_window(i, n_lat, length):
    wk = _TM + 2 * _WINDOW
    start = pl.multiple_of(jnp.clip(i * _TM - _WINDOW, 0, length - wk), _WINDOW)
    q_pos = i * _TM + lax.broadcasted_iota(jnp.int32, (_TM, wk), 0)
    k_pos = start + lax.broadcasted_iota(jnp.int32, (_TM, wk), 1)
    mask = (jnp.abs(q_pos - k_pos) <= _WINDOW) & (i < n_lat)
    return start, wk, mask


def _softmax_parts(q, k_loc, k_ctx, mask, sink):
    nt = (((1,), (1,)), ((), ()))
    s_loc = jnp.where(mask, lax.dot_general(q, k_loc, nt, preferred_element_type=_F32), _NEG_INF)
    s_ctx = lax.dot_general(q, k_ctx, nt, preferred_element_type=_F32)
    m = jnp.maximum(jnp.maximum(jnp.max(s_loc, axis=-1, keepdims=True), jnp.max(s_ctx, axis=-1, keepdims=True)),
                    sink)
    e_loc = jnp.exp(s_loc - m)
    e_ctx = jnp.exp(s_ctx - m)
    e_sink = jnp.exp(sink - m)
    inv = 1.0 / (jnp.sum(e_loc, axis=-1, keepdims=True) + jnp.sum(e_ctx, axis=-1, keepdims=True) + e_sink)
    return e_loc * inv, e_ctx * inv, e_sink * inv


def _attn_fwd(name, q, k, p, sinks, n_lat, length, kv_w):
    rows, q_w = q.shape
    ctx_len = rows - length
    n_heads = q_w // _HEAD_DIM
    n_kv = kv_w // _HEAD_DIM
    group = n_heads // n_kv
    v_cb = p.shape[1] // kv_w - 1
    hd = _HEAD_DIM

    def body(q_ref, k_ref, v_ref, sink_ref, o_ref):
        i = pl.program_id(0)
        start, wk, mask = _attn_window(i, n_lat, length)
        for h in range(n_kv):
            k_loc = k_ref[pl.ds(start, wk), h * hd:(h + 1) * hd]
            v_loc = v_ref[pl.ds(start, wk), h * hd:(h + 1) * hd]
            k_ctx = k_ref[length:length + ctx_len, h * hd:(h + 1) * hd]
            v_ctx = v_ref[length:length + ctx_len, h * hd:(h + 1) * hd]
            for g in range(group):
                n = h * group + g
                p_loc, p_ctx, _ = _softmax_parts(q_ref[:, n * hd:(n + 1) * hd], k_loc, k_ctx, mask,
                                                 sink_ref[:, n:n + 1])
                o = (jnp.dot(p_loc.astype(_MM), v_loc, preferred_element_type=_F32)
                     + jnp.dot(p_ctx.astype(_MM), v_ctx, preferred_element_type=_F32))
                o_ref[:, n * hd:(n + 1) * hd] = o.astype(o_ref.dtype)

    return pl.pallas_call(
        body, name=name, grid=(rows // _TM,),
        out_shape=jax.ShapeDtypeStruct((rows, q_w), _ACT),
        in_specs=[pl.BlockSpec((_TM, q_w), lambda i: (i, 0)), pl.BlockSpec((rows, kv_w), lambda i: (0, 0)),
                  pl.BlockSpec((rows, kv_w), lambda i: (0, v_cb)), pl.BlockSpec((1, n_heads), lambda i: (0, 0))],
        out_specs=pl.BlockSpec((_TM, q_w), lambda i: (i, 0)),
        compiler_params=_params(("arbitrary",)),
    )(q, k, p, sinks)


def _attn_bwd(name, q, k, p, sinks, do, n_lat, length, kv_w):
    rows, q_w = q.shape
    ctx_len = rows - length
    n_heads = q_w // _HEAD_DIM
    n_kv = kv_w // _HEAD_DIM
    group = n_heads // n_kv
    v_cb = p.shape[1] // kv_w - 1
    hd = _HEAD_DIM
    nt_dims = (((1,), (1,)), ((), ()))
    tn_dims = (((0,), (0,)), ((), ()))

    def body(q_ref, k_ref, v_ref, sink_ref, do_ref, dq_ref, dk_out, dv_out, ds_ref, dk_ref, dv_ref, out_sems):
        i = pl.program_id(0)

        @pl.when(i == 0)
        def _():
            dk_ref[...] = jnp.zeros_like(dk_ref)
            dv_ref[...] = jnp.zeros_like(dv_ref)
            ds_ref[...] = jnp.zeros_like(ds_ref)

        start, wk, mask = _attn_window(i, n_lat, length)
        head_lane = lax.broadcasted_iota(jnp.int32, (1, n_heads), 1)
        dsink = jnp.zeros((1, n_heads), _F32)
        for h in range(n_kv):
            cols = slice(h * hd, (h + 1) * hd)
            k_loc = k_ref[pl.ds(start, wk), cols]
            v_loc = v_ref[pl.ds(start, wk), cols]
            k_ctx = k_ref[length:length + ctx_len, cols]
            v_ctx = v_ref[length:length + ctx_len, cols]
            dk_loc = jnp.zeros((wk, hd), _F32)
            dv_loc = jnp.zeros((wk, hd), _F32)
            dk_ctx = jnp.zeros((ctx_len, hd), _F32)
            dv_ctx = jnp.zeros((ctx_len, hd), _F32)
            for g in range(group):
                n = h * group + g
                qh = q_ref[:, n * hd:(n + 1) * hd]
                doh = do_ref[:, n * hd:(n + 1) * hd].astype(_MM)
                p_loc, p_ctx, p_sink = _softmax_parts(qh, k_loc, k_ctx, mask, sink_ref[:, n:n + 1])
                dp_loc = lax.dot_general(doh, v_loc, nt_dims, preferred_element_type=_F32)
                dp_ctx = lax.dot_general(doh, v_ctx, nt_dims, preferred_element_type=_F32)
                dsum = (jnp.sum(p_loc * dp_loc, axis=-1, keepdims=True)
                        + jnp.sum(p_ctx * dp_ctx, axis=-1, keepdims=True))
                ds_loc = (p_loc * (dp_loc - dsum)).astype(_MM)
                ds_ctx = (p_ctx * (dp_ctx - dsum)).astype(_MM)
                dsink = dsink + jnp.where(head_lane == n, -jnp.sum(p_sink * dsum), 0.0)
                dq = (jnp.dot(ds_loc, k_loc, preferred_element_type=_F32)
                      + jnp.dot(ds_ctx, k_ctx, preferred_element_type=_F32))
                dq_ref[:, n * hd:(n + 1) * hd] = dq.astype(dq_ref.dtype)
                dk_loc += lax.dot_general(ds_loc, qh, tn_dims, preferred_element_type=_F32)
                dk_ctx += lax.dot_general(ds_ctx, qh, tn_dims, preferred_element_type=_F32)
                dv_loc += lax.dot_general(p_loc.astype(_MM), doh, tn_dims, preferred_element_type=_F32)
                dv_ctx += lax.dot_general(p_ctx.astype(_MM), doh, tn_dims, preferred_element_type=_F32)
            dk_ref[pl.ds(start, wk), cols] += dk_loc
            dv_ref[pl.ds(start, wk), cols] += dv_loc
            dk_ref[length:length + ctx_len, cols] += dk_ctx
            dv_ref[length:length + ctx_len, cols] += dv_ctx
        ds_ref[...] += dsink

        @pl.when(i == rows // _TM - 1)
        def _():
            copies = [pltpu.make_async_copy(dk_ref, dk_out, out_sems.at[0]),
                      pltpu.make_async_copy(dv_ref, dv_out, out_sems.at[1])]
            for cp in copies:
                cp.start()
            for cp in copies:
                cp.wait()

    return pl.pallas_call(
        body, name=name, grid=(rows // _TM,),
        out_shape=[jax.ShapeDtypeStruct((rows, q_w), _ACT), jax.ShapeDtypeStruct((rows, kv_w), _F32),
                   jax.ShapeDtypeStruct((rows, kv_w), _F32), jax.ShapeDtypeStruct((1, n_heads), _F32)],
        in_specs=[pl.BlockSpec((_TM, q_w), lambda i: (i, 0)), pl.BlockSpec((rows, kv_w), lambda i: (0, 0)),
                  pl.BlockSpec((rows, kv_w), lambda i: (0, v_cb)), pl.BlockSpec((1, n_heads), lambda i: (0, 0)),
                  pl.BlockSpec((_TM, q_w), lambda i: (i, 0))],
        out_specs=[pl.BlockSpec((_TM, q_w), lambda i: (i, 0)), pl.BlockSpec(memory_space=pl.ANY),
                   pl.BlockSpec(memory_space=pl.ANY), pl.BlockSpec((1, n_heads), lambda i: (0, 0))],
        scratch_shapes=[pltpu.VMEM((rows, kv_w), _F32), pltpu.VMEM((rows, kv_w), _F32),
                        pltpu.SemaphoreType.DMA((2,))],
        compiler_params=_params(("arbitrary",)),
    )(q, k, p, sinks, do)


def _loss_head(name, xs, gain, target, n_lat):
    rows, d = xs.shape

    def body(x_ref, g_ref, t_ref, loss_ref, dg_ref, dx_ref):
        i = pl.program_id(0)

        @pl.when(i == 0)
        def _():
            loss_ref[...] = jnp.zeros_like(loss_ref)
            dg_ref[...] = jnp.zeros_like(dg_ref)

        @pl.when(i < n_lat)
        def _():
            tv = t_ref[...]

            def f(gain_, x):
                y = x * lax.rsqrt(jnp.mean(x * x, axis=-1, keepdims=True) + _EPS) * gain_
                return 0.5 * jnp.sum(jnp.mean(jnp.square(y - tv), axis=-1))

            val, (dg, dx) = jax.value_and_grad(f, argnums=(0, 1))(g_ref[...], x_ref[...])
            loss_ref[...] += val
            dg_ref[...] += dg
            dx_ref[...] = dx

        @pl.when(i >= n_lat)
        def _():
            dx_ref[...] = jnp.zeros_like(dx_ref)

    return pl.pallas_call(
        body, name=name, grid=(rows // _TM,),
        out_shape=[jax.ShapeDtypeStruct((1, 128), _F32), jax.ShapeDtypeStruct((1, d), _F32),
                   jax.ShapeDtypeStruct((rows, d), _F32)],
        in_specs=[pl.BlockSpec((_TM, d), lambda i: (i, 0)), pl.BlockSpec((1, d), lambda i: (0, 0)),
                  pl.BlockSpec((_TM, d), lambda i: (jnp.minimum(i, n_lat - 1), 0))],
        out_specs=[pl.BlockSpec((1, 128), lambda i: (0, 0)), pl.BlockSpec((1, d), lambda i: (0, 0)),
                   pl.BlockSpec((_TM, d), lambda i: (i, 0))],
        compiler_params=_params(("arbitrary",)),
    )(xs, gain, target)


def _adamw(name, w, g, m, v):
    rows, cols = w.shape
    tr = _divisor(rows, 512, 8)
    b1, b2 = _ADAM["b1"], _ADAM["b2"]
    c1 = 1.0 - b1 ** _ADAM["step"]
    c2 = 1.0 - b2 ** _ADAM["step"]

    def body(w_ref, g_ref, m_ref, v_ref, d_ref, nm_ref, nv_ref):
        gv = g_ref[...]
        nm = b1 * m_ref[...] + (1.0 - b1) * gv
        nv = b2 * v_ref[...] + (1.0 - b2) * jnp.square(gv)
        d_ref[...] = -_ADAM["lr"] * ((nm / c1) / (jnp.sqrt(nv / c2) + _ADAM["eps"]) + _ADAM["wd"] * w_ref[...])
        nm_ref[...] = nm
        nv_ref[...] = nv

    spec = pl.BlockSpec((tr, cols), lambda i: (i, 0))
    return pl.pallas_call(
        body, name=name, grid=(rows // tr,),
        out_shape=[jax.ShapeDtypeStruct((rows, cols), _F32)] * 3,
        in_specs=[spec] * 4, out_specs=[spec] * 3,
        compiler_params=_params(("arbitrary",)),
    )(w, g, m, v)


def _pack(arrays, cols=128):
    flat = jnp.concatenate([a.reshape(-1).astype(_F32) for a in arrays])
    pad = (-flat.shape[0]) % (64 * cols)
    return jnp.pad(flat, (0, pad)).reshape(-1, cols)


def _unpack(flat, shapes):
    out, off = [], 0
    for s in shapes:
        n = 1
        for d in s:
            n *= d
        out.append(flat[..., off:off + n].reshape(flat.shape[:-1] + tuple(s)))
        off += n
    return out


def _gather_channels(parts):
    moved = jnp.moveaxis(parts, 0, -2)
    return moved.reshape(moved.shape[:-2] + (moved.shape[-2] * moved.shape[-1],))


def kernel(x, c, ctx, c_ctx, w_mod, b_mod, norm_mix, norm_ffn, w_in_ab, conv_a, conv_b, conv_b_bias, ln_b_gain, ln_b_bias, w_out_ab, w_qkv, w_o, sinks, w_up, w_conv_ffn, w_down, final_norm, loss_target, m_c_ctx, m_w_mod, m_b_mod, m_norm_mix, m_norm_ffn, m_w_in_ab, m_conv_a, m_conv_b, m_conv_b_bias, m_ln_b_gain, m_ln_b_bias, m_w_out_ab, m_w_qkv, m_w_o, m_sinks, m_w_up, m_w_conv_ffn, m_w_down, m_final_norm, v_c_ctx, v_w_mod, v_b_mod, v_norm_mix, v_norm_ffn, v_w_in_ab, v_conv_a, v_conv_b, v_conv_b_bias, v_ln_b_gain, v_ln_b_bias, v_w_out_ab, v_w_qkv, v_w_o, v_sinks, v_w_up, v_w_conv_ffn, v_w_down, v_final_norm):
    args = dict(locals())
    weight_names = ["c_ctx", "w_mod", "b_mod", "norm_mix", "norm_ffn", "w_in_ab", "conv_a", "conv_b", "conv_b_bias",
                    "ln_b_gain", "ln_b_bias", "w_out_ab", "w_qkv", "w_o", "sinks", "w_up", "w_conv_ffn", "w_down",
                    "final_norm"]
    length, d = x.shape[1], x.shape[2]
    ctx_len = ctx.shape[1]
    assert ctx_len == _TM and length % _TM == 0 and x.shape[0] == 1
    n_lat = length // _TM
    depth = w_mod.shape[0]
    n_even, n_odd = w_in_ab.shape[0], w_qkv.shape[0]
    a_w = conv_a.shape[2] * _NDEV
    b_w = conv_b.shape[2] * _NDEV
    assert a_w == b_w
    q_w = w_o.shape[1] * _NDEV
    kv_w = (w_qkv.shape[2] * _NDEV - q_w) // 2
    d_ff = w_down.shape[1] * _NDEV
    dev = 4 * lax.axis_index("x") + 2 * lax.axis_index("y") + lax.axis_index("c")

    small_shapes = [c.shape[1:], conv_a.shape, conv_b.shape, w_conv_ffn.shape]
    g0 = _all_gather_small("gather_small_params", _pack([c, conv_a, conv_b, w_conv_ffn]))
    c_parts, ca_parts, cb_parts, cf_parts = _unpack(g0.reshape(_NDEV, -1), small_shapes)
    conv_a_full = _gather_channels(ca_parts)
    conv_b_full = _gather_channels(cb_parts)
    conv_f_full = _gather_channels(cf_parts)

    cond = jnp.concatenate([c_parts, c_ctx[None], jnp.zeros((16 - _NDEV - 1, d), _F32)], axis=0)
    mod_cols = w_mod.shape[2]
    m_shard = _mod_forward("mod_forward", cond, w_mod)
    m_all = _all_gather_small("gather_mod", m_shard.reshape(depth * 16, mod_cols))
    m_all = jnp.moveaxis(m_all.reshape(_NDEV, depth, 16, mod_cols), 0, 2).reshape(depth, 16, _NDEV * mod_cols)
    m_all = m_all + b_mod[:, None, :]
    m_lat = lax.dynamic_index_in_dim(m_all, dev, axis=1, keepdims=False)
    m_ctx = m_all[:, _NDEV]

    def mod_vec(l, j):
        return jnp.stack([m_lat[l, j * d:(j + 1) * d], m_ctx[l, j * d:(j + 1) * d]])[:, None, :]

    def layer_mats(l):
        if l % 2 == 0:
            first = [("in", l // 2, w_in_ab[l // 2].T), ("out", l // 2, w_out_ab[l // 2])]
        else:
            first = [("qkv", l // 2, w_qkv[l // 2].T), ("o", l // 2, w_o[l // 2])]
        return first + [("up", l, w_up[l].T), ("down", l, w_down[l])]

    piece_mats = [layer_mats(0)[:2], layer_mats(0)[2:]] + [layer_mats(l) for l in range(1, depth)]
    n_pieces = len(piece_mats)
    slab_off, slab_r, piece_of, piece_keys, piece_rows, slabs = {}, {}, {}, [], [], []
    for p, mats in enumerate(piece_mats):
        off, keys = 0, []
        for fam, idx, mat in mats:
            slab_off[fam, idx], slab_r[fam], piece_of[fam, idx] = off, mat.shape[0], p
            off += mat.shape[0]
            keys.append((fam, idx))
        piece_keys.append(keys)
        piece_rows.append(off)
        slabs.append(jnp.concatenate([mat.astype(_MM) for _, _, mat in mats], axis=0))
    wgs = [_all_gather_slab("gather_weights_0", slabs[0])] + [None] * (n_pieces - 1)
    gathers, start_token = [None] * n_pieces, jnp.zeros((), _F32)
    for p in range(1, n_pieces):
        land = _fill_own_slot(f"gather_fill_{p}", [slabs[p]], [0], piece_rows[p])
        gathers[p] = _exchange_start(f"gather_start_{p}", [slabs[p]], land, [0])
        start_token = start_token + gathers[p][-1][0, 0]

    def wref(fam, idx):
        return wgs[piece_of[fam, idx]], slab_off[fam, idx], slab_r[fam]

    cos, sin = _rope_tables(length, ctx_len)
    xs = jnp.concatenate([x[0], ctx[0]], axis=0)

    def full(a):
        return (a.reshape(1, -1), "full")

    saved = []
    for l in range(depth):
        sv = {"x_in": xs}
        if l > 0:
            wgs[1 + l] = _exchange_wait(f"gather_wait_{1 + l}", gathers[1 + l], [0], xs)
        gain1 = full(norm_mix[l] + start_token) if l == 0 else full(norm_mix[l])
        (h1,) = _rowfn(f"norm_mix_{l}", _f_norm_mod, [gain1, (mod_vec(l, 0), "stream"), (mod_vec(l, 1), "stream")],
                       [(xs, d, 0)], [(_ACT, [d])], n_lat)
        sv["h1"] = h1
        if l % 2 == 0:
            e = l // 2
            p = _mm_nt(f"proj_in_{l}", h1, *wref("in", e), _ACT)
            (qm,) = _rowfn(f"premix_{l}", _f_premix, [], [(p, a_w, j) for j in range(5)],
                           [(None, [a_w]), (_ACT, [a_w, b_w])], n_lat)
            cv_a = _dwconv(f"conv_a_{l}", qm, 0, a_w, conv_a_full[e], _ACT, n_lat)
            cv_b = _dwconv(f"conv_b_{l}", qm, 1, b_w, conv_b_full[e], _ACT, n_lat)
            post_params = [full(conv_b_bias[e]), full(ln_b_gain[e]), full(ln_b_bias[e])]
            (z,) = _rowfn(f"postmix_{l}", _f_postmix, post_params, [(p, a_w, 0), (cv_a, a_w, 0), (cv_b, b_w, 0)],
                          [(_ACT, [a_w, b_w])], n_lat)
            y1 = _mm_nn(f"proj_out_{l}", z, *wref("out", e), _ACT)
            sv.update(p=p, qm=qm, cv_a=cv_a, cv_b=cv_b, z=z)
        else:
            o = l // 2
            p = _mm_nt(f"proj_qkv_{l}", h1, *wref("qkv", o), _ACT)
            qr, kr = _rope_fwd(f"rope_{l}", p, cos, sin, q_w, kv_w)
            sk = sinks[o].reshape(1, -1)
            z = _attn_fwd(f"attn_{l}", qr, kr, p, sk, n_lat, length, kv_w)
            y1 = _mm_nn(f"proj_o_{l}", z, *wref("o", o), _ACT)
            sv.update(p=p, qr=qr, kr=kr, z=z)
        (xs,) = _rowfn(f"residual_mix_{l}", _f_residual, [(mod_vec(l, 2), "stream")], [(xs, d, 0), (y1, d, 0)],
                       [(_F32, [d])], n_lat)
        sv.update(y1=y1, x_mid=xs)
        if l == 0:
            wgs[1] = _exchange_wait("gather_wait_1", gathers[1], [0], xs)
        (h2,) = _rowfn(f"norm_ffn_{l}", _f_norm_mod,
                       [full(norm_ffn[l]), (mod_vec(l, 3), "stream"), (mod_vec(l, 4), "stream")],
                       [(xs, d, 0)], [(_ACT, [d])], n_lat)
        pu = _mm_nt(f"proj_up_{l}", h2, *wref("up", l), _ACT)
        u = _dwconv(f"conv_ffn_{l}", pu, 0, 2 * d_ff, conv_f_full[l], _ACT, n_lat)
        (f,) = _rowfn(f"glu_{l}", _f_glu, [], [(u, d_ff, 0), (u, d_ff, 1)], [(_ACT, [d_ff])], n_lat)
        y2 = _mm_nn(f"proj_down_{l}", f, *wref("down", l), _ACT)
        (xs,) = _rowfn(f"residual_ffn_{l}", _f_residual, [(mod_vec(l, 5), "stream")], [(xs, d, 0), (y2, d, 0)],
                       [(_F32, [d])], n_lat)
        sv.update(h2=h2, pu=pu, u=u, f=f, y2=y2)
        saved.append(sv)

    loss_part, d_final_norm, dxs = _loss_head("loss_head", xs, final_norm.reshape(1, -1), loss_target[0], n_lat)
    loss = lax.psum(loss_part[0, 0], ("x", "y", "c"))

    wgrads = {}
    d_mod = [[None] * 6 for _ in range(depth)]
    d_norm_mix, d_norm_ffn = [None] * depth, [None] * depth
    d_conv_a, d_conv_b = [None] * n_even, [None] * n_even
    d_bias, d_ln_g, d_ln_b = [None] * n_even, [None] * n_even, [None] * n_even
    d_sinks = [None] * n_odd
    d_conv_f = [None] * depth
    exchanges, recvs, exchange_token = [None] * n_pieces, [None] * n_pieces, jnp.zeros((), _F32)

    def piece_parts(p):
        return ([wgrads[key].reshape(_NDEV, slab_r[key[0]], d) for key in piece_keys[p]],
                [slab_off[key] for key in piece_keys[p]])

    def start_exchange(p):
        parts, offsets = piece_parts(p)
        land = _fill_own_slot(f"exchange_fill_{p}", parts, offsets, piece_rows[p])
        exchanges[p] = _exchange_start(f"exchange_start_{p}", parts, land, offsets)
        return exchanges[p][-1][0, 0]

    for l in reversed(range(depth)):
        sv = saved[l]
        (dg,), (dy2,) = _rowfn_bwd(f"residual_ffn_bwd_{l}", _f_gate, [(mod_vec(l, 5) + exchange_token, "stream")],
                                   [(sv["y2"], d, 0)], [0], [(dxs, [d])], [(_ACT, [0])], n_lat)
        d_mod[l][5] = dg
        wgrads["down", l] = _mm_tn(f"wgrad_down_{l}", sv["f"], dy2, _ACT)
        df = _mm_nt(f"bwd_down_{l}", dy2, *wref("down", l), _ACT)
        _, (du,) = _rowfn_bwd(f"glu_bwd_{l}", _f_glu, [], [(sv["u"], d_ff, 0), (sv["u"], d_ff, 1)], [0, 1],
                              [(df, [d_ff])], [(_ACT, [0, 1])], n_lat)
        d_conv_f[l] = _dwconv_wgrad(f"conv_ffn_wgrad_{l}", du, sv["pu"], 0, 2 * d_ff, 3, n_lat)
        dpu = _dwconv(f"conv_ffn_bwd_{l}", du, 0, 2 * d_ff, conv_f_full[l][::-1], _ACT, n_lat)
        wgrads["up", l] = _mm_tn(f"wgrad_up_{l}", dpu, sv["h2"], _ACT)
        dh2 = _mm_nn(f"bwd_up_{l}", dpu, *wref("up", l), _ACT)
        (dgain, dsh, dsc), (dxs,) = _rowfn_bwd(
            f"norm_ffn_bwd_{l}", _f_norm_mod,
            [full(norm_ffn[l]), (mod_vec(l, 3), "stream"), (mod_vec(l, 4), "stream")], [(sv["x_mid"], d, 0)], [0],
            [(dh2, [d])], [(_F32, [0])], n_lat, residual=dxs)
        d_norm_ffn[l], d_mod[l][3], d_mod[l][4] = dgain, dsh, dsc
        mix_token = start_exchange(1) if l == 0 else jnp.zeros((), _F32)
        (dg,), (dy1,) = _rowfn_bwd(f"residual_mix_bwd_{l}", _f_gate, [(mod_vec(l, 2) + mix_token, "stream")],
                                   [(sv["y1"], d, 0)], [0], [(dxs, [d])], [(_ACT, [0])], n_lat)
        d_mod[l][2] = dg
        if l % 2 == 0:
            e = l // 2
            wgrads["out", e] = _mm_tn(f"wgrad_out_{l}", sv["z"], dy1, _ACT)
            dz = _mm_nt(f"bwd_out_{l}", dy1, *wref("out", e), _ACT)
            post_params = [full(conv_b_bias[e]), full(ln_b_gain[e]), full(ln_b_bias[e])]
            (dbias, dlg, dlb), (dgb, dcv_a, dcv_b) = _rowfn_bwd(
                f"postmix_bwd_{l}", _f_postmix, post_params,
                [(sv["p"], a_w, 0), (sv["cv_a"], a_w, 0), (sv["cv_b"], b_w, 0)], [0, 1, 2], [(dz, [a_w, b_w])],
                [(_ACT, [0]), (_ACT, [1]), (_ACT, [2])], n_lat)
            d_bias[e], d_ln_g[e], d_ln_b[e] = dbias, dlg, dlb
            d_conv_a[e] = _dwconv_wgrad(f"conv_a_wgrad_{l}", dcv_a, sv["qm"], 0, a_w, conv_a_full.shape[1], n_lat)
            d_conv_b[e] = _dwconv_wgrad(f"conv_b_wgrad_{l}", dcv_b, sv["qm"], 1, b_w, conv_b_full.shape[1], n_lat)
            dq_a = _dwconv(f"conv_a_bwd_{l}", dcv_a, 0, a_w, conv_a_full[e][::-1], _ACT, n_lat)
            dq_b = _dwconv(f"conv_b_bwd_{l}", dcv_b, 0, b_w, conv_b_full[e][::-1], _ACT, n_lat)
            _, (dp,) = _rowfn_bwd(f"premix_bwd_{l}", _f_premix, [], [(sv["p"], a_w, j) for j in range(5)],
                                  [0, 1, 2, 3, 4], [(dgb, [a_w]), (dq_a, [a_w]), (dq_b, [b_w])],
                                  [(_ACT, [0, 1, 2, 3, 4])], n_lat)
            wgrads["in", e] = _mm_tn(f"wgrad_in_{l}", dp, sv["h1"], _ACT)
            dh1 = _mm_nn(f"bwd_in_{l}", dp, *wref("in", e), _ACT)
        else:
            o = l // 2
            wgrads["o", o] = _mm_tn(f"wgrad_o_{l}", sv["z"], dy1, _ACT)
            dz = _mm_nt(f"bwd_o_{l}", dy1, *wref("o", o), _ACT)
            sk = sinks[o].reshape(1, -1)
            dqr, dkr, dv, dsk = _attn_bwd(f"attn_bwd_{l}", sv["qr"], sv["kr"], sv["p"], sk, dz, n_lat, length, kv_w)
            d_sinks[o] = dsk
            dp = _rope_bwd(f"rope_bwd_{l}", dqr, dkr, dv, cos, sin)
            wgrads["qkv", o] = _mm_tn(f"wgrad_qkv_{l}", dp, sv["h1"], _ACT)
            dh1 = _mm_nn(f"bwd_qkv_{l}", dp, *wref("qkv", o), _ACT)
        (dgain, dsh, dsc), (dxs,) = _rowfn_bwd(
            f"norm_mix_bwd_{l}", _f_norm_mod,
            [full(norm_mix[l]), (mod_vec(l, 0), "stream"), (mod_vec(l, 1), "stream")], [(sv["x_in"], d, 0)], [0],
            [(dh1, [d])], [(_F32, [0])], n_lat, residual=dxs)
        d_norm_mix[l], d_mod[l][0], d_mod[l][1] = dgain, dsh, dsc
        if l > 0:
            exchange_token = start_exchange(1 + l)
        else:
            parts, offsets = piece_parts(0)
            recvs[0] = _exchange_grads("exchange_weight_grads_0", parts, offsets, piece_rows[0])
    grad_x = dxs[:length][None]

    gsums = []
    for p in range(n_pieces):
        if p > 0:
            recvs[p] = _exchange_wait(f"exchange_wait_{p}", exchanges[p], piece_parts(p)[1], dxs)
        gsums.append(_sum_slots(f"sum_weight_grads_{p}", recvs[p]))

    def slab_grad(fam, count, transposed):
        mats = [gsums[piece_of[fam, i]][slab_off[fam, i]:slab_off[fam, i] + slab_r[fam]] for i in range(count)]
        return jnp.stack([m_.T if transposed else m_ for m_ in mats])

    grads = {
        "w_in_ab": slab_grad("in", n_even, True), "w_qkv": slab_grad("qkv", n_odd, True),
        "w_up": slab_grad("up", depth, True), "w_out_ab": slab_grad("out", n_even, False),
        "w_o": slab_grad("o", n_odd, False), "w_down": slab_grad("down", depth, False),
    }

    dm_dev = jnp.stack([jnp.concatenate([d_mod[l][j][:, 0, :] for j in range(6)], axis=-1)
                        for l in range(depth)])
    small_grads = [dm_dev, jnp.stack(d_norm_mix), jnp.stack(d_norm_ffn), jnp.stack(d_conv_a), jnp.stack(d_conv_b),
                   jnp.stack(d_bias), jnp.stack(d_ln_g), jnp.stack(d_ln_b), jnp.stack(d_sinks), jnp.stack(d_conv_f),
                   d_final_norm]
    sg_shapes = [a.shape for a in small_grads]
    sg_all = _all_gather_small("gather_small_grads", _pack(small_grads))
    sg_sum = _sum_slots("sum_small_grads", sg_all)
    (dm_sum, g_norm_mix, g_norm_ffn, g_conv_a, g_conv_b, g_bias, g_ln_g, g_ln_b, g_sinks, g_conv_f,
     g_final_norm) = _unpack(sg_sum.reshape(-1), sg_shapes)
    dm_each = _unpack(sg_all.reshape(_NDEV, -1), sg_shapes[:1])[0]

    def my_channels(a):
        width = a.shape[-1] // _NDEV
        return lax.dynamic_slice_in_dim(a, dev * width, width, axis=a.ndim - 1)

    grads["b_mod"] = dm_sum[:, 0] + dm_sum[:, 1]
    grads["norm_mix"] = g_norm_mix.reshape(depth, d)
    grads["norm_ffn"] = g_norm_ffn.reshape(depth, d)
    grads["conv_a"] = my_channels(g_conv_a)
    grads["conv_b"] = my_channels(g_conv_b)
    grads["conv_b_bias"] = g_bias.reshape(n_even, b_w)
    grads["ln_b_gain"] = g_ln_g.reshape(n_even, b_w)
    grads["ln_b_bias"] = g_ln_b.reshape(n_even, b_w)
    grads["sinks"] = g_sinks.reshape(n_odd, -1)
    grads["w_conv_ffn"] = my_channels(g_conv_f)
    grads["final_norm"] = g_final_norm.reshape(d)

    dm_rows = jnp.concatenate([jnp.moveaxis(dm_each[:, :, 0], 0, 1), dm_sum[:, 1:2],
                               jnp.zeros((depth, 16 - _NDEV - 1, 6 * d), _F32)], axis=1)
    dm_mine = my_channels(dm_rows)
    grads["w_mod"] = jnp.stack([_mm_tn(f"wgrad_mod_{l}", cond, dm_mine[l], _F32, silu_a=True) for l in range(depth)])
    dcond = _mod_backward_cond("mod_backward_cond", dm_mine, w_mod)
    dcond_all = _all_gather_small("gather_dcond", dcond)
    dcond_sum = _sum_slots("sum_dcond", dcond_all)[_NDEV]
    sg = jax.nn.sigmoid(c_ctx)
    grads["c_ctx"] = dcond_sum * (sg * (1.0 + c_ctx * (1.0 - sg)))

    big = ["w_mod", "w_in_ab", "w_out_ab", "w_qkv", "w_o", "w_up", "w_down"]
    small = [n for n in weight_names if n not in big]
    delta, new_m, new_v = {}, {}, {}
    for n in big:
        w = args[n]
        two_d = lambda a: a.reshape(-1, w.shape[-1])
        dl, nm, nv = _adamw(f"adamw_{n}", two_d(w), two_d(grads[n]), two_d(args["m_" + n]), two_d(args["v_" + n]))
        delta[n], new_m[n], new_v[n] = dl.reshape(w.shape), nm.reshape(w.shape), nv.reshape(w.shape)
    shapes = [args[n].shape for n in small]
    grads = {n: grads[n].reshape(args[n].shape) for n in grads}
    dl, nm, nv = _adamw("adamw_small", _pack([args[n] for n in small]), _pack([grads[n] for n in small]),
                        _pack([args["m_" + n] for n in small]), _pack([args["v_" + n] for n in small]))
    for res, packed in ((delta, dl), (new_m, nm), (new_v, nv)):
        for n, a in zip(small, _unpack(packed.reshape(-1), shapes)):
            res[n] = a

    return (loss, grad_x, *[grads[n] for n in weight_names], *[delta[n] for n in weight_names],
            *[new_m[n] for n in weight_names], *[new_v[n] for n in weight_names])
```

```python
import functools

import jax
import jax.numpy as jnp
from jax import lax
from jax.experimental import pallas as pl
from jax.experimental.pallas import tpu as pltpu

_F32 = jnp.float32
_MM = jnp.bfloat16
_ACT = jnp.bfloat16
_TM = 256
_HALO = 16
_LANES = 128
_CONV_ROWS = 128
_NDEV = 8
_HEAD_DIM = 64
_WINDOW = 128
_GRID_W = 64
_ROPE_THETA = 10000.0
_EPS = 1e-6
_NEG_INF = -1e30
_VMEM_LIMIT = 56 * 1024 * 1024
_ADAM = dict(lr=0.001, b1=0.9, b2=0.999, eps=1e-08, wd=0.01, step=10)
_MESH = pl.DeviceIdType.MESH


def _params(sem=None):
    return pltpu.CompilerParams(dimension_semantics=sem, vmem_limit_bytes=_VMEM_LIMIT)


def _divisor(n, cap, mult):
    if n <= cap:
        return n
    for d in range(cap - cap % mult, 0, -mult):
        if n % d == 0:
            return d
    raise ValueError(f"no tile for {n}")


def _my_coords():
    return lax.axis_index("x"), lax.axis_index("y"), lax.axis_index("c")


def _peer(k):
    x, y, c = _my_coords()
    px = 1 - x if k & 4 else x
    py = 1 - y if k & 2 else y
    pc = 1 - c if k & 1 else c
    return (px, py, pc), 4 * px + 2 * py + pc


def _all_gather_small(name, v):
    rows, cols = v.shape

    def body(v_ref, out_ref, send_sems, recv_sems):
        x, y, c = _my_coords()
        me = 4 * x + 2 * y + c
        out_ref[me] = v_ref[...]
        sends = []
        for k in range(1, _NDEV):
            peer, _ = _peer(k)
            cp = pltpu.make_async_remote_copy(
                src_ref=v_ref, dst_ref=out_ref.at[me], send_sem=send_sems.at[k - 1], recv_sem=recv_sems.at[k - 1],
                device_id=peer, device_id_type=_MESH)
            cp.start()
            sends.append(cp)
        for k in range(1, _NDEV):
            peer, pid = _peer(k)
            pltpu.make_async_remote_copy(
                src_ref=v_ref, dst_ref=out_ref.at[pid], send_sem=send_sems.at[k - 1], recv_sem=recv_sems.at[k - 1],
                device_id=peer, device_id_type=_MESH).wait_recv()
        for cp in sends:
            cp.wait_send()

    return pl.pallas_call(
        body, name=name,
        out_shape=jax.ShapeDtypeStruct((_NDEV, rows, cols), v.dtype),
        in_specs=[pl.BlockSpec(memory_space=pltpu.VMEM)],
        out_specs=pl.BlockSpec(memory_space=pltpu.VMEM),
        scratch_shapes=[pltpu.SemaphoreType.DMA((_NDEV - 1,)), pltpu.SemaphoreType.DMA((_NDEV - 1,))],
        compiler_params=pltpu.CompilerParams(vmem_limit_bytes=_VMEM_LIMIT),
    )(v)


def _all_gather_slab(name, slab):
    rows, cols = slab.shape

    def body(s_ref, out_ref, send_sems, recv_sems, local_sem):
        x, y, c = _my_coords()
        me = 4 * x + 2 * y + c
        local = pltpu.make_async_copy(s_ref, out_ref.at[me], local_sem)
        local.start()
        sends = []
        for k in range(1, _NDEV):
            peer, _ = _peer(k)
            cp = pltpu.make_async_remote_copy(
                src_ref=s_ref, dst_ref=out_ref.at[me], send_sem=send_sems.at[k - 1], recv_sem=recv_sems.at[k - 1],
                device_id=peer, device_id_type=_MESH)
            cp.start()
            sends.append(cp)
        for k in range(1, _NDEV):
            peer, pid = _peer(k)
            pltpu.make_async_remote_copy(
                src_ref=s_ref, dst_ref=out_ref.at[pid], send_sem=send_sems.at[k - 1], recv_sem=recv_sems.at[k - 1],
                device_id=peer, device_id_type=_MESH).wait_recv()
        for cp in sends:
            cp.wait_send()
        local.wait()

    return pl.pallas_call(
        body, name=name,
        out_shape=jax.ShapeDtypeStruct((_NDEV, rows, cols), slab.dtype),
        in_specs=[pl.BlockSpec(memory_space=pl.ANY)],
        out_specs=pl.BlockSpec(memory_space=pl.ANY),
        scratch_shapes=[pltpu.SemaphoreType.DMA((_NDEV - 1,)), pltpu.SemaphoreType.DMA((_NDEV - 1,)),
                        pltpu.SemaphoreType.DMA],
        compiler_params=pltpu.CompilerParams(vmem_limit_bytes=_VMEM_LIMIT),
    )(slab)


def _exchange_grads(name, parts, offsets, total_rows):
    n = len(parts)
    cols = parts[0].shape[2]
    dtype = parts[0].dtype

    def body(*refs):
        g_refs, out_ref = refs[:n], refs[n]
        send_sems, recv_sems, local_sems = refs[n + 1:]
        x, y, c = _my_coords()
        me = 4 * x + 2 * y + c

        def dst(slot, m):
            return out_ref.at[slot, pl.ds(offsets[m], parts[m].shape[1]), :]

        locals_ = []
        for m in range(n):
            cp = pltpu.make_async_copy(g_refs[m].at[me], dst(me, m), local_sems.at[m])
            cp.start()
            locals_.append(cp)
        sends = []
        for k in range(1, _NDEV):
            peer, pid = _peer(k)
            for m in range(n):
                cp = pltpu.make_async_remote_copy(
                    src_ref=g_refs[m].at[pid], dst_ref=dst(me, m), send_sem=send_sems.at[k - 1, m],
                    recv_sem=recv_sems.at[k - 1, m], device_id=peer, device_id_type=_MESH)
                cp.start()
                sends.append(cp)
        for k in range(1, _NDEV):
            peer, pid = _peer(k)
            for m in range(n):
                pltpu.make_async_remote_copy(
                    src_ref=g_refs[m].at[pid], dst_ref=dst(pid, m), send_sem=send_sems.at[k - 1, m],
                    recv_sem=recv_sems.at[k - 1, m], device_id=peer, device_id_type=_MESH).wait_recv()
        for cp in sends:
            cp.wait_send()
        for cp in locals_:
            cp.wait()

    return pl.pallas_call(
        body, name=name,
        out_shape=jax.ShapeDtypeStruct((_NDEV, total_rows, cols), dtype),
        in_specs=[pl.BlockSpec(memory_space=pl.ANY)] * n,
        out_specs=pl.BlockSpec(memory_space=pl.ANY),
        scratch_shapes=[pltpu.SemaphoreType.DMA((_NDEV - 1, n)), pltpu.SemaphoreType.DMA((_NDEV - 1, n)),
                        pltpu.SemaphoreType.DMA((n,))],
        compiler_params=pltpu.CompilerParams(vmem_limit_bytes=_VMEM_LIMIT),
    )(*parts)


def _sum_slots(name, v):
    _, rows, cols = v.shape
    tr = _divisor(rows, 512, 16)

    def body(v_ref, o_ref):
        acc = v_ref[0].astype(_F32)
        for e in range(1, _NDEV):
            acc = acc + v_ref[e].astype(_F32)
        o_ref[...] = acc

    return pl.pallas_call(
        body, name=name, grid=(rows // tr,),
        out_shape=jax.ShapeDtypeStruct((rows, cols), _F32),
        in_specs=[pl.BlockSpec((_NDEV, tr, cols), lambda i: (0, i, 0))],
        out_specs=pl.BlockSpec((tr, cols), lambda i: (i, 0)),
        compiler_params=_params(("arbitrary",)),
    )(v)


_HBM_SPEC = pl.BlockSpec(memory_space=pltpu.HBM)
_SEM_SPEC = pl.BlockSpec(memory_space=pltpu.SEMAPHORE)
_EFFECT = pltpu.SideEffectType.DATAFLOW_SIDE_EFFECTING


def _in_hbm(a):
    return pltpu.with_memory_space_constraint(a, pltpu.HBM)


def _block_for(ref, device):
    return ref if len(ref.shape) == 2 else ref.at[device]


def _fill_own_slot(name, srcs, offsets, total_rows):
    n = len(srcs)
    cols = srcs[0].shape[-1]

    def body(*refs):
        src_refs, out_ref, bufs, sems = refs[:n], refs[n], refs[n + 1:2 * n + 1], refs[2 * n + 1]
        x, y, c = _my_coords()
        me = 4 * x + 2 * y + c
        loads = [pltpu.make_async_copy(_block_for(src_refs[m], me), bufs[m], sems.at[0, m]) for m in range(n)]
        stores = [pltpu.make_async_copy(bufs[m], out_ref.at[me, pl.ds(offsets[m], srcs[m].shape[-2]), :],
                                        sems.at[1, m]) for m in range(n)]
        for copies in (loads, stores):
            for cp in copies:
                cp.start()
            for cp in copies:
                cp.wait()

    return pl.pallas_call(
        body, name=name,
        out_shape=jax.ShapeDtypeStruct((_NDEV, total_rows, cols), srcs[0].dtype),
        in_specs=[pl.BlockSpec(memory_space=pl.ANY)] * n,
        out_specs=pl.BlockSpec(memory_space=pl.ANY),
        scratch_shapes=[pltpu.VMEM(s.shape[-2:], s.dtype) for s in srcs] + [pltpu.SemaphoreType.DMA((2, n))],
        compiler_params=pltpu.CompilerParams(vmem_limit_bytes=_VMEM_LIMIT),
    )(*srcs)


def _exchange_start(name, srcs, land, offsets):
    n = len(srcs)

    def body(*refs):
        src_refs, land_ref = refs[:n], refs[n]
        send_sems, recv_sems, token = refs[n + 1], refs[n + 2], refs[-1]
        x, y, c = _my_coords()
        me = 4 * x + 2 * y + c
        for k in range(1, _NDEV):
            peer, pid = _peer(k)
            for m in range(n):
                pltpu.make_async_remote_copy(
                    src_ref=_block_for(src_refs[m], pid),
                    dst_ref=land_ref.at[me, pl.ds(offsets[m], srcs[m].shape[-2]), :],
                    send_sem=send_sems, recv_sem=recv_sems, device_id=peer, device_id_type=_MESH).start()
        token[...] = jnp.zeros_like(token)

    sems = pltpu.SemaphoreType.DMA(())
    return pl.pallas_call(
        body, name=name,
        out_shape=(sems, sems, *[pltpu.HBM(s.shape, s.dtype) for s in srcs], pltpu.HBM(land.shape, land.dtype),
                   jax.ShapeDtypeStruct((8, 128), _F32)),
        in_specs=[_HBM_SPEC] * (n + 1),
        out_specs=(_SEM_SPEC, _SEM_SPEC, *[_HBM_SPEC] * (n + 1), pl.BlockSpec(memory_space=pltpu.VMEM)),
        input_output_aliases={i: 2 + i for i in range(n + 1)},
        compiler_params=pltpu.CompilerParams(has_side_effects=_EFFECT),
    )(*[_in_hbm(s) for s in srcs], _in_hbm(land))


def _exchange_wait(name, started, offsets, after):
    send_sems, recv_sems = started[0], started[1]
    srcs, land = list(started[2:-2]), started[-2]
    n = len(srcs)

    def body(*refs):
        src_refs, land_ref = refs[:n], refs[n]
        send_sems_, recv_sems_ = refs[n + 1], refs[n + 2]
        others = land_ref.at[pl.ds(0, _NDEV - 1)]
        cp = pltpu.make_async_remote_copy(src_ref=others, dst_ref=others, send_sem=send_sems_, recv_sem=recv_sems_,
                                          device_id=_peer(1)[0], device_id_type=_MESH)
        cp.wait_send()
        cp.wait_recv()

    res = pl.pallas_call(
        body, name=name,
        out_shape=(*[pltpu.HBM(s.shape, s.dtype) for s in srcs], pltpu.HBM(land.shape, land.dtype)),
        in_specs=[_HBM_SPEC] * (n + 1) + [_SEM_SPEC, _SEM_SPEC, pl.BlockSpec(memory_space=pl.ANY)],
        out_specs=tuple([_HBM_SPEC] * (n + 1)),
        input_output_aliases={i: i for i in range(n + 1)},
        compiler_params=pltpu.CompilerParams(has_side_effects=_EFFECT),
    )(*srcs, land, send_sems, recv_sems, after)
    return res[n]


def _load_weight(wg_ref, wbuf, sems, off, r, step):
    @pl.when(step == 0)
    def _():
        copies = [pltpu.make_async_copy(wg_ref.at[e, pl.ds(off, r), :], wbuf.at[pl.ds(e * r, r), :], sems.at[e])
                  for e in range(_NDEV)]
        for cp in copies:
            cp.start()
        for cp in copies:
            cp.wait()


def _mm_tn(name, a, b, out_dtype, silu_a=False):
    rows, na = a.shape
    nb = b.shape[1]
    tr = _divisor(rows, 1536, 16)
    tn = _divisor(na, 1536, 128)
    steps = rows // tr

    def body(a_ref, b_ref, o_ref, acc):
        t = pl.program_id(1)

        @pl.when(t == 0)
        def _():
            acc[...] = jnp.zeros_like(acc)

        av = a_ref[...]
        if silu_a:
            av = av.astype(_F32)
            av = av * jax.nn.sigmoid(av)
        acc[...] += lax.dot_general(av.astype(_MM), b_ref[...].astype(_MM), (((0,), (0,)), ((), ())),
                                    preferred_element_type=_F32)

        @pl.when(t == steps - 1)
        def _():
            o_ref[...] = acc[...].astype(out_dtype)

    return pl.pallas_call(
        body, name=name, grid=(na // tn, steps),
        out_shape=jax.ShapeDtypeStruct((na, nb), out_dtype),
        in_specs=[pl.BlockSpec((tr, tn), lambda j, t: (t, j)), pl.BlockSpec((tr, nb), lambda j, t: (t, 0))],
        out_specs=pl.BlockSpec((tn, nb), lambda j, t: (j, 0)),
        scratch_shapes=[pltpu.VMEM((tn, nb), _F32)],
        compiler_params=_params(("arbitrary", "arbitrary")),
    )(a, b)


def _mod_forward(name, cond, w_mod):
    depth, d, n = w_mod.shape
    rows = cond.shape[0]

    def body(c_ref, w_ref, o_ref):
        cv = c_ref[...]
        a = (cv * jax.nn.sigmoid(cv)).astype(_MM)
        o_ref[...] = jnp.dot(a, w_ref[...].astype(_MM), preferred_element_type=_F32)

    return pl.pallas_call(
        body, name=name, grid=(depth,),
        out_shape=jax.ShapeDtypeStruct((depth, rows, n), _F32),
        in_specs=[pl.BlockSpec((rows, d), lambda l: (0, 0)), pl.BlockSpec((None, d, n), lambda l: (l, 0, 0))],
        out_specs=pl.BlockSpec((None, rows, n), lambda l: (l, 0, 0)),
        compiler_params=_params(("arbitrary",)),
    )(cond, w_mod)


def _mod_backward_cond(name, dm, w_mod):
    depth, d, n = w_mod.shape
    rows = dm.shape[1]

    def body(g_ref, w_ref, o_ref):
        @pl.when(pl.program_id(0) == 0)
        def _():
            o_ref[...] = jnp.zeros_like(o_ref)

        o_ref[...] += lax.dot_general(g_ref[...].astype(_MM), w_ref[...].astype(_MM), (((1,), (1,)), ((), ())),
                                      preferred_element_type=_F32)

    return pl.pallas_call(
        body, name=name, grid=(depth,),
        out_shape=jax.ShapeDtypeStruct((rows, d), _F32),
        in_specs=[pl.BlockSpec((None, rows, n), lambda l: (l, 0, 0)), pl.BlockSpec((None, d, n), lambda l: (l, 0, 0))],
        out_specs=pl.BlockSpec((rows, d), lambda l: (0, 0)),
        compiler_params=_params(("arbitrary",)),
    )(dm, w_mod)


def _param_spec(arr, kind, n_lat):
    if kind == "stream":
        return pl.BlockSpec((None,) + arr.shape[1:], lambda i: (i // n_lat, 0, 0))
    return pl.BlockSpec(arr.shape, lambda i: (0,) * arr.ndim)


def _rowfn(name, fn, params, xs, outs, n_lat):
    rows = xs[0][0].shape[0]
    np_, nx = len(params), len(xs)
    stored = [(dt, ws) for dt, ws in outs if dt is not None]

    def body(*refs):
        ps = [r[...].astype(_F32) for r in refs[:np_]]
        xv = [r[...].astype(_F32) for r in refs[np_:np_ + nx]]
        pieces = fn(ps, xv)
        o_refs = iter(refs[np_ + nx:])
        k = 0
        for dt, ws in outs:
            o_ref = next(o_refs) if dt is not None else None
            off = 0
            for w in ws:
                if o_ref is not None:
                    o_ref[:, off:off + w] = pieces[k].astype(dt)
                off += w
                k += 1

    return pl.pallas_call(
        body, name=name, grid=(rows // _TM,),
        out_shape=[jax.ShapeDtypeStruct((rows, sum(ws)), dt) for dt, ws in stored],
        in_specs=[_param_spec(a, kind, n_lat) for a, kind in params]
        + [pl.BlockSpec((_TM, w), lambda i, cb=cb: (i, cb)) for _, w, cb in xs],
        out_specs=[pl.BlockSpec((_TM, sum(ws)), lambda i: (i, 0)) for _, ws in stored],
        compiler_params=_params(("arbitrary",)),
    )(*[a for a, _ in params], *[a for a, _, _ in xs])


def _rowfn_bwd(name, fn, params, xs, diff, douts, dx_outs, n_lat, residual=None):
    rows = xs[0][0].shape[0]
    np_, nx, nd = len(params), len(xs), len(douts)
    nres = 0 if residual is None else 1
    nt = rows // _TM

    def body(*refs):
        i = pl.program_id(0)
        ps = [r[...].astype(_F32) for r in refs[:np_]]
        xv = [r[...].astype(_F32) for r in refs[np_:np_ + nx]]
        d_refs = refs[np_ + nx:np_ + nx + nd]
        res_ref = refs[np_ + nx + nd] if nres else None
        dp_refs = refs[np_ + nx + nd + nres:np_ + nx + nd + nres + np_]
        dx_refs = refs[np_ + nx + nd + nres + np_:]

        def f(ps_, xd):
            full = list(xv)
            for j, v in zip(diff, xd):
                full[j] = v
            return fn(ps_, full)

        _, vjp = jax.vjp(f, ps, [xv[j] for j in diff])
        cts = []
        for d_ref, (_, ws) in zip(d_refs, douts):
            off = 0
            for w in ws:
                cts.append(d_ref[:, off:off + w].astype(_F32))
                off += w
        dps, dxd = vjp(cts)
        grads = dict(zip(diff, dxd))
        for (dp_ref, (_, kind)), dp in zip(zip(dp_refs, params), dps):
            first = (i == 0) | (i == n_lat) if kind == "stream" else i == 0

            @pl.when(first)
            def _(dp_ref=dp_ref):
                dp_ref[...] = jnp.zeros_like(dp_ref)

            dp_ref[...] += dp
        for n_out, (dx_ref, (dt, idxs)) in enumerate(zip(dx_refs, dx_outs)):
            off = 0
            for j in idxs:
                w = xs[j][1]
                g = grads[j]
                if res_ref is not None and n_out == 0 and off == 0:
                    g = g + res_ref[...].astype(_F32)
                dx_ref[:, off:off + w] = g.astype(dt)
                off += w

    dp_shapes = [jax.ShapeDtypeStruct(a.shape, _F32) for a, _ in params]
    dx_shapes = [jax.ShapeDtypeStruct((rows, sum(xs[j][1] for j in idxs)), dt) for dt, idxs in dx_outs]
    in_specs = ([_param_spec(a, kind, n_lat) for a, kind in params]
                + [pl.BlockSpec((_TM, w), lambda i, cb=cb: (i, cb)) for _, w, cb in xs]
                + [pl.BlockSpec((_TM, sum(ws)), lambda i: (i, 0)) for _, ws in douts])
    operands = [a for a, _ in params] + [a for a, _, _ in xs] + [a for a, _ in douts]
    if nres:
        in_specs.append(pl.BlockSpec((_TM, residual.shape[1]), lambda i: (i, 0)))
        operands.append(residual)
    res = pl.pallas_call(
        body, name=name, grid=(nt,),
        out_shape=dp_shapes + dx_shapes,
        in_specs=in_specs,
        out_specs=[_param_spec(a, kind, n_lat) for a, kind in params]
        + [pl.BlockSpec((_TM, s.shape[1]), lambda i: (i, 0)) for s in dx_shapes],
        compiler_params=_params(("arbitrary",)),
    )(*operands)
    return list(res[:np_]), list(res[np_:])


def _f_norm_mod(ps, xs):
    gain, shift, scale = ps
    (x,) = xs
    y = x * lax.rsqrt(jnp.mean(x * x, axis=-1, keepdims=True) + _EPS) * gain
    return [y * (1.0 + scale) + shift]


def _f_premix(ps, xs):
    g_b, g_c, u_a, v_b, gate_b = xs
    return [g_b, g_c * u_a, v_b * jax.nn.sigmoid(gate_b)]


def _f_postmix(ps, xs):
    bias, ln_g, ln_b = ps
    g_b, cv_a, cv_b = xs
    u = cv_b + bias
    mu = jnp.mean(u, axis=-1, keepdims=True)
    var = jnp.mean(jnp.square(u - mu), axis=-1, keepdims=True)
    y = (u - mu) * lax.rsqrt(var + _EPS) * ln_g + ln_b
    return [g_b * cv_a, y * jax.nn.sigmoid(y)]


def _f_glu(ps, xs):
    a, g = xs
    return [g * jax.nn.sigmoid(g) * a]


def _accumulate_params(dp_refs, kinds, dps, i, n_lat):
    for dp_ref, kind, dp in zip(dp_refs, kinds, dps):
        first = (i == 0) | (i == n_lat) if kind == "stream" else i == 0

        @pl.when(first)
        def _(dp_ref=dp_ref):
            dp_ref[...] = jnp.zeros_like(dp_ref)

        dp_ref[...] += dp


def _row_spec(width):
    return pl.BlockSpec((_TM, width), lambda i: (i, 0))


def _weight_scratch(shape, wg):
    return [pltpu.VMEM(shape, wg.dtype), pltpu.SemaphoreType.DMA((_NDEV,))]


def _norm_proj(name, x, params, wg, off, r, n_lat):
    rows, d = x.shape
    n = _NDEV * r
    chunk = _divisor(n, 512, 128)

    def body(g_ref, sh_ref, sc_ref, x_ref, wg_ref, h_ref, p_ref, wbuf, sems):
        _load_weight(wg_ref, wbuf, sems, off, r, pl.program_id(0))
        (h,) = _f_norm_mod([g_ref[...], sh_ref[...], sc_ref[...]], [x_ref[...]])
        hb = h.astype(_MM)
        h_ref[...] = hb.astype(h_ref.dtype)
        for j in range(n // chunk):
            p_ref[:, j * chunk:(j + 1) * chunk] = lax.dot_general(
                hb, wbuf[j * chunk:(j + 1) * chunk, :], (((1,), (1,)), ((), ())),
                preferred_element_type=_F32).astype(p_ref.dtype)

    return pl.pallas_call(
        body, name=name, grid=(rows // _TM,),
        out_shape=[jax.ShapeDtypeStruct((rows, d), _ACT), jax.ShapeDtypeStruct((rows, n), _ACT)],
        in_specs=[_param_spec(a, kind, n_lat) for a, kind in params] + [_row_spec(d), pl.BlockSpec(memory_space=pl.ANY)],
        out_specs=[_row_spec(d), _row_spec(n)],
        scratch_shapes=_weight_scratch((n, d), wg),
        compiler_params=_params(("arbitrary",)),
    )(*[a for a, _ in params], x, wg)


def _proj_residual(name, a, wg, off, r, x, gate, n_lat):
    rows, kdim = a.shape
    d = x.shape[1]
    assert kdim == _NDEV * r

    def body(g_ref, a_ref, x_ref, wg_ref, y_ref, o_ref, wbuf, sems):
        _load_weight(wg_ref, wbuf, sems, off, r, pl.program_id(0))
        y = jnp.dot(a_ref[...].astype(_MM), wbuf[...], preferred_element_type=_F32)
        y_ref[...] = y.astype(y_ref.dtype)
        o_ref[...] = x_ref[...] + g_ref[...] * y

    return pl.pallas_call(
        body, name=name, grid=(rows // _TM,),
        out_shape=[jax.ShapeDtypeStruct((rows, d), _ACT), jax.ShapeDtypeStruct((rows, d), _F32)],
        in_specs=[_param_spec(gate, "stream", n_lat), _row_spec(kdim), _row_spec(d), pl.BlockSpec(memory_space=pl.ANY)],
        out_specs=[_row_spec(d), _row_spec(d)],
        scratch_shapes=_weight_scratch((kdim, d), wg),
        compiler_params=_params(("arbitrary",)),
    )(gate, a, x, wg)


def _gate_proj_bwd(name, dx, y, gate, wg, off, r, n_lat):
    rows, d = dx.shape
    n = _NDEV * r
    chunk = _divisor(n, 512, 128)

    def body(g_ref, dx_ref, y_ref, wg_ref, dg_ref, dy_ref, dz_ref, wbuf, sems):
        i = pl.program_id(0)
        _load_weight(wg_ref, wbuf, sems, off, r, i)
        dxv = dx_ref[...]
        _accumulate_params([dg_ref], ["stream"], [jnp.sum(dxv * y_ref[...].astype(_F32), axis=0, keepdims=True)],
                           i, n_lat)
        dy = (g_ref[...] * dxv).astype(_MM)
        dy_ref[...] = dy.astype(dy_ref.dtype)
        for j in range(n // chunk):
            dz_ref[:, j * chunk:(j + 1) * chunk] = lax.dot_general(
                dy, wbuf[j * chunk:(j + 1) * chunk, :], (((1,), (1,)), ((), ())),
                preferred_element_type=_F32).astype(dz_ref.dtype)

    return pl.pallas_call(
        body, name=name, grid=(rows // _TM,),
        out_shape=[jax.ShapeDtypeStruct(gate.shape, _F32), jax.ShapeDtypeStruct((rows, d), _ACT),
                   jax.ShapeDtypeStruct((rows, n), _ACT)],
        in_specs=[_param_spec(gate, "stream", n_lat), _row_spec(d), _row_spec(d), pl.BlockSpec(memory_space=pl.ANY)],
        out_specs=[_param_spec(gate, "stream", n_lat), _row_spec(d), _row_spec(n)],
        scratch_shapes=_weight_scratch((n, d), wg),
        compiler_params=_params(("arbitrary",)),
    )(gate, dx, y, wg)


def _proj_norm_bwd(name, dp, wg, off, r, x, params, dx_in, n_lat):
    rows, kdim = dp.shape
    d = x.shape[1]
    assert kdim == _NDEV * r
    kinds = [kind for _, kind in params]

    def body(g_ref, sh_ref, sc_ref, dp_ref, x_ref, dxin_ref, wg_ref, dg_ref, dsh_ref, dsc_ref, dx_ref, wbuf, sems):
        i = pl.program_id(0)
        _load_weight(wg_ref, wbuf, sems, off, r, i)
        dh = jnp.dot(dp_ref[...].astype(_MM), wbuf[...], preferred_element_type=_F32)
        _, vjp = jax.vjp(lambda ps, xv: _f_norm_mod(ps, [xv]), [g_ref[...], sh_ref[...], sc_ref[...]], x_ref[...])
        dps, dxn = vjp([dh])
        _accumulate_params([dg_ref, dsh_ref, dsc_ref], kinds, dps, i, n_lat)
        dx_ref[...] = dxin_ref[...] + dxn

    specs = [_param_spec(a, kind, n_lat) for a, kind in params]
    res = pl.pallas_call(
        body, name=name, grid=(rows // _TM,),
        out_shape=[jax.ShapeDtypeStruct(a.shape, _F32) for a, _ in params] + [jax.ShapeDtypeStruct((rows, d), _F32)],
        in_specs=specs + [_row_spec(kdim), _row_spec(d), _row_spec(d), pl.BlockSpec(memory_space=pl.ANY)],
        out_specs=specs + [_row_spec(d)],
        scratch_shapes=_weight_scratch((kdim, d), wg),
        compiler_params=_params(("arbitrary",)),
    )(*[a for a, _ in params], dp, x, dx_in, wg)
    return list(res[:3]), res[3]


def _conv_halo_specs(width, cb0, n_rows):
    per = _TM // _HALO
    last = n_rows // _HALO - 1
    return [
        pl.BlockSpec((_TM, width), lambda i, j: (i, cb0 + j)),
        pl.BlockSpec((_HALO, width), lambda i, j: (jnp.maximum(i * per - 1, 0), cb0 + j)),
        pl.BlockSpec((_HALO, width), lambda i, j: (jnp.minimum((i + 1) * per, last), cb0 + j)),
    ]


def _conv_window(main_ref, prev_ref, next_ref, r0, cols, i, n_lat, nt):
    if r0 == 0:
        has_prev = (i != 0) & (i != n_lat)
        head = jnp.where(has_prev, prev_ref[:, cols].astype(_F32), 0.0)
    else:
        head = main_ref[r0 - _HALO:r0, cols].astype(_F32)
    if r0 + _CONV_ROWS == _TM:
        has_next = (i != n_lat - 1) & (i != nt - 1)
        tail = jnp.where(has_next, next_ref[:, cols].astype(_F32), 0.0)
    else:
        tail = main_ref[r0 + _CONV_ROWS:r0 + _CONV_ROWS + _HALO, cols].astype(_F32)
    return jnp.concatenate([head, main_ref[r0:r0 + _CONV_ROWS, cols].astype(_F32), tail], axis=0)


def _shifted(win, offset):
    n = win.shape[0]
    rolled = win if offset == 0 else pltpu.roll(win, (-offset) % n, 0)
    return rolled[_HALO:_HALO + _CONV_ROWS]


def _dwconv(name, x, cb0, channels, taps, out_dtype, n_lat):
    rows = x.shape[0]
    ktaps = taps.shape[0]
    half = ktaps // 2
    width = _divisor(channels, 1536, 128)
    assert (cb0 * channels) % width == 0
    cb0 = cb0 * channels // width
    nt = rows // _TM

    def body(main_ref, prev_ref, next_ref, taps_ref, o_ref):
        i = pl.program_id(0)

        def chunk(j, carry):
            cols = pl.ds(pl.multiple_of(j * _LANES, _LANES), _LANES)
            for r0 in range(0, _TM, _CONV_ROWS):
                win = _conv_window(main_ref, prev_ref, next_ref, r0, cols, i, n_lat, nt)
                acc = taps_ref[0:1, cols] * _shifted(win, -half)
                for k in range(1, ktaps):
                    acc = acc + taps_ref[k:k + 1, cols] * _shifted(win, k - half)
                o_ref[r0:r0 + _CONV_ROWS, cols] = acc.astype(out_dtype)
            return carry

        lax.fori_loop(0, width // _LANES, chunk, 0)

    return pl.pallas_call(
        body, name=name, grid=(nt, channels // width),
        out_shape=jax.ShapeDtypeStruct((rows, channels), out_dtype),
        in_specs=_conv_halo_specs(width, cb0, rows) + [pl.BlockSpec((ktaps, width), lambda i, j: (0, j))],
        out_specs=pl.BlockSpec((_TM, width), lambda i, j: (i, j)),
        compiler_params=_params(("arbitrary", "arbitrary")),
    )(x, x, x, taps)


def _dwconv_wgrad(name, dy, x, cb0, channels, ktaps, n_lat):
    rows = x.shape[0]
    half = ktaps // 2
    width = _divisor(channels, 1536, 128)
    cb0 = cb0 * channels // width
    nt = rows // _TM

    def body(dy_ref, main_ref, prev_ref, next_ref, o_ref):
        i = pl.program_id(1)

        @pl.when(i == 0)
        def _():
            o_ref[...] = jnp.zeros_like(o_ref)

        def chunk(j, carry):
            cols = pl.ds(pl.multiple_of(j * _LANES, _LANES), _LANES)
            for r0 in range(0, _TM, _CONV_ROWS):
                dyv = dy_ref[r0:r0 + _CONV_ROWS, cols].astype(_F32)
                win = _conv_window(main_ref, prev_ref, next_ref, r0, cols, i, n_lat, nt)
                for k in range(ktaps):
                    o_ref[k:k + 1, cols] += jnp.sum(dyv * _shifted(win, k - half), axis=0, keepdims=True)
            return carry

        lax.fori_loop(0, width // _LANES, chunk, 0)

    per = _TM // _HALO
    last = rows // _HALO - 1
    return pl.pallas_call(
        body, name=name, grid=(channels // width, nt),
        out_shape=jax.ShapeDtypeStruct((ktaps, channels), _F32),
        in_specs=[
            pl.BlockSpec((_TM, width), lambda j, i: (i, j)),
            pl.BlockSpec((_TM, width), lambda j, i: (i, cb0 + j)),
            pl.BlockSpec((_HALO, width), lambda j, i: (jnp.maximum(i * per - 1, 0), cb0 + j)),
            pl.BlockSpec((_HALO, width), lambda j, i: (jnp.minimum((i + 1) * per, last), cb0 + j)),
        ],
        out_specs=pl.BlockSpec((ktaps, width), lambda j, i: (0, j)),
        compiler_params=_params(("arbitrary", "arbitrary")),
    )(dy, x, x, x)


def _rope_tables(length, ctx_len):
    t = jnp.arange(length)
    row = (t // _GRID_W).astype(_F32)
    col = (t % _GRID_W).astype(_F32)
    n_freq = _HEAD_DIM // 4
    inv_freq = _ROPE_THETA ** (-jnp.arange(n_freq, dtype=_F32) / n_freq)
    ang = jnp.concatenate([row[:, None] * inv_freq, col[:, None] * inv_freq], axis=-1)
    cos, sin = jnp.cos(ang), jnp.sin(ang)
    cos = jnp.concatenate([cos, jnp.ones((ctx_len, _HEAD_DIM // 2), _F32)], axis=0)
    sin = jnp.concatenate([sin, jnp.zeros((ctx_len, _HEAD_DIM // 2), _F32)], axis=0)
    return jnp.tile(cos, (1, 4)), jnp.tile(jnp.concatenate([-sin, sin], axis=-1), (1, 2))


def _rotate(v, cos_ref, sin_ref):
    width = v.shape[1]
    reps = width // 128
    cos = jnp.tile(cos_ref[...], (1, reps))
    sin = jnp.tile(sin_ref[...], (1, reps))
    return v * cos, sin, width


def _partner(v):
    width = v.shape[1]
    half = _HEAD_DIM // 2
    lane = lax.broadcasted_iota(jnp.int32, v.shape, 1)
    return jnp.where(lane % _HEAD_DIM < half, pltpu.roll(v, width - half, 1), pltpu.roll(v, half, 1))


def _rope_fwd(name, p, cos, sin, q_w, kv_w):
    rows, width = p.shape
    scale = _HEAD_DIM ** -0.5

    def body(p_ref, cos_ref, sin_ref, q_ref, k_ref):
        v = p_ref[:, :q_w + kv_w].astype(_F32)
        vc, s, _ = _rotate(v, cos_ref, sin_ref)
        y = vc + _partner(v) * s
        q_ref[...] = (y[:, :q_w] * scale).astype(q_ref.dtype)
        k_ref[...] = y[:, q_w:].astype(k_ref.dtype)

    return pl.pallas_call(
        body, name=name, grid=(rows // _TM,),
        out_shape=[jax.ShapeDtypeStruct((rows, q_w), _ACT), jax.ShapeDtypeStruct((rows, kv_w), _ACT)],
        in_specs=[pl.BlockSpec((_TM, width), lambda i: (i, 0)), pl.BlockSpec((_TM, 128), lambda i: (i, 0)),
                  pl.BlockSpec((_TM, 128), lambda i: (i, 0))],
        out_specs=[pl.BlockSpec((_TM, q_w), lambda i: (i, 0)), pl.BlockSpec((_TM, kv_w), lambda i: (i, 0))],
        compiler_params=_params(("arbitrary",)),
    )(p, cos, sin)


def _rope_bwd(name, dq, dk, dv, cos, sin):
    rows, q_w = dq.shape
    kv_w = dk.shape[1]
    scale = _HEAD_DIM ** -0.5

    def body(dq_ref, dk_ref, dv_ref, cos_ref, sin_ref, o_ref):
        dy = jnp.concatenate([dq_ref[...].astype(_F32) * scale, dk_ref[...].astype(_F32)], axis=1)
        dyc, s, _ = _rotate(dy, cos_ref, sin_ref)
        o_ref[:, :q_w + kv_w] = (dyc + _partner(dy * s)).astype(o_ref.dtype)
        o_ref[:, q_w + kv_w:] = dv_ref[...].astype(o_ref.dtype)

    return pl.pallas_call(
        body, name=name, grid=(rows // _TM,),
        out_shape=jax.ShapeDtypeStruct((rows, q_w + 2 * kv_w), _ACT),
        in_specs=[pl.BlockSpec((_TM, q_w), lambda i: (i, 0)), pl.BlockSpec((_TM, kv_w), lambda i: (i, 0)),
                  pl.BlockSpec((_TM, kv_w), lambda i: (i, 0)), pl.BlockSpec((_TM, 128), lambda i: (i, 0)),
                  pl.BlockSpec((_TM, 128), lambda i: (i, 0))],
        out_specs=pl.BlockSpec((_TM, q_w + 2 * kv_w), lambda i: (i, 0)),
        compiler_params=_params(("arbitrary",)),
    )(dq, dk, dv, cos, sin)


def _attn_window(i, n_lat, length):
    wk = _TM + 2 * _WINDOW
    start = pl.multiple_of(jnp.clip(i * _TM - _WINDOW, 0, length - wk), _WINDOW)
    q_pos = i * _TM + lax.broadcasted_iota(jnp.int32, (_TM, wk), 0)
    k_pos = start + lax.broadcasted_iota(jnp.int32, (_TM, wk), 1)
    mask = (jnp.abs(q_pos - k_pos) <= _WINDOW) & (i < n_lat)
    return start, wk, mask


def _softmax_parts(q, k_loc, k_ctx, mask, sink):
    nt = (((1,), (1,)), ((), ()))
    s_loc = jnp.where(mask, lax.dot_general(q, k_loc, nt, preferred_element_type=_F32), _NEG_INF)
    s_ctx = lax.dot_general(q, k_ctx, nt, preferred_element_type=_F32)
    m = jnp.maximum(jnp.maximum(jnp.max(s_loc, axis=-1, keepdims=True), jnp.max(s_ctx, axis=-1, keepdims=True)),
                    sink)
    e_loc = jnp.exp(s_loc - m)
    e_ctx = jnp.exp(s_ctx - m)
    e_sink = jnp.exp(sink - m)
    inv = 1.0 / (jnp.sum(e_loc, axis=-1, keepdims=True) + jnp.sum(e_ctx, axis=-1, keepdims=True) + e_sink)
    return e_loc * inv, e_ctx * inv, e_sink * inv


def _attn_fwd(name, q, k, p, sinks, n_lat, length, kv_w):
    rows, q_w = q.shape
    ctx_len = rows - length
    n_heads = q_w // _HEAD_DIM
    n_kv = kv_w // _HEAD_DIM
    group = n_heads // n_kv
    v_cb = p.shape[1] // kv_w - 1
    hd = _HEAD_DIM

    def body(q_ref, k_ref, v_ref, sink_ref, o_ref):
        i = pl.program_id(0)
        start, wk, mask = _attn_window(i, n_lat, length)
        for h in range(n_kv):
            k_loc = k_ref[pl.ds(start, wk), h * hd:(h + 1) * hd]
            v_loc = v_ref[pl.ds(start, wk), h * hd:(h + 1) * hd]
            k_ctx = k_ref[length:length + ctx_len, h * hd:(h + 1) * hd]
            v_ctx = v_ref[length:length + ctx_len, h * hd:(h + 1) * hd]
            for g in range(group):
                n = h * group + g
                p_loc, p_ctx, _ = _softmax_parts(q_ref[:, n * hd:(n + 1) * hd], k_loc, k_ctx, mask,
                                                 sink_ref[:, n:n + 1])
                o = (jnp.dot(p_loc.astype(_MM), v_loc, preferred_element_type=_F32)
                     + jnp.dot(p_ctx.astype(_MM), v_ctx, preferred_element_type=_F32))
                o_ref[:, n * hd:(n + 1) * hd] = o.astype(o_ref.dtype)

    return pl.pallas_call(
        body, name=name, grid=(rows // _TM,),
        out_shape=jax.ShapeDtypeStruct((rows, q_w), _ACT),
        in_specs=[pl.BlockSpec((_TM, q_w), lambda i: (i, 0)), pl.BlockSpec((rows, kv_w), lambda i: (0, 0)),
                  pl.BlockSpec((rows, kv_w), lambda i: (0, v_cb)), pl.BlockSpec((1, n_heads), lambda i: (0, 0))],
        out_specs=pl.BlockSpec((_TM, q_w), lambda i: (i, 0)),
        compiler_params=_params(("arbitrary",)),
    )(q, k, p, sinks)


def _attn_bwd(name, q, k, p, sinks, do, n_lat, length, kv_w):
    rows, q_w = q.shape
    ctx_len = rows - length
    n_heads = q_w // _HEAD_DIM
    n_kv = kv_w // _HEAD_DIM
    group = n_heads // n_kv
    v_cb = p.shape[1] // kv_w - 1
    hd = _HEAD_DIM
    nt_dims = (((1,), (1,)), ((), ()))
    tn_dims = (((0,), (0,)), ((), ()))

    def body(q_ref, k_ref, v_ref, sink_ref, do_ref, dq_ref, dk_out, dv_out, ds_ref, dk_ref, dv_ref, out_sems):
        i = pl.program_id(0)

        @pl.when(i == 0)
        def _():
            dk_ref[...] = jnp.zeros_like(dk_ref)
            dv_ref[...] = jnp.zeros_like(dv_ref)
            ds_ref[...] = jnp.zeros_like(ds_ref)

        start, wk, mask = _attn_window(i, n_lat, length)
        head_lane = lax.broadcasted_iota(jnp.int32, (1, n_heads), 1)
        dsink = jnp.zeros((1, n_heads), _F32)
        for h in range(n_kv):
            cols = slice(h * hd, (h + 1) * hd)
            k_loc = k_ref[pl.ds(start, wk), cols]
            v_loc = v_ref[pl.ds(start, wk), cols]
            k_ctx = k_ref[length:length + ctx_len, cols]
            v_ctx = v_ref[length:length + ctx_len, cols]
            dk_loc = jnp.zeros((wk, hd), _F32)
            dv_loc = jnp.zeros((wk, hd), _F32)
            dk_ctx = jnp.zeros((ctx_len, hd), _F32)
            dv_ctx = jnp.zeros((ctx_len, hd), _F32)
            for g in range(group):
                n = h * group + g
                qh = q_ref[:, n * hd:(n + 1) * hd]
                doh = do_ref[:, n * hd:(n + 1) * hd].astype(_MM)
                p_loc, p_ctx, p_sink = _softmax_parts(qh, k_loc, k_ctx, mask, sink_ref[:, n:n + 1])
                dp_loc = lax.dot_general(doh, v_loc, nt_dims, preferred_element_type=_F32)
                dp_ctx = lax.dot_general(doh, v_ctx, nt_dims, preferred_element_type=_F32)
                dsum = (jnp.sum(p_loc * dp_loc, axis=-1, keepdims=True)
                        + jnp.sum(p_ctx * dp_ctx, axis=-1, keepdims=True))
                ds_loc = (p_loc * (dp_loc - dsum)).astype(_MM)
                ds_ctx = (p_ctx * (dp_ctx - dsum)).astype(_MM)
                dsink = dsink + jnp.where(head_lane == n, -jnp.sum(p_sink * dsum), 0.0)
                dq = (jnp.dot(ds_loc, k_loc, preferred_element_type=_F32)
                      + jnp.dot(ds_ctx, k_ctx, preferred_element_type=_F32))
                dq_ref[:, n * hd:(n + 1) * hd] = dq.astype(dq_ref.dtype)
                dk_loc += lax.dot_general(ds_loc, qh, tn_dims, preferred_element_type=_F32)
                dk_ctx += lax.dot_general(ds_ctx, qh, tn_dims, preferred_element_type=_F32)
                dv_loc += lax.dot_general(p_loc.astype(_MM), doh, tn_dims, preferred_element_type=_F32)
                dv_ctx += lax.dot_general(p_ctx.astype(_MM), doh, tn_dims, preferred_element_type=_F32)
            dk_ref[pl.ds(start, wk), cols] += dk_loc
            dv_ref[pl.ds(start, wk), cols] += dv_loc
            dk_ref[length:length + ctx_len, cols] += dk_ctx
            dv_ref[length:length + ctx_len, cols] += dv_ctx
        ds_ref[...] += dsink

        @pl.when(i == rows // _TM - 1)
        def _():
            copies = [pltpu.make_async_copy(dk_ref, dk_out, out_sems.at[0]),
                      pltpu.make_async_copy(dv_ref, dv_out, out_sems.at[1])]
            for cp in copies:
                cp.start()
            for cp in copies:
                cp.wait()

    return pl.pallas_call(
        body, name=name, grid=(rows // _TM,),
        out_shape=[jax.ShapeDtypeStruct((rows, q_w), _ACT), jax.ShapeDtypeStruct((rows, kv_w), _F32),
                   jax.ShapeDtypeStruct((rows, kv_w), _F32), jax.ShapeDtypeStruct((1, n_heads), _F32)],
        in_specs=[pl.BlockSpec((_TM, q_w), lambda i: (i, 0)), pl.BlockSpec((rows, kv_w), lambda i: (0, 0)),
                  pl.BlockSpec((rows, kv_w), lambda i: (0, v_cb)), pl.BlockSpec((1, n_heads), lambda i: (0, 0)),
                  pl.BlockSpec((_TM, q_w), lambda i: (i, 0))],
        out_specs=[pl.BlockSpec((_TM, q_w), lambda i: (i, 0)), pl.BlockSpec(memory_space=pl.ANY),
                   pl.BlockSpec(memory_space=pl.ANY), pl.BlockSpec((1, n_heads), lambda i: (0, 0))],
        scratch_shapes=[pltpu.VMEM((rows, kv_w), _F32), pltpu.VMEM((rows, kv_w), _F32),
                        pltpu.SemaphoreType.DMA((2,))],
        compiler_params=_params(("arbitrary",)),
    )(q, k, p, sinks, do)


def _loss_head(name, xs, gain, target, n_lat):
    rows, d = xs.shape

    def body(x_ref, g_ref, t_ref, loss_ref, dg_ref, dx_ref):
        i = pl.program_id(0)

        @pl.when(i == 0)
        def _():
            loss_ref[...] = jnp.zeros_like(loss_ref)
            dg_ref[...] = jnp.zeros_like(dg_ref)

        @pl.when(i < n_lat)
        def _():
            tv = t_ref[...]

            def f(gain_, x):
                y = x * lax.rsqrt(jnp.mean(x * x, axis=-1, keepdims=True) + _EPS) * gain_
                return 0.5 * jnp.sum(jnp.mean(jnp.square(y - tv), axis=-1))

            val, (dg, dx) = jax.value_and_grad(f, argnums=(0, 1))(g_ref[...], x_ref[...])
            loss_ref[...] += val
            dg_ref[...] += dg
            dx_ref[...] = dx

        @pl.when(i >= n_lat)
        def _():
            dx_ref[...] = jnp.zeros_like(dx_ref)

    return pl.pallas_call(
        body, name=name, grid=(rows // _TM,),
        out_shape=[jax.ShapeDtypeStruct((1, 128), _F32), jax.ShapeDtypeStruct((1, d), _F32),
                   jax.ShapeDtypeStruct((rows, d), _F32)],
        in_specs=[pl.BlockSpec((_TM, d), lambda i: (i, 0)), pl.BlockSpec((1, d), lambda i: (0, 0)),
                  pl.BlockSpec((_TM, d), lambda i: (jnp.minimum(i, n_lat - 1), 0))],
        out_specs=[pl.BlockSpec((1, 128), lambda i: (0, 0)), pl.BlockSpec((1, d), lambda i: (0, 0)),
                   pl.BlockSpec((_TM, d), lambda i: (i, 0))],
        compiler_params=_params(("arbitrary",)),
    )(xs, gain, target)


def _adamw(name, w, g, m, v):
    rows, cols = w.shape
    tr = _divisor(rows, 512, 8)
    b1, b2 = _ADAM["b1"], _ADAM["b2"]
    c1 = 1.0 - b1 ** _ADAM["step"]
    c2 = 1.0 - b2 ** _ADAM["step"]

    def body(w_ref, g_ref, m_ref, v_ref, d_ref, nm_ref, nv_ref):
        gv = g_ref[...]
        nm = b1 * m_ref[...] + (1.0 - b1) * gv
        nv = b2 * v_ref[...] + (1.0 - b2) * jnp.square(gv)
        d_ref[...] = -_ADAM["lr"] * ((nm / c1) / (jnp.sqrt(nv / c2) + _ADAM["eps"]) + _ADAM["wd"] * w_ref[...])
        nm_ref[...] = nm
        nv_ref[...] = nv

    spec = pl.BlockSpec((tr, cols), lambda i: (i, 0))
    return pl.pallas_call(
        body, name=name, grid=(rows // tr,),
        out_shape=[jax.ShapeDtypeStruct((rows, cols), _F32)] * 3,
        in_specs=[spec] * 4, out_specs=[spec] * 3,
        compiler_params=_params(("arbitrary",)),
    )(w, g, m, v)


def _pack(arrays, cols=128):
    flat = jnp.concatenate([a.reshape(-1).astype(_F32) for a in arrays])
    pad = (-flat.shape[0]) % (64 * cols)
    return jnp.pad(flat, (0, pad)).reshape(-1, cols)


def _unpack(flat, shapes):
    out, off = [], 0
    for s in shapes:
        n = 1
        for d in s:
            n *= d
        out.append(flat[..., off:off + n].reshape(flat.shape[:-1] + tuple(s)))
        off += n
    return out


def _gather_channels(parts):
    moved = jnp.moveaxis(parts, 0, -2)
    return moved.reshape(moved.shape[:-2] + (moved.shape[-2] * moved.shape[-1],))


def kernel(x, c, ctx, c_ctx, w_mod, b_mod, norm_mix, norm_ffn, w_in_ab, conv_a, conv_b, conv_b_bias, ln_b_gain, ln_b_bias, w_out_ab, w_qkv, w_o, sinks, w_up, w_conv_ffn, w_down, final_norm, loss_target, m_c_ctx, m_w_mod, m_b_mod, m_norm_mix, m_norm_ffn, m_w_in_ab, m_conv_a, m_conv_b, m_conv_b_bias, m_ln_b_gain, m_ln_b_bias, m_w_out_ab, m_w_qkv, m_w_o, m_sinks, m_w_up, m_w_conv_ffn, m_w_down, m_final_norm, v_c_ctx, v_w_mod, v_b_mod, v_norm_mix, v_norm_ffn, v_w_in_ab, v_conv_a, v_conv_b, v_conv_b_bias, v_ln_b_gain, v_ln_b_bias, v_w_out_ab, v_w_qkv, v_w_o, v_sinks, v_w_up, v_w_conv_ffn, v_w_down, v_final_norm):
    args = dict(locals())
    weight_names = ["c_ctx", "w_mod", "b_mod", "norm_mix", "norm_ffn", "w_in_ab", "conv_a", "conv_b", "conv_b_bias",
                    "ln_b_gain", "ln_b_bias", "w_out_ab", "w_qkv", "w_o", "sinks", "w_up", "w_conv_ffn", "w_down",
                    "final_norm"]
    length, d = x.shape[1], x.shape[2]
    ctx_len = ctx.shape[1]
    assert ctx_len == _TM and length % _TM == 0 and x.shape[0] == 1
    n_lat = length // _TM
    depth = w_mod.shape[0]
    n_even, n_odd = w_in_ab.shape[0], w_qkv.shape[0]
    a_w = conv_a.shape[2] * _NDEV
    b_w = conv_b.shape[2] * _NDEV
    assert a_w == b_w
    q_w = w_o.shape[1] * _NDEV
    kv_w = (w_qkv.shape[2] * _NDEV - q_w) // 2
    d_ff = w_down.shape[1] * _NDEV
    dev = 4 * lax.axis_index("x") + 2 * lax.axis_index("y") + lax.axis_index("c")

    small_shapes = [c.shape[1:], conv_a.shape, conv_b.shape, w_conv_ffn.shape]
    g0 = _all_gather_small("gather_small_params", _pack([c, conv_a, conv_b, w_conv_ffn]))
    c_parts, ca_parts, cb_parts, cf_parts = _unpack(g0.reshape(_NDEV, -1), small_shapes)
    conv_a_full = _gather_channels(ca_parts)
    conv_b_full = _gather_channels(cb_parts)
    conv_f_full = _gather_channels(cf_parts)

    cond = jnp.concatenate([c_parts, c_ctx[None], jnp.zeros((16 - _NDEV - 1, d), _F32)], axis=0)
    mod_cols = w_mod.shape[2]
    m_shard = _mod_forward("mod_forward", cond, w_mod)
    m_all = _all_gather_small("gather_mod", m_shard.reshape(depth * 16, mod_cols))
    m_all = jnp.moveaxis(m_all.reshape(_NDEV, depth, 16, mod_cols), 0, 2).reshape(depth, 16, _NDEV * mod_cols)
    m_all = m_all + b_mod[:, None, :]
    m_lat = lax.dynamic_index_in_dim(m_all, dev, axis=1, keepdims=False)
    m_ctx = m_all[:, _NDEV]

    def mod_vec(l, j):
        return jnp.stack([m_lat[l, j * d:(j + 1) * d], m_ctx[l, j * d:(j + 1) * d]])[:, None, :]

    def layer_mats(l):
        if l % 2 == 0:
            first = [("in", l // 2, w_in_ab[l // 2].T), ("out", l // 2, w_out_ab[l // 2])]
        else:
            first = [("qkv", l // 2, w_qkv[l // 2].T), ("o", l // 2, w_o[l // 2])]
        return first + [("up", l, w_up[l].T), ("down", l, w_down[l])]

    piece_mats = [layer_mats(0)[:2], layer_mats(0)[2:]] + [layer_mats(l) for l in range(1, depth)]
    n_pieces = len(piece_mats)
    slab_off, slab_r, piece_of, piece_keys, piece_rows, slabs = {}, {}, {}, [], [], []
    for p, mats in enumerate(piece_mats):
        off, keys = 0, []
        for fam, idx, mat in mats:
            slab_off[fam, idx], slab_r[fam], piece_of[fam, idx] = off, mat.shape[0], p
            off += mat.shape[0]
            keys.append((fam, idx))
        piece_keys.append(keys)
        piece_rows.append(off)
        slabs.append(jnp.concatenate([mat.astype(_MM) for _, _, mat in mats], axis=0))
    wgs = [_all_gather_slab("gather_weights_0", slabs[0])] + [None] * (n_pieces - 1)
    gathers, start_token = [None] * n_pieces, jnp.zeros((), _F32)
    for p in range(1, n_pieces):
        land = _fill_own_slot(f"gather_fill_{p}", [slabs[p]], [0], piece_rows[p])
        gathers[p] = _exchange_start(f"gather_start_{p}", [slabs[p]], land, [0])
        start_token = start_token + gathers[p][-1][0, 0]

    def wref(fam, idx):
        return wgs[piece_of[fam, idx]], slab_off[fam, idx], slab_r[fam]

    cos, sin = _rope_tables(length, ctx_len)
    xs = jnp.concatenate([x[0], ctx[0]], axis=0)

    def full(a):
        return (a.reshape(1, -1), "full")

    saved = []
    for l in range(depth):
        sv = {"x_in": xs}
        if l > 0:
            wgs[1 + l] = _exchange_wait(f"gather_wait_{1 + l}", gathers[1 + l], [0], xs)
        gain1 = full(norm_mix[l] + start_token) if l == 0 else full(norm_mix[l])
        norm1 = [gain1, (mod_vec(l, 0), "stream"), (mod_vec(l, 1), "stream")]
        if l % 2 == 0:
            e = l // 2
            h1, p = _norm_proj(f"proj_in_{l}", xs, norm1, *wref("in", e), n_lat)
            (qm,) = _rowfn(f"premix_{l}", _f_premix, [], [(p, a_w, j) for j in range(5)],
                           [(None, [a_w]), (_ACT, [a_w, b_w])], n_lat)
            cv_a = _dwconv(f"conv_a_{l}", qm, 0, a_w, conv_a_full[e], _ACT, n_lat)
            cv_b = _dwconv(f"conv_b_{l}", qm, 1, b_w, conv_b_full[e], _ACT, n_lat)
            post_params = [full(conv_b_bias[e]), full(ln_b_gain[e]), full(ln_b_bias[e])]
            (z,) = _rowfn(f"postmix_{l}", _f_postmix, post_params, [(p, a_w, 0), (cv_a, a_w, 0), (cv_b, b_w, 0)],
                          [(_ACT, [a_w, b_w])], n_lat)
            y1, xs = _proj_residual(f"proj_out_{l}", z, *wref("out", e), xs, mod_vec(l, 2), n_lat)
            sv.update(p=p, qm=qm, cv_a=cv_a, cv_b=cv_b, z=z)
        else:
            o = l // 2
            h1, p = _norm_proj(f"proj_qkv_{l}", xs, norm1, *wref("qkv", o), n_lat)
            qr, kr = _rope_fwd(f"rope_{l}", p, cos, sin, q_w, kv_w)
            sk = sinks[o].reshape(1, -1)
            z = _attn_fwd(f"attn_{l}", qr, kr, p, sk, n_lat, length, kv_w)
            y1, xs = _proj_residual(f"proj_o_{l}", z, *wref("o", o), xs, mod_vec(l, 2), n_lat)
            sv.update(p=p, qr=qr, kr=kr, z=z)
        sv.update(h1=h1, y1=y1, x_mid=xs)
        if l == 0:
            wgs[1] = _exchange_wait("gather_wait_1", gathers[1], [0], xs)
        norm2 = [full(norm_ffn[l]), (mod_vec(l, 3), "stream"), (mod_vec(l, 4), "stream")]
        h2, pu = _norm_proj(f"proj_up_{l}", xs, norm2, *wref("up", l), n_lat)
        u = _dwconv(f"conv_ffn_{l}", pu, 0, 2 * d_ff, conv_f_full[l], _ACT, n_lat)
        (f,) = _rowfn(f"glu_{l}", _f_glu, [], [(u, d_ff, 0), (u, d_ff, 1)], [(_ACT, [d_ff])], n_lat)
        y2, xs = _proj_residual(f"proj_down_{l}", f, *wref("down", l), xs, mod_vec(l, 5), n_lat)
        sv.update(h2=h2, pu=pu, u=u, f=f, y2=y2)
        saved.append(sv)

    loss_part, d_final_norm, dxs = _loss_head("loss_head", xs, final_norm.reshape(1, -1), loss_target[0], n_lat)
    loss = lax.psum(loss_part[0, 0], ("x", "y", "c"))

    wgrads = {}
    d_mod = [[None] * 6 for _ in range(depth)]
    d_norm_mix, d_norm_ffn = [None] * depth, [None] * depth
    d_conv_a, d_conv_b = [None] * n_even, [None] * n_even
    d_bias, d_ln_g, d_ln_b = [None] * n_even, [None] * n_even, [None] * n_even
    d_sinks = [None] * n_odd
    d_conv_f = [None] * depth
    exchanges, recvs, exchange_token = [None] * n_pieces, [None] * n_pieces, jnp.zeros((), _F32)

    def piece_parts(p):
        return ([wgrads[key].reshape(_NDEV, slab_r[key[0]], d) for key in piece_keys[p]],
                [slab_off[key] for key in piece_keys[p]])

    def start_exchange(p):
        parts, offsets = piece_parts(p)
        land = _fill_own_slot(f"exchange_fill_{p}", parts, offsets, piece_rows[p])
        exchanges[p] = _exchange_start(f"exchange_start_{p}", parts, land, offsets)
        return exchanges[p][-1][0, 0]

    for l in reversed(range(depth)):
        sv = saved[l]
        d_mod[l][5], dy2, df = _gate_proj_bwd(f"bwd_down_{l}", dxs, sv["y2"], mod_vec(l, 5) + exchange_token,
                                              *wref("down", l), n_lat)
        wgrads["down", l] = _mm_tn(f"wgrad_down_{l}", sv["f"], dy2, _ACT)
        _, (du,) = _rowfn_bwd(f"glu_bwd_{l}", _f_glu, [], [(sv["u"], d_ff, 0), (sv["u"], d_ff, 1)], [0, 1],
                              [(df, [d_ff])], [(_ACT, [0, 1])], n_lat)
        d_conv_f[l] = _dwconv_wgrad(f"conv_ffn_wgrad_{l}", du, sv["pu"], 0, 2 * d_ff, 3, n_lat)
        dpu = _dwconv(f"conv_ffn_bwd_{l}", du, 0, 2 * d_ff, conv_f_full[l][::-1], _ACT, n_lat)
        wgrads["up", l] = _mm_tn(f"wgrad_up_{l}", dpu, sv["h2"], _ACT)
        norm2 = [full(norm_ffn[l]), (mod_vec(l, 3), "stream"), (mod_vec(l, 4), "stream")]
        (dgain, dsh, dsc), dxs = _proj_norm_bwd(f"bwd_up_{l}", dpu, *wref("up", l), sv["x_mid"], norm2, dxs, n_lat)
        d_norm_ffn[l], d_mod[l][3], d_mod[l][4] = dgain, dsh, dsc
        mix_token = start_exchange(1) if l == 0 else jnp.zeros((), _F32)
        gate1 = mod_vec(l, 2) + mix_token
        norm1 = [full(norm_mix[l]), (mod_vec(l, 0), "stream"), (mod_vec(l, 1), "stream")]
        if l % 2 == 0:
            e = l // 2
            d_mod[l][2], dy1, dz = _gate_proj_bwd(f"bwd_out_{l}", dxs, sv["y1"], gate1, *wref("out", e), n_lat)
            wgrads["out", e] = _mm_tn(f"wgrad_out_{l}", sv["z"], dy1, _ACT)
            post_params = [full(conv_b_bias[e]), full(ln_b_gain[e]), full(ln_b_bias[e])]
            (dbias, dlg, dlb), (dgb, dcv_a, dcv_b) = _rowfn_bwd(
                f"postmix_bwd_{l}", _f_postmix, post_params,
                [(sv["p"], a_w, 0), (sv["cv_a"], a_w, 0), (sv["cv_b"], b_w, 0)], [0, 1, 2], [(dz, [a_w, b_w])],
                [(_ACT, [0]), (_ACT, [1]), (_ACT, [2])], n_lat)
            d_bias[e], d_ln_g[e], d_ln_b[e] = dbias, dlg, dlb
            d_conv_a[e] = _dwconv_wgrad(f"conv_a_wgrad_{l}", dcv_a, sv["qm"], 0, a_w, conv_a_full.shape[1], n_lat)
            d_conv_b[e] = _dwconv_wgrad(f"conv_b_wgrad_{l}", dcv_b, sv["qm"], 1, b_w, conv_b_full.shape[1], n_lat)
            dq_a = _dwconv(f"conv_a_bwd_{l}", dcv_a, 0, a_w, conv_a_full[e][::-1], _ACT, n_lat)
            dq_b = _dwconv(f"conv_b_bwd_{l}", dcv_b, 0, b_w, conv_b_full[e][::-1], _ACT, n_lat)
            _, (dp,) = _rowfn_bwd(f"premix_bwd_{l}", _f_premix, [], [(sv["p"], a_w, j) for j in range(5)],
                                  [0, 1, 2, 3, 4], [(dgb, [a_w]), (dq_a, [a_w]), (dq_b, [b_w])],
                                  [(_ACT, [0, 1, 2, 3, 4])], n_lat)
            wgrads["in", e] = _mm_tn(f"wgrad_in_{l}", dp, sv["h1"], _ACT)
            (dgain, dsh, dsc), dxs = _proj_norm_bwd(f"bwd_in_{l}", dp, *wref("in", e), sv["x_in"], norm1, dxs, n_lat)
        else:
            o = l // 2
            d_mod[l][2], dy1, dz = _gate_proj_bwd(f"bwd_o_{l}", dxs, sv["y1"], gate1, *wref("o", o), n_lat)
            wgrads["o", o] = _mm_tn(f"wgrad_o_{l}", sv["z"], dy1, _ACT)
            sk = sinks[o].reshape(1, -1)
            dqr, dkr, dv, dsk = _attn_bwd(f"attn_bwd_{l}", sv["qr"], sv["kr"], sv["p"], sk, dz, n_lat, length, kv_w)
            d_sinks[o] = dsk
            dp = _rope_bwd(f"rope_bwd_{l}", dqr, dkr, dv, cos, sin)
            wgrads["qkv", o] = _mm_tn(f"wgrad_qkv_{l}", dp, sv["h1"], _ACT)
            (dgain, dsh, dsc), dxs = _proj_norm_bwd(f"bwd_qkv_{l}", dp, *wref("qkv", o), sv["x_in"], norm1, dxs, n_lat)
        d_norm_mix[l], d_mod[l][0], d_mod[l][1] = dgain, dsh, dsc
        if l > 0:
            exchange_token = start_exchange(1 + l)
        else:
            parts, offsets = piece_parts(0)
            recvs[0] = _exchange_grads("exchange_weight_grads_0", parts, offsets, piece_rows[0])
    grad_x = dxs[:length][None]

    gsums = []
    for p in range(n_pieces):
        if p > 0:
            recvs[p] = _exchange_wait(f"exchange_wait_{p}", exchanges[p], piece_parts(p)[1], dxs)
        gsums.append(_sum_slots(f"sum_weight_grads_{p}", recvs[p]))

    def slab_grad(fam, count, transposed):
        mats = [gsums[piece_of[fam, i]][slab_off[fam, i]:slab_off[fam, i] + slab_r[fam]] for i in range(count)]
        return jnp.stack([m_.T if transposed else m_ for m_ in mats])

    grads = {
        "w_in_ab": slab_grad("in", n_even, True), "w_qkv": slab_grad("qkv", n_odd, True),
        "w_up": slab_grad("up", depth, True), "w_out_ab": slab_grad("out", n_even, False),
        "w_o": slab_grad("o", n_odd, False), "w_down": slab_grad("down", depth, False),
    }

    dm_dev = jnp.stack([jnp.concatenate([d_mod[l][j][:, 0, :] for j in range(6)], axis=-1)
                        for l in range(depth)])
    small_grads = [dm_dev, jnp.stack(d_norm_mix), jnp.stack(d_norm_ffn), jnp.stack(d_conv_a), jnp.stack(d_conv_b),
                   jnp.stack(d_bias), jnp.stack(d_ln_g), jnp.stack(d_ln_b), jnp.stack(d_sinks), jnp.stack(d_conv_f),
                   d_final_norm]
    sg_shapes = [a.shape for a in small_grads]
    sg_all = _all_gather_small("gather_small_grads", _pack(small_grads))
    sg_sum = _sum_slots("sum_small_grads", sg_all)
    (dm_sum, g_norm_mix, g_norm_ffn, g_conv_a, g_conv_b, g_bias, g_ln_g, g_ln_b, g_sinks, g_conv_f,
     g_final_norm) = _unpack(sg_sum.reshape(-1), sg_shapes)
    dm_each = _unpack(sg_all.reshape(_NDEV, -1), sg_shapes[:1])[0]

    def my_channels(a):
        width = a.shape[-1] // _NDEV
        return lax.dynamic_slice_in_dim(a, dev * width, width, axis=a.ndim - 1)

    grads["b_mod"] = dm_sum[:, 0] + dm_sum[:, 1]
    grads["norm_mix"] = g_norm_mix.reshape(depth, d)
    grads["norm_ffn"] = g_norm_ffn.reshape(depth, d)
    grads["conv_a"] = my_channels(g_conv_a)
    grads["conv_b"] = my_channels(g_conv_b)
    grads["conv_b_bias"] = g_bias.reshape(n_even, b_w)
    grads["ln_b_gain"] = g_ln_g.reshape(n_even, b_w)
    grads["ln_b_bias"] = g_ln_b.reshape(n_even, b_w)
    grads["sinks"] = g_sinks.reshape(n_odd, -1)
    grads["w_conv_ffn"] = my_channels(g_conv_f)
    grads["final_norm"] = g_final_norm.reshape(d)

    dm_rows = jnp.concatenate([jnp.moveaxis(dm_each[:, :, 0], 0, 1), dm_sum[:, 1:2],
                               jnp.zeros((depth, 16 - _NDEV - 1, 6 * d), _F32)], axis=1)
    dm_mine = my_channels(dm_rows)
    grads["w_mod"] = jnp.stack([_mm_tn(f"wgrad_mod_{l}", cond, dm_mine[l], _F32, silu_a=True) for l in range(depth)])
    dcond = _mod_backward_cond("mod_backward_cond", dm_mine, w_mod)
    dcond_all = _all_gather_small("gather_dcond", dcond)
    dcond_sum = _sum_slots("sum_dcond", dcond_all)[_NDEV]
    sg = jax.nn.sigmoid(c_ctx)
    grads["c_ctx"] = dcond_sum * (sg * (1.0 + c_ctx * (1.0 - sg)))

    big = ["w_mod", "w_in_ab", "w_out_ab", "w_qkv", "w_o", "w_up", "w_down"]
    small = [n for n in weight_names if n not in big]
    delta, new_m, new_v = {}, {}, {}
    for n in big:
        w = args[n]
        two_d = lambda a: a.reshape(-1, w.shape[-1])
        dl, nm, nv = _adamw(f"adamw_{n}", two_d(w), two_d(grads[n]), two_d(args["m_" + n]), two_d(args["v_" + n]))
        delta[n], new_m[n], new_v[n] = dl.reshape(w.shape), nm.reshape(w.shape), nv.reshape(w.shape)
    shapes = [args[n].shape for n in small]
    grads = {n: grads[n].reshape(args[n].shape) for n in grads}
    dl, nm, nv = _adamw("adamw_small", _pack([args[n] for n in small]), _pack([grads[n] for n in small]),
                        _pack([args["m_" + n] for n in small]), _pack([args["v_" + n] for n in small]))
    for res, packed in ((delta, dl), (new_m, nm), (new_v, nv)):
        for n, a in zip(small, _unpack(packed.reshape(-1), shapes)):
            res[n] = a

    return (loss, grad_x, *[grads[n] for n in weight_names], *[delta[n] for n in weight_names],
            *[new_m[n] for n in weight_names], *[new_v[n] for n in weight_names])
```

```python
import functools

import jax
import jax.numpy as jnp
from jax import lax
from jax.experimental import pallas as pl
from jax.experimental.pallas import tpu as pltpu

_F32 = jnp.float32
_MM = jnp.bfloat16
_ACT = jnp.bfloat16
_TM = 256
_HALO = 16
_LANES = 128
_CONV_ROWS = 128
_NDEV = 8
_HEAD_DIM = 64
_WINDOW = 128
_GRID_W = 64
_ROPE_THETA = 10000.0
_EPS = 1e-6
_NEG_INF = -1e30
_VMEM_LIMIT = 56 * 1024 * 1024
_ADAM = dict(lr=0.001, b1=0.9, b2=0.999, eps=1e-08, wd=0.01, step=10)
_MESH = pl.DeviceIdType.MESH


def _params(sem=None):
    return pltpu.CompilerParams(dimension_semantics=sem, vmem_limit_bytes=_VMEM_LIMIT)


def _divisor(n, cap, mult):
    if n <= cap:
        return n
    for d in range(cap - cap % mult, 0, -mult):
        if n % d == 0:
            return d
    raise ValueError(f"no tile for {n}")


def _my_coords():
    return lax.axis_index("x"), lax.axis_index("y"), lax.axis_index("c")


def _peer(k):
    x, y, c = _my_coords()
    px = 1 - x if k & 4 else x
    py = 1 - y if k & 2 else y
    pc = 1 - c if k & 1 else c
    return (px, py, pc), 4 * px + 2 * py + pc


def _all_gather_small(name, v):
    rows, cols = v.shape

    def body(v_ref, out_ref, send_sems, recv_sems):
        x, y, c = _my_coords()
        me = 4 * x + 2 * y + c
        out_ref[me] = v_ref[...]
        sends = []
        for k in range(1, _NDEV):
            peer, _ = _peer(k)
            cp = pltpu.make_async_remote_copy(
                src_ref=v_ref, dst_ref=out_ref.at[me], send_sem=send_sems.at[k - 1], recv_sem=recv_sems.at[k - 1],
                device_id=peer, device_id_type=_MESH)
            cp.start()
            sends.append(cp)
        for k in range(1, _NDEV):
            peer, pid = _peer(k)
            pltpu.make_async_remote_copy(
                src_ref=v_ref, dst_ref=out_ref.at[pid], send_sem=send_sems.at[k - 1], recv_sem=recv_sems.at[k - 1],
                device_id=peer, device_id_type=_MESH).wait_recv()
        for cp in sends:
            cp.wait_send()

    return pl.pallas_call(
        body, name=name,
        out_shape=jax.ShapeDtypeStruct((_NDEV, rows, cols), v.dtype),
        in_specs=[pl.BlockSpec(memory_space=pltpu.VMEM)],
        out_specs=pl.BlockSpec(memory_space=pltpu.VMEM),
        scratch_shapes=[pltpu.SemaphoreType.DMA((_NDEV - 1,)), pltpu.SemaphoreType.DMA((_NDEV - 1,))],
        compiler_params=pltpu.CompilerParams(vmem_limit_bytes=_VMEM_LIMIT),
    )(v)


def _sum_slots(name, v):
    _, rows, cols = v.shape
    tr = _divisor(rows, 512, 16)

    def body(v_ref, o_ref):
        acc = v_ref[0].astype(_F32)
        for e in range(1, _NDEV):
            acc = acc + v_ref[e].astype(_F32)
        o_ref[...] = acc

    return pl.pallas_call(
        body, name=name, grid=(rows // tr,),
        out_shape=jax.ShapeDtypeStruct((rows, cols), _F32),
        in_specs=[pl.BlockSpec((_NDEV, tr, cols), lambda i: (0, i, 0))],
        out_specs=pl.BlockSpec((tr, cols), lambda i: (i, 0)),
        compiler_params=_params(("arbitrary",)),
    )(v)


_HBM_SPEC = pl.BlockSpec(memory_space=pltpu.HBM)
_SEM_SPEC = pl.BlockSpec(memory_space=pltpu.SEMAPHORE)
_EFFECT = pltpu.SideEffectType.DATAFLOW_SIDE_EFFECTING


def _in_hbm(a):
    return pltpu.with_memory_space_constraint(a, pltpu.HBM)


def _block_for(ref, device):
    return ref if len(ref.shape) == 2 else ref.at[device]


def _fill_own_slot(name, srcs, offsets, total_rows):
    n = len(srcs)
    cols = srcs[0].shape[-1]

    def body(*refs):
        src_refs, out_ref, bufs, sems = refs[:n], refs[n], refs[n + 1:2 * n + 1], refs[2 * n + 1]
        x, y, c = _my_coords()
        me = 4 * x + 2 * y + c
        loads = [pltpu.make_async_copy(_block_for(src_refs[m], me), bufs[m], sems.at[0, m]) for m in range(n)]
        stores = [pltpu.make_async_copy(bufs[m], out_ref.at[me, pl.ds(offsets[m], srcs[m].shape[-2]), :],
                                        sems.at[1, m]) for m in range(n)]
        for copies in (loads, stores):
            for cp in copies:
                cp.start()
            for cp in copies:
                cp.wait()

    return pl.pallas_call(
        body, name=name,
        out_shape=jax.ShapeDtypeStruct((_NDEV, total_rows, cols), srcs[0].dtype),
        in_specs=[pl.BlockSpec(memory_space=pl.ANY)] * n,
        out_specs=pl.BlockSpec(memory_space=pl.ANY),
        scratch_shapes=[pltpu.VMEM(s.shape[-2:], s.dtype) for s in srcs] + [pltpu.SemaphoreType.DMA((2, n))],
        compiler_params=pltpu.CompilerParams(vmem_limit_bytes=_VMEM_LIMIT),
    )(*srcs)


def _exchange_start(name, srcs, land, offsets):
    n = len(srcs)

    def body(*refs):
        src_refs, land_ref = refs[:n], refs[n]
        send_sems, recv_sems, token = refs[n + 1], refs[n + 2], refs[-1]
        x, y, c = _my_coords()
        me = 4 * x + 2 * y + c
        for k in range(1, _NDEV):
            peer, pid = _peer(k)
            for m in range(n):
                pltpu.make_async_remote_copy(
                    src_ref=_block_for(src_refs[m], pid),
                    dst_ref=land_ref.at[me, pl.ds(offsets[m], srcs[m].shape[-2]), :],
                    send_sem=send_sems, recv_sem=recv_sems, device_id=peer, device_id_type=_MESH).start()
        token[...] = jnp.zeros_like(token)

    sems = pltpu.SemaphoreType.DMA(())
    return pl.pallas_call(
        body, name=name,
        out_shape=(sems, sems, *[pltpu.HBM(s.shape, s.dtype) for s in srcs], pltpu.HBM(land.shape, land.dtype),
                   jax.ShapeDtypeStruct((8, 128), _F32)),
        in_specs=[_HBM_SPEC] * (n + 1),
        out_specs=(_SEM_SPEC, _SEM_SPEC, *[_HBM_SPEC] * (n + 1), pl.BlockSpec(memory_space=pltpu.VMEM)),
        input_output_aliases={i: 2 + i for i in range(n + 1)},
        compiler_params=pltpu.CompilerParams(has_side_effects=_EFFECT),
    )(*[_in_hbm(s) for s in srcs], _in_hbm(land))


def _exchange_wait(name, started, offsets, after):
    send_sems, recv_sems = started[0], started[1]
    srcs, land = list(started[2:-2]), started[-2]
    n = len(srcs)

    def body(*refs):
        src_refs, land_ref = refs[:n], refs[n]
        send_sems_, recv_sems_ = refs[n + 1], refs[n + 2]
        others = land_ref.at[pl.ds(0, _NDEV - 1)]
        cp = pltpu.make_async_remote_copy(src_ref=others, dst_ref=others, send_sem=send_sems_, recv_sem=recv_sems_,
                                          device_id=_peer(1)[0], device_id_type=_MESH)
        cp.wait_send()
        cp.wait_recv()

    res = pl.pallas_call(
        body, name=name,
        out_shape=(*[pltpu.HBM(s.shape, s.dtype) for s in srcs], pltpu.HBM(land.shape, land.dtype)),
        in_specs=[_HBM_SPEC] * (n + 1) + [_SEM_SPEC, _SEM_SPEC, pl.BlockSpec(memory_space=pl.ANY)],
        out_specs=tuple([_HBM_SPEC] * (n + 1)),
        input_output_aliases={i: i for i in range(n + 1)},
        compiler_params=pltpu.CompilerParams(has_side_effects=_EFFECT),
    )(*srcs, land, send_sems, recv_sems, after)
    return res[n]


def _load_weight(wg_ref, wbuf, sems, off, r, step):
    @pl.when(step == 0)
    def _():
        copies = [pltpu.make_async_copy(wg_ref.at[e, pl.ds(off, r), :], wbuf.at[pl.ds(e * r, r), :], sems.at[e])
                  for e in range(_NDEV)]
        for cp in copies:
            cp.start()
        for cp in copies:
            cp.wait()


def _mm_tn(name, a, b, out_dtype, silu_a=False):
    rows, na = a.shape
    nb = b.shape[1]
    tr = _divisor(rows, 1536, 16)
    tn = _divisor(na, 1536, 128)
    steps = rows // tr

    def body(a_ref, b_ref, o_ref, acc):
        t = pl.program_id(1)

        @pl.when(t == 0)
        def _():
            acc[...] = jnp.zeros_like(acc)

        av = a_ref[...]
        if silu_a:
            av = av.astype(_F32)
            av = av * jax.nn.sigmoid(av)
        acc[...] += lax.dot_general(av.astype(_MM), b_ref[...].astype(_MM), (((0,), (0,)), ((), ())),
                                    preferred_element_type=_F32)

        @pl.when(t == steps - 1)
        def _():
            o_ref[...] = acc[...].astype(out_dtype)

    return pl.pallas_call(
        body, name=name, grid=(na // tn, steps),
        out_shape=jax.ShapeDtypeStruct((na, nb), out_dtype),
        in_specs=[pl.BlockSpec((tr, tn), lambda j, t: (t, j)), pl.BlockSpec((tr, nb), lambda j, t: (t, 0))],
        out_specs=pl.BlockSpec((tn, nb), lambda j, t: (j, 0)),
        scratch_shapes=[pltpu.VMEM((tn, nb), _F32)],
        compiler_params=_params(("arbitrary", "arbitrary")),
    )(a, b)


def _mod_forward(name, cond, w_mod):
    depth, d, n = w_mod.shape
    rows = cond.shape[0]

    def body(c_ref, w_ref, o_ref):
        cv = c_ref[...]
        a = (cv * jax.nn.sigmoid(cv)).astype(_MM)
        o_ref[...] = jnp.dot(a, w_ref[...].astype(_MM), preferred_element_type=_F32)

    return pl.pallas_call(
        body, name=name, grid=(depth,),
        out_shape=jax.ShapeDtypeStruct((depth, rows, n), _F32),
        in_specs=[pl.BlockSpec((rows, d), lambda l: (0, 0)), pl.BlockSpec((None, d, n), lambda l: (l, 0, 0))],
        out_specs=pl.BlockSpec((None, rows, n), lambda l: (l, 0, 0)),
        compiler_params=_params(("arbitrary",)),
    )(cond, w_mod)


def _mod_backward_cond(name, dm, w_mod):
    depth, d, n = w_mod.shape
    rows = dm.shape[1]

    def body(g_ref, w_ref, o_ref):
        @pl.when(pl.program_id(0) == 0)
        def _():
            o_ref[...] = jnp.zeros_like(o_ref)

        o_ref[...] += lax.dot_general(g_ref[...].astype(_MM), w_ref[...].astype(_MM), (((1,), (1,)), ((), ())),
                                      preferred_element_type=_F32)

    return pl.pallas_call(
        body, name=name, grid=(depth,),
        out_shape=jax.ShapeDtypeStruct((rows, d), _F32),
        in_specs=[pl.BlockSpec((None, rows, n), lambda l: (l, 0, 0)), pl.BlockSpec((None, d, n), lambda l: (l, 0, 0))],
        out_specs=pl.BlockSpec((rows, d), lambda l: (0, 0)),
        compiler_params=_params(("arbitrary",)),
    )(dm, w_mod)


def _param_spec(arr, kind, n_lat):
    if kind == "stream":
        return pl.BlockSpec((None,) + arr.shape[1:], lambda i: (i // n_lat, 0, 0))
    return pl.BlockSpec(arr.shape, lambda i: (0,) * arr.ndim)


def _rowfn(name, fn, params, xs, outs, n_lat):
    rows = xs[0][0].shape[0]
    np_, nx = len(params), len(xs)
    stored = [(dt, ws) for dt, ws in outs if dt is not None]

    def body(*refs):
        ps = [r[...].astype(_F32) for r in refs[:np_]]
        xv = [r[...].astype(_F32) for r in refs[np_:np_ + nx]]
        pieces = fn(ps, xv)
        o_refs = iter(refs[np_ + nx:])
        k = 0
        for dt, ws in outs:
            o_ref = next(o_refs) if dt is not None else None
            off = 0
            for w in ws:
                if o_ref is not None:
                    o_ref[:, off:off + w] = pieces[k].astype(dt)
                off += w
                k += 1

    return pl.pallas_call(
        body, name=name, grid=(rows // _TM,),
        out_shape=[jax.ShapeDtypeStruct((rows, sum(ws)), dt) for dt, ws in stored],
        in_specs=[_param_spec(a, kind, n_lat) for a, kind in params]
        + [pl.BlockSpec((_TM, w), lambda i, cb=cb: (i, cb)) for _, w, cb in xs],
        out_specs=[pl.BlockSpec((_TM, sum(ws)), lambda i: (i, 0)) for _, ws in stored],
        compiler_params=_params(("arbitrary",)),
    )(*[a for a, _ in params], *[a for a, _, _ in xs])


def _rowfn_bwd(name, fn, params, xs, diff, douts, dx_outs, n_lat, residual=None):
    rows = xs[0][0].shape[0]
    np_, nx, nd = len(params), len(xs), len(douts)
    nres = 0 if residual is None else 1
    nt = rows // _TM

    def body(*refs):
        i = pl.program_id(0)
        ps = [r[...].astype(_F32) for r in refs[:np_]]
        xv = [r[...].astype(_F32) for r in refs[np_:np_ + nx]]
        d_refs = refs[np_ + nx:np_ + nx + nd]
        res_ref = refs[np_ + nx + nd] if nres else None
        dp_refs = refs[np_ + nx + nd + nres:np_ + nx + nd + nres + np_]
        dx_refs = refs[np_ + nx + nd + nres + np_:]

        def f(ps_, xd):
            full = list(xv)
            for j, v in zip(diff, xd):
                full[j] = v
            return fn(ps_, full)

        _, vjp = jax.vjp(f, ps, [xv[j] for j in diff])
        cts = []
        for d_ref, (_, ws) in zip(d_refs, douts):
            off = 0
            for w in ws:
                cts.append(d_ref[:, off:off + w].astype(_F32))
                off += w
        dps, dxd = vjp(cts)
        grads = dict(zip(diff, dxd))
        for (dp_ref, (_, kind)), dp in zip(zip(dp_refs, params), dps):
            first = (i == 0) | (i == n_lat) if kind == "stream" else i == 0

            @pl.when(first)
            def _(dp_ref=dp_ref):
                dp_ref[...] = jnp.zeros_like(dp_ref)

            dp_ref[...] += dp
        for n_out, (dx_ref, (dt, idxs)) in enumerate(zip(dx_refs, dx_outs)):
            off = 0
            for j in idxs:
                w = xs[j][1]
                g = grads[j]
                if res_ref is not None and n_out == 0 and off == 0:
                    g = g + res_ref[...].astype(_F32)
                dx_ref[:, off:off + w] = g.astype(dt)
                off += w

    dp_shapes = [jax.ShapeDtypeStruct(a.shape, _F32) for a, _ in params]
    dx_shapes = [jax.ShapeDtypeStruct((rows, sum(xs[j][1] for j in idxs)), dt) for dt, idxs in dx_outs]
    in_specs = ([_param_spec(a, kind, n_lat) for a, kind in params]
                + [pl.BlockSpec((_TM, w), lambda i, cb=cb: (i, cb)) for _, w, cb in xs]
                + [pl.BlockSpec((_TM, sum(ws)), lambda i: (i, 0)) for _, ws in douts])
    operands = [a for a, _ in params] + [a for a, _, _ in xs] + [a for a, _ in douts]
    if nres:
        in_specs.append(pl.BlockSpec((_TM, residual.shape[1]), lambda i: (i, 0)))
        operands.append(residual)
    res = pl.pallas_call(
        body, name=name, grid=(nt,),
        out_shape=dp_shapes + dx_shapes,
        in_specs=in_specs,
        out_specs=[_param_spec(a, kind, n_lat) for a, kind in params]
        + [pl.BlockSpec((_TM, s.shape[1]), lambda i: (i, 0)) for s in dx_shapes],
        compiler_params=_params(("arbitrary",)),
    )(*operands)
    return list(res[:np_]), list(res[np_:])


def _f_norm_mod(ps, xs):
    gain, shift, scale = ps
    (x,) = xs
    y = x * lax.rsqrt(jnp.mean(x * x, axis=-1, keepdims=True) + _EPS) * gain
    return [y * (1.0 + scale) + shift]


def _f_premix(ps, xs):
    g_b, g_c, u_a, v_b, gate_b = xs
    return [g_b, g_c * u_a, v_b * jax.nn.sigmoid(gate_b)]


def _f_postmix(ps, xs):
    bias, ln_g, ln_b = ps
    g_b, cv_a, cv_b = xs
    u = cv_b + bias
    mu = jnp.mean(u, axis=-1, keepdims=True)
    var = jnp.mean(jnp.square(u - mu), axis=-1, keepdims=True)
    y = (u - mu) * lax.rsqrt(var + _EPS) * ln_g + ln_b
    return [g_b * cv_a, y * jax.nn.sigmoid(y)]


def _accumulate_params(dp_refs, kinds, dps, i, n_lat):
    for dp_ref, kind, dp in zip(dp_refs, kinds, dps):
        first = (i == 0) | (i == n_lat) if kind == "stream" else i == 0

        @pl.when(first)
        def _(dp_ref=dp_ref):
            dp_ref[...] = jnp.zeros_like(dp_ref)

        dp_ref[...] += dp


def _row_spec(width):
    return pl.BlockSpec((_TM, width), lambda i: (i, 0))


def _weight_scratch(shape, wg):
    return [pltpu.VMEM(shape, wg.dtype), pltpu.SemaphoreType.DMA((_NDEV,))]


def _norm_proj(name, x, params, wg, off, r, n_lat):
    rows, d = x.shape
    n = _NDEV * r
    chunk = _divisor(n, 512, 128)

    def body(g_ref, sh_ref, sc_ref, x_ref, wg_ref, h_ref, p_ref, wbuf, sems):
        _load_weight(wg_ref, wbuf, sems, off, r, pl.program_id(0))
        (h,) = _f_norm_mod([g_ref[...], sh_ref[...], sc_ref[...]], [x_ref[...]])
        hb = h.astype(_MM)
        h_ref[...] = hb.astype(h_ref.dtype)
        for j in range(n // chunk):
            p_ref[:, j * chunk:(j + 1) * chunk] = lax.dot_general(
                hb, wbuf[j * chunk:(j + 1) * chunk, :], (((1,), (1,)), ((), ())),
                preferred_element_type=_F32).astype(p_ref.dtype)

    return pl.pallas_call(
        body, name=name, grid=(rows // _TM,),
        out_shape=[jax.ShapeDtypeStruct((rows, d), _ACT), jax.ShapeDtypeStruct((rows, n), _ACT)],
        in_specs=[_param_spec(a, kind, n_lat) for a, kind in params] + [_row_spec(d), pl.BlockSpec(memory_space=pl.ANY)],
        out_specs=[_row_spec(d), _row_spec(n)],
        scratch_shapes=_weight_scratch((n, d), wg),
        compiler_params=_params(("arbitrary",)),
    )(*[a for a, _ in params], x, wg)


def _proj_residual(name, a, wg, off, r, x, gate, n_lat):
    rows, kdim = a.shape
    d = x.shape[1]
    assert kdim == _NDEV * r

    def body(g_ref, a_ref, x_ref, wg_ref, y_ref, o_ref, wbuf, sems):
        _load_weight(wg_ref, wbuf, sems, off, r, pl.program_id(0))
        y = jnp.dot(a_ref[...].astype(_MM), wbuf[...], preferred_element_type=_F32)
        y_ref[...] = y.astype(y_ref.dtype)
        o_ref[...] = x_ref[...] + g_ref[...] * y

    return pl.pallas_call(
        body, name=name, grid=(rows // _TM,),
        out_shape=[jax.ShapeDtypeStruct((rows, d), _ACT), jax.ShapeDtypeStruct((rows, d), _F32)],
        in_specs=[_param_spec(gate, "stream", n_lat), _row_spec(kdim), _row_spec(d), pl.BlockSpec(memory_space=pl.ANY)],
        out_specs=[_row_spec(d), _row_spec(d)],
        scratch_shapes=_weight_scratch((kdim, d), wg),
        compiler_params=_params(("arbitrary",)),
    )(gate, a, x, wg)


def _gate_proj_bwd(name, dx, y, gate, wg, off, r, n_lat):
    rows, d = dx.shape
    n = _NDEV * r
    chunk = _divisor(n, 512, 128)

    def body(g_ref, dx_ref, y_ref, wg_ref, dg_ref, dy_ref, dz_ref, wbuf, sems):
        i = pl.program_id(0)
        _load_weight(wg_ref, wbuf, sems, off, r, i)
        dxv = dx_ref[...]
        _accumulate_params([dg_ref], ["stream"], [jnp.sum(dxv * y_ref[...].astype(_F32), axis=0, keepdims=True)],
                           i, n_lat)
        dy = (g_ref[...] * dxv).astype(_MM)
        dy_ref[...] = dy.astype(dy_ref.dtype)
        for j in range(n // chunk):
            dz_ref[:, j * chunk:(j + 1) * chunk] = lax.dot_general(
                dy, wbuf[j * chunk:(j + 1) * chunk, :], (((1,), (1,)), ((), ())),
                preferred_element_type=_F32).astype(dz_ref.dtype)

    return pl.pallas_call(
        body, name=name, grid=(rows // _TM,),
        out_shape=[jax.ShapeDtypeStruct(gate.shape, _F32), jax.ShapeDtypeStruct((rows, d), _ACT),
                   jax.ShapeDtypeStruct((rows, n), _ACT)],
        in_specs=[_param_spec(gate, "stream", n_lat), _row_spec(d), _row_spec(d), pl.BlockSpec(memory_space=pl.ANY)],
        out_specs=[_param_spec(gate, "stream", n_lat), _row_spec(d), _row_spec(n)],
        scratch_shapes=_weight_scratch((n, d), wg),
        compiler_params=_params(("arbitrary",)),
    )(gate, dx, y, wg)


def _proj_norm_bwd(name, dp, wg, off, r, x, params, dx_in, n_lat):
    rows, kdim = dp.shape
    d = x.shape[1]
    assert kdim == _NDEV * r
    kinds = [kind for _, kind in params]

    def body(g_ref, sh_ref, sc_ref, dp_ref, x_ref, dxin_ref, wg_ref, dg_ref, dsh_ref, dsc_ref, dx_ref, wbuf, sems):
        i = pl.program_id(0)
        _load_weight(wg_ref, wbuf, sems, off, r, i)
        dh = jnp.dot(dp_ref[...].astype(_MM), wbuf[...], preferred_element_type=_F32)
        _, vjp = jax.vjp(lambda ps, xv: _f_norm_mod(ps, [xv]), [g_ref[...], sh_ref[...], sc_ref[...]], x_ref[...])
        dps, dxn = vjp([dh])
        _accumulate_params([dg_ref, dsh_ref, dsc_ref], kinds, dps, i, n_lat)
        dx_ref[...] = dxin_ref[...] + dxn

    specs = [_param_spec(a, kind, n_lat) for a, kind in params]
    res = pl.pallas_call(
        body, name=name, grid=(rows // _TM,),
        out_shape=[jax.ShapeDtypeStruct(a.shape, _F32) for a, _ in params] + [jax.ShapeDtypeStruct((rows, d), _F32)],
        in_specs=specs + [_row_spec(kdim), _row_spec(d), _row_spec(d), pl.BlockSpec(memory_space=pl.ANY)],
        out_specs=specs + [_row_spec(d)],
        scratch_shapes=_weight_scratch((kdim, d), wg),
        compiler_params=_params(("arbitrary",)),
    )(*[a for a, _ in params], dp, x, dx_in, wg)
    return list(res[:3]), res[3]


def _conv_halo_specs(width, cb0, n_rows):
    per = _TM // _HALO
    last = n_rows // _HALO - 1
    return [
        pl.BlockSpec((_TM, width), lambda i, j: (i, cb0 + j)),
        pl.BlockSpec((_HALO, width), lambda i, j: (jnp.maximum(i * per - 1, 0), cb0 + j)),
        pl.BlockSpec((_HALO, width), lambda i, j: (jnp.minimum((i + 1) * per, last), cb0 + j)),
    ]


def _conv_window(main_ref, prev_ref, next_ref, r0, cols, i, n_lat, nt):
    if r0 == 0:
        has_prev = (i != 0) & (i != n_lat)
        head = jnp.where(has_prev, prev_ref[:, cols].astype(_F32), 0.0)
    else:
        head = main_ref[r0 - _HALO:r0, cols].astype(_F32)
    if r0 + _CONV_ROWS == _TM:
        has_next = (i != n_lat - 1) & (i != nt - 1)
        tail = jnp.where(has_next, next_ref[:, cols].astype(_F32), 0.0)
    else:
        tail = main_ref[r0 + _CONV_ROWS:r0 + _CONV_ROWS + _HALO, cols].astype(_F32)
    return jnp.concatenate([head, main_ref[r0:r0 + _CONV_ROWS, cols].astype(_F32), tail], axis=0)


def _shifted(win, offset):
    n = win.shape[0]
    rolled = win if offset == 0 else pltpu.roll(win, (-offset) % n, 0)
    return rolled[_HALO:_HALO + _CONV_ROWS]


def _dwconv(name, x, cb0, channels, taps, out_dtype, n_lat):
    rows = x.shape[0]
    ktaps = taps.shape[0]
    half = ktaps // 2
    width = _divisor(channels, 1536, 128)
    assert (cb0 * channels) % width == 0
    cb0 = cb0 * channels // width
    nt = rows // _TM

    def body(main_ref, prev_ref, next_ref, taps_ref, o_ref):
        i = pl.program_id(0)

        def chunk(j, carry):
            cols = pl.ds(pl.multiple_of(j * _LANES, _LANES), _LANES)
            for r0 in range(0, _TM, _CONV_ROWS):
                win = _conv_window(main_ref, prev_ref, next_ref, r0, cols, i, n_lat, nt)
                acc = taps_ref[0:1, cols] * _shifted(win, -half)
                for k in range(1, ktaps):
                    acc = acc + taps_ref[k:k + 1, cols] * _shifted(win, k - half)
                o_ref[r0:r0 + _CONV_ROWS, cols] = acc.astype(out_dtype)
            return carry

        lax.fori_loop(0, width // _LANES, chunk, 0)

    return pl.pallas_call(
        body, name=name, grid=(nt, channels // width),
        out_shape=jax.ShapeDtypeStruct((rows, channels), out_dtype),
        in_specs=_conv_halo_specs(width, cb0, rows) + [pl.BlockSpec((ktaps, width), lambda i, j: (0, j))],
        out_specs=pl.BlockSpec((_TM, width), lambda i, j: (i, j)),
        compiler_params=_params(("arbitrary", "arbitrary")),
    )(x, x, x, taps)


def _dwconv_wgrad(name, dy, x, cb0, channels, ktaps, n_lat):
    rows = x.shape[0]
    half = ktaps // 2
    width = _divisor(channels, 1536, 128)
    cb0 = cb0 * channels // width
    nt = rows // _TM

    def body(dy_ref, main_ref, prev_ref, next_ref, o_ref):
        i = pl.program_id(1)

        @pl.when(i == 0)
        def _():
            o_ref[...] = jnp.zeros_like(o_ref)

        def chunk(j, carry):
            cols = pl.ds(pl.multiple_of(j * _LANES, _LANES), _LANES)
            for r0 in range(0, _TM, _CONV_ROWS):
                dyv = dy_ref[r0:r0 + _CONV_ROWS, cols].astype(_F32)
                win = _conv_window(main_ref, prev_ref, next_ref, r0, cols, i, n_lat, nt)
                for k in range(ktaps):
                    o_ref[k:k + 1, cols] += jnp.sum(dyv * _shifted(win, k - half), axis=0, keepdims=True)
            return carry

        lax.fori_loop(0, width // _LANES, chunk, 0)

    per = _TM // _HALO
    last = rows // _HALO - 1
    return pl.pallas_call(
        body, name=name, grid=(channels // width, nt),
        out_shape=jax.ShapeDtypeStruct((ktaps, channels), _F32),
        in_specs=[
            pl.BlockSpec((_TM, width), lambda j, i: (i, j)),
            pl.BlockSpec((_TM, width), lambda j, i: (i, cb0 + j)),
            pl.BlockSpec((_HALO, width), lambda j, i: (jnp.maximum(i * per - 1, 0), cb0 + j)),
            pl.BlockSpec((_HALO, width), lambda j, i: (jnp.minimum((i + 1) * per, last), cb0 + j)),
        ],
        out_specs=pl.BlockSpec((ktaps, width), lambda j, i: (0, j)),
        compiler_params=_params(("arbitrary", "arbitrary")),
    )(dy, x, x, x)


def _ffn_halo_specs(width, n_rows):
    per = _TM // _HALO
    last = n_rows // _HALO - 1
    return [pl.BlockSpec((_TM, width), lambda i: (i, 0)),
            pl.BlockSpec((_HALO, width), lambda i: (jnp.maximum(i * per - 1, 0), 0)),
            pl.BlockSpec((_HALO, width), lambda i: (jnp.minimum((i + 1) * per, last), 0))]


def _ffn_act(name, pu, taps, n_lat):
    rows, c2 = pu.shape
    ff = c2 // 2
    ktaps = taps.shape[0]
    half = ktaps // 2
    nt = rows // _TM

    def body(main_ref, prev_ref, next_ref, taps_ref, o_ref):
        i = pl.program_id(0)

        def conv(cols, r0):
            win = _conv_window(main_ref, prev_ref, next_ref, r0, cols, i, n_lat, nt)
            acc = taps_ref[0:1, cols] * _shifted(win, -half)
            for k in range(1, ktaps):
                acc = acc + taps_ref[k:k + 1, cols] * _shifted(win, k - half)
            return acc

        def chunk(j, carry):
            c0 = pl.multiple_of(j * _LANES, _LANES)
            cols_a, cols_g = pl.ds(c0, _LANES), pl.ds(pl.multiple_of(ff + c0, _LANES), _LANES)
            for r0 in range(0, _TM, _CONV_ROWS):
                ua, ug = conv(cols_a, r0), conv(cols_g, r0)
                o_ref[r0:r0 + _CONV_ROWS, cols_a] = (ug * jax.nn.sigmoid(ug) * ua).astype(o_ref.dtype)
            return carry

        lax.fori_loop(0, ff // _LANES, chunk, 0)

    return pl.pallas_call(
        body, name=name, grid=(nt,),
        out_shape=jax.ShapeDtypeStruct((rows, ff), _ACT),
        in_specs=_ffn_halo_specs(c2, rows) + [pl.BlockSpec((ktaps, c2), lambda i: (0, 0))],
        out_specs=pl.BlockSpec((_TM, ff), lambda i: (i, 0)),
        compiler_params=_params(("arbitrary",)),
    )(pu, pu, pu, taps)


def _ffn_act_bwd(name, pu, df, taps, n_lat):
    rows, c2 = pu.shape
    ff = c2 // 2
    ktaps = taps.shape[0]
    half = ktaps // 2
    nt = rows // _TM
    inner = slice(_HALO, _HALO + _CONV_ROWS)

    def body(main_ref, prev_ref, next_ref, dmain_ref, dprev_ref, dnext_ref, taps_ref, dpu_ref, dt_ref):
        i = pl.program_id(0)

        @pl.when(i == 0)
        def _():
            dt_ref[...] = jnp.zeros_like(dt_ref)

        def chunk(j, carry):
            c0 = pl.multiple_of(j * _LANES, _LANES)
            cols_a, cols_g = pl.ds(c0, _LANES), pl.ds(pl.multiple_of(ff + c0, _LANES), _LANES)
            for r0 in range(0, _TM, _CONV_ROWS):
                wins = [_conv_window(main_ref, prev_ref, next_ref, r0, cols, i, n_lat, nt) for cols in (cols_a, cols_g)]
                n = wins[0].shape[0]
                shifted = [[w if k == half else pltpu.roll(w, (half - k) % n, 0) for k in range(ktaps)] for w in wins]
                ua, ug = [sum(taps_ref[k:k + 1, cols] * sh[k] for k in range(ktaps))
                          for cols, sh in zip((cols_a, cols_g), shifted)]
                dfw = _conv_window(dmain_ref, dprev_ref, dnext_ref, r0, cols_a, i, n_lat, nt)
                sig = jax.nn.sigmoid(ug)
                d_a = dfw * (ug * sig)
                d_g = dfw * ua * (sig * (1.0 + ug * (1.0 - sig)))
                for du, cols, sh in ((d_a, cols_a, shifted[0]), (d_g, cols_g, shifted[1])):
                    acc = taps_ref[0:1, cols] * _shifted(du, half)
                    for k in range(1, ktaps):
                        acc = acc + taps_ref[k:k + 1, cols] * _shifted(du, half - k)
                    dpu_ref[r0:r0 + _CONV_ROWS, cols] = acc.astype(dpu_ref.dtype)
                    for k in range(ktaps):
                        dt_ref[k:k + 1, cols] += jnp.sum(du[inner] * sh[k][inner], axis=0, keepdims=True)
            return carry

        lax.fori_loop(0, ff // _LANES, chunk, 0)

    return pl.pallas_call(
        body, name=name, grid=(nt,),
        out_shape=[jax.ShapeDtypeStruct((rows, c2), _ACT), jax.ShapeDtypeStruct((ktaps, c2), _F32)],
        in_specs=_ffn_halo_specs(c2, rows) + _ffn_halo_specs(ff, rows) + [pl.BlockSpec((ktaps, c2), lambda i: (0, 0))],
        out_specs=[pl.BlockSpec((_TM, c2), lambda i: (i, 0)), pl.BlockSpec((ktaps, c2), lambda i: (0, 0))],
        compiler_params=_params(("arbitrary",)),
    )(pu, pu, pu, df, df, df, taps)


def _rope_tables(length, ctx_len):
    t = jnp.arange(length)
    row = (t // _GRID_W).astype(_F32)
    col = (t % _GRID_W).astype(_F32)
    n_freq = _HEAD_DIM // 4
    inv_freq = _ROPE_THETA ** (-jnp.arange(n_freq, dtype=_F32) / n_freq)
    ang = jnp.concatenate([row[:, None] * inv_freq, col[:, None] * inv_freq], axis=-1)
    cos, sin = jnp.cos(ang), jnp.sin(ang)
    cos = jnp.concatenate([cos, jnp.ones((ctx_len, _HEAD_DIM // 2), _F32)], axis=0)
    sin = jnp.concatenate([sin, jnp.zeros((ctx_len, _HEAD_DIM // 2), _F32)], axis=0)
    return jnp.tile(cos, (1, 4)), jnp.tile(jnp.concatenate([-sin, sin], axis=-1), (1, 2))


def _rotate(v, cos_ref, sin_ref):
    width = v.shape[1]
    reps = width // 128
    cos = jnp.tile(cos_ref[...], (1, reps))
    sin = jnp.tile(sin_ref[...], (1, reps))
    return v * cos, sin, width


def _partner(v):
    width = v.shape[1]
    half = _HEAD_DIM // 2
    lane = lax.broadcasted_iota(jnp.int32, v.shape, 1)
    return jnp.where(lane % _HEAD_DIM < half, pltpu.roll(v, width - half, 1), pltpu.roll(v, half, 1))


def _rope_fwd(name, p, cos, sin, q_w, kv_w):
    rows, width = p.shape
    scale = _HEAD_DIM ** -0.5

    def body(p_ref, cos_ref, sin_ref, q_ref, k_ref):
        v = p_ref[:, :q_w + kv_w].astype(_F32)
        vc, s, _ = _rotate(v, cos_ref, sin_ref)
        y = vc + _partner(v) * s
        q_ref[...] = (y[:, :q_w] * scale).astype(q_ref.dtype)
        k_ref[...] = y[:, q_w:].astype(k_ref.dtype)

    return pl.pallas_call(
        body, name=name, grid=(rows // _TM,),
        out_shape=[jax.ShapeDtypeStruct((rows, q_w), _ACT), jax.ShapeDtypeStruct((rows, kv_w), _ACT)],
        in_specs=[pl.BlockSpec((_TM, width), lambda i: (i, 0)), pl.BlockSpec((_TM, 128), lambda i: (i, 0)),
                  pl.BlockSpec((_TM, 128), lambda i: (i, 0))],
        out_specs=[pl.BlockSpec((_TM, q_w), lambda i: (i, 0)), pl.BlockSpec((_TM, kv_w), lambda i: (i, 0))],
        compiler_params=_params(("arbitrary",)),
    )(p, cos, sin)


def _rope_bwd(name, dq, dk, dv, cos, sin):
    rows, q_w = dq.shape
    kv_w = dk.shape[1]
    scale = _HEAD_DIM ** -0.5

    def body(dq_ref, dk_ref, dv_ref, cos_ref, sin_ref, o_ref):
        dy = jnp.concatenate([dq_ref[...].astype(_F32) * scale, dk_ref[...].astype(_F32)], axis=1)
        dyc, s, _ = _rotate(dy, cos_ref, sin_ref)
        o_ref[:, :q_w + kv_w] = (dyc + _partner(dy * s)).astype(o_ref.dtype)
        o_ref[:, q_w + kv_w:] = dv_ref[...].astype(o_ref.dtype)

    return pl.pallas_call(
        body, name=name, grid=(rows // _TM,),
        out_shape=jax.ShapeDtypeStruct((rows, q_w + 2 * kv_w), _ACT),
        in_specs=[pl.BlockSpec((_TM, q_w), lambda i: (i, 0)), pl.BlockSpec((_TM, kv_w), lambda i: (i, 0)),
                  pl.BlockSpec((_TM, kv_w), lambda i: (i, 0)), pl.BlockSpec((_TM, 128), lambda i: (i, 0)),
                  pl.BlockSpec((_TM, 128), lambda i: (i, 0))],
        out_specs=pl.BlockSpec((_TM, q_w + 2 * kv_w), lambda i: (i, 0)),
        compiler_params=_params(("arbitrary",)),
    )(dq, dk, dv, cos, sin)


def _attn_window(i, n_lat, length):
    wk = _TM + 2 * _WINDOW
    start = pl.multiple_of(jnp.clip(i * _TM - _WINDOW, 0, length - wk), _WINDOW)
    q_pos = i * _TM + lax.broadcasted_iota(jnp.int32, (_TM, wk), 0)
    k_pos = start + lax.broadcasted_iota(jnp.int32, (_TM, wk), 1)
    mask = (jnp.abs(q_pos - k_pos) <= _WINDOW) & (i < n_lat)
    return start, wk, mask


def _softmax_parts(q, k_loc, k_ctx, mask, sink):
    nt = (((1,), (1,)), ((), ()))
    s_loc = jnp.where(mask, lax.dot_general(q, k_loc, nt, preferred_element_type=_F32), _NEG_INF)
    s_ctx = lax.dot_general(q, k_ctx, nt, preferred_element_type=_F32)
    m = jnp.maximum(jnp.maximum(jnp.max(s_loc, axis=-1, keepdims=True), jnp.max(s_ctx, axis=-1, keepdims=True)),
                    sink)
    e_loc = jnp.exp(s_loc - m)
    e_ctx = jnp.exp(s_ctx - m)
    e_sink = jnp.exp(sink - m)
    inv = 1.0 / (jnp.sum(e_loc, axis=-1, keepdims=True) + jnp.sum(e_ctx, axis=-1, keepdims=True) + e_sink)
    return e_loc * inv, e_ctx * inv, e_sink * inv


def _attn_fwd(name, q, k, p, sinks, n_lat, length, kv_w):
    rows, q_w = q.shape
    ctx_len = rows - length
    n_heads = q_w // _HEAD_DIM
    n_kv = kv_w // _HEAD_DIM
    group = n_heads // n_kv
    v_cb = p.shape[1] // kv_w - 1
    hd = _HEAD_DIM

    def body(q_ref, k_ref, v_ref, sink_ref, o_ref):
        i = pl.program_id(0)
        start, wk, mask = _attn_window(i, n_lat, length)
        for h in range(n_kv):
            k_loc = k_ref[pl.ds(start, wk), h * hd:(h + 1) * hd]
            v_loc = v_ref[pl.ds(start, wk), h * hd:(h + 1) * hd]
            k_ctx = k_ref[length:length + ctx_len, h * hd:(h + 1) * hd]
            v_ctx = v_ref[length:length + ctx_len, h * hd:(h + 1) * hd]
            for g in range(group):
                n = h * group + g
                p_loc, p_ctx, _ = _softmax_parts(q_ref[:, n * hd:(n + 1) * hd], k_loc, k_ctx, mask,
                                                 sink_ref[:, n:n + 1])
                o = (jnp.dot(p_loc.astype(_MM), v_loc, preferred_element_type=_F32)
                     + jnp.dot(p_ctx.astype(_MM), v_ctx, preferred_element_type=_F32))
                o_ref[:, n * hd:(n + 1) * hd] = o.astype(o_ref.dtype)

    return pl.pallas_call(
        body, name=name, grid=(rows // _TM,),
        out_shape=jax.ShapeDtypeStruct((rows, q_w), _ACT),
        in_specs=[pl.BlockSpec((_TM, q_w), lambda i: (i, 0)), pl.BlockSpec((rows, kv_w), lambda i: (0, 0)),
                  pl.BlockSpec((rows, kv_w), lambda i: (0, v_cb)), pl.BlockSpec((1, n_heads), lambda i: (0, 0))],
        out_specs=pl.BlockSpec((_TM, q_w), lambda i: (i, 0)),
        compiler_params=_params(("arbitrary",)),
    )(q, k, p, sinks)


def _attn_bwd(name, q, k, p, sinks, do, n_lat, length, kv_w):
    rows, q_w = q.shape
    ctx_len = rows - length
    n_heads = q_w // _HEAD_DIM
    n_kv = kv_w // _HEAD_DIM
    group = n_heads // n_kv
    v_cb = p.shape[1] // kv_w - 1
    hd = _HEAD_DIM
    nt_dims = (((1,), (1,)), ((), ()))
    tn_dims = (((0,), (0,)), ((), ()))

    def body(q_ref, k_ref, v_ref, sink_ref, do_ref, dq_ref, dk_out, dv_out, ds_ref, dk_ref, dv_ref, out_sems):
        i = pl.program_id(0)

        @pl.when(i == 0)
        def _():
            dk_ref[...] = jnp.zeros_like(dk_ref)
            dv_ref[...] = jnp.zeros_like(dv_ref)
            ds_ref[...] = jnp.zeros_like(ds_ref)

        start, wk, mask = _attn_window(i, n_lat, length)
        head_lane = lax.broadcasted_iota(jnp.int32, (1, n_heads), 1)
        dsink = jnp.zeros((1, n_heads), _F32)
        for h in range(n_kv):
            cols = slice(h * hd, (h + 1) * hd)
            k_loc = k_ref[pl.ds(start, wk), cols]
            v_loc = v_ref[pl.ds(start, wk), cols]
            k_ctx = k_ref[length:length + ctx_len, cols]
            v_ctx = v_ref[length:length + ctx_len, cols]
            dk_loc = jnp.zeros((wk, hd), _F32)
            dv_loc = jnp.zeros((wk, hd), _F32)
            dk_ctx = jnp.zeros((ctx_len, hd), _F32)
            dv_ctx = jnp.zeros((ctx_len, hd), _F32)
            for g in range(group):
                n = h * group + g
                qh = q_ref[:, n * hd:(n + 1) * hd]
                doh = do_ref[:, n * hd:(n + 1) * hd].astype(_MM)
                p_loc, p_ctx, p_sink = _softmax_parts(qh, k_loc, k_ctx, mask, sink_ref[:, n:n + 1])
                dp_loc = lax.dot_general(doh, v_loc, nt_dims, preferred_element_type=_F32)
                dp_ctx = lax.dot_general(doh, v_ctx, nt_dims, preferred_element_type=_F32)
                dsum = (jnp.sum(p_loc * dp_loc, axis=-1, keepdims=True)
                        + jnp.sum(p_ctx * dp_ctx, axis=-1, keepdims=True))
                ds_loc = (p_loc * (dp_loc - dsum)).astype(_MM)
                ds_ctx = (p_ctx * (dp_ctx - dsum)).astype(_MM)
                dsink = dsink + jnp.where(head_lane == n, -jnp.sum(p_sink * dsum), 0.0)
                dq = (jnp.dot(ds_loc, k_loc, preferred_element_type=_F32)
                      + jnp.dot(ds_ctx, k_ctx, preferred_element_type=_F32))
                dq_ref[:, n * hd:(n + 1) * hd] = dq.astype(dq_ref.dtype)
                dk_loc += lax.dot_general(ds_loc, qh, tn_dims, preferred_element_type=_F32)
                dk_ctx += lax.dot_general(ds_ctx, qh, tn_dims, preferred_element_type=_F32)
                dv_loc += lax.dot_general(p_loc.astype(_MM), doh, tn_dims, preferred_element_type=_F32)
                dv_ctx += lax.dot_general(p_ctx.astype(_MM), doh, tn_dims, preferred_element_type=_F32)
            dk_ref[pl.ds(start, wk), cols] += dk_loc
            dv_ref[pl.ds(start, wk), cols] += dv_loc
            dk_ref[length:length + ctx_len, cols] += dk_ctx
            dv_ref[length:length + ctx_len, cols] += dv_ctx
        ds_ref[...] += dsink

        @pl.when(i == rows // _TM - 1)
        def _():
            copies = [pltpu.make_async_copy(dk_ref, dk_out, out_sems.at[0]),
                      pltpu.make_async_copy(dv_ref, dv_out, out_sems.at[1])]
            for cp in copies:
                cp.start()
            for cp in copies:
                cp.wait()

    return pl.pallas_call(
        body, name=name, grid=(rows // _TM,),
        out_shape=[jax.ShapeDtypeStruct((rows, q_w), _ACT), jax.ShapeDtypeStruct((rows, kv_w), _F32),
                   jax.ShapeDtypeStruct((rows, kv_w), _F32), jax.ShapeDtypeStruct((1, n_heads), _F32)],
        in_specs=[pl.BlockSpec((_TM, q_w), lambda i: (i, 0)), pl.BlockSpec((rows, kv_w), lambda i: (0, 0)),
                  pl.BlockSpec((rows, kv_w), lambda i: (0, v_cb)), pl.BlockSpec((1, n_heads), lambda i: (0, 0)),
                  pl.BlockSpec((_TM, q_w), lambda i: (i, 0))],
        out_specs=[pl.BlockSpec((_TM, q_w), lambda i: (i, 0)), pl.BlockSpec(memory_space=pl.ANY),
                   pl.BlockSpec(memory_space=pl.ANY), pl.BlockSpec((1, n_heads), lambda i: (0, 0))],
        scratch_shapes=[pltpu.VMEM((rows, kv_w), _F32), pltpu.VMEM((rows, kv_w), _F32),
                        pltpu.SemaphoreType.DMA((2,))],
        compiler_params=_params(("arbitrary",)),
    )(q, k, p, sinks, do)


def _loss_head(name, xs, gain, target, n_lat):
    rows, d = xs.shape

    def body(x_ref, g_ref, t_ref, loss_ref, dg_ref, dx_ref):
        i = pl.program_id(0)

        @pl.when(i == 0)
        def _():
            loss_ref[...] = jnp.zeros_like(loss_ref)
            dg_ref[...] = jnp.zeros_like(dg_ref)

        @pl.when(i < n_lat)
        def _():
            tv = t_ref[...]

            def f(gain_, x):
                y = x * lax.rsqrt(jnp.mean(x * x, axis=-1, keepdims=True) + _EPS) * gain_
                return 0.5 * jnp.sum(jnp.mean(jnp.square(y - tv), axis=-1))

            val, (dg, dx) = jax.value_and_grad(f, argnums=(0, 1))(g_ref[...], x_ref[...])
            loss_ref[...] += val
            dg_ref[...] += dg
            dx_ref[...] = dx

        @pl.when(i >= n_lat)
        def _():
            dx_ref[...] = jnp.zeros_like(dx_ref)

    return pl.pallas_call(
        body, name=name, grid=(rows // _TM,),
        out_shape=[jax.ShapeDtypeStruct((1, 128), _F32), jax.ShapeDtypeStruct((1, d), _F32),
                   jax.ShapeDtypeStruct((rows, d), _F32)],
        in_specs=[pl.BlockSpec((_TM, d), lambda i: (i, 0)), pl.BlockSpec((1, d), lambda i: (0, 0)),
                  pl.BlockSpec((_TM, d), lambda i: (jnp.minimum(i, n_lat - 1), 0))],
        out_specs=[pl.BlockSpec((1, 128), lambda i: (0, 0)), pl.BlockSpec((1, d), lambda i: (0, 0)),
                   pl.BlockSpec((_TM, d), lambda i: (i, 0))],
        compiler_params=_params(("arbitrary",)),
    )(xs, gain, target)


def _adamw(name, w, g, m, v):
    rows, cols = w.shape
    tr = _divisor(rows, 512, 8)
    b1, b2 = _ADAM["b1"], _ADAM["b2"]
    c1 = 1.0 - b1 ** _ADAM["step"]
    c2 = 1.0 - b2 ** _ADAM["step"]

    def body(w_ref, g_ref, m_ref, v_ref, d_ref, nm_ref, nv_ref):
        gv = g_ref[...]
        nm = b1 * m_ref[...] + (1.0 - b1) * gv
        nv = b2 * v_ref[...] + (1.0 - b2) * jnp.square(gv)
        d_ref[...] = -_ADAM["lr"] * ((nm / c1) / (jnp.sqrt(nv / c2) + _ADAM["eps"]) + _ADAM["wd"] * w_ref[...])
        nm_ref[...] = nm
        nv_ref[...] = nv

    spec = pl.BlockSpec((tr, cols), lambda i: (i, 0))
    return pl.pallas_call(
        body, name=name, grid=(rows // tr,),
        out_shape=[jax.ShapeDtypeStruct((rows, cols), _F32)] * 3,
        in_specs=[spec] * 4, out_specs=[spec] * 3,
        compiler_params=_params(("arbitrary",)),
    )(w, g, m, v)


def _pack(arrays, cols=128):
    flat = jnp.concatenate([a.reshape(-1).astype(_F32) for a in arrays])
    pad = (-flat.shape[0]) % (64 * cols)
    return jnp.pad(flat, (0, pad)).reshape(-1, cols)


def _unpack(flat, shapes):
    out, off = [], 0
    for s in shapes:
        n = 1
        for d in s:
            n *= d
        out.append(flat[..., off:off + n].reshape(flat.shape[:-1] + tuple(s)))
        off += n
    return out


def _gather_channels(parts):
    moved = jnp.moveaxis(parts, 0, -2)
    return moved.reshape(moved.shape[:-2] + (moved.shape[-2] * moved.shape[-1],))


def kernel(x, c, ctx, c_ctx, w_mod, b_mod, norm_mix, norm_ffn, w_in_ab, conv_a, conv_b, conv_b_bias, ln_b_gain, ln_b_bias, w_out_ab, w_qkv, w_o, sinks, w_up, w_conv_ffn, w_down, final_norm, loss_target, m_c_ctx, m_w_mod, m_b_mod, m_norm_mix, m_norm_ffn, m_w_in_ab, m_conv_a, m_conv_b, m_conv_b_bias, m_ln_b_gain, m_ln_b_bias, m_w_out_ab, m_w_qkv, m_w_o, m_sinks, m_w_up, m_w_conv_ffn, m_w_down, m_final_norm, v_c_ctx, v_w_mod, v_b_mod, v_norm_mix, v_norm_ffn, v_w_in_ab, v_conv_a, v_conv_b, v_conv_b_bias, v_ln_b_gain, v_ln_b_bias, v_w_out_ab, v_w_qkv, v_w_o, v_sinks, v_w_up, v_w_conv_ffn, v_w_down, v_final_norm):
    args = dict(locals())
    weight_names = ["c_ctx", "w_mod", "b_mod", "norm_mix", "norm_ffn", "w_in_ab", "conv_a", "conv_b", "conv_b_bias",
                    "ln_b_gain", "ln_b_bias", "w_out_ab", "w_qkv", "w_o", "sinks", "w_up", "w_conv_ffn", "w_down",
                    "final_norm"]
    length, d = x.shape[1], x.shape[2]
    ctx_len = ctx.shape[1]
    assert ctx_len == _TM and length % _TM == 0 and x.shape[0] == 1
    n_lat = length // _TM
    depth = w_mod.shape[0]
    n_even, n_odd = w_in_ab.shape[0], w_qkv.shape[0]
    a_w = conv_a.shape[2] * _NDEV
    b_w = conv_b.shape[2] * _NDEV
    assert a_w == b_w
    q_w = w_o.shape[1] * _NDEV
    kv_w = (w_qkv.shape[2] * _NDEV - q_w) // 2
    d_ff = w_down.shape[1] * _NDEV
    dev = 4 * lax.axis_index("x") + 2 * lax.axis_index("y") + lax.axis_index("c")

    def layer_mats(l):
        if l % 2 == 0:
            first = [("in", l // 2, w_in_ab[l // 2].T), ("out", l // 2, w_out_ab[l // 2])]
        else:
            first = [("qkv", l // 2, w_qkv[l // 2].T), ("o", l // 2, w_o[l // 2])]
        return first + [("up", l, w_up[l].T), ("down", l, w_down[l])]

    piece_mats = [layer_mats(0)[:2], layer_mats(0)[2:]] + [layer_mats(l) for l in range(1, depth)]
    n_pieces = len(piece_mats)
    slab_off, slab_r, piece_of, piece_keys, piece_rows, slabs = {}, {}, {}, [], [], []
    for p, mats in enumerate(piece_mats):
        off, keys = 0, []
        for fam, idx, mat in mats:
            slab_off[fam, idx], slab_r[fam], piece_of[fam, idx] = off, mat.shape[0], p
            off += mat.shape[0]
            keys.append((fam, idx))
        piece_keys.append(keys)
        piece_rows.append(off)
        slabs.append(jnp.concatenate([mat.astype(_MM) for _, _, mat in mats], axis=0))
    wgs, gathers, start_token = [None] * n_pieces, [None] * n_pieces, jnp.zeros((), _F32)
    for p in range(n_pieces):
        land = _fill_own_slot(f"gather_fill_{p}", [slabs[p]], [0], piece_rows[p])
        gathers[p] = _exchange_start(f"gather_start_{p}", [slabs[p]], land, [0])
        start_token = start_token + gathers[p][-1][0, 0]

    def wref(fam, idx):
        return wgs[piece_of[fam, idx]], slab_off[fam, idx], slab_r[fam]

    small_shapes = [c.shape[1:], conv_a.shape, conv_b.shape, w_conv_ffn.shape]
    g0 = _all_gather_small("gather_small_params", _pack([c, conv_a, conv_b, w_conv_ffn]))
    c_parts, ca_parts, cb_parts, cf_parts = _unpack(g0.reshape(_NDEV, -1), small_shapes)
    conv_a_full = _gather_channels(ca_parts)
    conv_b_full = _gather_channels(cb_parts)
    conv_f_full = _gather_channels(cf_parts)

    cond = jnp.concatenate([c_parts, c_ctx[None], jnp.zeros((16 - _NDEV - 1, d), _F32)], axis=0)
    cond = cond + start_token
    mod_cols = w_mod.shape[2]
    m_shard = _mod_forward("mod_forward", cond, w_mod)
    m_all = _all_gather_small("gather_mod", m_shard.reshape(depth * 16, mod_cols))
    m_all = jnp.moveaxis(m_all.reshape(_NDEV, depth, 16, mod_cols), 0, 2).reshape(depth, 16, _NDEV * mod_cols)
    m_all = m_all + b_mod[:, None, :]
    m_lat = lax.dynamic_index_in_dim(m_all, dev, axis=1, keepdims=False)
    m_ctx = m_all[:, _NDEV]

    def mod_vec(l, j):
        return jnp.stack([m_lat[l, j * d:(j + 1) * d], m_ctx[l, j * d:(j + 1) * d]])[:, None, :]

    cos, sin = _rope_tables(length, ctx_len)
    xs = jnp.concatenate([x[0], ctx[0]], axis=0)

    def full(a):
        return (a.reshape(1, -1), "full")

    wgs[0] = _exchange_wait("gather_wait_0", gathers[0], [0], m_all)
    saved = []
    for l in range(depth):
        sv = {"x_in": xs}
        if l > 0:
            wgs[1 + l] = _exchange_wait(f"gather_wait_{1 + l}", gathers[1 + l], [0], xs)
        gain1 = full(norm_mix[l] + start_token) if l == 0 else full(norm_mix[l])
        norm1 = [gain1, (mod_vec(l, 0), "stream"), (mod_vec(l, 1), "stream")]
        if l % 2 == 0:
            e = l // 2
            h1, p = _norm_proj(f"proj_in_{l}", xs, norm1, *wref("in", e), n_lat)
            (qm,) = _rowfn(f"premix_{l}", _f_premix, [], [(p, a_w, j) for j in range(5)],
                           [(None, [a_w]), (_ACT, [a_w, b_w])], n_lat)
            cv_a = _dwconv(f"conv_a_{l}", qm, 0, a_w, conv_a_full[e], _ACT, n_lat)
            cv_b = _dwconv(f"conv_b_{l}", qm, 1, b_w, conv_b_full[e], _ACT, n_lat)
            post_params = [full(conv_b_bias[e]), full(ln_b_gain[e]), full(ln_b_bias[e])]
            (z,) = _rowfn(f"postmix_{l}", _f_postmix, post_params, [(p, a_w, 0), (cv_a, a_w, 0), (cv_b, b_w, 0)],
                          [(_ACT, [a_w, b_w])], n_lat)
            y1, xs = _proj_residual(f"proj_out_{l}", z, *wref("out", e), xs, mod_vec(l, 2), n_lat)
            sv.update(p=p, qm=qm, cv_a=cv_a, cv_b=cv_b, z=z)
        else:
            o = l // 2
            h1, p = _norm_proj(f"proj_qkv_{l}", xs, norm1, *wref("qkv", o), n_lat)
            qr, kr = _rope_fwd(f"rope_{l}", p, cos, sin, q_w, kv_w)
            sk = sinks[o].reshape(1, -1)
            z = _attn_fwd(f"attn_{l}", qr, kr, p, sk, n_lat, length, kv_w)
            y1, xs = _proj_residual(f"proj_o_{l}", z, *wref("o", o), xs, mod_vec(l, 2), n_lat)
            sv.update(p=p, qr=qr, kr=kr, z=z)
        sv.update(h1=h1, y1=y1, x_mid=xs)
        if l == 0:
            wgs[1] = _exchange_wait("gather_wait_1", gathers[1], [0], xs)
        norm2 = [full(norm_ffn[l]), (mod_vec(l, 3), "stream"), (mod_vec(l, 4), "stream")]
        h2, pu = _norm_proj(f"proj_up_{l}", xs, norm2, *wref("up", l), n_lat)
        f = _ffn_act(f"ffn_act_{l}", pu, conv_f_full[l], n_lat)
        y2, xs = _proj_residual(f"proj_down_{l}", f, *wref("down", l), xs, mod_vec(l, 5), n_lat)
        sv.update(h2=h2, pu=pu, f=f, y2=y2)
        saved.append(sv)

    loss_part, d_final_norm, dxs = _loss_head("loss_head", xs, final_norm.reshape(1, -1), loss_target[0], n_lat)
    loss = lax.psum(loss_part[0, 0], ("x", "y", "c"))

    wgrads = {}
    d_mod = [[None] * 6 for _ in range(depth)]
    d_norm_mix, d_norm_ffn = [None] * depth, [None] * depth
    d_conv_a, d_conv_b = [None] * n_even, [None] * n_even
    d_bias, d_ln_g, d_ln_b = [None] * n_even, [None] * n_even, [None] * n_even
    d_sinks = [None] * n_odd
    d_conv_f = [None] * depth
    exchanges, recvs, exchange_token = [None] * n_pieces, [None] * n_pieces, jnp.zeros((), _F32)

    def piece_parts(p):
        return ([wgrads[key].reshape(_NDEV, slab_r[key[0]], d) for key in piece_keys[p]],
                [slab_off[key] for key in piece_keys[p]])

    def start_exchange(p):
        parts, offsets = piece_parts(p)
        land = _fill_own_slot(f"exchange_fill_{p}", parts, offsets, piece_rows[p])
        exchanges[p] = _exchange_start(f"exchange_start_{p}", parts, land, offsets)
        return exchanges[p][-1][0, 0]

    for l in reversed(range(depth)):
        sv = saved[l]
        d_mod[l][5], dy2, df = _gate_proj_bwd(f"bwd_down_{l}", dxs, sv["y2"], mod_vec(l, 5) + exchange_token,
                                              *wref("down", l), n_lat)
        wgrads["down", l] = _mm_tn(f"wgrad_down_{l}", sv["f"], dy2, _ACT)
        dpu, d_conv_f[l] = _ffn_act_bwd(f"ffn_act_bwd_{l}", sv["pu"], df, conv_f_full[l], n_lat)
        wgrads["up", l] = _mm_tn(f"wgrad_up_{l}", dpu, sv["h2"], _ACT)
        norm2 = [full(norm_ffn[l]), (mod_vec(l, 3), "stream"), (mod_vec(l, 4), "stream")]
        (dgain, dsh, dsc), dxs = _proj_norm_bwd(f"bwd_up_{l}", dpu, *wref("up", l), sv["x_mid"], norm2, dxs, n_lat)
        d_norm_ffn[l], d_mod[l][3], d_mod[l][4] = dgain, dsh, dsc
        mix_token = start_exchange(1) if l == 0 else jnp.zeros((), _F32)
        gate1 = mod_vec(l, 2) + mix_token
        norm1 = [full(norm_mix[l]), (mod_vec(l, 0), "stream"), (mod_vec(l, 1), "stream")]
        if l % 2 == 0:
            e = l // 2
            d_mod[l][2], dy1, dz = _gate_proj_bwd(f"bwd_out_{l}", dxs, sv["y1"], gate1, *wref("out", e), n_lat)
            wgrads["out", e] = _mm_tn(f"wgrad_out_{l}", sv["z"], dy1, _ACT)
            post_params = [full(conv_b_bias[e]), full(ln_b_gain[e]), full(ln_b_bias[e])]
            (dbias, dlg, dlb), (dgb, dcv_a, dcv_b) = _rowfn_bwd(
                f"postmix_bwd_{l}", _f_postmix, post_params,
                [(sv["p"], a_w, 0), (sv["cv_a"], a_w, 0), (sv["cv_b"], b_w, 0)], [0, 1, 2], [(dz, [a_w, b_w])],
                [(_ACT, [0]), (_ACT, [1]), (_ACT, [2])], n_lat)
            d_bias[e], d_ln_g[e], d_ln_b[e] = dbias, dlg, dlb
            d_conv_a[e] = _dwconv_wgrad(f"conv_a_wgrad_{l}", dcv_a, sv["qm"], 0, a_w, conv_a_full.shape[1], n_lat)
            d_conv_b[e] = _dwconv_wgrad(f"conv_b_wgrad_{l}", dcv_b, sv["qm"], 1, b_w, conv_b_full.shape[1], n_lat)
            dq_a = _dwconv(f"conv_a_bwd_{l}", dcv_a, 0, a_w, conv_a_full[e][::-1], _ACT, n_lat)
            dq_b = _dwconv(f"conv_b_bwd_{l}", dcv_b, 0, b_w, conv_b_full[e][::-1], _ACT, n_lat)
            _, (dp,) = _rowfn_bwd(f"premix_bwd_{l}", _f_premix, [], [(sv["p"], a_w, j) for j in range(5)],
                                  [0, 1, 2, 3, 4], [(dgb, [a_w]), (dq_a, [a_w]), (dq_b, [b_w])],
                                  [(_ACT, [0, 1, 2, 3, 4])], n_lat)
            wgrads["in", e] = _mm_tn(f"wgrad_in_{l}", dp, sv["h1"], _ACT)
            if l == 0:
                norm1 = [full(norm_mix[l] + start_exchange(0))] + norm1[1:]
            (dgain, dsh, dsc), dxs = _proj_norm_bwd(f"bwd_in_{l}", dp, *wref("in", e), sv["x_in"], norm1, dxs, n_lat)
        else:
            o = l // 2
            d_mod[l][2], dy1, dz = _gate_proj_bwd(f"bwd_o_{l}", dxs, sv["y1"], gate1, *wref("o", o), n_lat)
            wgrads["o", o] = _mm_tn(f"wgrad_o_{l}", sv["z"], dy1, _ACT)
            sk = sinks[o].reshape(1, -1)
            dqr, dkr, dv, dsk = _attn_bwd(f"attn_bwd_{l}", sv["qr"], sv["kr"], sv["p"], sk, dz, n_lat, length, kv_w)
            d_sinks[o] = dsk
            dp = _rope_bwd(f"rope_bwd_{l}", dqr, dkr, dv, cos, sin)
            wgrads["qkv", o] = _mm_tn(f"wgrad_qkv_{l}", dp, sv["h1"], _ACT)
            (dgain, dsh, dsc), dxs = _proj_norm_bwd(f"bwd_qkv_{l}", dp, *wref("qkv", o), sv["x_in"], norm1, dxs, n_lat)
        d_norm_mix[l], d_mod[l][0], d_mod[l][1] = dgain, dsh, dsc
        if l > 0:
            exchange_token = start_exchange(1 + l)
    grad_x = dxs[:length][None]

    gsums = []
    for p in range(n_pieces):
        recvs[p] = _exchange_wait(f"exchange_wait_{p}", exchanges[p], piece_parts(p)[1], dxs)
        gsums.append(_sum_slots(f"sum_weight_grads_{p}", recvs[p]))

    def slab_grad(fam, count, transposed):
        mats = [gsums[piece_of[fam, i]][slab_off[fam, i]:slab_off[fam, i] + slab_r[fam]] for i in range(count)]
        return jnp.stack([m_.T if transposed else m_ for m_ in mats])

    grads = {
        "w_in_ab": slab_grad("in", n_even, True), "w_qkv": slab_grad("qkv", n_odd, True),
        "w_up": slab_grad("up", depth, True), "w_out_ab": slab_grad("out", n_even, False),
        "w_o": slab_grad("o", n_odd, False), "w_down": slab_grad("down", depth, False),
    }

    dm_dev = jnp.stack([jnp.concatenate([d_mod[l][j][:, 0, :] for j in range(6)], axis=-1)
                        for l in range(depth)])
    small_grads = [dm_dev, jnp.stack(d_norm_mix), jnp.stack(d_norm_ffn), jnp.stack(d_conv_a), jnp.stack(d_conv_b),
                   jnp.stack(d_bias), jnp.stack(d_ln_g), jnp.stack(d_ln_b), jnp.stack(d_sinks), jnp.stack(d_conv_f),
                   d_final_norm]
    sg_shapes = [a.shape for a in small_grads]
    sg_all = _all_gather_small("gather_small_grads", _pack(small_grads))
    sg_sum = _sum_slots("sum_small_grads", sg_all)
    (dm_sum, g_norm_mix, g_norm_ffn, g_conv_a, g_conv_b, g_bias, g_ln_g, g_ln_b, g_sinks, g_conv_f,
     g_final_norm) = _unpack(sg_sum.reshape(-1), sg_shapes)
    dm_each = _unpack(sg_all.reshape(_NDEV, -1), sg_shapes[:1])[0]

    def my_channels(a):
        width = a.shape[-1] // _NDEV
        return lax.dynamic_slice_in_dim(a, dev * width, width, axis=a.ndim - 1)

    grads["b_mod"] = dm_sum[:, 0] + dm_sum[:, 1]
    grads["norm_mix"] = g_norm_mix.reshape(depth, d)
    grads["norm_ffn"] = g_norm_ffn.reshape(depth, d)
    grads["conv_a"] = my_channels(g_conv_a)
    grads["conv_b"] = my_channels(g_conv_b)
    grads["conv_b_bias"] = g_bias.reshape(n_even, b_w)
    grads["ln_b_gain"] = g_ln_g.reshape(n_even, b_w)
    grads["ln_b_bias"] = g_ln_b.reshape(n_even, b_w)
    grads["sinks"] = g_sinks.reshape(n_odd, -1)
    grads["w_conv_ffn"] = my_channels(g_conv_f)
    grads["final_norm"] = g_final_norm.reshape(d)

    dm_rows = jnp.concatenate([jnp.moveaxis(dm_each[:, :, 0], 0, 1), dm_sum[:, 1:2],
                               jnp.zeros((depth, 16 - _NDEV - 1, 6 * d), _F32)], axis=1)
    dm_mine = my_channels(dm_rows)
    grads["w_mod"] = jnp.stack([_mm_tn(f"wgrad_mod_{l}", cond, dm_mine[l], _F32, silu_a=True) for l in range(depth)])
    dcond = _mod_backward_cond("mod_backward_cond", dm_mine, w_mod)
    dcond_all = _all_gather_small("gather_dcond", dcond)
    dcond_sum = _sum_slots("sum_dcond", dcond_all)[_NDEV]
    sg = jax.nn.sigmoid(c_ctx)
    grads["c_ctx"] = dcond_sum * (sg * (1.0 + c_ctx * (1.0 - sg)))

    big = ["w_mod", "w_in_ab", "w_out_ab", "w_qkv", "w_o", "w_up", "w_down"]
    small = [n for n in weight_names if n not in big]
    delta, new_m, new_v = {}, {}, {}
    for n in big:
        w = args[n]
        two_d = lambda a: a.reshape(-1, w.shape[-1])
        dl, nm, nv = _adamw(f"adamw_{n}", two_d(w), two_d(grads[n]), two_d(args["m_" + n]), two_d(args["v_" + n]))
        delta[n], new_m[n], new_v[n] = dl.reshape(w.shape), nm.reshape(w.shape), nv.reshape(w.shape)
    shapes = [args[n].shape for n in small]
    grads = {n: grads[n].reshape(args[n].shape) for n in grads}
    dl, nm, nv = _adamw("adamw_small", _pack([args[n] for n in small]), _pack([grads[n] for n in small]),
                        _pack([args["m_" + n] for n in small]), _pack([args["v_" + n] for n in small]))
    for res, packed in ((delta, dl), (new_m, nm), (new_v, nv)):
        for n, a in zip(small, _unpack(packed.reshape(-1), shapes)):
            res[n] = a

    return (loss, grad_x, *[grads[n] for n in weight_names], *[delta[n] for n in weight_names],
            *[new_m[n] for n in weight_names], *[new_v[n] for n in weight_names])
```

```python
import functools

import jax
import jax.numpy as jnp
from jax import lax
from jax.experimental import pallas as pl
from jax.experimental.pallas import tpu as pltpu

_F32 = jnp.float32
_MM = jnp.bfloat16
_ACT = jnp.bfloat16
_TM = 256
_HALO = 16
_LANES = 128
_CONV_ROWS = 128
_NDEV = 8
_HEAD_DIM = 64
_WINDOW = 128
_GRID_W = 64
_ROPE_THETA = 10000.0
_EPS = 1e-6
_NEG_INF = -1e30
_VMEM_LIMIT = 56 * 1024 * 1024
_ADAM = dict(lr=0.001, b1=0.9, b2=0.999, eps=1e-08, wd=0.01, step=10)
_MESH = pl.DeviceIdType.MESH


def _params(sem=None):
    return pltpu.CompilerParams(dimension_semantics=sem, vmem_limit_bytes=_VMEM_LIMIT)


def _divisor(n, cap, mult):
    if n <= cap:
        return n
    for d in range(cap - cap % mult, 0, -mult):
        if n % d == 0:
            return d
    raise ValueError(f"no tile for {n}")


def _my_coords():
    return lax.axis_index("x"), lax.axis_index("y"), lax.axis_index("c")


def _peer(k):
    x, y, c = _my_coords()
    px = 1 - x if k & 4 else x
    py = 1 - y if k & 2 else y
    pc = 1 - c if k & 1 else c
    return (px, py, pc), 4 * px + 2 * py + pc


def _all_gather_small(name, v):
    rows, cols = v.shape

    def body(v_ref, out_ref, send_sems, recv_sems):
        x, y, c = _my_coords()
        me = 4 * x + 2 * y + c
        out_ref[me] = v_ref[...]
        sends = []
        for k in range(1, _NDEV):
            peer, _ = _peer(k)
            cp = pltpu.make_async_remote_copy(
                src_ref=v_ref, dst_ref=out_ref.at[me], send_sem=send_sems.at[k - 1], recv_sem=recv_sems.at[k - 1],
                device_id=peer, device_id_type=_MESH)
            cp.start()
            sends.append(cp)
        for k in range(1, _NDEV):
            peer, pid = _peer(k)
            pltpu.make_async_remote_copy(
                src_ref=v_ref, dst_ref=out_ref.at[pid], send_sem=send_sems.at[k - 1], recv_sem=recv_sems.at[k - 1],
                device_id=peer, device_id_type=_MESH).wait_recv()
        for cp in sends:
            cp.wait_send()

    return pl.pallas_call(
        body, name=name,
        out_shape=jax.ShapeDtypeStruct((_NDEV, rows, cols), v.dtype),
        in_specs=[pl.BlockSpec(memory_space=pltpu.VMEM)],
        out_specs=pl.BlockSpec(memory_space=pltpu.VMEM),
        scratch_shapes=[pltpu.SemaphoreType.DMA((_NDEV - 1,)), pltpu.SemaphoreType.DMA((_NDEV - 1,))],
        compiler_params=pltpu.CompilerParams(vmem_limit_bytes=_VMEM_LIMIT),
    )(v)


def _sum_slots(name, v):
    _, rows, cols = v.shape
    tr = _divisor(rows, 512, 16)

    def body(v_ref, o_ref):
        acc = v_ref[0].astype(_F32)
        for e in range(1, _NDEV):
            acc = acc + v_ref[e].astype(_F32)
        o_ref[...] = acc

    return pl.pallas_call(
        body, name=name, grid=(rows // tr,),
        out_shape=jax.ShapeDtypeStruct((rows, cols), _F32),
        in_specs=[pl.BlockSpec((_NDEV, tr, cols), lambda i: (0, i, 0))],
        out_specs=pl.BlockSpec((tr, cols), lambda i: (i, 0)),
        compiler_params=_params(("arbitrary",)),
    )(v)


_HBM_SPEC = pl.BlockSpec(memory_space=pltpu.HBM)
_SEM_SPEC = pl.BlockSpec(memory_space=pltpu.SEMAPHORE)
_EFFECT = pltpu.SideEffectType.DATAFLOW_SIDE_EFFECTING


def _in_hbm(a):
    return pltpu.with_memory_space_constraint(a, pltpu.HBM)


def _block_for(ref, device):
    return ref if len(ref.shape) == 2 else ref.at[device]


def _fill_own_slot(name, srcs, offsets, total_rows):
    n = len(srcs)
    cols = srcs[0].shape[-1]

    def body(*refs):
        src_refs, out_ref, bufs, sems = refs[:n], refs[n], refs[n + 1:2 * n + 1], refs[2 * n + 1]
        x, y, c = _my_coords()
        me = 4 * x + 2 * y + c
        loads = [pltpu.make_async_copy(_block_for(src_refs[m], me), bufs[m], sems.at[0, m]) for m in range(n)]
        stores = [pltpu.make_async_copy(bufs[m], out_ref.at[me, pl.ds(offsets[m], srcs[m].shape[-2]), :],
                                        sems.at[1, m]) for m in range(n)]
        for copies in (loads, stores):
            for cp in copies:
                cp.start()
            for cp in copies:
                cp.wait()

    return pl.pallas_call(
        body, name=name,
        out_shape=jax.ShapeDtypeStruct((_NDEV, total_rows, cols), srcs[0].dtype),
        in_specs=[pl.BlockSpec(memory_space=pl.ANY)] * n,
        out_specs=pl.BlockSpec(memory_space=pl.ANY),
        scratch_shapes=[pltpu.VMEM(s.shape[-2:], s.dtype) for s in srcs] + [pltpu.SemaphoreType.DMA((2, n))],
        compiler_params=pltpu.CompilerParams(vmem_limit_bytes=_VMEM_LIMIT),
    )(*srcs)


def _exchange_start(name, srcs, land, offsets):
    n = len(srcs)

    def body(*refs):
        src_refs, land_ref = refs[:n], refs[n]
        send_sems, recv_sems, token = refs[n + 1], refs[n + 2], refs[-1]
        x, y, c = _my_coords()
        me = 4 * x + 2 * y + c
        for k in range(1, _NDEV):
            peer, pid = _peer(k)
            for m in range(n):
                pltpu.make_async_remote_copy(
                    src_ref=_block_for(src_refs[m], pid),
                    dst_ref=land_ref.at[me, pl.ds(offsets[m], srcs[m].shape[-2]), :],
                    send_sem=send_sems, recv_sem=recv_sems, device_id=peer, device_id_type=_MESH).start()
        token[...] = jnp.zeros_like(token)

    sems = pltpu.SemaphoreType.DMA(())
    return pl.pallas_call(
        body, name=name,
        out_shape=(sems, sems, *[pltpu.HBM(s.shape, s.dtype) for s in srcs], pltpu.HBM(land.shape, land.dtype),
                   jax.ShapeDtypeStruct((8, 128), _F32)),
        in_specs=[_HBM_SPEC] * (n + 1),
        out_specs=(_SEM_SPEC, _SEM_SPEC, *[_HBM_SPEC] * (n + 1), pl.BlockSpec(memory_space=pltpu.VMEM)),
        input_output_aliases={i: 2 + i for i in range(n + 1)},
        compiler_params=pltpu.CompilerParams(has_side_effects=_EFFECT),
    )(*[_in_hbm(s) for s in srcs], _in_hbm(land))


def _exchange_wait(name, started, offsets, after):
    send_sems, recv_sems = started[0], started[1]
    srcs, land = list(started[2:-2]), started[-2]
    n = len(srcs)

    def body(*refs):
        src_refs, land_ref = refs[:n], refs[n]
        send_sems_, recv_sems_ = refs[n + 1], refs[n + 2]
        others = land_ref.at[pl.ds(0, _NDEV - 1)]
        cp = pltpu.make_async_remote_copy(src_ref=others, dst_ref=others, send_sem=send_sems_, recv_sem=recv_sems_,
                                          device_id=_peer(1)[0], device_id_type=_MESH)
        cp.wait_send()
        cp.wait_recv()

    res = pl.pallas_call(
        body, name=name,
        out_shape=(*[pltpu.HBM(s.shape, s.dtype) for s in srcs], pltpu.HBM(land.shape, land.dtype)),
        in_specs=[_HBM_SPEC] * (n + 1) + [_SEM_SPEC, _SEM_SPEC, pl.BlockSpec(memory_space=pl.ANY)],
        out_specs=tuple([_HBM_SPEC] * (n + 1)),
        input_output_aliases={i: i for i in range(n + 1)},
        compiler_params=pltpu.CompilerParams(has_side_effects=_EFFECT),
    )(*srcs, land, send_sems, recv_sems, after)
    return res[n]


def _load_weight(wg_ref, wbuf, sems, off, r, step):
    @pl.when(step == 0)
    def _():
        copies = [pltpu.make_async_copy(wg_ref.at[e, pl.ds(off, r), :], wbuf.at[pl.ds(e * r, r), :], sems.at[e])
                  for e in range(_NDEV)]
        for cp in copies:
            cp.start()
        for cp in copies:
            cp.wait()


def _mm_tn(name, a, b, out_dtype, silu_a=False):
    rows, na = a.shape
    nb = b.shape[1]
    tr = _divisor(rows, 1536, 16)
    tn = _divisor(na, 1536, 128)
    steps = rows // tr

    def body(a_ref, b_ref, o_ref, acc):
        t = pl.program_id(1)

        @pl.when(t == 0)
        def _():
            acc[...] = jnp.zeros_like(acc)

        av = a_ref[...]
        if silu_a:
            av = av.astype(_F32)
            av = av * jax.nn.sigmoid(av)
        acc[...] += lax.dot_general(av.astype(_MM), b_ref[...].astype(_MM), (((0,), (0,)), ((), ())),
                                    preferred_element_type=_F32)

        @pl.when(t == steps - 1)
        def _():
            o_ref[...] = acc[...].astype(out_dtype)

    return pl.pallas_call(
        body, name=name, grid=(na // tn, steps),
        out_shape=jax.ShapeDtypeStruct((na, nb), out_dtype),
        in_specs=[pl.BlockSpec((tr, tn), lambda j, t: (t, j)), pl.BlockSpec((tr, nb), lambda j, t: (t, 0))],
        out_specs=pl.BlockSpec((tn, nb), lambda j, t: (j, 0)),
        scratch_shapes=[pltpu.VMEM((tn, nb), _F32)],
        compiler_params=_params(("arbitrary", "arbitrary")),
    )(a, b)


def _mod_forward(name, cond, w_mod):
    depth, d, n = w_mod.shape
    rows = cond.shape[0]

    def body(c_ref, w_ref, o_ref):
        cv = c_ref[...]
        a = (cv * jax.nn.sigmoid(cv)).astype(_MM)
        o_ref[...] = jnp.dot(a, w_ref[...].astype(_MM), preferred_element_type=_F32)

    return pl.pallas_call(
        body, name=name, grid=(depth,),
        out_shape=jax.ShapeDtypeStruct((depth, rows, n), _F32),
        in_specs=[pl.BlockSpec((rows, d), lambda l: (0, 0)), pl.BlockSpec((None, d, n), lambda l: (l, 0, 0))],
        out_specs=pl.BlockSpec((None, rows, n), lambda l: (l, 0, 0)),
        compiler_params=_params(("arbitrary",)),
    )(cond, w_mod)


def _mod_backward_cond(name, dm, w_mod):
    depth, d, n = w_mod.shape
    rows = dm.shape[1]

    def body(g_ref, w_ref, o_ref):
        @pl.when(pl.program_id(0) == 0)
        def _():
            o_ref[...] = jnp.zeros_like(o_ref)

        o_ref[...] += lax.dot_general(g_ref[...].astype(_MM), w_ref[...].astype(_MM), (((1,), (1,)), ((), ())),
                                      preferred_element_type=_F32)

    return pl.pallas_call(
        body, name=name, grid=(depth,),
        out_shape=jax.ShapeDtypeStruct((rows, d), _F32),
        in_specs=[pl.BlockSpec((None, rows, n), lambda l: (l, 0, 0)), pl.BlockSpec((None, d, n), lambda l: (l, 0, 0))],
        out_specs=pl.BlockSpec((rows, d), lambda l: (0, 0)),
        compiler_params=_params(("arbitrary",)),
    )(dm, w_mod)


def _param_spec(arr, kind, n_lat):
    if kind == "stream":
        return pl.BlockSpec((None,) + arr.shape[1:], lambda i: (i // n_lat, 0, 0))
    return pl.BlockSpec(arr.shape, lambda i: (0,) * arr.ndim)


def _rowfn(name, fn, params, xs, outs, n_lat):
    rows = xs[0][0].shape[0]
    np_, nx = len(params), len(xs)
    stored = [(dt, ws) for dt, ws in outs if dt is not None]

    def body(*refs):
        ps = [r[...].astype(_F32) for r in refs[:np_]]
        xv = [r[...].astype(_F32) for r in refs[np_:np_ + nx]]
        pieces = fn(ps, xv)
        o_refs = iter(refs[np_ + nx:])
        k = 0
        for dt, ws in outs:
            o_ref = next(o_refs) if dt is not None else None
            off = 0
            for w in ws:
                if o_ref is not None:
                    o_ref[:, off:off + w] = pieces[k].astype(dt)
                off += w
                k += 1

    return pl.pallas_call(
        body, name=name, grid=(rows // _TM,),
        out_shape=[jax.ShapeDtypeStruct((rows, sum(ws)), dt) for dt, ws in stored],
        in_specs=[_param_spec(a, kind, n_lat) for a, kind in params]
        + [pl.BlockSpec((_TM, w), lambda i, cb=cb: (i, cb)) for _, w, cb in xs],
        out_specs=[pl.BlockSpec((_TM, sum(ws)), lambda i: (i, 0)) for _, ws in stored],
        compiler_params=_params(("arbitrary",)),
    )(*[a for a, _ in params], *[a for a, _, _ in xs])


def _rowfn_bwd(name, fn, params, xs, diff, douts, dx_outs, n_lat, residual=None):
    rows = xs[0][0].shape[0]
    np_, nx, nd = len(params), len(xs), len(douts)
    nres = 0 if residual is None else 1
    nt = rows // _TM

    def body(*refs):
        i = pl.program_id(0)
        ps = [r[...].astype(_F32) for r in refs[:np_]]
        xv = [r[...].astype(_F32) for r in refs[np_:np_ + nx]]
        d_refs = refs[np_ + nx:np_ + nx + nd]
        res_ref = refs[np_ + nx + nd] if nres else None
        dp_refs = refs[np_ + nx + nd + nres:np_ + nx + nd + nres + np_]
        dx_refs = refs[np_ + nx + nd + nres + np_:]

        def f(ps_, xd):
            full = list(xv)
            for j, v in zip(diff, xd):
                full[j] = v
            return fn(ps_, full)

        _, vjp = jax.vjp(f, ps, [xv[j] for j in diff])
        cts = []
        for d_ref, (_, ws) in zip(d_refs, douts):
            off = 0
            for w in ws:
                cts.append(d_ref[:, off:off + w].astype(_F32))
                off += w
        dps, dxd = vjp(cts)
        grads = dict(zip(diff, dxd))
        for (dp_ref, (_, kind)), dp in zip(zip(dp_refs, params), dps):
            first = (i == 0) | (i == n_lat) if kind == "stream" else i == 0

            @pl.when(first)
            def _(dp_ref=dp_ref):
                dp_ref[...] = jnp.zeros_like(dp_ref)

            dp_ref[...] += dp
        for n_out, (dx_ref, (dt, idxs)) in enumerate(zip(dx_refs, dx_outs)):
            off = 0
            for j in idxs:
                w = xs[j][1]
                g = grads[j]
                if res_ref is not None and n_out == 0 and off == 0:
                    g = g + res_ref[...].astype(_F32)
                dx_ref[:, off:off + w] = g.astype(dt)
                off += w

    dp_shapes = [jax.ShapeDtypeStruct(a.shape, _F32) for a, _ in params]
    dx_shapes = [jax.ShapeDtypeStruct((rows, sum(xs[j][1] for j in idxs)), dt) for dt, idxs in dx_outs]
    in_specs = ([_param_spec(a, kind, n_lat) for a, kind in params]
                + [pl.BlockSpec((_TM, w), lambda i, cb=cb: (i, cb)) for _, w, cb in xs]
                + [pl.BlockSpec((_TM, sum(ws)), lambda i: (i, 0)) for _, ws in douts])
    operands = [a for a, _ in params] + [a for a, _, _ in xs] + [a for a, _ in douts]
    if nres:
        in_specs.append(pl.BlockSpec((_TM, residual.shape[1]), lambda i: (i, 0)))
        operands.append(residual)
    res = pl.pallas_call(
        body, name=name, grid=(nt,),
        out_shape=dp_shapes + dx_shapes,
        in_specs=in_specs,
        out_specs=[_param_spec(a, kind, n_lat) for a, kind in params]
        + [pl.BlockSpec((_TM, s.shape[1]), lambda i: (i, 0)) for s in dx_shapes],
        compiler_params=_params(("arbitrary",)),
    )(*operands)
    return list(res[:np_]), list(res[np_:])


def _f_norm_mod(ps, xs):
    gain, shift, scale = ps
    (x,) = xs
    y = x * lax.rsqrt(jnp.mean(x * x, axis=-1, keepdims=True) + _EPS) * gain
    return [y * (1.0 + scale) + shift]


def _f_premix(ps, xs):
    g_b, g_c, u_a, v_b, gate_b = xs
    return [g_b, g_c * u_a, v_b * jax.nn.sigmoid(gate_b)]


def _f_postmix(ps, xs):
    bias, ln_g, ln_b = ps
    g_b, cv_a, cv_b = xs
    u = cv_b + bias
    mu = jnp.mean(u, axis=-1, keepdims=True)
    var = jnp.mean(jnp.square(u - mu), axis=-1, keepdims=True)
    y = (u - mu) * lax.rsqrt(var + _EPS) * ln_g + ln_b
    return [g_b * cv_a, y * jax.nn.sigmoid(y)]


def _accumulate_params(dp_refs, kinds, dps, i, n_lat):
    for dp_ref, kind, dp in zip(dp_refs, kinds, dps):
        first = (i == 0) | (i == n_lat) if kind == "stream" else i == 0

        @pl.when(first)
        def _(dp_ref=dp_ref):
            dp_ref[...] = jnp.zeros_like(dp_ref)

        dp_ref[...] += dp


def _row_spec(width):
    return pl.BlockSpec((_TM, width), lambda i: (i, 0))


def _weight_scratch(shape, wg):
    return [pltpu.VMEM(shape, wg.dtype), pltpu.SemaphoreType.DMA((_NDEV,))]


def _norm_proj(name, x, params, wg, off, r, n_lat):
    rows, d = x.shape
    n = _NDEV * r
    chunk = _divisor(n, 512, 128)

    def body(g_ref, sh_ref, sc_ref, x_ref, wg_ref, h_ref, p_ref, wbuf, sems):
        _load_weight(wg_ref, wbuf, sems, off, r, pl.program_id(0))
        (h,) = _f_norm_mod([g_ref[...], sh_ref[...], sc_ref[...]], [x_ref[...]])
        hb = h.astype(_MM)
        h_ref[...] = hb.astype(h_ref.dtype)
        for j in range(n // chunk):
            p_ref[:, j * chunk:(j + 1) * chunk] = lax.dot_general(
                hb, wbuf[j * chunk:(j + 1) * chunk, :], (((1,), (1,)), ((), ())),
                preferred_element_type=_F32).astype(p_ref.dtype)

    return pl.pallas_call(
        body, name=name, grid=(rows // _TM,),
        out_shape=[jax.ShapeDtypeStruct((rows, d), _ACT), jax.ShapeDtypeStruct((rows, n), _ACT)],
        in_specs=[_param_spec(a, kind, n_lat) for a, kind in params] + [_row_spec(d), pl.BlockSpec(memory_space=pl.ANY)],
        out_specs=[_row_spec(d), _row_spec(n)],
        scratch_shapes=_weight_scratch((n, d), wg),
        compiler_params=_params(("arbitrary",)),
    )(*[a for a, _ in params], x, wg)


def _proj_residual(name, a, wg, off, r, x, gate, n_lat):
    rows, kdim = a.shape
    d = x.shape[1]
    assert kdim == _NDEV * r

    def body(g_ref, a_ref, x_ref, wg_ref, y_ref, o_ref, wbuf, sems):
        _load_weight(wg_ref, wbuf, sems, off, r, pl.program_id(0))
        y = jnp.dot(a_ref[...].astype(_MM), wbuf[...], preferred_element_type=_F32)
        y_ref[...] = y.astype(y_ref.dtype)
        o_ref[...] = x_ref[...] + g_ref[...] * y

    return pl.pallas_call(
        body, name=name, grid=(rows // _TM,),
        out_shape=[jax.ShapeDtypeStruct((rows, d), _ACT), jax.ShapeDtypeStruct((rows, d), _F32)],
        in_specs=[_param_spec(gate, "stream", n_lat), _row_spec(kdim), _row_spec(d), pl.BlockSpec(memory_space=pl.ANY)],
        out_specs=[_row_spec(d), _row_spec(d)],
        scratch_shapes=_weight_scratch((kdim, d), wg),
        compiler_params=_params(("arbitrary",)),
    )(gate, a, x, wg)


def _gate_proj_bwd(name, dx, y, gate, wg, off, r, n_lat):
    rows, d = dx.shape
    n = _NDEV * r
    chunk = _divisor(n, 512, 128)

    def body(g_ref, dx_ref, y_ref, wg_ref, dg_ref, dy_ref, dz_ref, wbuf, sems):
        i = pl.program_id(0)
        _load_weight(wg_ref, wbuf, sems, off, r, i)
        dxv = dx_ref[...]
        _accumulate_params([dg_ref], ["stream"], [jnp.sum(dxv * y_ref[...].astype(_F32), axis=0, keepdims=True)],
                           i, n_lat)
        dy = (g_ref[...] * dxv).astype(_MM)
        dy_ref[...] = dy.astype(dy_ref.dtype)
        for j in range(n // chunk):
            dz_ref[:, j * chunk:(j + 1) * chunk] = lax.dot_general(
                dy, wbuf[j * chunk:(j + 1) * chunk, :], (((1,), (1,)), ((), ())),
                preferred_element_type=_F32).astype(dz_ref.dtype)

    return pl.pallas_call(
        body, name=name, grid=(rows // _TM,),
        out_shape=[jax.ShapeDtypeStruct(gate.shape, _F32), jax.ShapeDtypeStruct((rows, d), _ACT),
                   jax.ShapeDtypeStruct((rows, n), _ACT)],
        in_specs=[_param_spec(gate, "stream", n_lat), _row_spec(d), _row_spec(d), pl.BlockSpec(memory_space=pl.ANY)],
        out_specs=[_param_spec(gate, "stream", n_lat), _row_spec(d), _row_spec(n)],
        scratch_shapes=_weight_scratch((n, d), wg),
        compiler_params=_params(("arbitrary",)),
    )(gate, dx, y, wg)


def _proj_norm_bwd(name, dp, wg, off, r, x, params, dx_in, n_lat):
    rows, kdim = dp.shape
    d = x.shape[1]
    assert kdim == _NDEV * r
    kinds = [kind for _, kind in params]

    def body(g_ref, sh_ref, sc_ref, dp_ref, x_ref, dxin_ref, wg_ref, dg_ref, dsh_ref, dsc_ref, dx_ref, wbuf, sems):
        i = pl.program_id(0)
        _load_weight(wg_ref, wbuf, sems, off, r, i)
        dh = jnp.dot(dp_ref[...].astype(_MM), wbuf[...], preferred_element_type=_F32)
        _, vjp = jax.vjp(lambda ps, xv: _f_norm_mod(ps, [xv]), [g_ref[...], sh_ref[...], sc_ref[...]], x_ref[...])
        dps, dxn = vjp([dh])
        _accumulate_params([dg_ref, dsh_ref, dsc_ref], kinds, dps, i, n_lat)
        dx_ref[...] = dxin_ref[...] + dxn

    specs = [_param_spec(a, kind, n_lat) for a, kind in params]
    res = pl.pallas_call(
        body, name=name, grid=(rows // _TM,),
        out_shape=[jax.ShapeDtypeStruct(a.shape, _F32) for a, _ in params] + [jax.ShapeDtypeStruct((rows, d), _F32)],
        in_specs=specs + [_row_spec(kdim), _row_spec(d), _row_spec(d), pl.BlockSpec(memory_space=pl.ANY)],
        out_specs=specs + [_row_spec(d)],
        scratch_shapes=_weight_scratch((kdim, d), wg),
        compiler_params=_params(("arbitrary",)),
    )(*[a for a, _ in params], dp, x, dx_in, wg)
    return list(res[:3]), res[3]


def _conv_halo_specs(width, cb0, n_rows):
    per = _TM // _HALO
    last = n_rows // _HALO - 1
    return [
        pl.BlockSpec((_TM, width), lambda i, j: (i, cb0 + j)),
        pl.BlockSpec((_HALO, width), lambda i, j: (jnp.maximum(i * per - 1, 0), cb0 + j)),
        pl.BlockSpec((_HALO, width), lambda i, j: (jnp.minimum((i + 1) * per, last), cb0 + j)),
    ]


def _conv_window(main_ref, prev_ref, next_ref, r0, cols, i, n_lat, nt):
    if r0 == 0:
        has_prev = (i != 0) & (i != n_lat)
        head = jnp.where(has_prev, prev_ref[:, cols].astype(_F32), 0.0)
    else:
        head = main_ref[r0 - _HALO:r0, cols].astype(_F32)
    if r0 + _CONV_ROWS == _TM:
        has_next = (i != n_lat - 1) & (i != nt - 1)
        tail = jnp.where(has_next, next_ref[:, cols].astype(_F32), 0.0)
    else:
        tail = main_ref[r0 + _CONV_ROWS:r0 + _CONV_ROWS + _HALO, cols].astype(_F32)
    return jnp.concatenate([head, main_ref[r0:r0 + _CONV_ROWS, cols].astype(_F32), tail], axis=0)


def _shifted(win, offset):
    n = win.shape[0]
    rolled = win if offset == 0 else pltpu.roll(win, (-offset) % n, 0)
    return rolled[_HALO:_HALO + _CONV_ROWS]


def _dwconv(name, x, cb0, channels, taps, out_dtype, n_lat):
    rows = x.shape[0]
    ktaps = taps.shape[0]
    half = ktaps // 2
    width = _divisor(channels, 1536, 128)
    assert (cb0 * channels) % width == 0
    cb0 = cb0 * channels // width
    nt = rows // _TM

    def body(main_ref, prev_ref, next_ref, taps_ref, o_ref):
        i = pl.program_id(0)

        def chunk(j, carry):
            cols = pl.ds(pl.multiple_of(j * _LANES, _LANES), _LANES)
            for r0 in range(0, _TM, _CONV_ROWS):
                win = _conv_window(main_ref, prev_ref, next_ref, r0, cols, i, n_lat, nt)
                acc = taps_ref[0:1, cols] * _shifted(win, -half)
                for k in range(1, ktaps):
                    acc = acc + taps_ref[k:k + 1, cols] * _shifted(win, k - half)
                o_ref[r0:r0 + _CONV_ROWS, cols] = acc.astype(out_dtype)
            return carry

        lax.fori_loop(0, width // _LANES, chunk, 0)

    return pl.pallas_call(
        body, name=name, grid=(nt, channels // width),
        out_shape=jax.ShapeDtypeStruct((rows, channels), out_dtype),
        in_specs=_conv_halo_specs(width, cb0, rows) + [pl.BlockSpec((ktaps, width), lambda i, j: (0, j))],
        out_specs=pl.BlockSpec((_TM, width), lambda i, j: (i, j)),
        compiler_params=_params(("arbitrary", "arbitrary")),
    )(x, x, x, taps)


def _dwconv_wgrad(name, dy, x, cb0, channels, ktaps, n_lat):
    rows = x.shape[0]
    half = ktaps // 2
    width = _divisor(channels, 1536, 128)
    cb0 = cb0 * channels // width
    nt = rows // _TM

    def body(dy_ref, main_ref, prev_ref, next_ref, o_ref):
        i = pl.program_id(1)

        @pl.when(i == 0)
        def _():
            o_ref[...] = jnp.zeros_like(o_ref)

        def chunk(j, carry):
            cols = pl.ds(pl.multiple_of(j * _LANES, _LANES), _LANES)
            for r0 in range(0, _TM, _CONV_ROWS):
                dyv = dy_ref[r0:r0 + _CONV_ROWS, cols].astype(_F32)
                win = _conv_window(main_ref, prev_ref, next_ref, r0, cols, i, n_lat, nt)
                for k in range(ktaps):
                    o_ref[k:k + 1, cols] += jnp.sum(dyv * _shifted(win, k - half), axis=0, keepdims=True)
            return carry

        lax.fori_loop(0, width // _LANES, chunk, 0)

    per = _TM // _HALO
    last = rows // _HALO - 1
    return pl.pallas_call(
        body, name=name, grid=(channels // width, nt),
        out_shape=jax.ShapeDtypeStruct((ktaps, channels), _F32),
        in_specs=[
            pl.BlockSpec((_TM, width), lambda j, i: (i, j)),
            pl.BlockSpec((_TM, width), lambda j, i: (i, cb0 + j)),
            pl.BlockSpec((_HALO, width), lambda j, i: (jnp.maximum(i * per - 1, 0), cb0 + j)),
            pl.BlockSpec((_HALO, width), lambda j, i: (jnp.minimum((i + 1) * per, last), cb0 + j)),
        ],
        out_specs=pl.BlockSpec((ktaps, width), lambda j, i: (0, j)),
        compiler_params=_params(("arbitrary", "arbitrary")),
    )(dy, x, x, x)


def _ffn_halo_specs(width, n_rows):
    per = _TM // _HALO
    last = n_rows // _HALO - 1
    return [pl.BlockSpec((_TM, width), lambda i: (i, 0)),
            pl.BlockSpec((_HALO, width), lambda i: (jnp.maximum(i * per - 1, 0), 0)),
            pl.BlockSpec((_HALO, width), lambda i: (jnp.minimum((i + 1) * per, last), 0))]


def _ffn_act(name, pu, taps, n_lat):
    rows, c2 = pu.shape
    ff = c2 // 2
    ktaps = taps.shape[0]
    half = ktaps // 2
    nt = rows // _TM

    def body(main_ref, prev_ref, next_ref, taps_ref, o_ref):
        i = pl.program_id(0)

        def conv(cols, r0):
            win = _conv_window(main_ref, prev_ref, next_ref, r0, cols, i, n_lat, nt)
            acc = taps_ref[0:1, cols] * _shifted(win, -half)
            for k in range(1, ktaps):
                acc = acc + taps_ref[k:k + 1, cols] * _shifted(win, k - half)
            return acc

        def chunk(j, carry):
            c0 = pl.multiple_of(j * _LANES, _LANES)
            cols_a, cols_g = pl.ds(c0, _LANES), pl.ds(pl.multiple_of(ff + c0, _LANES), _LANES)
            for r0 in range(0, _TM, _CONV_ROWS):
                ua, ug = conv(cols_a, r0), conv(cols_g, r0)
                o_ref[r0:r0 + _CONV_ROWS, cols_a] = (ug * jax.nn.sigmoid(ug) * ua).astype(o_ref.dtype)
            return carry

        lax.fori_loop(0, ff // _LANES, chunk, 0)

    return pl.pallas_call(
        body, name=name, grid=(nt,),
        out_shape=jax.ShapeDtypeStruct((rows, ff), _ACT),
        in_specs=_ffn_halo_specs(c2, rows) + [pl.BlockSpec((ktaps, c2), lambda i: (0, 0))],
        out_specs=pl.BlockSpec((_TM, ff), lambda i: (i, 0)),
        compiler_params=_params(("arbitrary",)),
    )(pu, pu, pu, taps)


def _ffn_act_bwd(name, pu, df, taps, n_lat):
    rows, c2 = pu.shape
    ff = c2 // 2
    ktaps = taps.shape[0]
    half = ktaps // 2
    nt = rows // _TM
    inner = slice(_HALO, _HALO + _CONV_ROWS)

    def body(main_ref, prev_ref, next_ref, dmain_ref, dprev_ref, dnext_ref, taps_ref, dpu_ref, dt_ref):
        i = pl.program_id(0)

        @pl.when(i == 0)
        def _():
            dt_ref[...] = jnp.zeros_like(dt_ref)

        def chunk(j, carry):
            c0 = pl.multiple_of(j * _LANES, _LANES)
            cols_a, cols_g = pl.ds(c0, _LANES), pl.ds(pl.multiple_of(ff + c0, _LANES), _LANES)
            for r0 in range(0, _TM, _CONV_ROWS):
                wins = [_conv_window(main_ref, prev_ref, next_ref, r0, cols, i, n_lat, nt) for cols in (cols_a, cols_g)]
                n = wins[0].shape[0]
                shifted = [[w if k == half else pltpu.roll(w, (half - k) % n, 0) for k in range(ktaps)] for w in wins]
                ua, ug = [sum(taps_ref[k:k + 1, cols] * sh[k] for k in range(ktaps))
                          for cols, sh in zip((cols_a, cols_g), shifted)]
                dfw = _conv_window(dmain_ref, dprev_ref, dnext_ref, r0, cols_a, i, n_lat, nt)
                sig = jax.nn.sigmoid(ug)
                d_a = dfw * (ug * sig)
                d_g = dfw * ua * (sig * (1.0 + ug * (1.0 - sig)))
                for du, cols, sh in ((d_a, cols_a, shifted[0]), (d_g, cols_g, shifted[1])):
                    acc = taps_ref[0:1, cols] * _shifted(du, half)
                    for k in range(1, ktaps):
                        acc = acc + taps_ref[k:k + 1, cols] * _shifted(du, half - k)
                    dpu_ref[r0:r0 + _CONV_ROWS, cols] = acc.astype(dpu_ref.dtype)
                    for k in range(ktaps):
                        dt_ref[k:k + 1, cols] += jnp.sum(du[inner] * sh[k][inner], axis=0, keepdims=True)
            return carry

        lax.fori_loop(0, ff // _LANES, chunk, 0)

    return pl.pallas_call(
        body, name=name, grid=(nt,),
        out_shape=[jax.ShapeDtypeStruct((rows, c2), _ACT), jax.ShapeDtypeStruct((ktaps, c2), _F32)],
        in_specs=_ffn_halo_specs(c2, rows) + _ffn_halo_specs(ff, rows) + [pl.BlockSpec((ktaps, c2), lambda i: (0, 0))],
        out_specs=[pl.BlockSpec((_TM, c2), lambda i: (i, 0)), pl.BlockSpec((ktaps, c2), lambda i: (0, 0))],
        compiler_params=_params(("arbitrary",)),
    )(pu, pu, pu, df, df, df, taps)


def _rope_tables(length, ctx_len):
    t = jnp.arange(length)
    row = (t // _GRID_W).astype(_F32)
    col = (t % _GRID_W).astype(_F32)
    n_freq = _HEAD_DIM // 4
    inv_freq = _ROPE_THETA ** (-jnp.arange(n_freq, dtype=_F32) / n_freq)
    ang = jnp.concatenate([row[:, None] * inv_freq, col[:, None] * inv_freq], axis=-1)
    cos, sin = jnp.cos(ang), jnp.sin(ang)
    cos = jnp.concatenate([cos, jnp.ones((ctx_len, _HEAD_DIM // 2), _F32)], axis=0)
    sin = jnp.concatenate([sin, jnp.zeros((ctx_len, _HEAD_DIM // 2), _F32)], axis=0)
    return jnp.tile(cos, (1, 4)), jnp.tile(jnp.concatenate([-sin, sin], axis=-1), (1, 2))


def _rotate(v, cos_ref, sin_ref):
    width = v.shape[1]
    reps = width // 128
    cos = jnp.tile(cos_ref[...], (1, reps))
    sin = jnp.tile(sin_ref[...], (1, reps))
    return v * cos, sin, width


def _partner(v):
    width = v.shape[1]
    half = _HEAD_DIM // 2
    lane = lax.broadcasted_iota(jnp.int32, v.shape, 1)
    return jnp.where(lane % _HEAD_DIM < half, pltpu.roll(v, width - half, 1), pltpu.roll(v, half, 1))


def _rope_fwd(name, p, cos, sin, q_w, kv_w):
    rows, width = p.shape
    scale = _HEAD_DIM ** -0.5

    def body(p_ref, cos_ref, sin_ref, q_ref, k_ref):
        v = p_ref[:, :q_w + kv_w].astype(_F32)
        vc, s, _ = _rotate(v, cos_ref, sin_ref)
        y = vc + _partner(v) * s
        q_ref[...] = (y[:, :q_w] * scale).astype(q_ref.dtype)
        k_ref[...] = y[:, q_w:].astype(k_ref.dtype)

    return pl.pallas_call(
        body, name=name, grid=(rows // _TM,),
        out_shape=[jax.ShapeDtypeStruct((rows, q_w), _ACT), jax.ShapeDtypeStruct((rows, kv_w), _ACT)],
        in_specs=[pl.BlockSpec((_TM, width), lambda i: (i, 0)), pl.BlockSpec((_TM, 128), lambda i: (i, 0)),
                  pl.BlockSpec((_TM, 128), lambda i: (i, 0))],
        out_specs=[pl.BlockSpec((_TM, q_w), lambda i: (i, 0)), pl.BlockSpec((_TM, kv_w), lambda i: (i, 0))],
        compiler_params=_params(("arbitrary",)),
    )(p, cos, sin)


def _rope_bwd(name, dq, dk, dv, cos, sin):
    rows, q_w = dq.shape
    kv_w = dk.shape[1]
    scale = _HEAD_DIM ** -0.5

    def body(dq_ref, dk_ref, dv_ref, cos_ref, sin_ref, o_ref):
        dy = jnp.concatenate([dq_ref[...].astype(_F32) * scale, dk_ref[...].astype(_F32)], axis=1)
        dyc, s, _ = _rotate(dy, cos_ref, sin_ref)
        o_ref[:, :q_w + kv_w] = (dyc + _partner(dy * s)).astype(o_ref.dtype)
        o_ref[:, q_w + kv_w:] = dv_ref[...].astype(o_ref.dtype)

    return pl.pallas_call(
        body, name=name, grid=(rows // _TM,),
        out_shape=jax.ShapeDtypeStruct((rows, q_w + 2 * kv_w), _ACT),
        in_specs=[pl.BlockSpec((_TM, q_w), lambda i: (i, 0)), pl.BlockSpec((_TM, kv_w), lambda i: (i, 0)),
                  pl.BlockSpec((_TM, kv_w), lambda i: (i, 0)), pl.BlockSpec((_TM, 128), lambda i: (i, 0)),
                  pl.BlockSpec((_TM, 128), lambda i: (i, 0))],
        out_specs=pl.BlockSpec((_TM, q_w + 2 * kv_w), lambda i: (i, 0)),
        compiler_params=_params(("arbitrary",)),
    )(dq, dk, dv, cos, sin)


def _attn_window(i, n_lat, length):
    wk = _TM + 2 * _WINDOW
    start = pl.multiple_of(jnp.clip(i * _TM - _WINDOW, 0, length - wk), _WINDOW)
    q_pos = i * _TM + lax.broadcasted_iota(jnp.int32, (_TM, wk), 0)
    k_pos = start + lax.broadcasted_iota(jnp.int32, (_TM, wk), 1)
    mask = (jnp.abs(q_pos - k_pos) <= _WINDOW) & (i < n_lat)
    return start, wk, mask


def _softmax_parts(q, k_loc, k_ctx, mask, sink):
    nt = (((1,), (1,)), ((), ()))
    s_loc = jnp.where(mask, lax.dot_general(q, k_loc, nt, preferred_element_type=_F32), _NEG_INF)
    s_ctx = lax.dot_general(q, k_ctx, nt, preferred_element_type=_F32)
    m = jnp.maximum(jnp.maximum(jnp.max(s_loc, axis=-1, keepdims=True), jnp.max(s_ctx, axis=-1, keepdims=True)),
                    sink)
    e_loc = jnp.exp(s_loc - m)
    e_ctx = jnp.exp(s_ctx - m)
    e_sink = jnp.exp(sink - m)
    inv = 1.0 / (jnp.sum(e_loc, axis=-1, keepdims=True) + jnp.sum(e_ctx, axis=-1, keepdims=True) + e_sink)
    return e_loc * inv, e_ctx * inv, e_sink * inv


def _attn_fwd(name, q, k, p, sinks, n_lat, length, kv_w):
    rows, q_w = q.shape
    ctx_len = rows - length
    n_heads = q_w // _HEAD_DIM
    n_kv = kv_w // _HEAD_DIM
    group = n_heads // n_kv
    v_cb = p.shape[1] // kv_w - 1
    hd = _HEAD_DIM

    def body(q_ref, k_ref, v_ref, sink_ref, o_ref):
        i = pl.program_id(0)
        start, wk, mask = _attn_window(i, n_lat, length)
        for h in range(n_kv):
            k_loc = k_ref[pl.ds(start, wk), h * hd:(h + 1) * hd]
            v_loc = v_ref[pl.ds(start, wk), h * hd:(h + 1) * hd]
            k_ctx = k_ref[length:length + ctx_len, h * hd:(h + 1) * hd]
            v_ctx = v_ref[length:length + ctx_len, h * hd:(h + 1) * hd]
            for g in range(group):
                n = h * group + g
                p_loc, p_ctx, _ = _softmax_parts(q_ref[:, n * hd:(n + 1) * hd], k_loc, k_ctx, mask,
                                                 sink_ref[:, n:n + 1])
                o = (jnp.dot(p_loc.astype(_MM), v_loc, preferred_element_type=_F32)
                     + jnp.dot(p_ctx.astype(_MM), v_ctx, preferred_element_type=_F32))
                o_ref[:, n * hd:(n + 1) * hd] = o.astype(o_ref.dtype)

    return pl.pallas_call(
        body, name=name, grid=(rows // _TM,),
        out_shape=jax.ShapeDtypeStruct((rows, q_w), _ACT),
        in_specs=[pl.BlockSpec((_TM, q_w), lambda i: (i, 0)), pl.BlockSpec((rows, kv_w), lambda i: (0, 0)),
                  pl.BlockSpec((rows, kv_w), lambda i: (0, v_cb)), pl.BlockSpec((1, n_heads), lambda i: (0, 0))],
        out_specs=pl.BlockSpec((_TM, q_w), lambda i: (i, 0)),
        compiler_params=_params(("arbitrary",)),
    )(q, k, p, sinks)


def _attn_bwd(name, q, k, p, sinks, do, n_lat, length, kv_w):
    rows, q_w = q.shape
    ctx_len = rows - length
    n_heads = q_w // _HEAD_DIM
    n_kv = kv_w // _HEAD_DIM
    group = n_heads // n_kv
    v_cb = p.shape[1] // kv_w - 1
    hd = _HEAD_DIM
    nt_dims = (((1,), (1,)), ((), ()))
    tn_dims = (((0,), (0,)), ((), ()))

    def body(q_ref, k_ref, v_ref, sink_ref, do_ref, dq_ref, dk_out, dv_out, ds_ref, dk_ref, dv_ref, out_sems):
        i = pl.program_id(0)

        @pl.when(i == 0)
        def _():
            dk_ref[...] = jnp.zeros_like(dk_ref)
            dv_ref[...] = jnp.zeros_like(dv_ref)
            ds_ref[...] = jnp.zeros_like(ds_ref)

        start, wk, mask = _attn_window(i, n_lat, length)
        head_lane = lax.broadcasted_iota(jnp.int32, (1, n_heads), 1)
        dsink = jnp.zeros((1, n_heads), _F32)
        for h in range(n_kv):
            cols = slice(h * hd, (h + 1) * hd)
            k_loc = k_ref[pl.ds(start, wk), cols]
            v_loc = v_ref[pl.ds(start, wk), cols]
            k_ctx = k_ref[length:length + ctx_len, cols]
            v_ctx = v_ref[length:length + ctx_len, cols]
            dk_loc = jnp.zeros((wk, hd), _F32)
            dv_loc = jnp.zeros((wk, hd), _F32)
            dk_ctx = jnp.zeros((ctx_len, hd), _F32)
            dv_ctx = jnp.zeros((ctx_len, hd), _F32)
            for g in range(group):
                n = h * group + g
                qh = q_ref[:, n * hd:(n + 1) * hd]
                doh = do_ref[:, n * hd:(n + 1) * hd].astype(_MM)
                p_loc, p_ctx, p_sink = _softmax_parts(qh, k_loc, k_ctx, mask, sink_ref[:, n:n + 1])
                dp_loc = lax.dot_general(doh, v_loc, nt_dims, preferred_element_type=_F32)
                dp_ctx = lax.dot_general(doh, v_ctx, nt_dims, preferred_element_type=_F32)
                dsum = (jnp.sum(p_loc * dp_loc, axis=-1, keepdims=True)
                        + jnp.sum(p_ctx * dp_ctx, axis=-1, keepdims=True))
                ds_loc = (p_loc * (dp_loc - dsum)).astype(_MM)
                ds_ctx = (p_ctx * (dp_ctx - dsum)).astype(_MM)
                dsink = dsink + jnp.where(head_lane == n, -jnp.sum(p_sink * dsum), 0.0)
                dq = (jnp.dot(ds_loc, k_loc, preferred_element_type=_F32)
                      + jnp.dot(ds_ctx, k_ctx, preferred_element_type=_F32))
                dq_ref[:, n * hd:(n + 1) * hd] = dq.astype(dq_ref.dtype)
                dk_loc += lax.dot_general(ds_loc, qh, tn_dims, preferred_element_type=_F32)
                dk_ctx += lax.dot_general(ds_ctx, qh, tn_dims, preferred_element_type=_F32)
                dv_loc += lax.dot_general(p_loc.astype(_MM), doh, tn_dims, preferred_element_type=_F32)
                dv_ctx += lax.dot_general(p_ctx.astype(_MM), doh, tn_dims, preferred_element_type=_F32)
            dk_ref[pl.ds(start, wk), cols] += dk_loc
            dv_ref[pl.ds(start, wk), cols] += dv_loc
            dk_ref[length:length + ctx_len, cols] += dk_ctx
            dv_ref[length:length + ctx_len, cols] += dv_ctx
        ds_ref[...] += dsink

        @pl.when(i == rows // _TM - 1)
        def _():
            copies = [pltpu.make_async_copy(dk_ref, dk_out, out_sems.at[0]),
                      pltpu.make_async_copy(dv_ref, dv_out, out_sems.at[1])]
            for cp in copies:
                cp.start()
            for cp in copies:
                cp.wait()

    return pl.pallas_call(
        body, name=name, grid=(rows // _TM,),
        out_shape=[jax.ShapeDtypeStruct((rows, q_w), _ACT), jax.ShapeDtypeStruct((rows, kv_w), _F32),
                   jax.ShapeDtypeStruct((rows, kv_w), _F32), jax.ShapeDtypeStruct((1, n_heads), _F32)],
        in_specs=[pl.BlockSpec((_TM, q_w), lambda i: (i, 0)), pl.BlockSpec((rows, kv_w), lambda i: (0, 0)),
                  pl.BlockSpec((rows, kv_w), lambda i: (0, v_cb)), pl.BlockSpec((1, n_heads), lambda i: (0, 0)),
                  pl.BlockSpec((_TM, q_w), lambda i: (i, 0))],
        out_specs=[pl.BlockSpec((_TM, q_w), lambda i: (i, 0)), pl.BlockSpec(memory_space=pl.ANY),
                   pl.BlockSpec(memory_space=pl.ANY), pl.BlockSpec((1, n_heads), lambda i: (0, 0))],
        scratch_shapes=[pltpu.VMEM((rows, kv_w), _F32), pltpu.VMEM((rows, kv_w), _F32),
                        pltpu.SemaphoreType.DMA((2,))],
        compiler_params=_params(("arbitrary",)),
    )(q, k, p, sinks, do)


def _loss_head(name, xs, gain, target, n_lat):
    rows, d = xs.shape

    def body(x_ref, g_ref, t_ref, loss_ref, dg_ref, dx_ref):
        i = pl.program_id(0)

        @pl.when(i == 0)
        def _():
            loss_ref[...] = jnp.zeros_like(loss_ref)
            dg_ref[...] = jnp.zeros_like(dg_ref)

        @pl.when(i < n_lat)
        def _():
            tv = t_ref[...]

            def f(gain_, x):
                y = x * lax.rsqrt(jnp.mean(x * x, axis=-1, keepdims=True) + _EPS) * gain_
                return 0.5 * jnp.sum(jnp.mean(jnp.square(y - tv), axis=-1))

            val, (dg, dx) = jax.value_and_grad(f, argnums=(0, 1))(g_ref[...], x_ref[...])
            loss_ref[...] += val
            dg_ref[...] += dg
            dx_ref[...] = dx

        @pl.when(i >= n_lat)
        def _():
            dx_ref[...] = jnp.zeros_like(dx_ref)

    return pl.pallas_call(
        body, name=name, grid=(rows // _TM,),
        out_shape=[jax.ShapeDtypeStruct((1, 128), _F32), jax.ShapeDtypeStruct((1, d), _F32),
                   jax.ShapeDtypeStruct((rows, d), _F32)],
        in_specs=[pl.BlockSpec((_TM, d), lambda i: (i, 0)), pl.BlockSpec((1, d), lambda i: (0, 0)),
                  pl.BlockSpec((_TM, d), lambda i: (jnp.minimum(i, n_lat - 1), 0))],
        out_specs=[pl.BlockSpec((1, 128), lambda i: (0, 0)), pl.BlockSpec((1, d), lambda i: (0, 0)),
                   pl.BlockSpec((_TM, d), lambda i: (i, 0))],
        compiler_params=_params(("arbitrary",)),
    )(xs, gain, target)


def _adamw(name, w, g, m, v):
    rows, cols = w.shape
    tr = _divisor(rows, 512, 8)
    b1, b2 = _ADAM["b1"], _ADAM["b2"]
    c1 = 1.0 - b1 ** _ADAM["step"]
    c2 = 1.0 - b2 ** _ADAM["step"]

    def body(w_ref, g_ref, m_ref, v_ref, d_ref, nm_ref, nv_ref):
        gv = g_ref[...]
        nm = b1 * m_ref[...] + (1.0 - b1) * gv
        nv = b2 * v_ref[...] + (1.0 - b2) * jnp.square(gv)
        d_ref[...] = -_ADAM["lr"] * ((nm / c1) / (jnp.sqrt(nv / c2) + _ADAM["eps"]) + _ADAM["wd"] * w_ref[...])
        nm_ref[...] = nm
        nv_ref[...] = nv

    spec = pl.BlockSpec((tr, cols), lambda i: (i, 0))
    return pl.pallas_call(
        body, name=name, grid=(rows // tr,),
        out_shape=[jax.ShapeDtypeStruct((rows, cols), _F32)] * 3,
        in_specs=[spec] * 4, out_specs=[spec] * 3,
        compiler_params=_params(("arbitrary",)),
    )(w, g, m, v)


def _pack(arrays, cols=128):
    flat = jnp.concatenate([a.reshape(-1).astype(_F32) for a in arrays])
    pad = (-flat.shape[0]) % (64 * cols)
    return jnp.pad(flat, (0, pad)).reshape(-1, cols)


def _unpack(flat, shapes):
    out, off = [], 0
    for s in shapes:
        n = 1
        for d in s:
            n *= d
        out.append(flat[..., off:off + n].reshape(flat.shape[:-1] + tuple(s)))
        off += n
    return out


def _gather_channels(parts):
    moved = jnp.moveaxis(parts, 0, -2)
    return moved.reshape(moved.shape[:-2] + (moved.shape[-2] * moved.shape[-1],))


def kernel(x, c, ctx, c_ctx, w_mod, b_mod, norm_mix, norm_ffn, w_in_ab, conv_a, conv_b, conv_b_bias, ln_b_gain, ln_b_bias, w_out_ab, w_qkv, w_o, sinks, w_up, w_conv_ffn, w_down, final_norm, loss_target, m_c_ctx, m_w_mod, m_b_mod, m_norm_mix, m_norm_ffn, m_w_in_ab, m_conv_a, m_conv_b, m_conv_b_bias, m_ln_b_gain, m_ln_b_bias, m_w_out_ab, m_w_qkv, m_w_o, m_sinks, m_w_up, m_w_conv_ffn, m_w_down, m_final_norm, v_c_ctx, v_w_mod, v_b_mod, v_norm_mix, v_norm_ffn, v_w_in_ab, v_conv_a, v_conv_b, v_conv_b_bias, v_ln_b_gain, v_ln_b_bias, v_w_out_ab, v_w_qkv, v_w_o, v_sinks, v_w_up, v_w_conv_ffn, v_w_down, v_final_norm):
    args = dict(locals())
    weight_names = ["c_ctx", "w_mod", "b_mod", "norm_mix", "norm_ffn", "w_in_ab", "conv_a", "conv_b", "conv_b_bias",
                    "ln_b_gain", "ln_b_bias", "w_out_ab", "w_qkv", "w_o", "sinks", "w_up", "w_conv_ffn", "w_down",
                    "final_norm"]
    length, d = x.shape[1], x.shape[2]
    ctx_len = ctx.shape[1]
    assert ctx_len == _TM and length % _TM == 0 and x.shape[0] == 1
    n_lat = length // _TM
    depth = w_mod.shape[0]
    n_even, n_odd = w_in_ab.shape[0], w_qkv.shape[0]
    a_w = conv_a.shape[2] * _NDEV
    b_w = conv_b.shape[2] * _NDEV
    assert a_w == b_w
    q_w = w_o.shape[1] * _NDEV
    kv_w = (w_qkv.shape[2] * _NDEV - q_w) // 2
    d_ff = w_down.shape[1] * _NDEV
    dev = 4 * lax.axis_index("x") + 2 * lax.axis_index("y") + lax.axis_index("c")

    small_shapes = [c.shape[1:], conv_a.shape, conv_b.shape, w_conv_ffn.shape]
    g0 = _all_gather_small("gather_small_params", _pack([c, conv_a, conv_b, w_conv_ffn]))
    c_parts, ca_parts, cb_parts, cf_parts = _unpack(g0.reshape(_NDEV, -1), small_shapes)
    conv_a_full = _gather_channels(ca_parts)
    conv_b_full = _gather_channels(cb_parts)
    conv_f_full = _gather_channels(cf_parts)

    cond = jnp.concatenate([c_parts, c_ctx[None], jnp.zeros((16 - _NDEV - 1, d), _F32)], axis=0)
    mod_cols = w_mod.shape[2]
    m_shard = _mod_forward("mod_forward", cond, w_mod)
    m_all = _all_gather_small("gather_mod", m_shard.reshape(depth * 16, mod_cols))
    m_all = jnp.moveaxis(m_all.reshape(_NDEV, depth, 16, mod_cols), 0, 2).reshape(depth, 16, _NDEV * mod_cols)
    m_all = m_all + b_mod[:, None, :]
    m_lat = lax.dynamic_index_in_dim(m_all, dev, axis=1, keepdims=False)
    m_ctx = m_all[:, _NDEV]

    def mod_vec(l, j):
        return jnp.stack([m_lat[l, j * d:(j + 1) * d], m_ctx[l, j * d:(j + 1) * d]])[:, None, :]

    def layer_mats(l):
        if l % 2 == 0:
            first = [("in", l // 2, w_in_ab[l // 2].T), ("out", l // 2, w_out_ab[l // 2])]
        else:
            first = [("qkv", l // 2, w_qkv[l // 2].T), ("o", l // 2, w_o[l // 2])]
        return first + [("up", l, w_up[l].T), ("down", l, w_down[l])]

    late = 0.0 * m_all[0, 0, 0]
    piece_mats = [layer_mats(0)[:2], layer_mats(0)[2:]] + [layer_mats(l) for l in range(1, depth)]
    n_pieces = len(piece_mats)
    slab_off, slab_r, piece_of, piece_keys, piece_rows, slabs = {}, {}, {}, [], [], []
    for p, mats in enumerate(piece_mats):
        off, keys = 0, []
        for fam, idx, mat in mats:
            slab_off[fam, idx], slab_r[fam], piece_of[fam, idx] = off, mat.shape[0], p
            off += mat.shape[0]
            keys.append((fam, idx))
        piece_keys.append(keys)
        piece_rows.append(off)
        slabs.append(jnp.concatenate([(mat + late).astype(_MM) for _, _, mat in mats], axis=0))
    wgs, gathers, start_token = [None] * n_pieces, [None] * n_pieces, jnp.zeros((), _F32)
    for p in range(n_pieces):
        land = _fill_own_slot(f"gather_fill_{p}", [slabs[p]], [0], piece_rows[p])
        gathers[p] = _exchange_start(f"gather_start_{p}", [slabs[p]], land, [0])
        start_token = start_token + gathers[p][-1][0, 0]

    def wref(fam, idx):
        return wgs[piece_of[fam, idx]], slab_off[fam, idx], slab_r[fam]

    cos, sin = _rope_tables(length, ctx_len)
    xs = jnp.concatenate([x[0], ctx[0]], axis=0)

    def full(a):
        return (a.reshape(1, -1), "full")

    wgs[0] = _exchange_wait("gather_wait_0", gathers[0], [0], m_all)
    saved = []
    for l in range(depth):
        sv = {"x_in": xs}
        if l > 0:
            wgs[1 + l] = _exchange_wait(f"gather_wait_{1 + l}", gathers[1 + l], [0], xs)
        gain1 = full(norm_mix[l] + start_token) if l == 0 else full(norm_mix[l])
        norm1 = [gain1, (mod_vec(l, 0), "stream"), (mod_vec(l, 1), "stream")]
        if l % 2 == 0:
            e = l // 2
            h1, p = _norm_proj(f"proj_in_{l}", xs, norm1, *wref("in", e), n_lat)
            (qm,) = _rowfn(f"premix_{l}", _f_premix, [], [(p, a_w, j) for j in range(5)],
                           [(None, [a_w]), (_ACT, [a_w, b_w])], n_lat)
            cv_a = _dwconv(f"conv_a_{l}", qm, 0, a_w, conv_a_full[e], _ACT, n_lat)
            cv_b = _dwconv(f"conv_b_{l}", qm, 1, b_w, conv_b_full[e], _ACT, n_lat)
            post_params = [full(conv_b_bias[e]), full(ln_b_gain[e]), full(ln_b_bias[e])]
            (z,) = _rowfn(f"postmix_{l}", _f_postmix, post_params, [(p, a_w, 0), (cv_a, a_w, 0), (cv_b, b_w, 0)],
                          [(_ACT, [a_w, b_w])], n_lat)
            y1, xs = _proj_residual(f"proj_out_{l}", z, *wref("out", e), xs, mod_vec(l, 2), n_lat)
            sv.update(p=p, qm=qm, cv_a=cv_a, cv_b=cv_b, z=z)
        else:
            o = l // 2
            h1, p = _norm_proj(f"proj_qkv_{l}", xs, norm1, *wref("qkv", o), n_lat)
            qr, kr = _rope_fwd(f"rope_{l}", p, cos, sin, q_w, kv_w)
            sk = sinks[o].reshape(1, -1)
            z = _attn_fwd(f"attn_{l}", qr, kr, p, sk, n_lat, length, kv_w)
            y1, xs = _proj_residual(f"proj_o_{l}", z, *wref("o", o), xs, mod_vec(l, 2), n_lat)
            sv.update(p=p, qr=qr, kr=kr, z=z)
        sv.update(h1=h1, y1=y1, x_mid=xs)
        if l == 0:
            wgs[1] = _exchange_wait("gather_wait_1", gathers[1], [0], xs)
        norm2 = [full(norm_ffn[l]), (mod_vec(l, 3), "stream"), (mod_vec(l, 4), "stream")]
        h2, pu = _norm_proj(f"proj_up_{l}", xs, norm2, *wref("up", l), n_lat)
        f = _ffn_act(f"ffn_act_{l}", pu, conv_f_full[l], n_lat)
        y2, xs = _proj_residual(f"proj_down_{l}", f, *wref("down", l), xs, mod_vec(l, 5), n_lat)
        sv.update(h2=h2, pu=pu, f=f, y2=y2)
        saved.append(sv)

    loss_part, d_final_norm, dxs = _loss_head("loss_head", xs, final_norm.reshape(1, -1), loss_target[0], n_lat)
    loss = lax.psum(loss_part[0, 0], ("x", "y", "c"))

    wgrads = {}
    d_mod = [[None] * 6 for _ in range(depth)]
    d_norm_mix, d_norm_ffn = [None] * depth, [None] * depth
    d_conv_a, d_conv_b = [None] * n_even, [None] * n_even
    d_bias, d_ln_g, d_ln_b = [None] * n_even, [None] * n_even, [None] * n_even
    d_sinks = [None] * n_odd
    d_conv_f = [None] * depth
    exchanges, recvs, exchange_token = [None] * n_pieces, [None] * n_pieces, jnp.zeros((), _F32)

    def piece_parts(p):
        return ([wgrads[key].reshape(_NDEV, slab_r[key[0]], d) for key in piece_keys[p]],
                [slab_off[key] for key in piece_keys[p]])

    def start_exchange(p):
        parts, offsets = piece_parts(p)
        land = _fill_own_slot(f"exchange_fill_{p}", parts, offsets, piece_rows[p])
        exchanges[p] = _exchange_start(f"exchange_start_{p}", parts, land, offsets)
        return exchanges[p][-1][0, 0]

    for l in reversed(range(depth)):
        sv = saved[l]
        d_mod[l][5], dy2, df = _gate_proj_bwd(f"bwd_down_{l}", dxs, sv["y2"], mod_vec(l, 5) + exchange_token,
                                              *wref("down", l), n_lat)
        wgrads["down", l] = _mm_tn(f"wgrad_down_{l}", sv["f"], dy2, _ACT)
        dpu, d_conv_f[l] = _ffn_act_bwd(f"ffn_act_bwd_{l}", sv["pu"], df, conv_f_full[l], n_lat)
        wgrads["up", l] = _mm_tn(f"wgrad_up_{l}", dpu, sv["h2"], _ACT)
        norm2 = [full(norm_ffn[l]), (mod_vec(l, 3), "stream"), (mod_vec(l, 4), "stream")]
        (dgain, dsh, dsc), dxs = _proj_norm_bwd(f"bwd_up_{l}", dpu, *wref("up", l), sv["x_mid"], norm2, dxs, n_lat)
        d_norm_ffn[l], d_mod[l][3], d_mod[l][4] = dgain, dsh, dsc
        mix_token = start_exchange(1) if l == 0 else jnp.zeros((), _F32)
        gate1 = mod_vec(l, 2) + mix_token
        norm1 = [full(norm_mix[l]), (mod_vec(l, 0), "stream"), (mod_vec(l, 1), "stream")]
        if l % 2 == 0:
            e = l // 2
            d_mod[l][2], dy1, dz = _gate_proj_bwd(f"bwd_out_{l}", dxs, sv["y1"], gate1, *wref("out", e), n_lat)
            wgrads["out", e] = _mm_tn(f"wgrad_out_{l}", sv["z"], dy1, _ACT)
            post_params = [full(conv_b_bias[e]), full(ln_b_gain[e]), full(ln_b_bias[e])]
            (dbias, dlg, dlb), (dgb, dcv_a, dcv_b) = _rowfn_bwd(
                f"postmix_bwd_{l}", _f_postmix, post_params,
                [(sv["p"], a_w, 0), (sv["cv_a"], a_w, 0), (sv["cv_b"], b_w, 0)], [0, 1, 2], [(dz, [a_w, b_w])],
                [(_ACT, [0]), (_ACT, [1]), (_ACT, [2])], n_lat)
            d_bias[e], d_ln_g[e], d_ln_b[e] = dbias, dlg, dlb
            d_conv_a[e] = _dwconv_wgrad(f"conv_a_wgrad_{l}", dcv_a, sv["qm"], 0, a_w, conv_a_full.shape[1], n_lat)
            d_conv_b[e] = _dwconv_wgrad(f"conv_b_wgrad_{l}", dcv_b, sv["qm"], 1, b_w, conv_b_full.shape[1], n_lat)
            dq_a = _dwconv(f"conv_a_bwd_{l}", dcv_a, 0, a_w, conv_a_full[e][::-1], _ACT, n_lat)
            dq_b = _dwconv(f"conv_b_bwd_{l}", dcv_b, 0, b_w, conv_b_full[e][::-1], _ACT, n_lat)
            _, (dp,) = _rowfn_bwd(f"premix_bwd_{l}", _f_premix, [], [(sv["p"], a_w, j) for j in range(5)],
                                  [0, 1, 2, 3, 4], [(dgb, [a_w]), (dq_a, [a_w]), (dq_b, [b_w])],
                                  [(_ACT, [0, 1, 2, 3, 4])], n_lat)
            wgrads["in", e] = _mm_tn(f"wgrad_in_{l}", dp, sv["h1"], _ACT)
            if l == 0:
                norm1 = [full(norm_mix[l] + start_exchange(0))] + norm1[1:]
            (dgain, dsh, dsc), dxs = _proj_norm_bwd(f"bwd_in_{l}", dp, *wref("in", e), sv["x_in"], norm1, dxs, n_lat)
        else:
            o = l // 2
            d_mod[l][2], dy1, dz = _gate_proj_bwd(f"bwd_o_{l}", dxs, sv["y1"], gate1, *wref("o", o), n_lat)
            wgrads["o", o] = _mm_tn(f"wgrad_o_{l}", sv["z"], dy1, _ACT)
            sk = sinks[o].reshape(1, -1)
            dqr, dkr, dv, dsk = _attn_bwd(f"attn_bwd_{l}", sv["qr"], sv["kr"], sv["p"], sk, dz, n_lat, length, kv_w)
            d_sinks[o] = dsk
            dp = _rope_bwd(f"rope_bwd_{l}", dqr, dkr, dv, cos, sin)
            wgrads["qkv", o] = _mm_tn(f"wgrad_qkv_{l}", dp, sv["h1"], _ACT)
            (dgain, dsh, dsc), dxs = _proj_norm_bwd(f"bwd_qkv_{l}", dp, *wref("qkv", o), sv["x_in"], norm1, dxs, n_lat)
        d_norm_mix[l], d_mod[l][0], d_mod[l][1] = dgain, dsh, dsc
        if l > 0:
            exchange_token = start_exchange(1 + l)
    grad_x = dxs[:length][None]

    gsums = []
    for p in range(n_pieces):
        recvs[p] = _exchange_wait(f"exchange_wait_{p}", exchanges[p], piece_parts(p)[1], dxs)
        gsums.append(_sum_slots(f"sum_weight_grads_{p}", recvs[p]))

    def slab_grad(fam, count, transposed):
        mats = [gsums[piece_of[fam, i]][slab_off[fam, i]:slab_off[fam, i] + slab_r[fam]] for i in range(count)]
        return jnp.stack([m_.T if transposed else m_ for m_ in mats])

    grads = {
        "w_in_ab": slab_grad("in", n_even, True), "w_qkv": slab_grad("qkv", n_odd, True),
        "w_up": slab_grad("up", depth, True), "w_out_ab": slab_grad("out", n_even, False),
        "w_o": slab_grad("o", n_odd, False), "w_down": slab_grad("down", depth, False),
    }

    dm_dev = jnp.stack([jnp.concatenate([d_mod[l][j][:, 0, :] for j in range(6)], axis=-1)
                        for l in range(depth)])
    small_grads = [dm_dev, jnp.stack(d_norm_mix), jnp.stack(d_norm_ffn), jnp.stack(d_conv_a), jnp.stack(d_conv_b),
                   jnp.stack(d_bias), jnp.stack(d_ln_g), jnp.stack(d_ln_b), jnp.stack(d_sinks), jnp.stack(d_conv_f),
                   d_final_norm]
    sg_shapes = [a.shape for a in small_grads]
    sg_all = _all_gather_small("gather_small_grads", _pack(small_grads))
    sg_sum = _sum_slots("sum_small_grads", sg_all)
    (dm_sum, g_norm_mix, g_norm_ffn, g_conv_a, g_conv_b, g_bias, g_ln_g, g_ln_b, g_sinks, g_conv_f,
     g_final_norm) = _unpack(sg_sum.reshape(-1), sg_shapes)
    dm_each = _unpack(sg_all.reshape(_NDEV, -1), sg_shapes[:1])[0]

    def my_channels(a):
        width = a.shape[-1] // _NDEV
        return lax.dynamic_slice_in_dim(a, dev * width, width, axis=a.ndim - 1)

    grads["b_mod"] = dm_sum[:, 0] + dm_sum[:, 1]
    grads["norm_mix"] = g_norm_mix.reshape(depth, d)
    grads["norm_ffn"] = g_norm_ffn.reshape(depth, d)
    grads["conv_a"] = my_channels(g_conv_a)
    grads["conv_b"] = my_channels(g_conv_b)
    grads["conv_b_bias"] = g_bias.reshape(n_even, b_w)
    grads["ln_b_gain"] = g_ln_g.reshape(n_even, b_w)
    grads["ln_b_bias"] = g_ln_b.reshape(n_even, b_w)
    grads["sinks"] = g_sinks.reshape(n_odd, -1)
    grads["w_conv_ffn"] = my_channels(g_conv_f)
    grads["final_norm"] = g_final_norm.reshape(d)

    dm_rows = jnp.concatenate([jnp.moveaxis(dm_each[:, :, 0], 0, 1), dm_sum[:, 1:2],
                               jnp.zeros((depth, 16 - _NDEV - 1, 6 * d), _F32)], axis=1)
    dm_mine = my_channels(dm_rows)
    grads["w_mod"] = jnp.stack([_mm_tn(f"wgrad_mod_{l}", cond, dm_mine[l], _F32, silu_a=True) for l in range(depth)])
    dcond = _mod_backward_cond("mod_backward_cond", dm_mine, w_mod)
    dcond_all = _all_gather_small("gather_dcond", dcond)
    dcond_sum = _sum_slots("sum_dcond", dcond_all)[_NDEV]
    sg = jax.nn.sigmoid(c_ctx)
    grads["c_ctx"] = dcond_sum * (sg * (1.0 + c_ctx * (1.0 - sg)))

    big = ["w_mod", "w_in_ab", "w_out_ab", "w_qkv", "w_o", "w_up", "w_down"]
    small = [n for n in weight_names if n not in big]
    delta, new_m, new_v = {}, {}, {}
    for n in big:
        w = args[n]
        two_d = lambda a: a.reshape(-1, w.shape[-1])
        dl, nm, nv = _adamw(f"adamw_{n}", two_d(w), two_d(grads[n]), two_d(args["m_" + n]), two_d(args["v_" + n]))
        delta[n], new_m[n], new_v[n] = dl.reshape(w.shape), nm.reshape(w.shape), nv.reshape(w.shape)
    shapes = [args[n].shape for n in small]
    grads = {n: grads[n].reshape(args[n].shape) for n in grads}
    dl, nm, nv = _adamw("adamw_small", _pack([args[n] for n in small]), _pack([grads[n] for n in small]),
                        _pack([args["m_" + n] for n in small]), _pack([args["v_" + n] for n in small]))
    for res, packed in ((delta, dl), (new_m, nm), (new_v, nv)):
        for n, a in zip(small, _unpack(packed.reshape(-1), shapes)):
            res[n] = a

    return (loss, grad_x, *[grads[n] for n in weight_names], *[delta[n] for n in weight_names],
            *[new_m[n] for n in weight_names], *[new_v[n] for n in weight_names])
```

```python
import functools

import jax
import jax.numpy as jnp
from jax import lax
from jax.experimental import pallas as pl
from jax.experimental.pallas import tpu as pltpu

_F32 = jnp.float32
_MM = jnp.bfloat16
_ACT = jnp.bfloat16
_TM = 256
_HALO = 16
_LANES = 128
_CONV_ROWS = 128
_NDEV = 8
_HEAD_DIM = 64
_WINDOW = 128
_GRID_W = 64
_ROPE_THETA = 10000.0
_EPS = 1e-6
_NEG_INF = -1e30
_VMEM_LIMIT = 56 * 1024 * 1024
_ADAM = dict(lr=0.001, b1=0.9, b2=0.999, eps=1e-08, wd=0.01, step=10)
_MESH = pl.DeviceIdType.MESH


def _params(sem=None):
    return pltpu.CompilerParams(dimension_semantics=sem, vmem_limit_bytes=_VMEM_LIMIT)


def _divisor(n, cap, mult):
    if n <= cap:
        return n
    for d in range(cap - cap % mult, 0, -mult):
        if n % d == 0:
            return d
    raise ValueError(f"no tile for {n}")


def _my_coords():
    return lax.axis_index("x"), lax.axis_index("y"), lax.axis_index("c")


def _peer(k):
    x, y, c = _my_coords()
    px = 1 - x if k & 4 else x
    py = 1 - y if k & 2 else y
    pc = 1 - c if k & 1 else c
    return (px, py, pc), 4 * px + 2 * py + pc


def _all_gather_small(name, v):
    rows, cols = v.shape

    def body(v_ref, out_ref, send_sems, recv_sems):
        x, y, c = _my_coords()
        me = 4 * x + 2 * y + c
        out_ref[me] = v_ref[...]
        sends = []
        for k in range(1, _NDEV):
            peer, _ = _peer(k)
            cp = pltpu.make_async_remote_copy(
                src_ref=v_ref, dst_ref=out_ref.at[me], send_sem=send_sems.at[k - 1], recv_sem=recv_sems.at[k - 1],
                device_id=peer, device_id_type=_MESH)
            cp.start()
            sends.append(cp)
        for k in range(1, _NDEV):
            peer, pid = _peer(k)
            pltpu.make_async_remote_copy(
                src_ref=v_ref, dst_ref=out_ref.at[pid], send_sem=send_sems.at[k - 1], recv_sem=recv_sems.at[k - 1],
                device_id=peer, device_id_type=_MESH).wait_recv()
        for cp in sends:
            cp.wait_send()

    return pl.pallas_call(
        body, name=name,
        out_shape=jax.ShapeDtypeStruct((_NDEV, rows, cols), v.dtype),
        in_specs=[pl.BlockSpec(memory_space=pltpu.VMEM)],
        out_specs=pl.BlockSpec(memory_space=pltpu.VMEM),
        scratch_shapes=[pltpu.SemaphoreType.DMA((_NDEV - 1,)), pltpu.SemaphoreType.DMA((_NDEV - 1,))],
        compiler_params=pltpu.CompilerParams(vmem_limit_bytes=_VMEM_LIMIT),
    )(v)


def _sum_slots(name, v):
    _, rows, cols = v.shape
    tr = _divisor(rows, 1024, 16)

    def body(v_ref, o_ref):
        acc = v_ref[0].astype(_F32)
        for e in range(1, _NDEV):
            acc = acc + v_ref[e].astype(_F32)
        o_ref[...] = acc

    return pl.pallas_call(
        body, name=name, grid=(rows // tr,),
        out_shape=jax.ShapeDtypeStruct((rows, cols), _F32),
        in_specs=[pl.BlockSpec((_NDEV, tr, cols), lambda i: (0, i, 0))],
        out_specs=pl.BlockSpec((tr, cols), lambda i: (i, 0)),
        compiler_params=_params(("arbitrary",)),
    )(v)


_HBM_SPEC = pl.BlockSpec(memory_space=pltpu.HBM)
_SEM_SPEC = pl.BlockSpec(memory_space=pltpu.SEMAPHORE)
_EFFECT = pltpu.SideEffectType.DATAFLOW_SIDE_EFFECTING


def _in_hbm(a):
    return pltpu.with_memory_space_constraint(a, pltpu.HBM)


def _block_for(ref, device):
    return ref if len(ref.shape) == 2 else ref.at[device]


def _fill_own_slot(name, srcs, offsets, total_rows):
    n = len(srcs)
    cols = srcs[0].shape[-1]

    def body(*refs):
        src_refs, out_ref, bufs, sems = refs[:n], refs[n], refs[n + 1:2 * n + 1], refs[2 * n + 1]
        x, y, c = _my_coords()
        me = 4 * x + 2 * y + c
        loads = [pltpu.make_async_copy(_block_for(src_refs[m], me), bufs[m], sems.at[0, m]) for m in range(n)]
        stores = [pltpu.make_async_copy(bufs[m], out_ref.at[me, pl.ds(offsets[m], srcs[m].shape[-2]), :],
                                        sems.at[1, m]) for m in range(n)]
        for copies in (loads, stores):
            for cp in copies:
                cp.start()
            for cp in copies:
                cp.wait()

    return pl.pallas_call(
        body, name=name,
        out_shape=jax.ShapeDtypeStruct((_NDEV, total_rows, cols), srcs[0].dtype),
        in_specs=[pl.BlockSpec(memory_space=pl.ANY)] * n,
        out_specs=pl.BlockSpec(memory_space=pl.ANY),
        scratch_shapes=[pltpu.VMEM(s.shape[-2:], s.dtype) for s in srcs] + [pltpu.SemaphoreType.DMA((2, n))],
        compiler_params=pltpu.CompilerParams(vmem_limit_bytes=_VMEM_LIMIT),
    )(*srcs)


def _exchange_start(name, srcs, land, offsets):
    n = len(srcs)

    def body(*refs):
        src_refs, land_ref = refs[:n], refs[n]
        send_sems, recv_sems, token = refs[n + 1], refs[n + 2], refs[-1]
        x, y, c = _my_coords()
        me = 4 * x + 2 * y + c
        for k in range(1, _NDEV):
            peer, pid = _peer(k)
            for m in range(n):
                pltpu.make_async_remote_copy(
                    src_ref=_block_for(src_refs[m], pid),
                    dst_ref=land_ref.at[me, pl.ds(offsets[m], srcs[m].shape[-2]), :],
                    send_sem=send_sems, recv_sem=recv_sems, device_id=peer, device_id_type=_MESH).start()
        token[...] = jnp.zeros_like(token)

    sems = pltpu.SemaphoreType.DMA(())
    return pl.pallas_call(
        body, name=name,
        out_shape=(sems, sems, *[pltpu.HBM(s.shape, s.dtype) for s in srcs], pltpu.HBM(land.shape, land.dtype),
                   jax.ShapeDtypeStruct((8, 128), _F32)),
        in_specs=[_HBM_SPEC] * (n + 1),
        out_specs=(_SEM_SPEC, _SEM_SPEC, *[_HBM_SPEC] * (n + 1), pl.BlockSpec(memory_space=pltpu.VMEM)),
        input_output_aliases={i: 2 + i for i in range(n + 1)},
        compiler_params=pltpu.CompilerParams(has_side_effects=_EFFECT),
    )(*[_in_hbm(s) for s in srcs], _in_hbm(land))


def _exchange_wait(name, started, offsets, after):
    send_sems, recv_sems = started[0], started[1]
    srcs, land = list(started[2:-2]), started[-2]
    n = len(srcs)

    def body(*refs):
        src_refs, land_ref = refs[:n], refs[n]
        send_sems_, recv_sems_ = refs[n + 1], refs[n + 2]
        others = land_ref.at[pl.ds(0, _NDEV - 1)]
        cp = pltpu.make_async_remote_copy(src_ref=others, dst_ref=others, send_sem=send_sems_, recv_sem=recv_sems_,
                                          device_id=_peer(1)[0], device_id_type=_MESH)
        cp.wait_send()
        cp.wait_recv()

    res = pl.pallas_call(
        body, name=name,
        out_shape=(*[pltpu.HBM(s.shape, s.dtype) for s in srcs], pltpu.HBM(land.shape, land.dtype)),
        in_specs=[_HBM_SPEC] * (n + 1) + [_SEM_SPEC, _SEM_SPEC, pl.BlockSpec(memory_space=pl.ANY)],
        out_specs=tuple([_HBM_SPEC] * (n + 1)),
        input_output_aliases={i: i for i in range(n + 1)},
        compiler_params=pltpu.CompilerParams(has_side_effects=_EFFECT),
    )(*srcs, land, send_sems, recv_sems, after)
    return res[n]


def _load_weight(wg_ref, wbuf, sems, off, r, step):
    @pl.when(step == 0)
    def _():
        copies = [pltpu.make_async_copy(wg_ref.at[e, pl.ds(off, r), :], wbuf.at[pl.ds(e * r, r), :], sems.at[e])
                  for e in range(_NDEV)]
        for cp in copies:
            cp.start()
        for cp in copies:
            cp.wait()


def _mm_tn(name, a, b, out_dtype, silu_a=False):
    rows, na = a.shape
    nb = b.shape[1]
    tr = _divisor(rows, 1536, 16)
    tn = _divisor(na, 1536, 128)
    steps = rows // tr

    def body(a_ref, b_ref, o_ref, acc):
        t = pl.program_id(1)

        @pl.when(t == 0)
        def _():
            acc[...] = jnp.zeros_like(acc)

        av = a_ref[...]
        if silu_a:
            av = av.astype(_F32)
            av = av * jax.nn.sigmoid(av)
        acc[...] += lax.dot_general(av.astype(_MM), b_ref[...].astype(_MM), (((0,), (0,)), ((), ())),
                                    preferred_element_type=_F32)

        @pl.when(t == steps - 1)
        def _():
            o_ref[...] = acc[...].astype(out_dtype)

    return pl.pallas_call(
        body, name=name, grid=(na // tn, steps),
        out_shape=jax.ShapeDtypeStruct((na, nb), out_dtype),
        in_specs=[pl.BlockSpec((tr, tn), lambda j, t: (t, j)), pl.BlockSpec((tr, nb), lambda j, t: (t, 0))],
        out_specs=pl.BlockSpec((tn, nb), lambda j, t: (j, 0)),
        scratch_shapes=[pltpu.VMEM((tn, nb), _F32)],
        compiler_params=_params(("arbitrary", "arbitrary")),
    )(a, b)


def _mod_forward(name, cond, w_mod):
    depth, d, n = w_mod.shape
    rows = cond.shape[0]

    def body(c_ref, w_ref, o_ref):
        cv = c_ref[...]
        a = (cv * jax.nn.sigmoid(cv)).astype(_MM)
        o_ref[...] = jnp.dot(a, w_ref[...].astype(_MM), preferred_element_type=_F32)

    return pl.pallas_call(
        body, name=name, grid=(depth,),
        out_shape=jax.ShapeDtypeStruct((depth, rows, n), _F32),
        in_specs=[pl.BlockSpec((rows, d), lambda l: (0, 0)), pl.BlockSpec((None, d, n), lambda l: (l, 0, 0))],
        out_specs=pl.BlockSpec((None, rows, n), lambda l: (l, 0, 0)),
        compiler_params=_params(("arbitrary",)),
    )(cond, w_mod)


def _mod_backward_cond(name, dm, w_mod):
    depth, d, n = w_mod.shape
    rows = dm.shape[1]

    def body(g_ref, w_ref, o_ref):
        @pl.when(pl.program_id(0) == 0)
        def _():
            o_ref[...] = jnp.zeros_like(o_ref)

        o_ref[...] += lax.dot_general(g_ref[...].astype(_MM), w_ref[...].astype(_MM), (((1,), (1,)), ((), ())),
                                      preferred_element_type=_F32)

    return pl.pallas_call(
        body, name=name, grid=(depth,),
        out_shape=jax.ShapeDtypeStruct((rows, d), _F32),
        in_specs=[pl.BlockSpec((None, rows, n), lambda l: (l, 0, 0)), pl.BlockSpec((None, d, n), lambda l: (l, 0, 0))],
        out_specs=pl.BlockSpec((rows, d), lambda l: (0, 0)),
        compiler_params=_params(("arbitrary",)),
    )(dm, w_mod)


def _param_spec(arr, kind, n_lat):
    if kind == "stream":
        return pl.BlockSpec((None,) + arr.shape[1:], lambda i: (i // n_lat, 0, 0))
    return pl.BlockSpec(arr.shape, lambda i: (0,) * arr.ndim)


def _rowfn(name, fn, params, xs, outs, n_lat):
    rows = xs[0][0].shape[0]
    np_, nx = len(params), len(xs)
    stored = [(dt, ws) for dt, ws in outs if dt is not None]

    def body(*refs):
        ps = [r[...].astype(_F32) for r in refs[:np_]]
        xv = [r[...].astype(_F32) for r in refs[np_:np_ + nx]]
        pieces = fn(ps, xv)
        o_refs = iter(refs[np_ + nx:])
        k = 0
        for dt, ws in outs:
            o_ref = next(o_refs) if dt is not None else None
            off = 0
            for w in ws:
                if o_ref is not None:
                    o_ref[:, off:off + w] = pieces[k].astype(dt)
                off += w
                k += 1

    return pl.pallas_call(
        body, name=name, grid=(rows // _TM,),
        out_shape=[jax.ShapeDtypeStruct((rows, sum(ws)), dt) for dt, ws in stored],
        in_specs=[_param_spec(a, kind, n_lat) for a, kind in params]
        + [pl.BlockSpec((_TM, w), lambda i, cb=cb: (i, cb)) for _, w, cb in xs],
        out_specs=[pl.BlockSpec((_TM, sum(ws)), lambda i: (i, 0)) for _, ws in stored],
        compiler_params=_params(("arbitrary",)),
    )(*[a for a, _ in params], *[a for a, _, _ in xs])


def _rowfn_bwd(name, fn, params, xs, diff, douts, dx_outs, n_lat):
    rows = xs[0][0].shape[0]
    np_, nx, nd = len(params), len(xs), len(douts)
    nt = rows // _TM

    def body(*refs):
        i = pl.program_id(0)
        ps = [r[...].astype(_F32) for r in refs[:np_]]
        xv = [r[...].astype(_F32) for r in refs[np_:np_ + nx]]
        d_refs = refs[np_ + nx:np_ + nx + nd]
        dp_refs = refs[np_ + nx + nd:np_ + nx + nd + np_]
        dx_refs = refs[np_ + nx + nd + np_:]

        def f(ps_, xd):
            full = list(xv)
            for j, v in zip(diff, xd):
                full[j] = v
            return fn(ps_, full)

        _, vjp = jax.vjp(f, ps, [xv[j] for j in diff])
        cts = []
        for d_ref, (_, ws) in zip(d_refs, douts):
            off = 0
            for w in ws:
                cts.append(d_ref[:, off:off + w].astype(_F32))
                off += w
        dps, dxd = vjp(cts)
        grads = dict(zip(diff, dxd))
        _accumulate_params(dp_refs, [kind for _, kind in params], dps, i, n_lat)
        for dx_ref, (dt, idxs) in zip(dx_refs, dx_outs):
            off = 0
            for j in idxs:
                w = xs[j][1]
                dx_ref[:, off:off + w] = grads[j].astype(dt)
                off += w

    dp_shapes = [jax.ShapeDtypeStruct(a.shape, _F32) for a, _ in params]
    dx_shapes = [jax.ShapeDtypeStruct((rows, sum(xs[j][1] for j in idxs)), dt) for dt, idxs in dx_outs]
    in_specs = ([_param_spec(a, kind, n_lat) for a, kind in params]
                + [pl.BlockSpec((_TM, w), lambda i, cb=cb: (i, cb)) for _, w, cb in xs]
                + [pl.BlockSpec((_TM, sum(ws)), lambda i: (i, 0)) for _, ws in douts])
    operands = [a for a, _ in params] + [a for a, _, _ in xs] + [a for a, _ in douts]
    res = pl.pallas_call(
        body, name=name, grid=(nt,),
        out_shape=dp_shapes + dx_shapes,
        in_specs=in_specs,
        out_specs=[_param_spec(a, kind, n_lat) for a, kind in params]
        + [pl.BlockSpec((_TM, s.shape[1]), lambda i: (i, 0)) for s in dx_shapes],
        compiler_params=_params(("arbitrary",)),
    )(*operands)
    return list(res[:np_]), list(res[np_:])


def _f_norm_mod(ps, xs):
    gain, shift, scale = ps
    (x,) = xs
    y = x * lax.rsqrt(jnp.mean(x * x, axis=-1, keepdims=True) + _EPS) * gain
    return [y * (1.0 + scale) + shift]


def _f_premix(ps, xs):
    g_b, g_c, u_a, v_b, gate_b = xs
    return [g_b, g_c * u_a, v_b * jax.nn.sigmoid(gate_b)]


def _f_postmix(ps, xs):
    bias, ln_g, ln_b = ps
    g_b, cv_a, cv_b = xs
    u = cv_b + bias
    mu = jnp.mean(u, axis=-1, keepdims=True)
    var = jnp.mean(jnp.square(u - mu), axis=-1, keepdims=True)
    y = (u - mu) * lax.rsqrt(var + _EPS) * ln_g + ln_b
    return [g_b * cv_a, y * jax.nn.sigmoid(y)]


def _accumulate_params(dp_refs, kinds, dps, i, n_lat):
    for dp_ref, kind, dp in zip(dp_refs, kinds, dps):
        first = (i == 0) | (i == n_lat) if kind == "stream" else i == 0

        @pl.when(first)
        def _(dp_ref=dp_ref):
            dp_ref[...] = jnp.zeros_like(dp_ref)

        dp_ref[...] += dp


def _row_spec(width):
    return pl.BlockSpec((_TM, width), lambda i: (i, 0))


def _weight_scratch(shape, wg):
    return [pltpu.VMEM(shape, wg.dtype), pltpu.SemaphoreType.DMA((_NDEV,))]


def _norm_proj(name, x, params, wg, off, r, n_lat):
    rows, d = x.shape
    n = _NDEV * r
    chunk = _divisor(n, 512, 128)

    def body(g_ref, sh_ref, sc_ref, x_ref, wg_ref, h_ref, p_ref, wbuf, sems):
        _load_weight(wg_ref, wbuf, sems, off, r, pl.program_id(0))
        (h,) = _f_norm_mod([g_ref[...], sh_ref[...], sc_ref[...]], [x_ref[...]])
        hb = h.astype(_MM)
        h_ref[...] = hb.astype(h_ref.dtype)
        for j in range(n // chunk):
            p_ref[:, j * chunk:(j + 1) * chunk] = lax.dot_general(
                hb, wbuf[j * chunk:(j + 1) * chunk, :], (((1,), (1,)), ((), ())),
                preferred_element_type=_F32).astype(p_ref.dtype)

    return pl.pallas_call(
        body, name=name, grid=(rows // _TM,),
        out_shape=[jax.ShapeDtypeStruct((rows, d), _ACT), jax.ShapeDtypeStruct((rows, n), _ACT)],
        in_specs=[_param_spec(a, kind, n_lat) for a, kind in params] + [_row_spec(d), pl.BlockSpec(memory_space=pl.ANY)],
        out_specs=[_row_spec(d), _row_spec(n)],
        scratch_shapes=_weight_scratch((n, d), wg),
        compiler_params=_params(("arbitrary",)),
    )(*[a for a, _ in params], x, wg)


def _proj_residual(name, a, wg, off, r, x, gate, n_lat):
    rows, kdim = a.shape
    d = x.shape[1]
    assert kdim == _NDEV * r

    def body(g_ref, a_ref, x_ref, wg_ref, y_ref, o_ref, wbuf, sems):
        _load_weight(wg_ref, wbuf, sems, off, r, pl.program_id(0))
        y = jnp.dot(a_ref[...].astype(_MM), wbuf[...], preferred_element_type=_F32)
        y_ref[...] = y.astype(y_ref.dtype)
        o_ref[...] = x_ref[...] + g_ref[...] * y

    return pl.pallas_call(
        body, name=name, grid=(rows // _TM,),
        out_shape=[jax.ShapeDtypeStruct((rows, d), _ACT), jax.ShapeDtypeStruct((rows, d), _F32)],
        in_specs=[_param_spec(gate, "stream", n_lat), _row_spec(kdim), _row_spec(d), pl.BlockSpec(memory_space=pl.ANY)],
        out_specs=[_row_spec(d), _row_spec(d)],
        scratch_shapes=_weight_scratch((kdim, d), wg),
        compiler_params=_params(("arbitrary",)),
    )(gate, a, x, wg)


def _gate_proj_bwd(name, dx, y, gate, wg, off, r, n_lat):
    rows, d = dx.shape
    n = _NDEV * r
    chunk = _divisor(n, 512, 128)

    def body(g_ref, dx_ref, y_ref, wg_ref, dg_ref, dy_ref, dz_ref, wbuf, sems):
        i = pl.program_id(0)
        _load_weight(wg_ref, wbuf, sems, off, r, i)
        dxv = dx_ref[...]
        _accumulate_params([dg_ref], ["stream"], [jnp.sum(dxv * y_ref[...].astype(_F32), axis=0, keepdims=True)],
                           i, n_lat)
        dy = (g_ref[...] * dxv).astype(_MM)
        dy_ref[...] = dy.astype(dy_ref.dtype)
        for j in range(n // chunk):
            dz_ref[:, j * chunk:(j + 1) * chunk] = lax.dot_general(
                dy, wbuf[j * chunk:(j + 1) * chunk, :], (((1,), (1,)), ((), ())),
                preferred_element_type=_F32).astype(dz_ref.dtype)

    return pl.pallas_call(
        body, name=name, grid=(rows // _TM,),
        out_shape=[jax.ShapeDtypeStruct(gate.shape, _F32), jax.ShapeDtypeStruct((rows, d), _ACT),
                   jax.ShapeDtypeStruct((rows, n), _ACT)],
        in_specs=[_param_spec(gate, "stream", n_lat), _row_spec(d), _row_spec(d), pl.BlockSpec(memory_space=pl.ANY)],
        out_specs=[_param_spec(gate, "stream", n_lat), _row_spec(d), _row_spec(n)],
        scratch_shapes=_weight_scratch((n, d), wg),
        compiler_params=_params(("arbitrary",)),
    )(gate, dx, y, wg)


def _proj_norm_bwd(name, dp, wg, off, r, x, params, dx_in, n_lat):
    rows, kdim = dp.shape
    d = x.shape[1]
    assert kdim == _NDEV * r
    kinds = [kind for _, kind in params]

    def body(g_ref, sh_ref, sc_ref, dp_ref, x_ref, dxin_ref, wg_ref, dg_ref, dsh_ref, dsc_ref, dx_ref, wbuf, sems):
        i = pl.program_id(0)
        _load_weight(wg_ref, wbuf, sems, off, r, i)
        dh = jnp.dot(dp_ref[...].astype(_MM), wbuf[...], preferred_element_type=_F32)
        _, vjp = jax.vjp(lambda ps, xv: _f_norm_mod(ps, [xv]), [g_ref[...], sh_ref[...], sc_ref[...]], x_ref[...])
        dps, dxn = vjp([dh])
        _accumulate_params([dg_ref, dsh_ref, dsc_ref], kinds, dps, i, n_lat)
        dx_ref[...] = dxin_ref[...] + dxn

    specs = [_param_spec(a, kind, n_lat) for a, kind in params]
    res = pl.pallas_call(
        body, name=name, grid=(rows // _TM,),
        out_shape=[jax.ShapeDtypeStruct(a.shape, _F32) for a, _ in params] + [jax.ShapeDtypeStruct((rows, d), _F32)],
        in_specs=specs + [_row_spec(kdim), _row_spec(d), _row_spec(d), pl.BlockSpec(memory_space=pl.ANY)],
        out_specs=specs + [_row_spec(d)],
        scratch_shapes=_weight_scratch((kdim, d), wg),
        compiler_params=_params(("arbitrary",)),
    )(*[a for a, _ in params], dp, x, dx_in, wg)
    return list(res[:3]), res[3]


def _conv_halo_specs(width, cb0, n_rows):
    per = _TM // _HALO
    last = n_rows // _HALO - 1
    return [
        pl.BlockSpec((_TM, width), lambda i, j: (i, cb0 + j)),
        pl.BlockSpec((_HALO, width), lambda i, j: (jnp.maximum(i * per - 1, 0), cb0 + j)),
        pl.BlockSpec((_HALO, width), lambda i, j: (jnp.minimum((i + 1) * per, last), cb0 + j)),
    ]


def _conv_window(main_ref, prev_ref, next_ref, r0, cols, i, n_lat, nt):
    if r0 == 0:
        has_prev = (i != 0) & (i != n_lat)
        head = jnp.where(has_prev, prev_ref[:, cols].astype(_F32), 0.0)
    else:
        head = main_ref[r0 - _HALO:r0, cols].astype(_F32)
    if r0 + _CONV_ROWS == _TM:
        has_next = (i != n_lat - 1) & (i != nt - 1)
        tail = jnp.where(has_next, next_ref[:, cols].astype(_F32), 0.0)
    else:
        tail = main_ref[r0 + _CONV_ROWS:r0 + _CONV_ROWS + _HALO, cols].astype(_F32)
    return jnp.concatenate([head, main_ref[r0:r0 + _CONV_ROWS, cols].astype(_F32), tail], axis=0)


def _shifted(win, offset):
    n = win.shape[0]
    rolled = win if offset == 0 else pltpu.roll(win, (-offset) % n, 0)
    return rolled[_HALO:_HALO + _CONV_ROWS]


def _dwconv(name, x, cb0, channels, taps, out_dtype, n_lat):
    rows = x.shape[0]
    ktaps = taps.shape[0]
    half = ktaps // 2
    width = _divisor(channels, 1536, 128)
    assert (cb0 * channels) % width == 0
    cb0 = cb0 * channels // width
    nt = rows // _TM

    def body(main_ref, prev_ref, next_ref, taps_ref, o_ref):
        i = pl.program_id(0)

        def chunk(j, carry):
            cols = pl.ds(pl.multiple_of(j * _LANES, _LANES), _LANES)
            for r0 in range(0, _TM, _CONV_ROWS):
                win = _conv_window(main_ref, prev_ref, next_ref, r0, cols, i, n_lat, nt)
                acc = taps_ref[0:1, cols] * _shifted(win, -half)
                for k in range(1, ktaps):
                    acc = acc + taps_ref[k:k + 1, cols] * _shifted(win, k - half)
                o_ref[r0:r0 + _CONV_ROWS, cols] = acc.astype(out_dtype)
            return carry

        lax.fori_loop(0, width // _LANES, chunk, 0)

    return pl.pallas_call(
        body, name=name, grid=(nt, channels // width),
        out_shape=jax.ShapeDtypeStruct((rows, channels), out_dtype),
        in_specs=_conv_halo_specs(width, cb0, rows) + [pl.BlockSpec((ktaps, width), lambda i, j: (0, j))],
        out_specs=pl.BlockSpec((_TM, width), lambda i, j: (i, j)),
        compiler_params=_params(("arbitrary", "arbitrary")),
    )(x, x, x, taps)


def _dwconv_wgrad(name, dy, x, cb0, channels, ktaps, n_lat):
    rows = x.shape[0]
    half = ktaps // 2
    width = _divisor(channels, 1536, 128)
    cb0 = cb0 * channels // width
    nt = rows // _TM

    def body(dy_ref, main_ref, prev_ref, next_ref, o_ref):
        i = pl.program_id(1)

        @pl.when(i == 0)
        def _():
            o_ref[...] = jnp.zeros_like(o_ref)

        def chunk(j, carry):
            cols = pl.ds(pl.multiple_of(j * _LANES, _LANES), _LANES)
            for r0 in range(0, _TM, _CONV_ROWS):
                dyv = dy_ref[r0:r0 + _CONV_ROWS, cols].astype(_F32)
                win = _conv_window(main_ref, prev_ref, next_ref, r0, cols, i, n_lat, nt)
                for k in range(ktaps):
                    o_ref[k:k + 1, cols] += jnp.sum(dyv * _shifted(win, k - half), axis=0, keepdims=True)
            return carry

        lax.fori_loop(0, width // _LANES, chunk, 0)

    per = _TM // _HALO
    last = rows // _HALO - 1
    return pl.pallas_call(
        body, name=name, grid=(channels // width, nt),
        out_shape=jax.ShapeDtypeStruct((ktaps, channels), _F32),
        in_specs=[
            pl.BlockSpec((_TM, width), lambda j, i: (i, j)),
            pl.BlockSpec((_TM, width), lambda j, i: (i, cb0 + j)),
            pl.BlockSpec((_HALO, width), lambda j, i: (jnp.maximum(i * per - 1, 0), cb0 + j)),
            pl.BlockSpec((_HALO, width), lambda j, i: (jnp.minimum((i + 1) * per, last), cb0 + j)),
        ],
        out_specs=pl.BlockSpec((ktaps, width), lambda j, i: (0, j)),
        compiler_params=_params(("arbitrary", "arbitrary")),
    )(dy, x, x, x)


def _ffn_halo_specs(width, n_rows):
    per = _TM // _HALO
    last = n_rows // _HALO - 1
    return [pl.BlockSpec((_TM, width), lambda i: (i, 0)),
            pl.BlockSpec((_HALO, width), lambda i: (jnp.maximum(i * per - 1, 0), 0)),
            pl.BlockSpec((_HALO, width), lambda i: (jnp.minimum((i + 1) * per, last), 0))]


def _ffn_act(name, pu, taps, n_lat):
    rows, c2 = pu.shape
    ff = c2 // 2
    ktaps = taps.shape[0]
    half = ktaps // 2
    nt = rows // _TM

    def body(main_ref, prev_ref, next_ref, taps_ref, o_ref):
        i = pl.program_id(0)

        def conv(cols, r0):
            win = _conv_window(main_ref, prev_ref, next_ref, r0, cols, i, n_lat, nt)
            acc = taps_ref[0:1, cols] * _shifted(win, -half)
            for k in range(1, ktaps):
                acc = acc + taps_ref[k:k + 1, cols] * _shifted(win, k - half)
            return acc

        def chunk(j, carry):
            c0 = pl.multiple_of(j * _LANES, _LANES)
            cols_a, cols_g = pl.ds(c0, _LANES), pl.ds(pl.multiple_of(ff + c0, _LANES), _LANES)
            for r0 in range(0, _TM, _CONV_ROWS):
                ua, ug = conv(cols_a, r0), conv(cols_g, r0)
                o_ref[r0:r0 + _CONV_ROWS, cols_a] = (ug * jax.nn.sigmoid(ug) * ua).astype(o_ref.dtype)
            return carry

        lax.fori_loop(0, ff // _LANES, chunk, 0)

    return pl.pallas_call(
        body, name=name, grid=(nt,),
        out_shape=jax.ShapeDtypeStruct((rows, ff), _ACT),
        in_specs=_ffn_halo_specs(c2, rows) + [pl.BlockSpec((ktaps, c2), lambda i: (0, 0))],
        out_specs=pl.BlockSpec((_TM, ff), lambda i: (i, 0)),
        compiler_params=_params(("arbitrary",)),
    )(pu, pu, pu, taps)


def _ffn_act_bwd(name, pu, df, taps, n_lat):
    rows, c2 = pu.shape
    ff = c2 // 2
    ktaps = taps.shape[0]
    half = ktaps // 2
    nt = rows // _TM
    inner = slice(_HALO, _HALO + _CONV_ROWS)

    def body(main_ref, prev_ref, next_ref, dmain_ref, dprev_ref, dnext_ref, taps_ref, dpu_ref, dt_ref):
        i = pl.program_id(0)

        @pl.when(i == 0)
        def _():
            dt_ref[...] = jnp.zeros_like(dt_ref)

        def chunk(j, carry):
            c0 = pl.multiple_of(j * _LANES, _LANES)
            cols_a, cols_g = pl.ds(c0, _LANES), pl.ds(pl.multiple_of(ff + c0, _LANES), _LANES)
            for r0 in range(0, _TM, _CONV_ROWS):
                wins = [_conv_window(main_ref, prev_ref, next_ref, r0, cols, i, n_lat, nt) for cols in (cols_a, cols_g)]
                n = wins[0].shape[0]
                shifted = [[w if k == half else pltpu.roll(w, (half - k) % n, 0) for k in range(ktaps)] for w in wins]
                ua, ug = [sum(taps_ref[k:k + 1, cols] * sh[k] for k in range(ktaps))
                          for cols, sh in zip((cols_a, cols_g), shifted)]
                dfw = _conv_window(dmain_ref, dprev_ref, dnext_ref, r0, cols_a, i, n_lat, nt)
                sig = jax.nn.sigmoid(ug)
                d_a = dfw * (ug * sig)
                d_g = dfw * ua * (sig * (1.0 + ug * (1.0 - sig)))
                for du, cols, sh in ((d_a, cols_a, shifted[0]), (d_g, cols_g, shifted[1])):
                    acc = taps_ref[0:1, cols] * _shifted(du, half)
                    for k in range(1, ktaps):
                        acc = acc + taps_ref[k:k + 1, cols] * _shifted(du, half - k)
                    dpu_ref[r0:r0 + _CONV_ROWS, cols] = acc.astype(dpu_ref.dtype)
                    for k in range(ktaps):
                        dt_ref[k:k + 1, cols] += jnp.sum(du[inner] * sh[k][inner], axis=0, keepdims=True)
            return carry

        lax.fori_loop(0, ff // _LANES, chunk, 0)

    return pl.pallas_call(
        body, name=name, grid=(nt,),
        out_shape=[jax.ShapeDtypeStruct((rows, c2), _ACT), jax.ShapeDtypeStruct((ktaps, c2), _F32)],
        in_specs=_ffn_halo_specs(c2, rows) + _ffn_halo_specs(ff, rows) + [pl.BlockSpec((ktaps, c2), lambda i: (0, 0))],
        out_specs=[pl.BlockSpec((_TM, c2), lambda i: (i, 0)), pl.BlockSpec((ktaps, c2), lambda i: (0, 0))],
        compiler_params=_params(("arbitrary",)),
    )(pu, pu, pu, df, df, df, taps)


def _rope_tables(length, ctx_len):
    t = jnp.arange(length)
    row = (t // _GRID_W).astype(_F32)
    col = (t % _GRID_W).astype(_F32)
    n_freq = _HEAD_DIM // 4
    inv_freq = _ROPE_THETA ** (-jnp.arange(n_freq, dtype=_F32) / n_freq)
    ang = jnp.concatenate([row[:, None] * inv_freq, col[:, None] * inv_freq], axis=-1)
    cos, sin = jnp.cos(ang), jnp.sin(ang)
    cos = jnp.concatenate([cos, jnp.ones((ctx_len, _HEAD_DIM // 2), _F32)], axis=0)
    sin = jnp.concatenate([sin, jnp.zeros((ctx_len, _HEAD_DIM // 2), _F32)], axis=0)
    return jnp.tile(cos, (1, 4)), jnp.tile(jnp.concatenate([-sin, sin], axis=-1), (1, 2))


def _rotate(v, cos_ref, sin_ref):
    width = v.shape[1]
    reps = width // 128
    cos = jnp.tile(cos_ref[...], (1, reps))
    sin = jnp.tile(sin_ref[...], (1, reps))
    return v * cos, sin, width


def _partner(v):
    width = v.shape[1]
    half = _HEAD_DIM // 2
    lane = lax.broadcasted_iota(jnp.int32, v.shape, 1)
    return jnp.where(lane % _HEAD_DIM < half, pltpu.roll(v, width - half, 1), pltpu.roll(v, half, 1))


def _rope_fwd(name, p, cos, sin, q_w, kv_w):
    rows, width = p.shape
    scale = _HEAD_DIM ** -0.5

    def body(p_ref, cos_ref, sin_ref, q_ref, k_ref):
        v = p_ref[:, :q_w + kv_w].astype(_F32)
        vc, s, _ = _rotate(v, cos_ref, sin_ref)
        y = vc + _partner(v) * s
        q_ref[...] = (y[:, :q_w] * scale).astype(q_ref.dtype)
        k_ref[...] = y[:, q_w:].astype(k_ref.dtype)

    return pl.pallas_call(
        body, name=name, grid=(rows // _TM,),
        out_shape=[jax.ShapeDtypeStruct((rows, q_w), _ACT), jax.ShapeDtypeStruct((rows, kv_w), _ACT)],
        in_specs=[pl.BlockSpec((_TM, width), lambda i: (i, 0)), pl.BlockSpec((_TM, 128), lambda i: (i, 0)),
                  pl.BlockSpec((_TM, 128), lambda i: (i, 0))],
        out_specs=[pl.BlockSpec((_TM, q_w), lambda i: (i, 0)), pl.BlockSpec((_TM, kv_w), lambda i: (i, 0))],
        compiler_params=_params(("arbitrary",)),
    )(p, cos, sin)


def _rope_bwd(name, dq, dk, dv, cos, sin):
    rows, q_w = dq.shape
    kv_w = dk.shape[1]
    scale = _HEAD_DIM ** -0.5

    def body(dq_ref, dk_ref, dv_ref, cos_ref, sin_ref, o_ref):
        dy = jnp.concatenate([dq_ref[...].astype(_F32) * scale, dk_ref[...].astype(_F32)], axis=1)
        dyc, s, _ = _rotate(dy, cos_ref, sin_ref)
        o_ref[:, :q_w + kv_w] = (dyc + _partner(dy * s)).astype(o_ref.dtype)
        o_ref[:, q_w + kv_w:] = dv_ref[...].astype(o_ref.dtype)

    return pl.pallas_call(
        body, name=name, grid=(rows // _TM,),
        out_shape=jax.ShapeDtypeStruct((rows, q_w + 2 * kv_w), _ACT),
        in_specs=[pl.BlockSpec((_TM, q_w), lambda i: (i, 0)), pl.BlockSpec((_TM, kv_w), lambda i: (i, 0)),
                  pl.BlockSpec((_TM, kv_w), lambda i: (i, 0)), pl.BlockSpec((_TM, 128), lambda i: (i, 0)),
                  pl.BlockSpec((_TM, 128), lambda i: (i, 0))],
        out_specs=pl.BlockSpec((_TM, q_w + 2 * kv_w), lambda i: (i, 0)),
        compiler_params=_params(("arbitrary",)),
    )(dq, dk, dv, cos, sin)


def _attn_window(i, n_lat, length):
    wk = _TM + 2 * _WINDOW
    start = pl.multiple_of(jnp.clip(i * _TM - _WINDOW, 0, length - wk), _WINDOW)
    q_pos = i * _TM + lax.broadcasted_iota(jnp.int32, (_TM, wk), 0)
    k_pos = start + lax.broadcasted_iota(jnp.int32, (_TM, wk), 1)
    mask = (jnp.abs(q_pos - k_pos) <= _WINDOW) & (i < n_lat)
    return start, wk, mask


def _softmax_parts(q, k_loc, k_ctx, mask, sink):
    nt = (((1,), (1,)), ((), ()))
    s_loc = jnp.where(mask, lax.dot_general(q, k_loc, nt, preferred_element_type=_F32), _NEG_INF)
    s_ctx = lax.dot_general(q, k_ctx, nt, preferred_element_type=_F32)
    m = jnp.maximum(jnp.maximum(jnp.max(s_loc, axis=-1, keepdims=True), jnp.max(s_ctx, axis=-1, keepdims=True)),
                    sink)
    e_loc = jnp.exp(s_loc - m)
    e_ctx = jnp.exp(s_ctx - m)
    e_sink = jnp.exp(sink - m)
    inv = 1.0 / (jnp.sum(e_loc, axis=-1, keepdims=True) + jnp.sum(e_ctx, axis=-1, keepdims=True) + e_sink)
    return e_loc * inv, e_ctx * inv, e_sink * inv


def _attn_fwd(name, q, k, p, sinks, n_lat, length, kv_w):
    rows, q_w = q.shape
    ctx_len = rows - length
    n_heads = q_w // _HEAD_DIM
    n_kv = kv_w // _HEAD_DIM
    group = n_heads // n_kv
    v_cb = p.shape[1] // kv_w - 1
    hd = _HEAD_DIM

    def body(q_ref, k_ref, v_ref, sink_ref, o_ref):
        i = pl.program_id(0)
        start, wk, mask = _attn_window(i, n_lat, length)
        for h in range(n_kv):
            k_loc = k_ref[pl.ds(start, wk), h * hd:(h + 1) * hd]
            v_loc = v_ref[pl.ds(start, wk), h * hd:(h + 1) * hd]
            k_ctx = k_ref[length:length + ctx_len, h * hd:(h + 1) * hd]
            v_ctx = v_ref[length:length + ctx_len, h * hd:(h + 1) * hd]
            for g in range(group):
                n = h * group + g
                p_loc, p_ctx, _ = _softmax_parts(q_ref[:, n * hd:(n + 1) * hd], k_loc, k_ctx, mask,
                                                 sink_ref[:, n:n + 1])
                o = (jnp.dot(p_loc.astype(_MM), v_loc, preferred_element_type=_F32)
                     + jnp.dot(p_ctx.astype(_MM), v_ctx, preferred_element_type=_F32))
                o_ref[:, n * hd:(n + 1) * hd] = o.astype(o_ref.dtype)

    return pl.pallas_call(
        body, name=name, grid=(rows // _TM,),
        out_shape=jax.ShapeDtypeStruct((rows, q_w), _ACT),
        in_specs=[pl.BlockSpec((_TM, q_w), lambda i: (i, 0)), pl.BlockSpec((rows, kv_w), lambda i: (0, 0)),
                  pl.BlockSpec((rows, kv_w), lambda i: (0, v_cb)), pl.BlockSpec((1, n_heads), lambda i: (0, 0))],
        out_specs=pl.BlockSpec((_TM, q_w), lambda i: (i, 0)),
        compiler_params=_params(("arbitrary",)),
    )(q, k, p, sinks)


def _attn_bwd(name, q, k, p, sinks, do, n_lat, length, kv_w):
    rows, q_w = q.shape
    ctx_len = rows - length
    n_heads = q_w // _HEAD_DIM
    n_kv = kv_w // _HEAD_DIM
    group = n_heads // n_kv
    v_cb = p.shape[1] // kv_w - 1
    hd = _HEAD_DIM
    nt_dims = (((1,), (1,)), ((), ()))
    tn_dims = (((0,), (0,)), ((), ()))

    def body(q_ref, k_ref, v_ref, sink_ref, do_ref, dq_ref, dk_out, dv_out, ds_ref, dk_ref, dv_ref, out_sems):
        i = pl.program_id(0)

        @pl.when(i == 0)
        def _():
            dk_ref[...] = jnp.zeros_like(dk_ref)
            dv_ref[...] = jnp.zeros_like(dv_ref)
            ds_ref[...] = jnp.zeros_like(ds_ref)

        start, wk, mask = _attn_window(i, n_lat, length)
        head_lane = lax.broadcasted_iota(jnp.int32, (1, n_heads), 1)
        dsink = jnp.zeros((1, n_heads), _F32)
        for h in range(n_kv):
            cols = slice(h * hd, (h + 1) * hd)
            k_loc = k_ref[pl.ds(start, wk), cols]
            v_loc = v_ref[pl.ds(start, wk), cols]
            k_ctx = k_ref[length:length + ctx_len, cols]
            v_ctx = v_ref[length:length + ctx_len, cols]
            dk_loc = jnp.zeros((wk, hd), _F32)
            dv_loc = jnp.zeros((wk, hd), _F32)
            dk_ctx = jnp.zeros((ctx_len, hd), _F32)
            dv_ctx = jnp.zeros((ctx_len, hd), _F32)
            for g in range(group):
                n = h * group + g
                qh = q_ref[:, n * hd:(n + 1) * hd]
                doh = do_ref[:, n * hd:(n + 1) * hd].astype(_MM)
                p_loc, p_ctx, p_sink = _softmax_parts(qh, k_loc, k_ctx, mask, sink_ref[:, n:n + 1])
                dp_loc = lax.dot_general(doh, v_loc, nt_dims, preferred_element_type=_F32)
                dp_ctx = lax.dot_general(doh, v_ctx, nt_dims, preferred_element_type=_F32)
                dsum = (jnp.sum(p_loc * dp_loc, axis=-1, keepdims=True)
                        + jnp.sum(p_ctx * dp_ctx, axis=-1, keepdims=True))
                ds_loc = (p_loc * (dp_loc - dsum)).astype(_MM)
                ds_ctx = (p_ctx * (dp_ctx - dsum)).astype(_MM)
                dsink = dsink + jnp.where(head_lane == n, -jnp.sum(p_sink * dsum), 0.0)
                dq = (jnp.dot(ds_loc, k_loc, preferred_element_type=_F32)
                      + jnp.dot(ds_ctx, k_ctx, preferred_element_type=_F32))
                dq_ref[:, n * hd:(n + 1) * hd] = dq.astype(dq_ref.dtype)
                dk_loc += lax.dot_general(ds_loc, qh, tn_dims, preferred_element_type=_F32)
                dk_ctx += lax.dot_general(ds_ctx, qh, tn_dims, preferred_element_type=_F32)
                dv_loc += lax.dot_general(p_loc.astype(_MM), doh, tn_dims, preferred_element_type=_F32)
                dv_ctx += lax.dot_general(p_ctx.astype(_MM), doh, tn_dims, preferred_element_type=_F32)
            dk_ref[pl.ds(start, wk), cols] += dk_loc
            dv_ref[pl.ds(start, wk), cols] += dv_loc
            dk_ref[length:length + ctx_len, cols] += dk_ctx
            dv_ref[length:length + ctx_len, cols] += dv_ctx
        ds_ref[...] += dsink

        @pl.when(i == rows // _TM - 1)
        def _():
            copies = [pltpu.make_async_copy(dk_ref, dk_out, out_sems.at[0]),
                      pltpu.make_async_copy(dv_ref, dv_out, out_sems.at[1])]
            for cp in copies:
                cp.start()
            for cp in copies:
                cp.wait()

    return pl.pallas_call(
        body, name=name, grid=(rows // _TM,),
        out_shape=[jax.ShapeDtypeStruct((rows, q_w), _ACT), jax.ShapeDtypeStruct((rows, kv_w), _F32),
                   jax.ShapeDtypeStruct((rows, kv_w), _F32), jax.ShapeDtypeStruct((1, n_heads), _F32)],
        in_specs=[pl.BlockSpec((_TM, q_w), lambda i: (i, 0)), pl.BlockSpec((rows, kv_w), lambda i: (0, 0)),
                  pl.BlockSpec((rows, kv_w), lambda i: (0, v_cb)), pl.BlockSpec((1, n_heads), lambda i: (0, 0)),
                  pl.BlockSpec((_TM, q_w), lambda i: (i, 0))],
        out_specs=[pl.BlockSpec((_TM, q_w), lambda i: (i, 0)), pl.BlockSpec(memory_space=pl.ANY),
                   pl.BlockSpec(memory_space=pl.ANY), pl.BlockSpec((1, n_heads), lambda i: (0, 0))],
        scratch_shapes=[pltpu.VMEM((rows, kv_w), _F32), pltpu.VMEM((rows, kv_w), _F32),
                        pltpu.SemaphoreType.DMA((2,))],
        compiler_params=_params(("arbitrary",)),
    )(q, k, p, sinks, do)


def _loss_head(name, xs, gain, target, n_lat):
    rows, d = xs.shape

    def body(x_ref, g_ref, t_ref, loss_ref, dg_ref, dx_ref):
        i = pl.program_id(0)

        @pl.when(i == 0)
        def _():
            loss_ref[...] = jnp.zeros_like(loss_ref)
            dg_ref[...] = jnp.zeros_like(dg_ref)

        @pl.when(i < n_lat)
        def _():
            tv = t_ref[...]

            def f(gain_, x):
                y = x * lax.rsqrt(jnp.mean(x * x, axis=-1, keepdims=True) + _EPS) * gain_
                return 0.5 * jnp.sum(jnp.mean(jnp.square(y - tv), axis=-1))

            val, (dg, dx) = jax.value_and_grad(f, argnums=(0, 1))(g_ref[...], x_ref[...])
            loss_ref[...] += val
            dg_ref[...] += dg
            dx_ref[...] = dx

        @pl.when(i >= n_lat)
        def _():
            dx_ref[...] = jnp.zeros_like(dx_ref)

    return pl.pallas_call(
        body, name=name, grid=(rows // _TM,),
        out_shape=[jax.ShapeDtypeStruct((1, 128), _F32), jax.ShapeDtypeStruct((1, d), _F32),
                   jax.ShapeDtypeStruct((rows, d), _F32)],
        in_specs=[pl.BlockSpec((_TM, d), lambda i: (i, 0)), pl.BlockSpec((1, d), lambda i: (0, 0)),
                  pl.BlockSpec((_TM, d), lambda i: (jnp.minimum(i, n_lat - 1), 0))],
        out_specs=[pl.BlockSpec((1, 128), lambda i: (0, 0)), pl.BlockSpec((1, d), lambda i: (0, 0)),
                   pl.BlockSpec((_TM, d), lambda i: (i, 0))],
        compiler_params=_params(("arbitrary",)),
    )(xs, gain, target)


def _adamw(name, w, g, m, v):
    rows, cols = w.shape
    tr = _divisor(rows, 512, 8)
    b1, b2 = _ADAM["b1"], _ADAM["b2"]
    c1 = 1.0 - b1 ** _ADAM["step"]
    c2 = 1.0 - b2 ** _ADAM["step"]

    def body(w_ref, g_ref, m_ref, v_ref, d_ref, nm_ref, nv_ref):
        gv = g_ref[...]
        nm = b1 * m_ref[...] + (1.0 - b1) * gv
        nv = b2 * v_ref[...] + (1.0 - b2) * jnp.square(gv)
        d_ref[...] = -_ADAM["lr"] * ((nm / c1) / (jnp.sqrt(nv / c2) + _ADAM["eps"]) + _ADAM["wd"] * w_ref[...])
        nm_ref[...] = nm
        nv_ref[...] = nv

    spec = pl.BlockSpec((tr, cols), lambda i: (i, 0))
    return pl.pallas_call(
        body, name=name, grid=(rows // tr,),
        out_shape=[jax.ShapeDtypeStruct((rows, cols), _F32)] * 3,
        in_specs=[spec] * 4, out_specs=[spec] * 3,
        compiler_params=_params(("arbitrary",)),
    )(w, g, m, v)


def _pack(arrays, cols=128):
    flat = jnp.concatenate([a.reshape(-1).astype(_F32) for a in arrays])
    pad = (-flat.shape[0]) % (64 * cols)
    return jnp.pad(flat, (0, pad)).reshape(-1, cols)


def _unpack(flat, shapes):
    out, off = [], 0
    for s in shapes:
        n = 1
        for d in s:
            n *= d
        out.append(flat[..., off:off + n].reshape(flat.shape[:-1] + tuple(s)))
        off += n
    return out


def _gather_channels(parts):
    moved = jnp.moveaxis(parts, 0, -2)
    return moved.reshape(moved.shape[:-2] + (moved.shape[-2] * moved.shape[-1],))


def kernel(x, c, ctx, c_ctx, w_mod, b_mod, norm_mix, norm_ffn, w_in_ab, conv_a, conv_b, conv_b_bias, ln_b_gain, ln_b_bias, w_out_ab, w_qkv, w_o, sinks, w_up, w_conv_ffn, w_down, final_norm, loss_target, m_c_ctx, m_w_mod, m_b_mod, m_norm_mix, m_norm_ffn, m_w_in_ab, m_conv_a, m_conv_b, m_conv_b_bias, m_ln_b_gain, m_ln_b_bias, m_w_out_ab, m_w_qkv, m_w_o, m_sinks, m_w_up, m_w_conv_ffn, m_w_down, m_final_norm, v_c_ctx, v_w_mod, v_b_mod, v_norm_mix, v_norm_ffn, v_w_in_ab, v_conv_a, v_conv_b, v_conv_b_bias, v_ln_b_gain, v_ln_b_bias, v_w_out_ab, v_w_qkv, v_w_o, v_sinks, v_w_up, v_w_conv_ffn, v_w_down, v_final_norm):
    args = dict(locals())
    weight_names = ["c_ctx", "w_mod", "b_mod", "norm_mix", "norm_ffn", "w_in_ab", "conv_a", "conv_b", "conv_b_bias",
                    "ln_b_gain", "ln_b_bias", "w_out_ab", "w_qkv", "w_o", "sinks", "w_up", "w_conv_ffn", "w_down",
                    "final_norm"]
    length, d = x.shape[1], x.shape[2]
    ctx_len = ctx.shape[1]
    assert ctx_len == _TM and length % _TM == 0 and x.shape[0] == 1
    n_lat = length // _TM
    depth = w_mod.shape[0]
    n_even, n_odd = w_in_ab.shape[0], w_qkv.shape[0]
    a_w = conv_a.shape[2] * _NDEV
    b_w = conv_b.shape[2] * _NDEV
    assert a_w == b_w
    q_w = w_o.shape[1] * _NDEV
    kv_w = (w_qkv.shape[2] * _NDEV - q_w) // 2
    d_ff = w_down.shape[1] * _NDEV
    dev = 4 * lax.axis_index("x") + 2 * lax.axis_index("y") + lax.axis_index("c")

    small_shapes = [c.shape[1:], conv_a.shape, conv_b.shape, w_conv_ffn.shape]
    g0 = _all_gather_small("gather_small_params", _pack([c, conv_a, conv_b, w_conv_ffn]))
    c_parts, ca_parts, cb_parts, cf_parts = _unpack(g0.reshape(_NDEV, -1), small_shapes)
    conv_a_full = _gather_channels(ca_parts)
    conv_b_full = _gather_channels(cb_parts)
    conv_f_full = _gather_channels(cf_parts)

    cond = jnp.concatenate([c_parts, c_ctx[None], jnp.zeros((16 - _NDEV - 1, d), _F32)], axis=0)
    mod_cols = w_mod.shape[2]
    m_shard = _mod_forward("mod_forward", cond, w_mod)
    m_all = _all_gather_small("gather_mod", m_shard.reshape(depth * 16, mod_cols))
    m_all = jnp.moveaxis(m_all.reshape(_NDEV, depth, 16, mod_cols), 0, 2).reshape(depth, 16, _NDEV * mod_cols)
    m_all = m_all + b_mod[:, None, :]
    m_lat = lax.dynamic_index_in_dim(m_all, dev, axis=1, keepdims=False)
    m_ctx = m_all[:, _NDEV]

    def mod_vec(l, j):
        return jnp.stack([m_lat[l, j * d:(j + 1) * d], m_ctx[l, j * d:(j + 1) * d]])[:, None, :]

    def layer_mats(l):
        if l % 2 == 0:
            first = [("in", l // 2, w_in_ab[l // 2].T), ("out", l // 2, w_out_ab[l // 2])]
        else:
            first = [("qkv", l // 2, w_qkv[l // 2].T), ("o", l // 2, w_o[l // 2])]
        return first + [("up", l, w_up[l].T), ("down", l, w_down[l])]

    late = 0.0 * m_all[0, 0, 0]
    piece_mats = [layer_mats(0)[:2], layer_mats(0)[2:]] + [layer_mats(l) for l in range(1, depth)]
    n_pieces = len(piece_mats)
    slab_off, slab_r, piece_of, piece_keys, piece_rows, slabs = {}, {}, {}, [], [], []
    for p, mats in enumerate(piece_mats):
        off, keys = 0, []
        for fam, idx, mat in mats:
            slab_off[fam, idx], slab_r[fam], piece_of[fam, idx] = off, mat.shape[0], p
            off += mat.shape[0]
            keys.append((fam, idx))
        piece_keys.append(keys)
        piece_rows.append(off)
        slabs.append(jnp.concatenate([(mat + late).astype(_MM) for _, _, mat in mats], axis=0))
    wgs, gathers, start_token = [None] * n_pieces, [None] * n_pieces, jnp.zeros((), _F32)
    for p in range(n_pieces):
        land = _fill_own_slot(f"gather_fill_{p}", [slabs[p]], [0], piece_rows[p])
        gathers[p] = _exchange_start(f"gather_start_{p}", [slabs[p]], land, [0])
        start_token = start_token + gathers[p][-1][0, 0]

    def wref(fam, idx):
        return wgs[piece_of[fam, idx]], slab_off[fam, idx], slab_r[fam]

    cos, sin = _rope_tables(length, ctx_len)
    xs = jnp.concatenate([x[0], ctx[0]], axis=0)

    def full(a):
        return (a.reshape(1, -1), "full")

    wgs[0] = _exchange_wait("gather_wait_0", gathers[0], [0], m_all)
    saved = []
    for l in range(depth):
        sv = {"x_in": xs}
        if l > 0:
            wgs[1 + l] = _exchange_wait(f"gather_wait_{1 + l}", gathers[1 + l], [0], xs)
        gain1 = full(norm_mix[l] + start_token) if l == 0 else full(norm_mix[l])
        norm1 = [gain1, (mod_vec(l, 0), "stream"), (mod_vec(l, 1), "stream")]
        if l % 2 == 0:
            e = l // 2
            h1, p = _norm_proj(f"proj_in_{l}", xs, norm1, *wref("in", e), n_lat)
            (qm,) = _rowfn(f"premix_{l}", _f_premix, [], [(p, a_w, j) for j in range(5)],
                           [(None, [a_w]), (_ACT, [a_w, b_w])], n_lat)
            cv_a = _dwconv(f"conv_a_{l}", qm, 0, a_w, conv_a_full[e], _ACT, n_lat)
            cv_b = _dwconv(f"conv_b_{l}", qm, 1, b_w, conv_b_full[e], _ACT, n_lat)
            post_params = [full(conv_b_bias[e]), full(ln_b_gain[e]), full(ln_b_bias[e])]
            (z,) = _rowfn(f"postmix_{l}", _f_postmix, post_params, [(p, a_w, 0), (cv_a, a_w, 0), (cv_b, b_w, 0)],
                          [(_ACT, [a_w, b_w])], n_lat)
            y1, xs = _proj_residual(f"proj_out_{l}", z, *wref("out", e), xs, mod_vec(l, 2), n_lat)
            sv.update(p=p, qm=qm, cv_a=cv_a, cv_b=cv_b, z=z)
        else:
            o = l // 2
            h1, p = _norm_proj(f"proj_qkv_{l}", xs, norm1, *wref("qkv", o), n_lat)
            qr, kr = _rope_fwd(f"rope_{l}", p, cos, sin, q_w, kv_w)
            sk = sinks[o].reshape(1, -1)
            z = _attn_fwd(f"attn_{l}", qr, kr, p, sk, n_lat, length, kv_w)
            y1, xs = _proj_residual(f"proj_o_{l}", z, *wref("o", o), xs, mod_vec(l, 2), n_lat)
            sv.update(p=p, qr=qr, kr=kr, z=z)
        sv.update(h1=h1, y1=y1, x_mid=xs)
        if l == 0:
            wgs[1] = _exchange_wait("gather_wait_1", gathers[1], [0], xs)
        norm2 = [full(norm_ffn[l]), (mod_vec(l, 3), "stream"), (mod_vec(l, 4), "stream")]
        h2, pu = _norm_proj(f"proj_up_{l}", xs, norm2, *wref("up", l), n_lat)
        f = _ffn_act(f"ffn_act_{l}", pu, conv_f_full[l], n_lat)
        y2, xs = _proj_residual(f"proj_down_{l}", f, *wref("down", l), xs, mod_vec(l, 5), n_lat)
        sv.update(h2=h2, pu=pu, f=f, y2=y2)
        saved.append(sv)

    loss_part, d_final_norm, dxs = _loss_head("loss_head", xs, final_norm.reshape(1, -1), loss_target[0], n_lat)
    loss = lax.psum(loss_part[0, 0], ("x", "y", "c"))

    wgrads = {}
    d_mod = [[None] * 6 for _ in range(depth)]
    d_norm_mix, d_norm_ffn = [None] * depth, [None] * depth
    d_conv_a, d_conv_b = [None] * n_even, [None] * n_even
    d_bias, d_ln_g, d_ln_b = [None] * n_even, [None] * n_even, [None] * n_even
    d_sinks = [None] * n_odd
    d_conv_f = [None] * depth
    exchanges, recvs, exchange_token = [None] * n_pieces, [None] * n_pieces, jnp.zeros((), _F32)

    def piece_parts(p):
        return ([wgrads[key].reshape(_NDEV, slab_r[key[0]], d) for key in piece_keys[p]],
                [slab_off[key] for key in piece_keys[p]])

    def start_exchange(p):
        parts, offsets = piece_parts(p)
        land = _fill_own_slot(f"exchange_fill_{p}", parts, offsets, piece_rows[p])
        exchanges[p] = _exchange_start(f"exchange_start_{p}", parts, land, offsets)
        return exchanges[p][-1][0, 0]

    for l in reversed(range(depth)):
        sv = saved[l]
        d_mod[l][5], dy2, df = _gate_proj_bwd(f"bwd_down_{l}", dxs, sv["y2"], mod_vec(l, 5) + exchange_token,
                                              *wref("down", l), n_lat)
        wgrads["down", l] = _mm_tn(f"wgrad_down_{l}", sv["f"], dy2, _ACT)
        dpu, d_conv_f[l] = _ffn_act_bwd(f"ffn_act_bwd_{l}", sv["pu"], df, conv_f_full[l], n_lat)
        wgrads["up", l] = _mm_tn(f"wgrad_up_{l}", dpu, sv["h2"], _ACT)
        norm2 = [full(norm_ffn[l]), (mod_vec(l, 3), "stream"), (mod_vec(l, 4), "stream")]
        (dgain, dsh, dsc), dxs = _proj_norm_bwd(f"bwd_up_{l}", dpu, *wref("up", l), sv["x_mid"], norm2, dxs, n_lat)
        d_norm_ffn[l], d_mod[l][3], d_mod[l][4] = dgain, dsh, dsc
        mix_token = start_exchange(1) if l == 0 else jnp.zeros((), _F32)
        gate1 = mod_vec(l, 2) + mix_token
        norm1 = [full(norm_mix[l]), (mod_vec(l, 0), "stream"), (mod_vec(l, 1), "stream")]
        if l % 2 == 0:
            e = l // 2
            d_mod[l][2], dy1, dz = _gate_proj_bwd(f"bwd_out_{l}", dxs, sv["y1"], gate1, *wref("out", e), n_lat)
            wgrads["out", e] = _mm_tn(f"wgrad_out_{l}", sv["z"], dy1, _ACT)
            post_params = [full(conv_b_bias[e]), full(ln_b_gain[e]), full(ln_b_bias[e])]
            (dbias, dlg, dlb), (dgb, dcv_a, dcv_b) = _rowfn_bwd(
                f"postmix_bwd_{l}", _f_postmix, post_params,
                [(sv["p"], a_w, 0), (sv["cv_a"], a_w, 0), (sv["cv_b"], b_w, 0)], [0, 1, 2], [(dz, [a_w, b_w])],
                [(_ACT, [0]), (_ACT, [1]), (_ACT, [2])], n_lat)
            d_bias[e], d_ln_g[e], d_ln_b[e] = dbias, dlg, dlb
            d_conv_a[e] = _dwconv_wgrad(f"conv_a_wgrad_{l}", dcv_a, sv["qm"], 0, a_w, conv_a_full.shape[1], n_lat)
            d_conv_b[e] = _dwconv_wgrad(f"conv_b_wgrad_{l}", dcv_b, sv["qm"], 1, b_w, conv_b_full.shape[1], n_lat)
            dq_a = _dwconv(f"conv_a_bwd_{l}", dcv_a, 0, a_w, conv_a_full[e][::-1], _ACT, n_lat)
            dq_b = _dwconv(f"conv_b_bwd_{l}", dcv_b, 0, b_w, conv_b_full[e][::-1], _ACT, n_lat)
            _, (dp,) = _rowfn_bwd(f"premix_bwd_{l}", _f_premix, [], [(sv["p"], a_w, j) for j in range(5)],
                                  [0, 1, 2, 3, 4], [(dgb, [a_w]), (dq_a, [a_w]), (dq_b, [b_w])],
                                  [(_ACT, [0, 1, 2, 3, 4])], n_lat)
            wgrads["in", e] = _mm_tn(f"wgrad_in_{l}", dp, sv["h1"], _ACT)
            if l == 0:
                norm1 = [full(norm_mix[l] + start_exchange(0))] + norm1[1:]
            (dgain, dsh, dsc), dxs = _proj_norm_bwd(f"bwd_in_{l}", dp, *wref("in", e), sv["x_in"], norm1, dxs, n_lat)
        else:
            o = l // 2
            d_mod[l][2], dy1, dz = _gate_proj_bwd(f"bwd_o_{l}", dxs, sv["y1"], gate1, *wref("o", o), n_lat)
            wgrads["o", o] = _mm_tn(f"wgrad_o_{l}", sv["z"], dy1, _ACT)
            sk = sinks[o].reshape(1, -1)
            dqr, dkr, dv, dsk = _attn_bwd(f"attn_bwd_{l}", sv["qr"], sv["kr"], sv["p"], sk, dz, n_lat, length, kv_w)
            d_sinks[o] = dsk
            dp = _rope_bwd(f"rope_bwd_{l}", dqr, dkr, dv, cos, sin)
            wgrads["qkv", o] = _mm_tn(f"wgrad_qkv_{l}", dp, sv["h1"], _ACT)
            (dgain, dsh, dsc), dxs = _proj_norm_bwd(f"bwd_qkv_{l}", dp, *wref("qkv", o), sv["x_in"], norm1, dxs, n_lat)
        d_norm_mix[l], d_mod[l][0], d_mod[l][1] = dgain, dsh, dsc
        if l > 0:
            exchange_token = start_exchange(1 + l)
    grad_x = dxs[:length][None]

    dm_dev = jnp.stack([jnp.concatenate([d_mod[l][j][:, 0, :] for j in range(6)], axis=-1)
                        for l in range(depth)])
    small_grads = [dm_dev, jnp.stack(d_norm_mix), jnp.stack(d_norm_ffn), jnp.stack(d_conv_a), jnp.stack(d_conv_b),
                   jnp.stack(d_bias), jnp.stack(d_ln_g), jnp.stack(d_ln_b), jnp.stack(d_sinks), jnp.stack(d_conv_f),
                   d_final_norm]
    sg_shapes = [a.shape for a in small_grads]
    sg_packed = _pack(small_grads)
    sg_land = _fill_own_slot("small_grads_fill", [sg_packed], [0], sg_packed.shape[0])
    sg_started = _exchange_start("small_grads_start", [sg_packed], sg_land, [0])

    gsums = []
    for p in range(n_pieces):
        after = sg_started[-1] if p == 0 else dxs
        recvs[p] = _exchange_wait(f"exchange_wait_{p}", exchanges[p], piece_parts(p)[1], after)
        gsums.append(_sum_slots(f"sum_weight_grads_{p}", recvs[p]))

    def slab_grad(fam, count, transposed):
        mats = [gsums[piece_of[fam, i]][slab_off[fam, i]:slab_off[fam, i] + slab_r[fam]] for i in range(count)]
        return jnp.stack([m_.T if transposed else m_ for m_ in mats])

    grads = {
        "w_in_ab": slab_grad("in", n_even, True), "w_qkv": slab_grad("qkv", n_odd, True),
        "w_up": slab_grad("up", depth, True), "w_out_ab": slab_grad("out", n_even, False),
        "w_o": slab_grad("o", n_odd, False), "w_down": slab_grad("down", depth, False),
    }

    weight_names_big = ["w_in_ab", "w_out_ab", "w_qkv", "w_o", "w_up", "w_down"]
    delta, new_m, new_v = {}, {}, {}

    def adam_big(n):
        w = args[n]
        flat = [a.reshape(-1, w.shape[-1]) for a in (w, grads[n], args["m_" + n], args["v_" + n])]
        dl, nm, nv = _adamw(f"adamw_{n}", *flat)
        delta[n], new_m[n], new_v[n] = dl.reshape(w.shape), nm.reshape(w.shape), nv.reshape(w.shape)

    for n in weight_names_big:
        adam_big(n)

    sg_all = _exchange_wait("small_grads_wait", sg_started, [0], new_v[weight_names_big[-1]])
    sg_sum = _sum_slots("sum_small_grads", sg_all)
    (dm_sum, g_norm_mix, g_norm_ffn, g_conv_a, g_conv_b, g_bias, g_ln_g, g_ln_b, g_sinks, g_conv_f,
     g_final_norm) = _unpack(sg_sum.reshape(-1), sg_shapes)
    dm_each = _unpack(sg_all.reshape(_NDEV, -1), sg_shapes[:1])[0]

    def my_channels(a):
        width = a.shape[-1] // _NDEV
        return lax.dynamic_slice_in_dim(a, dev * width, width, axis=a.ndim - 1)

    grads["b_mod"] = dm_sum[:, 0] + dm_sum[:, 1]
    grads["norm_mix"] = g_norm_mix.reshape(depth, d)
    grads["norm_ffn"] = g_norm_ffn.reshape(depth, d)
    grads["conv_a"] = my_channels(g_conv_a)
    grads["conv_b"] = my_channels(g_conv_b)
    grads["conv_b_bias"] = g_bias.reshape(n_even, b_w)
    grads["ln_b_gain"] = g_ln_g.reshape(n_even, b_w)
    grads["ln_b_bias"] = g_ln_b.reshape(n_even, b_w)
    grads["sinks"] = g_sinks.reshape(n_odd, -1)
    grads["w_conv_ffn"] = my_channels(g_conv_f)
    grads["final_norm"] = g_final_norm.reshape(d)

    dm_rows = jnp.concatenate([jnp.moveaxis(dm_each[:, :, 0], 0, 1), dm_sum[:, 1:2],
                               jnp.zeros((depth, 16 - _NDEV - 1, 6 * d), _F32)], axis=1)
    dm_mine = my_channels(dm_rows)
    grads["w_mod"] = jnp.stack([_mm_tn(f"wgrad_mod_{l}", cond, dm_mine[l], _F32, silu_a=True) for l in range(depth)])
    dcond = _mod_backward_cond("mod_backward_cond", dm_mine, w_mod)
    dcond_all = _all_gather_small("gather_dcond", dcond)
    dcond_sum = _sum_slots("sum_dcond", dcond_all)[_NDEV]
    sg = jax.nn.sigmoid(c_ctx)
    grads["c_ctx"] = dcond_sum * (sg * (1.0 + c_ctx * (1.0 - sg)))

    adam_big("w_mod")
    small = [n for n in weight_names if n != "w_mod" and n not in weight_names_big]
    shapes = [args[n].shape for n in small]
    grads = {n: grads[n].reshape(args[n].shape) for n in grads}
    dl, nm, nv = _adamw("adamw_small", _pack([args[n] for n in small]), _pack([grads[n] for n in small]),
                        _pack([args["m_" + n] for n in small]), _pack([args["v_" + n] for n in small]))
    for res, packed in ((delta, dl), (new_m, nm), (new_v, nv)):
        for n, a in zip(small, _unpack(packed.reshape(-1), shapes)):
            res[n] = a

    return (loss, grad_x, *[grads[n] for n in weight_names], *[delta[n] for n in weight_names],
            *[new_m[n] for n in weight_names], *[new_v[n] for n in weight_names])
```

```python
import functools

import jax
import jax.numpy as jnp
from jax import lax
from jax.experimental import pallas as pl
from jax.experimental.pallas import tpu as pltpu

_F32 = jnp.float32
_MM = jnp.bfloat16
_ACT = jnp.bfloat16
_TM = 256
_HALO = 16
_LANES = 128
_CONV_ROWS = 128
_NDEV = 8
_HEAD_DIM = 64
_WINDOW = 128
_GRID_W = 64
_ROPE_THETA = 10000.0
_EPS = 1e-6
_NEG_INF = -1e30
_VMEM_LIMIT = 56 * 1024 * 1024
_ADAM = dict(lr=0.001, b1=0.9, b2=0.999, eps=1e-08, wd=0.01, step=10)
_MESH = pl.DeviceIdType.MESH


def _params(sem=None):
    return pltpu.CompilerParams(dimension_semantics=sem, vmem_limit_bytes=_VMEM_LIMIT)


def _divisor(n, cap, mult):
    if n <= cap:
        return n
    for d in range(cap - cap % mult, 0, -mult):
        if n % d == 0:
            return d
    raise ValueError(f"no tile for {n}")


def _my_coords():
    return lax.axis_index("x"), lax.axis_index("y"), lax.axis_index("c")


def _peer(k):
    x, y, c = _my_coords()
    px = 1 - x if k & 4 else x
    py = 1 - y if k & 2 else y
    pc = 1 - c if k & 1 else c
    return (px, py, pc), 4 * px + 2 * py + pc


def _all_gather_small(name, v):
    rows, cols = v.shape

    def body(v_ref, out_ref, send_sems, recv_sems):
        x, y, c = _my_coords()
        me = 4 * x + 2 * y + c
        out_ref[me] = v_ref[...]
        sends = []
        for k in range(1, _NDEV):
            peer, _ = _peer(k)
            cp = pltpu.make_async_remote_copy(
                src_ref=v_ref, dst_ref=out_ref.at[me], send_sem=send_sems.at[k - 1], recv_sem=recv_sems.at[k - 1],
                device_id=peer, device_id_type=_MESH)
            cp.start()
            sends.append(cp)
        for k in range(1, _NDEV):
            peer, pid = _peer(k)
            pltpu.make_async_remote_copy(
                src_ref=v_ref, dst_ref=out_ref.at[pid], send_sem=send_sems.at[k - 1], recv_sem=recv_sems.at[k - 1],
                device_id=peer, device_id_type=_MESH).wait_recv()
        for cp in sends:
            cp.wait_send()

    return pl.pallas_call(
        body, name=name,
        out_shape=jax.ShapeDtypeStruct((_NDEV, rows, cols), v.dtype),
        in_specs=[pl.BlockSpec(memory_space=pltpu.VMEM)],
        out_specs=pl.BlockSpec(memory_space=pltpu.VMEM),
        scratch_shapes=[pltpu.SemaphoreType.DMA((_NDEV - 1,)), pltpu.SemaphoreType.DMA((_NDEV - 1,))],
        compiler_params=pltpu.CompilerParams(vmem_limit_bytes=_VMEM_LIMIT),
    )(v)


def _sum_slots(name, v):
    _, rows, cols = v.shape
    tr = _divisor(rows, 1024, 16)

    def body(v_ref, o_ref):
        acc = v_ref[0].astype(_F32)
        for e in range(1, _NDEV):
            acc = acc + v_ref[e].astype(_F32)
        o_ref[...] = acc

    return pl.pallas_call(
        body, name=name, grid=(rows // tr,),
        out_shape=jax.ShapeDtypeStruct((rows, cols), _F32),
        in_specs=[pl.BlockSpec((_NDEV, tr, cols), lambda i: (0, i, 0))],
        out_specs=pl.BlockSpec((tr, cols), lambda i: (i, 0)),
        compiler_params=_params(("arbitrary",)),
    )(v)


_HBM_SPEC = pl.BlockSpec(memory_space=pltpu.HBM)
_SEM_SPEC = pl.BlockSpec(memory_space=pltpu.SEMAPHORE)
_EFFECT = pltpu.SideEffectType.DATAFLOW_SIDE_EFFECTING


def _in_hbm(a):
    return pltpu.with_memory_space_constraint(a, pltpu.HBM)


def _block_for(ref, device):
    return ref if len(ref.shape) == 2 else ref.at[device]


def _fill_own_slot(name, srcs, offsets, total_rows):
    n = len(srcs)
    cols = srcs[0].shape[-1]

    def body(*refs):
        src_refs, out_ref, bufs, sems = refs[:n], refs[n], refs[n + 1:2 * n + 1], refs[2 * n + 1]
        x, y, c = _my_coords()
        me = 4 * x + 2 * y + c
        loads = [pltpu.make_async_copy(_block_for(src_refs[m], me), bufs[m], sems.at[0, m]) for m in range(n)]
        stores = [pltpu.make_async_copy(bufs[m], out_ref.at[me, pl.ds(offsets[m], srcs[m].shape[-2]), :],
                                        sems.at[1, m]) for m in range(n)]
        for copies in (loads, stores):
            for cp in copies:
                cp.start()
            for cp in copies:
                cp.wait()

    return pl.pallas_call(
        body, name=name,
        out_shape=jax.ShapeDtypeStruct((_NDEV, total_rows, cols), srcs[0].dtype),
        in_specs=[pl.BlockSpec(memory_space=pl.ANY)] * n,
        out_specs=pl.BlockSpec(memory_space=pl.ANY),
        scratch_shapes=[pltpu.VMEM(s.shape[-2:], s.dtype) for s in srcs] + [pltpu.SemaphoreType.DMA((2, n))],
        compiler_params=pltpu.CompilerParams(vmem_limit_bytes=_VMEM_LIMIT),
    )(*srcs)


def _exchange_start(name, srcs, land, offsets):
    n = len(srcs)

    def body(*refs):
        src_refs, land_ref = refs[:n], refs[n]
        send_sems, recv_sems, token = refs[n + 1], refs[n + 2], refs[-1]
        x, y, c = _my_coords()
        me = 4 * x + 2 * y + c
        for k in range(1, _NDEV):
            peer, pid = _peer(k)
            for m in range(n):
                pltpu.make_async_remote_copy(
                    src_ref=_block_for(src_refs[m], pid),
                    dst_ref=land_ref.at[me, pl.ds(offsets[m], srcs[m].shape[-2]), :],
                    send_sem=send_sems, recv_sem=recv_sems, device_id=peer, device_id_type=_MESH).start()
        token[...] = jnp.zeros_like(token)

    sems = pltpu.SemaphoreType.DMA(())
    return pl.pallas_call(
        body, name=name,
        out_shape=(sems, sems, *[pltpu.HBM(s.shape, s.dtype) for s in srcs], pltpu.HBM(land.shape, land.dtype),
                   jax.ShapeDtypeStruct((8, 128), _F32)),
        in_specs=[_HBM_SPEC] * (n + 1),
        out_specs=(_SEM_SPEC, _SEM_SPEC, *[_HBM_SPEC] * (n + 1), pl.BlockSpec(memory_space=pltpu.VMEM)),
        input_output_aliases={i: 2 + i for i in range(n + 1)},
        compiler_params=pltpu.CompilerParams(has_side_effects=_EFFECT),
    )(*[_in_hbm(s) for s in srcs], _in_hbm(land))


def _exchange_wait(name, started, offsets, after):
    send_sems, recv_sems = started[0], started[1]
    srcs, land = list(started[2:-2]), started[-2]
    n = len(srcs)

    def body(*refs):
        src_refs, land_ref = refs[:n], refs[n]
        send_sems_, recv_sems_ = refs[n + 1], refs[n + 2]
        others = land_ref.at[pl.ds(0, _NDEV - 1)]
        cp = pltpu.make_async_remote_copy(src_ref=others, dst_ref=others, send_sem=send_sems_, recv_sem=recv_sems_,
                                          device_id=_peer(1)[0], device_id_type=_MESH)
        cp.wait_send()
        cp.wait_recv()

    res = pl.pallas_call(
        body, name=name,
        out_shape=(*[pltpu.HBM(s.shape, s.dtype) for s in srcs], pltpu.HBM(land.shape, land.dtype)),
        in_specs=[_HBM_SPEC] * (n + 1) + [_SEM_SPEC, _SEM_SPEC, pl.BlockSpec(memory_space=pl.ANY)],
        out_specs=tuple([_HBM_SPEC] * (n + 1)),
        input_output_aliases={i: i for i in range(n + 1)},
        compiler_params=pltpu.CompilerParams(has_side_effects=_EFFECT),
    )(*srcs, land, send_sems, recv_sems, after)
    return res[n]


def _load_weight(wg_ref, wbuf, sems, off, r, step):
    @pl.when(step == 0)
    def _():
        copies = [pltpu.make_async_copy(wg_ref.at[e, pl.ds(off, r), :], wbuf.at[pl.ds(e * r, r), :], sems.at[e])
                  for e in range(_NDEV)]
        for cp in copies:
            cp.start()
        for cp in copies:
            cp.wait()


def _mm_tn(name, a, b, out_dtype, silu_a=False):
    rows, na = a.shape
    nb = b.shape[1]
    tr = _divisor(rows, 1536, 16)
    tn = _divisor(na, 1536, 128)
    steps = rows // tr

    def body(a_ref, b_ref, o_ref, acc):
        t = pl.program_id(1)

        @pl.when(t == 0)
        def _():
            acc[...] = jnp.zeros_like(acc)

        av = a_ref[...]
        if silu_a:
            av = av.astype(_F32)
            av = av * jax.nn.sigmoid(av)
        acc[...] += lax.dot_general(av.astype(_MM), b_ref[...].astype(_MM), (((0,), (0,)), ((), ())),
                                    preferred_element_type=_F32)

        @pl.when(t == steps - 1)
        def _():
            o_ref[...] = acc[...].astype(out_dtype)

    return pl.pallas_call(
        body, name=name, grid=(na // tn, steps),
        out_shape=jax.ShapeDtypeStruct((na, nb), out_dtype),
        in_specs=[pl.BlockSpec((tr, tn), lambda j, t: (t, j)), pl.BlockSpec((tr, nb), lambda j, t: (t, 0))],
        out_specs=pl.BlockSpec((tn, nb), lambda j, t: (j, 0)),
        scratch_shapes=[pltpu.VMEM((tn, nb), _F32)],
        compiler_params=_params(("arbitrary", "arbitrary")),
    )(a, b)


def _mod_forward(name, cond, w_mod):
    depth, d, n = w_mod.shape
    rows = cond.shape[0]

    def body(c_ref, w_ref, o_ref):
        cv = c_ref[...]
        a = (cv * jax.nn.sigmoid(cv)).astype(_MM)
        o_ref[...] = jnp.dot(a, w_ref[...].astype(_MM), preferred_element_type=_F32)

    return pl.pallas_call(
        body, name=name, grid=(depth,),
        out_shape=jax.ShapeDtypeStruct((depth, rows, n), _F32),
        in_specs=[pl.BlockSpec((rows, d), lambda l: (0, 0)), pl.BlockSpec((None, d, n), lambda l: (l, 0, 0))],
        out_specs=pl.BlockSpec((None, rows, n), lambda l: (l, 0, 0)),
        compiler_params=_params(("arbitrary",)),
    )(cond, w_mod)


def _mod_backward_cond(name, dm, w_mod):
    depth, d, n = w_mod.shape
    rows = dm.shape[1]

    def body(g_ref, w_ref, o_ref):
        @pl.when(pl.program_id(0) == 0)
        def _():
            o_ref[...] = jnp.zeros_like(o_ref)

        o_ref[...] += lax.dot_general(g_ref[...].astype(_MM), w_ref[...].astype(_MM), (((1,), (1,)), ((), ())),
                                      preferred_element_type=_F32)

    return pl.pallas_call(
        body, name=name, grid=(depth,),
        out_shape=jax.ShapeDtypeStruct((rows, d), _F32),
        in_specs=[pl.BlockSpec((None, rows, n), lambda l: (l, 0, 0)), pl.BlockSpec((None, d, n), lambda l: (l, 0, 0))],
        out_specs=pl.BlockSpec((rows, d), lambda l: (0, 0)),
        compiler_params=_params(("arbitrary",)),
    )(dm, w_mod)


def _param_spec(arr, kind, n_lat):
    if kind == "stream":
        return pl.BlockSpec((None,) + arr.shape[1:], lambda i: (i // n_lat, 0, 0))
    return pl.BlockSpec(arr.shape, lambda i: (0,) * arr.ndim)


def _rowfn(name, fn, params, xs, outs, n_lat):
    rows = xs[0][0].shape[0]
    np_, nx = len(params), len(xs)
    stored = [(dt, ws) for dt, ws in outs if dt is not None]

    def body(*refs):
        ps = [r[...].astype(_F32) for r in refs[:np_]]
        xv = [r[...].astype(_F32) for r in refs[np_:np_ + nx]]
        pieces = fn(ps, xv)
        o_refs = iter(refs[np_ + nx:])
        k = 0
        for dt, ws in outs:
            o_ref = next(o_refs) if dt is not None else None
            off = 0
            for w in ws:
                if o_ref is not None:
                    o_ref[:, off:off + w] = pieces[k].astype(dt)
                off += w
                k += 1

    return pl.pallas_call(
        body, name=name, grid=(rows // _TM,),
        out_shape=[jax.ShapeDtypeStruct((rows, sum(ws)), dt) for dt, ws in stored],
        in_specs=[_param_spec(a, kind, n_lat) for a, kind in params]
        + [pl.BlockSpec((_TM, w), lambda i, cb=cb: (i, cb)) for _, w, cb in xs],
        out_specs=[pl.BlockSpec((_TM, sum(ws)), lambda i: (i, 0)) for _, ws in stored],
        compiler_params=_params(("arbitrary",)),
    )(*[a for a, _ in params], *[a for a, _, _ in xs])


def _rowfn_bwd(name, fn, params, xs, diff, douts, dx_outs, n_lat):
    rows = xs[0][0].shape[0]
    np_, nx, nd = len(params), len(xs), len(douts)
    nt = rows // _TM

    def body(*refs):
        i = pl.program_id(0)
        ps = [r[...].astype(_F32) for r in refs[:np_]]
        xv = [r[...].astype(_F32) for r in refs[np_:np_ + nx]]
        d_refs = refs[np_ + nx:np_ + nx + nd]
        dp_refs = refs[np_ + nx + nd:np_ + nx + nd + np_]
        dx_refs = refs[np_ + nx + nd + np_:]

        def f(ps_, xd):
            full = list(xv)
            for j, v in zip(diff, xd):
                full[j] = v
            return fn(ps_, full)

        _, vjp = jax.vjp(f, ps, [xv[j] for j in diff])
        cts = []
        for d_ref, (_, ws) in zip(d_refs, douts):
            off = 0
            for w in ws:
                cts.append(d_ref[:, off:off + w].astype(_F32))
                off += w
        dps, dxd = vjp(cts)
        grads = dict(zip(diff, dxd))
        _accumulate_params(dp_refs, [kind for _, kind in params], dps, i, n_lat)
        for dx_ref, (dt, idxs) in zip(dx_refs, dx_outs):
            off = 0
            for j in idxs:
                w = xs[j][1]
                dx_ref[:, off:off + w] = grads[j].astype(dt)
                off += w

    dp_shapes = [jax.ShapeDtypeStruct(a.shape, _F32) for a, _ in params]
    dx_shapes = [jax.ShapeDtypeStruct((rows, sum(xs[j][1] for j in idxs)), dt) for dt, idxs in dx_outs]
    in_specs = ([_param_spec(a, kind, n_lat) for a, kind in params]
                + [pl.BlockSpec((_TM, w), lambda i, cb=cb: (i, cb)) for _, w, cb in xs]
                + [pl.BlockSpec((_TM, sum(ws)), lambda i: (i, 0)) for _, ws in douts])
    operands = [a for a, _ in params] + [a for a, _, _ in xs] + [a for a, _ in douts]
    res = pl.pallas_call(
        body, name=name, grid=(nt,),
        out_shape=dp_shapes + dx_shapes,
        in_specs=in_specs,
        out_specs=[_param_spec(a, kind, n_lat) for a, kind in params]
        + [pl.BlockSpec((_TM, s.shape[1]), lambda i: (i, 0)) for s in dx_shapes],
        compiler_params=_params(("arbitrary",)),
    )(*operands)
    return list(res[:np_]), list(res[np_:])


def _f_norm_mod(ps, xs):
    gain, shift, scale = ps
    (x,) = xs
    y = x * lax.rsqrt(jnp.mean(x * x, axis=-1, keepdims=True) + _EPS) * gain
    return [y * (1.0 + scale) + shift]


def _f_premix(ps, xs):
    g_b, g_c, u_a, v_b, gate_b = xs
    return [g_b, g_c * u_a, v_b * jax.nn.sigmoid(gate_b)]


def _f_postmix(ps, xs):
    bias, ln_g, ln_b = ps
    g_b, cv_a, cv_b = xs
    u = cv_b + bias
    mu = jnp.mean(u, axis=-1, keepdims=True)
    var = jnp.mean(jnp.square(u - mu), axis=-1, keepdims=True)
    y = (u - mu) * lax.rsqrt(var + _EPS) * ln_g + ln_b
    return [g_b * cv_a, y * jax.nn.sigmoid(y)]


def _accumulate_params(dp_refs, kinds, dps, i, n_lat):
    for dp_ref, kind, dp in zip(dp_refs, kinds, dps):
        first = (i == 0) | (i == n_lat) if kind == "stream" else i == 0

        @pl.when(first)
        def _(dp_ref=dp_ref):
            dp_ref[...] = jnp.zeros_like(dp_ref)

        dp_ref[...] += dp


def _row_spec(width):
    return pl.BlockSpec((_TM, width), lambda i: (i, 0))


def _weight_scratch(shape, wg):
    return [pltpu.VMEM(shape, wg.dtype), pltpu.SemaphoreType.DMA((_NDEV,))]


def _norm_proj(name, x, params, wg, off, r, n_lat):
    rows, d = x.shape
    n = _NDEV * r
    chunk = _divisor(n, 512, 128)

    def body(g_ref, sh_ref, sc_ref, x_ref, wg_ref, h_ref, p_ref, wbuf, sems):
        _load_weight(wg_ref, wbuf, sems, off, r, pl.program_id(0))
        (h,) = _f_norm_mod([g_ref[...], sh_ref[...], sc_ref[...]], [x_ref[...]])
        hb = h.astype(_MM)
        h_ref[...] = hb.astype(h_ref.dtype)
        for j in range(n // chunk):
            p_ref[:, j * chunk:(j + 1) * chunk] = lax.dot_general(
                hb, wbuf[j * chunk:(j + 1) * chunk, :], (((1,), (1,)), ((), ())),
                preferred_element_type=_F32).astype(p_ref.dtype)

    return pl.pallas_call(
        body, name=name, grid=(rows // _TM,),
        out_shape=[jax.ShapeDtypeStruct((rows, d), _ACT), jax.ShapeDtypeStruct((rows, n), _ACT)],
        in_specs=[_param_spec(a, kind, n_lat) for a, kind in params] + [_row_spec(d), pl.BlockSpec(memory_space=pl.ANY)],
        out_specs=[_row_spec(d), _row_spec(n)],
        scratch_shapes=_weight_scratch((n, d), wg),
        compiler_params=_params(("arbitrary",)),
    )(*[a for a, _ in params], x, wg)


def _proj_residual(name, a, wg, off, r, x, gate, n_lat):
    rows, kdim = a.shape
    d = x.shape[1]
    assert kdim == _NDEV * r

    def body(g_ref, a_ref, x_ref, wg_ref, y_ref, o_ref, wbuf, sems):
        _load_weight(wg_ref, wbuf, sems, off, r, pl.program_id(0))
        y = jnp.dot(a_ref[...].astype(_MM), wbuf[...], preferred_element_type=_F32)
        y_ref[...] = y.astype(y_ref.dtype)
        o_ref[...] = x_ref[...] + g_ref[...] * y

    return pl.pallas_call(
        body, name=name, grid=(rows // _TM,),
        out_shape=[jax.ShapeDtypeStruct((rows, d), _ACT), jax.ShapeDtypeStruct((rows, d), _F32)],
        in_specs=[_param_spec(gate, "stream", n_lat), _row_spec(kdim), _row_spec(d), pl.BlockSpec(memory_space=pl.ANY)],
        out_specs=[_row_spec(d), _row_spec(d)],
        scratch_shapes=_weight_scratch((kdim, d), wg),
        compiler_params=_params(("arbitrary",)),
    )(gate, a, x, wg)


def _gate_proj_bwd(name, dx, y, gate, wg, off, r, n_lat):
    rows, d = dx.shape
    n = _NDEV * r
    chunk = _divisor(n, 512, 128)

    def body(g_ref, dx_ref, y_ref, wg_ref, dg_ref, dy_ref, dz_ref, wbuf, sems):
        i = pl.program_id(0)
        _load_weight(wg_ref, wbuf, sems, off, r, i)
        dxv = dx_ref[...]
        _accumulate_params([dg_ref], ["stream"], [jnp.sum(dxv * y_ref[...].astype(_F32), axis=0, keepdims=True)],
                           i, n_lat)
        dy = (g_ref[...] * dxv).astype(_MM)
        dy_ref[...] = dy.astype(dy_ref.dtype)
        for j in range(n // chunk):
            dz_ref[:, j * chunk:(j + 1) * chunk] = lax.dot_general(
                dy, wbuf[j * chunk:(j + 1) * chunk, :], (((1,), (1,)), ((), ())),
                preferred_element_type=_F32).astype(dz_ref.dtype)

    return pl.pallas_call(
        body, name=name, grid=(rows // _TM,),
        out_shape=[jax.ShapeDtypeStruct(gate.shape, _F32), jax.ShapeDtypeStruct((rows, d), _ACT),
                   jax.ShapeDtypeStruct((rows, n), _ACT)],
        in_specs=[_param_spec(gate, "stream", n_lat), _row_spec(d), _row_spec(d), pl.BlockSpec(memory_space=pl.ANY)],
        out_specs=[_param_spec(gate, "stream", n_lat), _row_spec(d), _row_spec(n)],
        scratch_shapes=_weight_scratch((n, d), wg),
        compiler_params=_params(("arbitrary",)),
    )(gate, dx, y, wg)


def _proj_norm_bwd(name, dp, wg, off, r, x, params, dx_in, n_lat):
    rows, kdim = dp.shape
    d = x.shape[1]
    assert kdim == _NDEV * r
    kinds = [kind for _, kind in params]

    def body(g_ref, sh_ref, sc_ref, dp_ref, x_ref, dxin_ref, wg_ref, dg_ref, dsh_ref, dsc_ref, dx_ref, wbuf, sems):
        i = pl.program_id(0)
        _load_weight(wg_ref, wbuf, sems, off, r, i)
        dh = jnp.dot(dp_ref[...].astype(_MM), wbuf[...], preferred_element_type=_F32)
        _, vjp = jax.vjp(lambda ps, xv: _f_norm_mod(ps, [xv]), [g_ref[...], sh_ref[...], sc_ref[...]], x_ref[...])
        dps, dxn = vjp([dh])
        _accumulate_params([dg_ref, dsh_ref, dsc_ref], kinds, dps, i, n_lat)
        dx_ref[...] = dxin_ref[...] + dxn

    specs = [_param_spec(a, kind, n_lat) for a, kind in params]
    res = pl.pallas_call(
        body, name=name, grid=(rows // _TM,),
        out_shape=[jax.ShapeDtypeStruct(a.shape, _F32) for a, _ in params] + [jax.ShapeDtypeStruct((rows, d), _F32)],
        in_specs=specs + [_row_spec(kdim), _row_spec(d), _row_spec(d), pl.BlockSpec(memory_space=pl.ANY)],
        out_specs=specs + [_row_spec(d)],
        scratch_shapes=_weight_scratch((kdim, d), wg),
        compiler_params=_params(("arbitrary",)),
    )(*[a for a, _ in params], dp, x, dx_in, wg)
    return list(res[:3]), res[3]


def _conv_halo_specs(width, cb0, n_rows):
    per = _TM // _HALO
    last = n_rows // _HALO - 1
    return [
        pl.BlockSpec((_TM, width), lambda i, j: (i, cb0 + j)),
        pl.BlockSpec((_HALO, width), lambda i, j: (jnp.maximum(i * per - 1, 0), cb0 + j)),
        pl.BlockSpec((_HALO, width), lambda i, j: (jnp.minimum((i + 1) * per, last), cb0 + j)),
    ]


def _conv_window(main_ref, prev_ref, next_ref, r0, cols, i, n_lat, nt):
    if r0 == 0:
        has_prev = (i != 0) & (i != n_lat)
        head = jnp.where(has_prev, prev_ref[:, cols].astype(_F32), 0.0)
    else:
        head = main_ref[r0 - _HALO:r0, cols].astype(_F32)
    if r0 + _CONV_ROWS == _TM:
        has_next = (i != n_lat - 1) & (i != nt - 1)
        tail = jnp.where(has_next, next_ref[:, cols].astype(_F32), 0.0)
    else:
        tail = main_ref[r0 + _CONV_ROWS:r0 + _CONV_ROWS + _HALO, cols].astype(_F32)
    return jnp.concatenate([head, main_ref[r0:r0 + _CONV_ROWS, cols].astype(_F32), tail], axis=0)


def _shifted(win, offset):
    n = win.shape[0]
    rolled = win if offset == 0 else pltpu.roll(win, (-offset) % n, 0)
    return rolled[_HALO:_HALO + _CONV_ROWS]


def _dwconv(name, x, cb0, channels, taps, out_dtype, n_lat):
    rows = x.shape[0]
    ktaps = taps.shape[0]
    half = ktaps // 2
    width = _divisor(channels, 1536, 128)
    assert (cb0 * channels) % width == 0
    cb0 = cb0 * channels // width
    nt = rows // _TM

    def body(main_ref, prev_ref, next_ref, taps_ref, o_ref):
        i = pl.program_id(0)

        def chunk(j, carry):
            cols = pl.ds(pl.multiple_of(j * _LANES, _LANES), _LANES)
            for r0 in range(0, _TM, _CONV_ROWS):
                win = _conv_window(main_ref, prev_ref, next_ref, r0, cols, i, n_lat, nt)
                acc = taps_ref[0:1, cols] * _shifted(win, -half)
                for k in range(1, ktaps):
                    acc = acc + taps_ref[k:k + 1, cols] * _shifted(win, k - half)
                o_ref[r0:r0 + _CONV_ROWS, cols] = acc.astype(out_dtype)
            return carry

        lax.fori_loop(0, width // _LANES, chunk, 0)

    return pl.pallas_call(
        body, name=name, grid=(nt, channels // width),
        out_shape=jax.ShapeDtypeStruct((rows, channels), out_dtype),
        in_specs=_conv_halo_specs(width, cb0, rows) + [pl.BlockSpec((ktaps, width), lambda i, j: (0, j))],
        out_specs=pl.BlockSpec((_TM, width), lambda i, j: (i, j)),
        compiler_params=_params(("arbitrary", "arbitrary")),
    )(x, x, x, taps)


def _dwconv_wgrad(name, dy, x, cb0, channels, ktaps, n_lat):
    rows = x.shape[0]
    half = ktaps // 2
    width = _divisor(channels, 1536, 128)
    cb0 = cb0 * channels // width
    nt = rows // _TM

    def body(dy_ref, main_ref, prev_ref, next_ref, o_ref):
        i = pl.program_id(1)

        @pl.when(i == 0)
        def _():
            o_ref[...] = jnp.zeros_like(o_ref)

        def chunk(j, carry):
            cols = pl.ds(pl.multiple_of(j * _LANES, _LANES), _LANES)
            for r0 in range(0, _TM, _CONV_ROWS):
                dyv = dy_ref[r0:r0 + _CONV_ROWS, cols].astype(_F32)
                win = _conv_window(main_ref, prev_ref, next_ref, r0, cols, i, n_lat, nt)
                for k in range(ktaps):
                    o_ref[k:k + 1, cols] += jnp.sum(dyv * _shifted(win, k - half), axis=0, keepdims=True)
            return carry

        lax.fori_loop(0, width // _LANES, chunk, 0)

    per = _TM // _HALO
    last = rows // _HALO - 1
    return pl.pallas_call(
        body, name=name, grid=(channels // width, nt),
        out_shape=jax.ShapeDtypeStruct((ktaps, channels), _F32),
        in_specs=[
            pl.BlockSpec((_TM, width), lambda j, i: (i, j)),
            pl.BlockSpec((_TM, width), lambda j, i: (i, cb0 + j)),
            pl.BlockSpec((_HALO, width), lambda j, i: (jnp.maximum(i * per - 1, 0), cb0 + j)),
            pl.BlockSpec((_HALO, width), lambda j, i: (jnp.minimum((i + 1) * per, last), cb0 + j)),
        ],
        out_specs=pl.BlockSpec((ktaps, width), lambda j, i: (0, j)),
        compiler_params=_params(("arbitrary", "arbitrary")),
    )(dy, x, x, x)


def _ffn_halo_specs(width, n_rows):
    per = _TM // _HALO
    last = n_rows // _HALO - 1
    return [pl.BlockSpec((_TM, width), lambda i: (i, 0)),
            pl.BlockSpec((_HALO, width), lambda i: (jnp.maximum(i * per - 1, 0), 0)),
            pl.BlockSpec((_HALO, width), lambda i: (jnp.minimum((i + 1) * per, last), 0))]


def _ffn_act(name, pu, taps, n_lat):
    rows, c2 = pu.shape
    ff = c2 // 2
    ktaps = taps.shape[0]
    half = ktaps // 2
    nt = rows // _TM

    def body(main_ref, prev_ref, next_ref, taps_ref, o_ref):
        i = pl.program_id(0)

        def conv(cols, r0):
            win = _conv_window(main_ref, prev_ref, next_ref, r0, cols, i, n_lat, nt)
            acc = taps_ref[0:1, cols] * _shifted(win, -half)
            for k in range(1, ktaps):
                acc = acc + taps_ref[k:k + 1, cols] * _shifted(win, k - half)
            return acc

        def chunk(j, carry):
            c0 = pl.multiple_of(j * _LANES, _LANES)
            cols_a, cols_g = pl.ds(c0, _LANES), pl.ds(pl.multiple_of(ff + c0, _LANES), _LANES)
            for r0 in range(0, _TM, _CONV_ROWS):
                ua, ug = conv(cols_a, r0), conv(cols_g, r0)
                o_ref[r0:r0 + _CONV_ROWS, cols_a] = (ug * jax.nn.sigmoid(ug) * ua).astype(o_ref.dtype)
            return carry

        lax.fori_loop(0, ff // _LANES, chunk, 0)

    return pl.pallas_call(
        body, name=name, grid=(nt,),
        out_shape=jax.ShapeDtypeStruct((rows, ff), _ACT),
        in_specs=_ffn_halo_specs(c2, rows) + [pl.BlockSpec((ktaps, c2), lambda i: (0, 0))],
        out_specs=pl.BlockSpec((_TM, ff), lambda i: (i, 0)),
        compiler_params=_params(("arbitrary",)),
    )(pu, pu, pu, taps)


def _ffn_act_bwd(name, pu, df, taps, n_lat):
    rows, c2 = pu.shape
    ff = c2 // 2
    ktaps = taps.shape[0]
    half = ktaps // 2
    nt = rows // _TM
    inner = slice(_HALO, _HALO + _CONV_ROWS)

    def body(main_ref, prev_ref, next_ref, dmain_ref, dprev_ref, dnext_ref, taps_ref, dpu_ref, dt_ref):
        i = pl.program_id(0)

        @pl.when(i == 0)
        def _():
            dt_ref[...] = jnp.zeros_like(dt_ref)

        def chunk(j, carry):
            c0 = pl.multiple_of(j * _LANES, _LANES)
            cols_a, cols_g = pl.ds(c0, _LANES), pl.ds(pl.multiple_of(ff + c0, _LANES), _LANES)
            for r0 in range(0, _TM, _CONV_ROWS):
                wins = [_conv_window(main_ref, prev_ref, next_ref, r0, cols, i, n_lat, nt) for cols in (cols_a, cols_g)]
                n = wins[0].shape[0]
                shifted = [[w if k == half else pltpu.roll(w, (half - k) % n, 0) for k in range(ktaps)] for w in wins]
                ua, ug = [sum(taps_ref[k:k + 1, cols] * sh[k] for k in range(ktaps))
                          for cols, sh in zip((cols_a, cols_g), shifted)]
                dfw = _conv_window(dmain_ref, dprev_ref, dnext_ref, r0, cols_a, i, n_lat, nt)
                sig = jax.nn.sigmoid(ug)
                d_a = dfw * (ug * sig)
                d_g = dfw * ua * (sig * (1.0 + ug * (1.0 - sig)))
                for du, cols, sh in ((d_a, cols_a, shifted[0]), (d_g, cols_g, shifted[1])):
                    acc = taps_ref[0:1, cols] * _shifted(du, half)
                    for k in range(1, ktaps):
                        acc = acc + taps_ref[k:k + 1, cols] * _shifted(du, half - k)
                    dpu_ref[r0:r0 + _CONV_ROWS, cols] = acc.astype(dpu_ref.dtype)
                    for k in range(ktaps):
                        dt_ref[k:k + 1, cols] += jnp.sum(du[inner] * sh[k][inner], axis=0, keepdims=True)
            return carry

        lax.fori_loop(0, ff // _LANES, chunk, 0)

    return pl.pallas_call(
        body, name=name, grid=(nt,),
        out_shape=[jax.ShapeDtypeStruct((rows, c2), _ACT), jax.ShapeDtypeStruct((ktaps, c2), _F32)],
        in_specs=_ffn_halo_specs(c2, rows) + _ffn_halo_specs(ff, rows) + [pl.BlockSpec((ktaps, c2), lambda i: (0, 0))],
        out_specs=[pl.BlockSpec((_TM, c2), lambda i: (i, 0)), pl.BlockSpec((ktaps, c2), lambda i: (0, 0))],
        compiler_params=_params(("arbitrary",)),
    )(pu, pu, pu, df, df, df, taps)


def _rope_tables(length, ctx_len):
    t = jnp.arange(length)
    row = (t // _GRID_W).astype(_F32)
    col = (t % _GRID_W).astype(_F32)
    n_freq = _HEAD_DIM // 4
    inv_freq = _ROPE_THETA ** (-jnp.arange(n_freq, dtype=_F32) / n_freq)
    ang = jnp.concatenate([row[:, None] * inv_freq, col[:, None] * inv_freq], axis=-1)
    cos, sin = jnp.cos(ang), jnp.sin(ang)
    cos = jnp.concatenate([cos, jnp.ones((ctx_len, _HEAD_DIM // 2), _F32)], axis=0)
    sin = jnp.concatenate([sin, jnp.zeros((ctx_len, _HEAD_DIM // 2), _F32)], axis=0)
    return jnp.tile(cos, (1, 4)), jnp.tile(jnp.concatenate([-sin, sin], axis=-1), (1, 2))


def _rotate(v, cos_ref, sin_ref):
    width = v.shape[1]
    reps = width // 128
    cos = jnp.tile(cos_ref[...], (1, reps))
    sin = jnp.tile(sin_ref[...], (1, reps))
    return v * cos, sin, width


def _partner(v):
    width = v.shape[1]
    half = _HEAD_DIM // 2
    lane = lax.broadcasted_iota(jnp.int32, v.shape, 1)
    return jnp.where(lane % _HEAD_DIM < half, pltpu.roll(v, width - half, 1), pltpu.roll(v, half, 1))


def _rope_fwd(name, p, cos, sin, q_w, kv_w):
    rows, width = p.shape
    scale = _HEAD_DIM ** -0.5

    def body(p_ref, cos_ref, sin_ref, q_ref, k_ref):
        v = p_ref[:, :q_w + kv_w].astype(_F32)
        vc, s, _ = _rotate(v, cos_ref, sin_ref)
        y = vc + _partner(v) * s
        q_ref[...] = (y[:, :q_w] * scale).astype(q_ref.dtype)
        k_ref[...] = y[:, q_w:].astype(k_ref.dtype)

    return pl.pallas_call(
        body, name=name, grid=(rows // _TM,),
        out_shape=[jax.ShapeDtypeStruct((rows, q_w), _ACT), jax.ShapeDtypeStruct((rows, kv_w), _ACT)],
        in_specs=[pl.BlockSpec((_TM, width), lambda i: (i, 0)), pl.BlockSpec((_TM, 128), lambda i: (i, 0)),
                  pl.BlockSpec((_TM, 128), lambda i: (i, 0))],
        out_specs=[pl.BlockSpec((_TM, q_w), lambda i: (i, 0)), pl.BlockSpec((_TM, kv_w), lambda i: (i, 0))],
        compiler_params=_params(("arbitrary",)),
    )(p, cos, sin)


def _rope_bwd(name, dq, dk, dv, cos, sin):
    rows, q_w = dq.shape
    kv_w = dk.shape[1]
    scale = _HEAD_DIM ** -0.5

    def body(dq_ref, dk_ref, dv_ref, cos_ref, sin_ref, o_ref):
        dy = jnp.concatenate([dq_ref[...].astype(_F32) * scale, dk_ref[...].astype(_F32)], axis=1)
        dyc, s, _ = _rotate(dy, cos_ref, sin_ref)
        o_ref[:, :q_w + kv_w] = (dyc + _partner(dy * s)).astype(o_ref.dtype)
        o_ref[:, q_w + kv_w:] = dv_ref[...].astype(o_ref.dtype)

    return pl.pallas_call(
        body, name=name, grid=(rows // _TM,),
        out_shape=jax.ShapeDtypeStruct((rows, q_w + 2 * kv_w), _ACT),
        in_specs=[pl.BlockSpec((_TM, q_w), lambda i: (i, 0)), pl.BlockSpec((_TM, kv_w), lambda i: (i, 0)),
                  pl.BlockSpec((_TM, kv_w), lambda i: (i, 0)), pl.BlockSpec((_TM, 128), lambda i: (i, 0)),
                  pl.BlockSpec((_TM, 128), lambda i: (i, 0))],
        out_specs=pl.BlockSpec((_TM, q_w + 2 * kv_w), lambda i: (i, 0)),
        compiler_params=_params(("arbitrary",)),
    )(dq, dk, dv, cos, sin)


def _attn_window(i, n_lat, length):
    wk = _TM + 2 * _WINDOW
    start = pl.multiple_of(jnp.clip(i * _TM - _WINDOW, 0, length - wk), _WINDOW)
    q_pos = i * _TM + lax.broadcasted_iota(jnp.int32, (_TM, wk), 0)
    k_pos = start + lax.broadcasted_iota(jnp.int32, (_TM, wk), 1)
    mask = (jnp.abs(q_pos - k_pos) <= _WINDOW) & (i < n_lat)
    return start, wk, mask


def _softmax_parts(q, k_loc, k_ctx, mask, sink):
    nt = (((1,), (1,)), ((), ()))
    s_loc = jnp.where(mask, lax.dot_general(q, k_loc, nt, preferred_element_type=_F32), _NEG_INF)
    s_ctx = lax.dot_general(q, k_ctx, nt, preferred_element_type=_F32)
    m = jnp.maximum(jnp.maximum(jnp.max(s_loc, axis=-1, keepdims=True), jnp.max(s_ctx, axis=-1, keepdims=True)),
                    sink)
    e_loc = jnp.exp(s_loc - m)
    e_ctx = jnp.exp(s_ctx - m)
    e_sink = jnp.exp(sink - m)
    inv = 1.0 / (jnp.sum(e_loc, axis=-1, keepdims=True) + jnp.sum(e_ctx, axis=-1, keepdims=True) + e_sink)
    return e_loc * inv, e_ctx * inv, e_sink * inv


def _attn_fwd(name, q, k, p, sinks, n_lat, length, kv_w):
    rows, q_w = q.shape
    ctx_len = rows - length
    n_heads = q_w // _HEAD_DIM
    n_kv = kv_w // _HEAD_DIM
    group = n_heads // n_kv
    v_cb = p.shape[1] // kv_w - 1
    hd = _HEAD_DIM

    def body(q_ref, k_ref, v_ref, sink_ref, o_ref):
        i = pl.program_id(0)
        start, wk, mask = _attn_window(i, n_lat, length)
        for h in range(n_kv):
            k_loc = k_ref[pl.ds(start, wk), h * hd:(h + 1) * hd]
            v_loc = v_ref[pl.ds(start, wk), h * hd:(h + 1) * hd]
            k_ctx = k_ref[length:length + ctx_len, h * hd:(h + 1) * hd]
            v_ctx = v_ref[length:length + ctx_len, h * hd:(h + 1) * hd]
            for g in range(group):
                n = h * group + g
                p_loc, p_ctx, _ = _softmax_parts(q_ref[:, n * hd:(n + 1) * hd], k_loc, k_ctx, mask,
                                                 sink_ref[:, n:n + 1])
                o = (jnp.dot(p_loc.astype(_MM), v_loc, preferred_element_type=_F32)
                     + jnp.dot(p_ctx.astype(_MM), v_ctx, preferred_element_type=_F32))
                o_ref[:, n * hd:(n + 1) * hd] = o.astype(o_ref.dtype)

    return pl.pallas_call(
        body, name=name, grid=(rows // _TM,),
        out_shape=jax.ShapeDtypeStruct((rows, q_w), _ACT),
        in_specs=[pl.BlockSpec((_TM, q_w), lambda i: (i, 0)), pl.BlockSpec((rows, kv_w), lambda i: (0, 0)),
                  pl.BlockSpec((rows, kv_w), lambda i: (0, v_cb)), pl.BlockSpec((1, n_heads), lambda i: (0, 0))],
        out_specs=pl.BlockSpec((_TM, q_w), lambda i: (i, 0)),
        compiler_params=_params(("arbitrary",)),
    )(q, k, p, sinks)


def _attn_bwd(name, q, k, p, sinks, do, n_lat, length, kv_w):
    rows, q_w = q.shape
    ctx_len = rows - length
    n_heads = q_w // _HEAD_DIM
    n_kv = kv_w // _HEAD_DIM
    group = n_heads // n_kv
    v_cb = p.shape[1] // kv_w - 1
    hd = _HEAD_DIM
    nt_dims = (((1,), (1,)), ((), ()))
    tn_dims = (((0,), (0,)), ((), ()))

    def body(q_ref, k_ref, v_ref, sink_ref, do_ref, dq_ref, dk_out, dv_out, ds_ref, dk_ref, dv_ref, out_sems):
        i = pl.program_id(0)

        @pl.when(i == 0)
        def _():
            dk_ref[...] = jnp.zeros_like(dk_ref)
            dv_ref[...] = jnp.zeros_like(dv_ref)
            ds_ref[...] = jnp.zeros_like(ds_ref)

        start, wk, mask = _attn_window(i, n_lat, length)
        head_lane = lax.broadcasted_iota(jnp.int32, (1, n_heads), 1)
        dsink = jnp.zeros((1, n_heads), _F32)
        for h in range(n_kv):
            cols = slice(h * hd, (h + 1) * hd)
            k_loc = k_ref[pl.ds(start, wk), cols]
            v_loc = v_ref[pl.ds(start, wk), cols]
            k_ctx = k_ref[length:length + ctx_len, cols]
            v_ctx = v_ref[length:length + ctx_len, cols]
            dk_loc = jnp.zeros((wk, hd), _F32)
            dv_loc = jnp.zeros((wk, hd), _F32)
            dk_ctx = jnp.zeros((ctx_len, hd), _F32)
            dv_ctx = jnp.zeros((ctx_len, hd), _F32)
            for g in range(group):
                n = h * group + g
                qh = q_ref[:, n * hd:(n + 1) * hd]
                doh = do_ref[:, n * hd:(n + 1) * hd].astype(_MM)
                p_loc, p_ctx, p_sink = _softmax_parts(qh, k_loc, k_ctx, mask, sink_ref[:, n:n + 1])
                dp_loc = lax.dot_general(doh, v_loc, nt_dims, preferred_element_type=_F32)
                dp_ctx = lax.dot_general(doh, v_ctx, nt_dims, preferred_element_type=_F32)
                dsum = (jnp.sum(p_loc * dp_loc, axis=-1, keepdims=True)
                        + jnp.sum(p_ctx * dp_ctx, axis=-1, keepdims=True))
                ds_loc = (p_loc * (dp_loc - dsum)).astype(_MM)
                ds_ctx = (p_ctx * (dp_ctx - dsum)).astype(_MM)
                dsink = dsink + jnp.where(head_lane == n, -jnp.sum(p_sink * dsum), 0.0)
                dq = (jnp.dot(ds_loc, k_loc, preferred_element_type=_F32)
                      + jnp.dot(ds_ctx, k_ctx, preferred_element_type=_F32))
                dq_ref[:, n * hd:(n + 1) * hd] = dq.astype(dq_ref.dtype)
                dk_loc += lax.dot_general(ds_loc, qh, tn_dims, preferred_element_type=_F32)
                dk_ctx += lax.dot_general(ds_ctx, qh, tn_dims, preferred_element_type=_F32)
                dv_loc += lax.dot_general(p_loc.astype(_MM), doh, tn_dims, preferred_element_type=_F32)
                dv_ctx += lax.dot_general(p_ctx.astype(_MM), doh, tn_dims, preferred_element_type=_F32)
            dk_ref[pl.ds(start, wk), cols] += dk_loc
            dv_ref[pl.ds(start, wk), cols] += dv_loc
            dk_ref[length:length + ctx_len, cols] += dk_ctx
            dv_ref[length:length + ctx_len, cols] += dv_ctx
        ds_ref[...] += dsink

        @pl.when(i == rows // _TM - 1)
        def _():
            copies = [pltpu.make_async_copy(dk_ref, dk_out, out_sems.at[0]),
                      pltpu.make_async_copy(dv_ref, dv_out, out_sems.at[1])]
            for cp in copies:
                cp.start()
            for cp in copies:
                cp.wait()

    return pl.pallas_call(
        body, name=name, grid=(rows // _TM,),
        out_shape=[jax.ShapeDtypeStruct((rows, q_w), _ACT), jax.ShapeDtypeStruct((rows, kv_w), _F32),
                   jax.ShapeDtypeStruct((rows, kv_w), _F32), jax.ShapeDtypeStruct((1, n_heads), _F32)],
        in_specs=[pl.BlockSpec((_TM, q_w), lambda i: (i, 0)), pl.BlockSpec((rows, kv_w), lambda i: (0, 0)),
                  pl.BlockSpec((rows, kv_w), lambda i: (0, v_cb)), pl.BlockSpec((1, n_heads), lambda i: (0, 0)),
                  pl.BlockSpec((_TM, q_w), lambda i: (i, 0))],
        out_specs=[pl.BlockSpec((_TM, q_w), lambda i: (i, 0)), pl.BlockSpec(memory_space=pl.ANY),
                   pl.BlockSpec(memory_space=pl.ANY), pl.BlockSpec((1, n_heads), lambda i: (0, 0))],
        scratch_shapes=[pltpu.VMEM((rows, kv_w), _F32), pltpu.VMEM((rows, kv_w), _F32),
                        pltpu.SemaphoreType.DMA((2,))],
        compiler_params=_params(("arbitrary",)),
    )(q, k, p, sinks, do)


def _loss_head(name, xs, gain, target, n_lat):
    rows, d = xs.shape

    def body(x_ref, g_ref, t_ref, loss_ref, dg_ref, dx_ref):
        i = pl.program_id(0)

        @pl.when(i == 0)
        def _():
            loss_ref[...] = jnp.zeros_like(loss_ref)
            dg_ref[...] = jnp.zeros_like(dg_ref)

        @pl.when(i < n_lat)
        def _():
            tv = t_ref[...]

            def f(gain_, x):
                y = x * lax.rsqrt(jnp.mean(x * x, axis=-1, keepdims=True) + _EPS) * gain_
                return 0.5 * jnp.sum(jnp.mean(jnp.square(y - tv), axis=-1))

            val, (dg, dx) = jax.value_and_grad(f, argnums=(0, 1))(g_ref[...], x_ref[...])
            loss_ref[...] += val
            dg_ref[...] += dg
            dx_ref[...] = dx

        @pl.when(i >= n_lat)
        def _():
            dx_ref[...] = jnp.zeros_like(dx_ref)

    return pl.pallas_call(
        body, name=name, grid=(rows // _TM,),
        out_shape=[jax.ShapeDtypeStruct((1, 128), _F32), jax.ShapeDtypeStruct((1, d), _F32),
                   jax.ShapeDtypeStruct((rows, d), _F32)],
        in_specs=[pl.BlockSpec((_TM, d), lambda i: (i, 0)), pl.BlockSpec((1, d), lambda i: (0, 0)),
                  pl.BlockSpec((_TM, d), lambda i: (jnp.minimum(i, n_lat - 1), 0))],
        out_specs=[pl.BlockSpec((1, 128), lambda i: (0, 0)), pl.BlockSpec((1, d), lambda i: (0, 0)),
                   pl.BlockSpec((_TM, d), lambda i: (i, 0))],
        compiler_params=_params(("arbitrary",)),
    )(xs, gain, target)


def _adamw(name, w, g, m, v):
    rows, cols = w.shape
    tr = _divisor(rows, 512, 8)
    b1, b2 = _ADAM["b1"], _ADAM["b2"]
    c1 = 1.0 - b1 ** _ADAM["step"]
    c2 = 1.0 - b2 ** _ADAM["step"]

    def body(w_ref, g_ref, m_ref, v_ref, d_ref, nm_ref, nv_ref):
        gv = g_ref[...]
        nm = b1 * m_ref[...] + (1.0 - b1) * gv
        nv = b2 * v_ref[...] + (1.0 - b2) * jnp.square(gv)
        d_ref[...] = -_ADAM["lr"] * ((nm / c1) / (jnp.sqrt(nv / c2) + _ADAM["eps"]) + _ADAM["wd"] * w_ref[...])
        nm_ref[...] = nm
        nv_ref[...] = nv

    spec = pl.BlockSpec((tr, cols), lambda i: (i, 0))
    return pl.pallas_call(
        body, name=name, grid=(rows // tr,),
        out_shape=[jax.ShapeDtypeStruct((rows, cols), _F32)] * 3,
        in_specs=[spec] * 4, out_specs=[spec] * 3,
        compiler_params=_params(("arbitrary",)),
    )(w, g, m, v)


def _pack(arrays, cols=128):
    flat = jnp.concatenate([a.reshape(-1).astype(_F32) for a in arrays])
    pad = (-flat.shape[0]) % (64 * cols)
    return jnp.pad(flat, (0, pad)).reshape(-1, cols)


def _unpack(flat, shapes):
    out, off = [], 0
    for s in shapes:
        n = 1
        for d in s:
            n *= d
        out.append(flat[..., off:off + n].reshape(flat.shape[:-1] + tuple(s)))
        off += n
    return out


def _gather_channels(parts):
    moved = jnp.moveaxis(parts, 0, -2)
    return moved.reshape(moved.shape[:-2] + (moved.shape[-2] * moved.shape[-1],))


def kernel(x, c, ctx, c_ctx, w_mod, b_mod, norm_mix, norm_ffn, w_in_ab, conv_a, conv_b, conv_b_bias, ln_b_gain, ln_b_bias, w_out_ab, w_qkv, w_o, sinks, w_up, w_conv_ffn, w_down, final_norm, loss_target, m_c_ctx, m_w_mod, m_b_mod, m_norm_mix, m_norm_ffn, m_w_in_ab, m_conv_a, m_conv_b, m_conv_b_bias, m_ln_b_gain, m_ln_b_bias, m_w_out_ab, m_w_qkv, m_w_o, m_sinks, m_w_up, m_w_conv_ffn, m_w_down, m_final_norm, v_c_ctx, v_w_mod, v_b_mod, v_norm_mix, v_norm_ffn, v_w_in_ab, v_conv_a, v_conv_b, v_conv_b_bias, v_ln_b_gain, v_ln_b_bias, v_w_out_ab, v_w_qkv, v_w_o, v_sinks, v_w_up, v_w_conv_ffn, v_w_down, v_final_norm):
    args = dict(locals())
    weight_names = ["c_ctx", "w_mod", "b_mod", "norm_mix", "norm_ffn", "w_in_ab", "conv_a", "conv_b", "conv_b_bias",
                    "ln_b_gain", "ln_b_bias", "w_out_ab", "w_qkv", "w_o", "sinks", "w_up", "w_conv_ffn", "w_down",
                    "final_norm"]
    length, d = x.shape[1], x.shape[2]
    ctx_len = ctx.shape[1]
    assert ctx_len == _TM and length % _TM == 0 and x.shape[0] == 1
    n_lat = length // _TM
    depth = w_mod.shape[0]
    n_even, n_odd = w_in_ab.shape[0], w_qkv.shape[0]
    a_w = conv_a.shape[2] * _NDEV
    b_w = conv_b.shape[2] * _NDEV
    assert a_w == b_w
    q_w = w_o.shape[1] * _NDEV
    kv_w = (w_qkv.shape[2] * _NDEV - q_w) // 2
    d_ff = w_down.shape[1] * _NDEV
    dev = 4 * lax.axis_index("x") + 2 * lax.axis_index("y") + lax.axis_index("c")

    small_shapes = [c.shape[1:], conv_a.shape, conv_b.shape, w_conv_ffn.shape]
    g0 = _all_gather_small("gather_small_params", _pack([c, conv_a, conv_b, w_conv_ffn]))
    c_parts, ca_parts, cb_parts, cf_parts = _unpack(g0.reshape(_NDEV, -1), small_shapes)
    conv_a_full = _gather_channels(ca_parts)
    conv_b_full = _gather_channels(cb_parts)
    conv_f_full = _gather_channels(cf_parts)

    cond = jnp.concatenate([c_parts, c_ctx[None], jnp.zeros((16 - _NDEV - 1, d), _F32)], axis=0)
    mod_cols = w_mod.shape[2]
    m_shard = _mod_forward("mod_forward", cond, w_mod)
    m_all = _all_gather_small("gather_mod", m_shard.reshape(depth * 16, mod_cols))
    m_all = jnp.moveaxis(m_all.reshape(_NDEV, depth, 16, mod_cols), 0, 2).reshape(depth, 16, _NDEV * mod_cols)
    m_all = m_all + b_mod[:, None, :]
    m_lat = lax.dynamic_index_in_dim(m_all, dev, axis=1, keepdims=False)
    m_ctx = m_all[:, _NDEV]

    def mod_vec(l, j):
        return jnp.stack([m_lat[l, j * d:(j + 1) * d], m_ctx[l, j * d:(j + 1) * d]])[:, None, :]

    def layer_mats(l):
        if l % 2 == 0:
            first = [("in", l // 2, w_in_ab[l // 2].T), ("out", l // 2, w_out_ab[l // 2])]
        else:
            first = [("qkv", l // 2, w_qkv[l // 2].T), ("o", l // 2, w_o[l // 2])]
        return first + [("up", l, w_up[l].T), ("down", l, w_down[l])]

    late = 0.0 * m_all[0, 0, 0]
    piece_mats = [layer_mats(0)[:2], layer_mats(0)[2:]] + [layer_mats(l) for l in range(1, depth)]
    n_pieces = len(piece_mats)
    slab_off, slab_r, piece_of, piece_keys, piece_rows, slabs = {}, {}, {}, [], [], []
    for p, mats in enumerate(piece_mats):
        off, keys = 0, []
        for fam, idx, mat in mats:
            slab_off[fam, idx], slab_r[fam], piece_of[fam, idx] = off, mat.shape[0], p
            off += mat.shape[0]
            keys.append((fam, idx))
        piece_keys.append(keys)
        piece_rows.append(off)
        slabs.append(jnp.concatenate([(mat + late).astype(_MM) for _, _, mat in mats], axis=0))
    wgs, gathers, start_token = [None] * n_pieces, [None] * n_pieces, jnp.zeros((), _F32)
    for p in range(n_pieces):
        land = _fill_own_slot(f"gather_fill_{p}", [slabs[p]], [0], piece_rows[p])
        gathers[p] = _exchange_start(f"gather_start_{p}", [slabs[p]], land, [0])
        start_token = start_token + gathers[p][-1][0, 0]

    def wref(fam, idx):
        return wgs[piece_of[fam, idx]], slab_off[fam, idx], slab_r[fam]

    cos, sin = _rope_tables(length, ctx_len)
    xs = jnp.concatenate([x[0], ctx[0]], axis=0)

    def full(a):
        return (a.reshape(1, -1), "full")

    wgs[0] = _exchange_wait("gather_wait_0", gathers[0], [0], m_all)
    saved = []
    for l in range(depth):
        sv = {"x_in": xs}
        if l > 0:
            wgs[1 + l] = _exchange_wait(f"gather_wait_{1 + l}", gathers[1 + l], [0], xs)
        gain1 = full(norm_mix[l] + start_token) if l == 0 else full(norm_mix[l])
        norm1 = [gain1, (mod_vec(l, 0), "stream"), (mod_vec(l, 1), "stream")]
        if l % 2 == 0:
            e = l // 2
            h1, p = _norm_proj(f"proj_in_{l}", xs, norm1, *wref("in", e), n_lat)
            (qm,) = _rowfn(f"premix_{l}", _f_premix, [], [(p, a_w, j) for j in range(5)],
                           [(None, [a_w]), (_ACT, [a_w, b_w])], n_lat)
            cv_a = _dwconv(f"conv_a_{l}", qm, 0, a_w, conv_a_full[e], _ACT, n_lat)
            cv_b = _dwconv(f"conv_b_{l}", qm, 1, b_w, conv_b_full[e], _ACT, n_lat)
            post_params = [full(conv_b_bias[e]), full(ln_b_gain[e]), full(ln_b_bias[e])]
            (z,) = _rowfn(f"postmix_{l}", _f_postmix, post_params, [(p, a_w, 0), (cv_a, a_w, 0), (cv_b, b_w, 0)],
                          [(_ACT, [a_w, b_w])], n_lat)
            y1, xs = _proj_residual(f"proj_out_{l}", z, *wref("out", e), xs, mod_vec(l, 2), n_lat)
            sv.update(p=p, qm=qm, cv_a=cv_a, cv_b=cv_b, z=z)
        else:
            o = l // 2
            h1, p = _norm_proj(f"proj_qkv_{l}", xs, norm1, *wref("qkv", o), n_lat)
            qr, kr = _rope_fwd(f"rope_{l}", p, cos, sin, q_w, kv_w)
            sk = sinks[o].reshape(1, -1)
            z = _attn_fwd(f"attn_{l}", qr, kr, p, sk, n_lat, length, kv_w)
            y1, xs = _proj_residual(f"proj_o_{l}", z, *wref("o", o), xs, mod_vec(l, 2), n_lat)
            sv.update(p=p, qr=qr, kr=kr, z=z)
        sv.update(h1=h1, y1=y1, x_mid=xs)
        if l == 0:
            wgs[1] = _exchange_wait("gather_wait_1", gathers[1], [0], xs)
        norm2 = [full(norm_ffn[l]), (mod_vec(l, 3), "stream"), (mod_vec(l, 4), "stream")]
        h2, pu = _norm_proj(f"proj_up_{l}", xs, norm2, *wref("up", l), n_lat)
        f = _ffn_act(f"ffn_act_{l}", pu, conv_f_full[l], n_lat)
        y2, xs = _proj_residual(f"proj_down_{l}", f, *wref("down", l), xs, mod_vec(l, 5), n_lat)
        sv.update(h2=h2, pu=pu, f=f, y2=y2)
        saved.append(sv)

    loss_part, d_final_norm, dxs = _loss_head("loss_head", xs, final_norm.reshape(1, -1), loss_target[0], n_lat)
    loss = lax.psum(loss_part[0, 0], ("x", "y", "c"))

    wgrads = {}
    d_mod = [[None] * 6 for _ in range(depth)]
    d_norm_mix, d_norm_ffn = [None] * depth, [None] * depth
    d_conv_a, d_conv_b = [None] * n_even, [None] * n_even
    d_bias, d_ln_g, d_ln_b = [None] * n_even, [None] * n_even, [None] * n_even
    d_sinks = [None] * n_odd
    d_conv_f = [None] * depth
    exchanges, recvs, exchange_token = [None] * n_pieces, [None] * n_pieces, jnp.zeros((), _F32)

    def piece_parts(p):
        return ([wgrads[key].reshape(_NDEV, slab_r[key[0]], d) for key in piece_keys[p]],
                [slab_off[key] for key in piece_keys[p]])

    def start_exchange(p):
        parts, offsets = piece_parts(p)
        land = _fill_own_slot(f"exchange_fill_{p}", parts, offsets, piece_rows[p])
        exchanges[p] = _exchange_start(f"exchange_start_{p}", parts, land, offsets)
        return exchanges[p][-1][0, 0]

    for l in reversed(range(depth)):
        sv = saved[l]
        d_mod[l][5], dy2, df = _gate_proj_bwd(f"bwd_down_{l}", dxs, sv["y2"], mod_vec(l, 5) + exchange_token,
                                              *wref("down", l), n_lat)
        wgrads["down", l] = _mm_tn(f"wgrad_down_{l}", sv["f"], dy2, _ACT)
        dpu, d_conv_f[l] = _ffn_act_bwd(f"ffn_act_bwd_{l}", sv["pu"], df, conv_f_full[l], n_lat)
        wgrads["up", l] = _mm_tn(f"wgrad_up_{l}", dpu, sv["h2"], _ACT)
        norm2 = [full(norm_ffn[l]), (mod_vec(l, 3), "stream"), (mod_vec(l, 4), "stream")]
        (dgain, dsh, dsc), dxs = _proj_norm_bwd(f"bwd_up_{l}", dpu, *wref("up", l), sv["x_mid"], norm2, dxs, n_lat)
        d_norm_ffn[l], d_mod[l][3], d_mod[l][4] = dgain, dsh, dsc
        mix_token = start_exchange(1) if l == 0 else jnp.zeros((), _F32)
        gate1 = mod_vec(l, 2) + mix_token
        norm1 = [full(norm_mix[l]), (mod_vec(l, 0), "stream"), (mod_vec(l, 1), "stream")]
        if l % 2 == 0:
            e = l // 2
            d_mod[l][2], dy1, dz = _gate_proj_bwd(f"bwd_out_{l}", dxs, sv["y1"], gate1, *wref("out", e), n_lat)
            wgrads["out", e] = _mm_tn(f"wgrad_out_{l}", sv["z"], dy1, _ACT)
            post_params = [full(conv_b_bias[e]), full(ln_b_gain[e]), full(ln_b_bias[e])]
            (dbias, dlg, dlb), (dgb, dcv_a, dcv_b) = _rowfn_bwd(
                f"postmix_bwd_{l}", _f_postmix, post_params,
                [(sv["p"], a_w, 0), (sv["cv_a"], a_w, 0), (sv["cv_b"], b_w, 0)], [0, 1, 2], [(dz, [a_w, b_w])],
                [(_ACT, [0]), (_ACT, [1]), (_ACT, [2])], n_lat)
            d_bias[e], d_ln_g[e], d_ln_b[e] = dbias, dlg, dlb
            d_conv_a[e] = _dwconv_wgrad(f"conv_a_wgrad_{l}", dcv_a, sv["qm"], 0, a_w, conv_a_full.shape[1], n_lat)
            d_conv_b[e] = _dwconv_wgrad(f"conv_b_wgrad_{l}", dcv_b, sv["qm"], 1, b_w, conv_b_full.shape[1], n_lat)
            dq_a = _dwconv(f"conv_a_bwd_{l}", dcv_a, 0, a_w, conv_a_full[e][::-1], _ACT, n_lat)
            dq_b = _dwconv(f"conv_b_bwd_{l}", dcv_b, 0, b_w, conv_b_full[e][::-1], _ACT, n_lat)
            _, (dp,) = _rowfn_bwd(f"premix_bwd_{l}", _f_premix, [], [(sv["p"], a_w, j) for j in range(5)],
                                  [0, 1, 2, 3, 4], [(dgb, [a_w]), (dq_a, [a_w]), (dq_b, [b_w])],
                                  [(_ACT, [0, 1, 2, 3, 4])], n_lat)
            wgrads["in", e] = _mm_tn(f"wgrad_in_{l}", dp, sv["h1"], _ACT)
            if l == 0:
                norm1 = [full(norm_mix[l] + start_exchange(0))] + norm1[1:]
            (dgain, dsh, dsc), dxs = _proj_norm_bwd(f"bwd_in_{l}", dp, *wref("in", e), sv["x_in"], norm1, dxs, n_lat)
        else:
            o = l // 2
            d_mod[l][2], dy1, dz = _gate_proj_bwd(f"bwd_o_{l}", dxs, sv["y1"], gate1, *wref("o", o), n_lat)
            wgrads["o", o] = _mm_tn(f"wgrad_o_{l}", sv["z"], dy1, _ACT)
            sk = sinks[o].reshape(1, -1)
            dqr, dkr, dv, dsk = _attn_bwd(f"attn_bwd_{l}", sv["qr"], sv["kr"], sv["p"], sk, dz, n_lat, length, kv_w)
            d_sinks[o] = dsk
            dp = _rope_bwd(f"rope_bwd_{l}", dqr, dkr, dv, cos, sin)
            wgrads["qkv", o] = _mm_tn(f"wgrad_qkv_{l}", dp, sv["h1"], _ACT)
            (dgain, dsh, dsc), dxs = _proj_norm_bwd(f"bwd_qkv_{l}", dp, *wref("qkv", o), sv["x_in"], norm1, dxs, n_lat)
        d_norm_mix[l], d_mod[l][0], d_mod[l][1] = dgain, dsh, dsc
        if l > 0:
            exchange_token = start_exchange(1 + l)
    grad_x = dxs[:length][None]

    gsums = []
    for p in range(n_pieces):
        recvs[p] = _exchange_wait(f"exchange_wait_{p}", exchanges[p], piece_parts(p)[1], dxs)
        gsums.append(_sum_slots(f"sum_weight_grads_{p}", recvs[p]))

    def slab_grad(fam, count, transposed):
        mats = [gsums[piece_of[fam, i]][slab_off[fam, i]:slab_off[fam, i] + slab_r[fam]] for i in range(count)]
        return jnp.stack([m_.T if transposed else m_ for m_ in mats])

    grads = {
        "w_in_ab": slab_grad("in", n_even, True), "w_qkv": slab_grad("qkv", n_odd, True),
        "w_up": slab_grad("up", depth, True), "w_out_ab": slab_grad("out", n_even, False),
        "w_o": slab_grad("o", n_odd, False), "w_down": slab_grad("down", depth, False),
    }

    weight_names_big = ["w_in_ab", "w_out_ab", "w_qkv", "w_o", "w_up", "w_down"]
    delta, new_m, new_v = {}, {}, {}

    def adam_big(n):
        w = args[n]
        flat = [a.reshape(-1, w.shape[-1]) for a in (w, grads[n], args["m_" + n], args["v_" + n])]
        dl, nm, nv = _adamw(f"adamw_{n}", *flat)
        delta[n], new_m[n], new_v[n] = dl.reshape(w.shape), nm.reshape(w.shape), nv.reshape(w.shape)

    for n in weight_names_big:
        adam_big(n)

    dm_dev = jnp.stack([jnp.concatenate([d_mod[l][j][:, 0, :] for j in range(6)], axis=-1)
                        for l in range(depth)])
    small_grads = [dm_dev, jnp.stack(d_norm_mix), jnp.stack(d_norm_ffn), jnp.stack(d_conv_a), jnp.stack(d_conv_b),
                   jnp.stack(d_bias), jnp.stack(d_ln_g), jnp.stack(d_ln_b), jnp.stack(d_sinks), jnp.stack(d_conv_f),
                   d_final_norm]
    sg_shapes = [a.shape for a in small_grads]
    sg_all = _all_gather_small("gather_small_grads", _pack(small_grads))
    sg_sum = _sum_slots("sum_small_grads", sg_all)
    (dm_sum, g_norm_mix, g_norm_ffn, g_conv_a, g_conv_b, g_bias, g_ln_g, g_ln_b, g_sinks, g_conv_f,
     g_final_norm) = _unpack(sg_sum.reshape(-1), sg_shapes)
    dm_each = _unpack(sg_all.reshape(_NDEV, -1), sg_shapes[:1])[0]

    def my_channels(a):
        width = a.shape[-1] // _NDEV
        return lax.dynamic_slice_in_dim(a, dev * width, width, axis=a.ndim - 1)

    grads["b_mod"] = dm_sum[:, 0] + dm_sum[:, 1]
    grads["norm_mix"] = g_norm_mix.reshape(depth, d)
    grads["norm_ffn"] = g_norm_ffn.reshape(depth, d)
    grads["conv_a"] = my_channels(g_conv_a)
    grads["conv_b"] = my_channels(g_conv_b)
    grads["conv_b_bias"] = g_bias.reshape(n_even, b_w)
    grads["ln_b_gain"] = g_ln_g.reshape(n_even, b_w)
    grads["ln_b_bias"] = g_ln_b.reshape(n_even, b_w)
    grads["sinks"] = g_sinks.reshape(n_odd, -1)
    grads["w_conv_ffn"] = my_channels(g_conv_f)
    grads["final_norm"] = g_final_norm.reshape(d)

    dm_rows = jnp.concatenate([jnp.moveaxis(dm_each[:, :, 0], 0, 1), dm_sum[:, 1:2],
                               jnp.zeros((depth, 16 - _NDEV - 1, 6 * d), _F32)], axis=1)
    dm_mine = my_channels(dm_rows)
    grads["w_mod"] = jnp.stack([_mm_tn(f"wgrad_mod_{l}", cond, dm_mine[l], _F32, silu_a=True) for l in range(depth)])
    dcond = _mod_backward_cond("mod_backward_cond", dm_mine, w_mod)
    dcond_all = _all_gather_small("gather_dcond", dcond)
    dcond_sum = _sum_slots("sum_dcond", dcond_all)[_NDEV]
    sg = jax.nn.sigmoid(c_ctx)
    grads["c_ctx"] = dcond_sum * (sg * (1.0 + c_ctx * (1.0 - sg)))

    adam_big("w_mod")
    small = [n for n in weight_names if n != "w_mod" and n not in weight_names_big]
    shapes = [args[n].shape for n in small]
    grads = {n: grads[n].reshape(args[n].shape) for n in grads}
    dl, nm, nv = _adamw("adamw_small", _pack([args[n] for n in small]), _pack([grads[n] for n in small]),
                        _pack([args["m_" + n] for n in small]), _pack([args["v_" + n] for n in small]))
    for res, packed in ((delta, dl), (new_m, nm), (new_v, nv)):
        for n, a in zip(small, _unpack(packed.reshape(-1), shapes)):
            res[n] = a

    return (loss, grad_x, *[grads[n] for n in weight_names], *[delta[n] for n in weight_names],
            *[new_m[n] for n in weight_names], *[new_v[n] for n in weight_names])
```

```python
import functools

import jax
import jax.numpy as jnp
from jax import lax
from jax.experimental import pallas as pl
from jax.experimental.pallas import tpu as pltpu

_F32 = jnp.float32
_MM = jnp.bfloat16
_ACT = jnp.bfloat16
_TM = 256
_HALO = 16
_LANES = 128
_CONV_ROWS = 128
_NDEV = 8
_HEAD_DIM = 64
_WINDOW = 128
_GRID_W = 64
_ROPE_THETA = 10000.0
_EPS = 1e-6
_NEG_INF = -1e30
_VMEM_LIMIT = 56 * 1024 * 1024
_ADAM = dict(lr=0.001, b1=0.9, b2=0.999, eps=1e-08, wd=0.01, step=10)
_MESH = pl.DeviceIdType.MESH


def _params(sem=None):
    return pltpu.CompilerParams(dimension_semantics=sem, vmem_limit_bytes=_VMEM_LIMIT)


def _divisor(n, cap, mult):
    if n <= cap:
        return n
    for d in range(cap - cap % mult, 0, -mult):
        if n % d == 0:
            return d
    raise ValueError(f"no tile for {n}")


def _my_coords():
    return lax.axis_index("x"), lax.axis_index("y"), lax.axis_index("c")


def _peer(k):
    x, y, c = _my_coords()
    px = 1 - x if k & 4 else x
    py = 1 - y if k & 2 else y
    pc = 1 - c if k & 1 else c
    return (px, py, pc), 4 * px + 2 * py + pc


def _all_gather_small(name, v):
    rows, cols = v.shape

    def body(v_ref, out_ref, send_sems, recv_sems):
        x, y, c = _my_coords()
        me = 4 * x + 2 * y + c
        out_ref[me] = v_ref[...]
        sends = []
        for k in range(1, _NDEV):
            peer, _ = _peer(k)
            cp = pltpu.make_async_remote_copy(
                src_ref=v_ref, dst_ref=out_ref.at[me], send_sem=send_sems.at[k - 1], recv_sem=recv_sems.at[k - 1],
                device_id=peer, device_id_type=_MESH)
            cp.start()
            sends.append(cp)
        for k in range(1, _NDEV):
            peer, pid = _peer(k)
            pltpu.make_async_remote_copy(
                src_ref=v_ref, dst_ref=out_ref.at[pid], send_sem=send_sems.at[k - 1], recv_sem=recv_sems.at[k - 1],
                device_id=peer, device_id_type=_MESH).wait_recv()
        for cp in sends:
            cp.wait_send()

    return pl.pallas_call(
        body, name=name,
        out_shape=jax.ShapeDtypeStruct((_NDEV, rows, cols), v.dtype),
        in_specs=[pl.BlockSpec(memory_space=pltpu.VMEM)],
        out_specs=pl.BlockSpec(memory_space=pltpu.VMEM),
        scratch_shapes=[pltpu.SemaphoreType.DMA((_NDEV - 1,)), pltpu.SemaphoreType.DMA((_NDEV - 1,))],
        compiler_params=pltpu.CompilerParams(vmem_limit_bytes=_VMEM_LIMIT),
    )(v)


def _sum_slots(name, v):
    _, rows, cols = v.shape
    tr = _divisor(rows, 1024, 16)

    def body(v_ref, o_ref):
        acc = v_ref[0].astype(_F32)
        for e in range(1, _NDEV):
            acc = acc + v_ref[e].astype(_F32)
        o_ref[...] = acc

    return pl.pallas_call(
        body, name=name, grid=(rows // tr,),
        out_shape=jax.ShapeDtypeStruct((rows, cols), _F32),
        in_specs=[pl.BlockSpec((_NDEV, tr, cols), lambda i: (0, i, 0))],
        out_specs=pl.BlockSpec((tr, cols), lambda i: (i, 0)),
        compiler_params=_params(("arbitrary",)),
    )(v)


_HBM_SPEC = pl.BlockSpec(memory_space=pltpu.HBM)
_SEM_SPEC = pl.BlockSpec(memory_space=pltpu.SEMAPHORE)
_EFFECT = pltpu.SideEffectType.DATAFLOW_SIDE_EFFECTING


def _in_hbm(a):
    return pltpu.with_memory_space_constraint(a, pltpu.HBM)


def _block_for(ref, device):
    return ref if len(ref.shape) == 2 else ref.at[device]


def _fill_own_slot(name, srcs, offsets, total_rows):
    n = len(srcs)
    cols = srcs[0].shape[-1]

    def body(*refs):
        src_refs, out_ref, bufs, sems = refs[:n], refs[n], refs[n + 1:2 * n + 1], refs[2 * n + 1]
        x, y, c = _my_coords()
        me = 4 * x + 2 * y + c
        loads = [pltpu.make_async_copy(_block_for(src_refs[m], me), bufs[m], sems.at[0, m]) for m in range(n)]
        stores = [pltpu.make_async_copy(bufs[m], out_ref.at[me, pl.ds(offsets[m], srcs[m].shape[-2]), :],
                                        sems.at[1, m]) for m in range(n)]
        for copies in (loads, stores):
            for cp in copies:
                cp.start()
            for cp in copies:
                cp.wait()

    return pl.pallas_call(
        body, name=name,
        out_shape=jax.ShapeDtypeStruct((_NDEV, total_rows, cols), srcs[0].dtype),
        in_specs=[pl.BlockSpec(memory_space=pl.ANY)] * n,
        out_specs=pl.BlockSpec(memory_space=pl.ANY),
        scratch_shapes=[pltpu.VMEM(s.shape[-2:], s.dtype) for s in srcs] + [pltpu.SemaphoreType.DMA((2, n))],
        compiler_params=pltpu.CompilerParams(vmem_limit_bytes=_VMEM_LIMIT),
    )(*srcs)


def _exchange_start(name, srcs, land, offsets):
    n = len(srcs)

    def body(*refs):
        src_refs, land_ref = refs[:n], refs[n]
        send_sems, recv_sems, token = refs[n + 1], refs[n + 2], refs[-1]
        x, y, c = _my_coords()
        me = 4 * x + 2 * y + c
        for k in range(1, _NDEV):
            peer, pid = _peer(k)
            for m in range(n):
                pltpu.make_async_remote_copy(
                    src_ref=_block_for(src_refs[m], pid),
                    dst_ref=land_ref.at[me, pl.ds(offsets[m], srcs[m].shape[-2]), :],
                    send_sem=send_sems, recv_sem=recv_sems, device_id=peer, device_id_type=_MESH).start()
        token[...] = jnp.zeros_like(token)

    sems = pltpu.SemaphoreType.DMA(())
    return pl.pallas_call(
        body, name=name,
        out_shape=(sems, sems, *[pltpu.HBM(s.shape, s.dtype) for s in srcs], pltpu.HBM(land.shape, land.dtype),
                   jax.ShapeDtypeStruct((8, 128), _F32)),
        in_specs=[_HBM_SPEC] * (n + 1),
        out_specs=(_SEM_SPEC, _SEM_SPEC, *[_HBM_SPEC] * (n + 1), pl.BlockSpec(memory_space=pltpu.VMEM)),
        input_output_aliases={i: 2 + i for i in range(n + 1)},
        compiler_params=pltpu.CompilerParams(has_side_effects=_EFFECT),
    )(*[_in_hbm(s) for s in srcs], _in_hbm(land))


def _exchange_wait(name, started, offsets, after):
    send_sems, recv_sems = started[0], started[1]
    srcs, land = list(started[2:-2]), started[-2]
    n = len(srcs)

    def body(*refs):
        src_refs, land_ref = refs[:n], refs[n]
        send_sems_, recv_sems_ = refs[n + 1], refs[n + 2]
        others = land_ref.at[pl.ds(0, _NDEV - 1)]
        cp = pltpu.make_async_remote_copy(src_ref=others, dst_ref=others, send_sem=send_sems_, recv_sem=recv_sems_,
                                          device_id=_peer(1)[0], device_id_type=_MESH)
        cp.wait_send()
        cp.wait_recv()

    res = pl.pallas_call(
        body, name=name,
        out_shape=(*[pltpu.HBM(s.shape, s.dtype) for s in srcs], pltpu.HBM(land.shape, land.dtype)),
        in_specs=[_HBM_SPEC] * (n + 1) + [_SEM_SPEC, _SEM_SPEC, pl.BlockSpec(memory_space=pl.ANY)],
        out_specs=tuple([_HBM_SPEC] * (n + 1)),
        input_output_aliases={i: i for i in range(n + 1)},
        compiler_params=pltpu.CompilerParams(has_side_effects=_EFFECT),
    )(*srcs, land, send_sems, recv_sems, after)
    return res[n]


def _load_weight(wg_ref, wbuf, sems, off, r, step):
    @pl.when(step == 0)
    def _():
        copies = [pltpu.make_async_copy(wg_ref.at[e, pl.ds(off, r), :], wbuf.at[pl.ds(e * r, r), :], sems.at[e])
                  for e in range(_NDEV)]
        for cp in copies:
            cp.start()
        for cp in copies:
            cp.wait()


def _mm_tn(name, a, b, out_dtype, silu_a=False):
    rows, na = a.shape
    nb = b.shape[1]
    tr = _divisor(rows, 3072, 16)
    tn = _divisor(na, 1536, 128)
    steps = rows // tr

    def body(a_ref, b_ref, o_ref, acc):
        t = pl.program_id(1)

        @pl.when(t == 0)
        def _():
            acc[...] = jnp.zeros_like(acc)

        av = a_ref[...]
        if silu_a:
            av = av.astype(_F32)
            av = av * jax.nn.sigmoid(av)
        acc[...] += lax.dot_general(av.astype(_MM), b_ref[...].astype(_MM), (((0,), (0,)), ((), ())),
                                    preferred_element_type=_F32)

        @pl.when(t == steps - 1)
        def _():
            o_ref[...] = acc[...].astype(out_dtype)

    return pl.pallas_call(
        body, name=name, grid=(na // tn, steps),
        out_shape=jax.ShapeDtypeStruct((na, nb), out_dtype),
        in_specs=[pl.BlockSpec((tr, tn), lambda j, t: (t, j)), pl.BlockSpec((tr, nb), lambda j, t: (t, 0))],
        out_specs=pl.BlockSpec((tn, nb), lambda j, t: (j, 0)),
        scratch_shapes=[pltpu.VMEM((tn, nb), _F32)],
        compiler_params=_params(("arbitrary", "arbitrary")),
    )(a, b)


def _mod_forward(name, cond, w_mod):
    depth, d, n = w_mod.shape
    rows = cond.shape[0]

    def body(c_ref, w_ref, o_ref):
        cv = c_ref[...]
        a = (cv * jax.nn.sigmoid(cv)).astype(_MM)
        o_ref[...] = jnp.dot(a, w_ref[...].astype(_MM), preferred_element_type=_F32)

    return pl.pallas_call(
        body, name=name, grid=(depth,),
        out_shape=jax.ShapeDtypeStruct((depth, rows, n), _F32),
        in_specs=[pl.BlockSpec((rows, d), lambda l: (0, 0)), pl.BlockSpec((None, d, n), lambda l: (l, 0, 0))],
        out_specs=pl.BlockSpec((None, rows, n), lambda l: (l, 0, 0)),
        compiler_params=_params(("arbitrary",)),
    )(cond, w_mod)


def _mod_backward_cond(name, dm, w_mod):
    depth, d, n = w_mod.shape
    rows = dm.shape[1]

    def body(g_ref, w_ref, o_ref):
        @pl.when(pl.program_id(0) == 0)
        def _():
            o_ref[...] = jnp.zeros_like(o_ref)

        o_ref[...] += lax.dot_general(g_ref[...].astype(_MM), w_ref[...].astype(_MM), (((1,), (1,)), ((), ())),
                                      preferred_element_type=_F32)

    return pl.pallas_call(
        body, name=name, grid=(depth,),
        out_shape=jax.ShapeDtypeStruct((rows, d), _F32),
        in_specs=[pl.BlockSpec((None, rows, n), lambda l: (l, 0, 0)), pl.BlockSpec((None, d, n), lambda l: (l, 0, 0))],
        out_specs=pl.BlockSpec((rows, d), lambda l: (0, 0)),
        compiler_params=_params(("arbitrary",)),
    )(dm, w_mod)


def _param_spec(arr, kind, n_lat):
    if kind == "stream":
        return pl.BlockSpec((None,) + arr.shape[1:], lambda i: (i // n_lat, 0, 0))
    return pl.BlockSpec(arr.shape, lambda i: (0,) * arr.ndim)


def _rowfn(name, fn, params, xs, outs, n_lat):
    rows = xs[0][0].shape[0]
    np_, nx = len(params), len(xs)
    stored = [(dt, ws) for dt, ws in outs if dt is not None]

    def body(*refs):
        ps = [r[...].astype(_F32) for r in refs[:np_]]
        xv = [r[...].astype(_F32) for r in refs[np_:np_ + nx]]
        pieces = fn(ps, xv)
        o_refs = iter(refs[np_ + nx:])
        k = 0
        for dt, ws in outs:
            o_ref = next(o_refs) if dt is not None else None
            off = 0
            for w in ws:
                if o_ref is not None:
                    o_ref[:, off:off + w] = pieces[k].astype(dt)
                off += w
                k += 1

    return pl.pallas_call(
        body, name=name, grid=(rows // _TM,),
        out_shape=[jax.ShapeDtypeStruct((rows, sum(ws)), dt) for dt, ws in stored],
        in_specs=[_param_spec(a, kind, n_lat) for a, kind in params]
        + [pl.BlockSpec((_TM, w), lambda i, cb=cb: (i, cb)) for _, w, cb in xs],
        out_specs=[pl.BlockSpec((_TM, sum(ws)), lambda i: (i, 0)) for _, ws in stored],
        compiler_params=_params(("arbitrary",)),
    )(*[a for a, _ in params], *[a for a, _, _ in xs])


def _rowfn_bwd(name, fn, params, xs, diff, douts, dx_outs, n_lat):
    rows = xs[0][0].shape[0]
    np_, nx, nd = len(params), len(xs), len(douts)
    nt = rows // _TM

    def body(*refs):
        i = pl.program_id(0)
        ps = [r[...].astype(_F32) for r in refs[:np_]]
        xv = [r[...].astype(_F32) for r in refs[np_:np_ + nx]]
        d_refs = refs[np_ + nx:np_ + nx + nd]
        dp_refs = refs[np_ + nx + nd:np_ + nx + nd + np_]
        dx_refs = refs[np_ + nx + nd + np_:]

        def f(ps_, xd):
            full = list(xv)
            for j, v in zip(diff, xd):
                full[j] = v
            return fn(ps_, full)

        _, vjp = jax.vjp(f, ps, [xv[j] for j in diff])
        cts = []
        for d_ref, (_, ws) in zip(d_refs, douts):
            off = 0
            for w in ws:
                cts.append(d_ref[:, off:off + w].astype(_F32))
                off += w
        dps, dxd = vjp(cts)
        grads = dict(zip(diff, dxd))
        _accumulate_params(dp_refs, [kind for _, kind in params], dps, i, n_lat)
        for dx_ref, (dt, idxs) in zip(dx_refs, dx_outs):
            off = 0
            for j in idxs:
                w = xs[j][1]
                dx_ref[:, off:off + w] = grads[j].astype(dt)
                off += w

    dp_shapes = [jax.ShapeDtypeStruct(a.shape, _F32) for a, _ in params]
    dx_shapes = [jax.ShapeDtypeStruct((rows, sum(xs[j][1] for j in idxs)), dt) for dt, idxs in dx_outs]
    in_specs = ([_param_spec(a, kind, n_lat) for a, kind in params]
                + [pl.BlockSpec((_TM, w), lambda i, cb=cb: (i, cb)) for _, w, cb in xs]
                + [pl.BlockSpec((_TM, sum(ws)), lambda i: (i, 0)) for _, ws in douts])
    operands = [a for a, _ in params] + [a for a, _, _ in xs] + [a for a, _ in douts]
    res = pl.pallas_call(
        body, name=name, grid=(nt,),
        out_shape=dp_shapes + dx_shapes,
        in_specs=in_specs,
        out_specs=[_param_spec(a, kind, n_lat) for a, kind in params]
        + [pl.BlockSpec((_TM, s.shape[1]), lambda i: (i, 0)) for s in dx_shapes],
        compiler_params=_params(("arbitrary",)),
    )(*operands)
    return list(res[:np_]), list(res[np_:])


def _f_norm_mod(ps, xs):
    gain, shift, scale = ps
    (x,) = xs
    y = x * lax.rsqrt(jnp.mean(x * x, axis=-1, keepdims=True) + _EPS) * gain
    return [y * (1.0 + scale) + shift]


def _f_premix(ps, xs):
    g_b, g_c, u_a, v_b, gate_b = xs
    return [g_b, g_c * u_a, v_b * jax.nn.sigmoid(gate_b)]


def _f_postmix(ps, xs):
    bias, ln_g, ln_b = ps
    g_b, cv_a, cv_b = xs
    u = cv_b + bias
    mu = jnp.mean(u, axis=-1, keepdims=True)
    var = jnp.mean(jnp.square(u - mu), axis=-1, keepdims=True)
    y = (u - mu) * lax.rsqrt(var + _EPS) * ln_g + ln_b
    return [g_b * cv_a, y * jax.nn.sigmoid(y)]


def _accumulate_params(dp_refs, kinds, dps, i, n_lat):
    for dp_ref, kind, dp in zip(dp_refs, kinds, dps):
        first = (i == 0) | (i == n_lat) if kind == "stream" else i == 0

        @pl.when(first)
        def _(dp_ref=dp_ref):
            dp_ref[...] = jnp.zeros_like(dp_ref)

        dp_ref[...] += dp


def _row_spec(width):
    return pl.BlockSpec((_TM, width), lambda i: (i, 0))


def _weight_scratch(shape, wg):
    return [pltpu.VMEM(shape, wg.dtype), pltpu.SemaphoreType.DMA((_NDEV,))]


def _norm_proj(name, x, params, wg, off, r, n_lat):
    rows, d = x.shape
    n = _NDEV * r
    chunk = _divisor(n, 512, 128)

    def body(g_ref, sh_ref, sc_ref, x_ref, wg_ref, h_ref, p_ref, wbuf, sems):
        _load_weight(wg_ref, wbuf, sems, off, r, pl.program_id(0))
        (h,) = _f_norm_mod([g_ref[...], sh_ref[...], sc_ref[...]], [x_ref[...]])
        hb = h.astype(_MM)
        h_ref[...] = hb.astype(h_ref.dtype)
        for j in range(n // chunk):
            p_ref[:, j * chunk:(j + 1) * chunk] = lax.dot_general(
                hb, wbuf[j * chunk:(j + 1) * chunk, :], (((1,), (1,)), ((), ())),
                preferred_element_type=_F32).astype(p_ref.dtype)

    return pl.pallas_call(
        body, name=name, grid=(rows // _TM,),
        out_shape=[jax.ShapeDtypeStruct((rows, d), _ACT), jax.ShapeDtypeStruct((rows, n), _ACT)],
        in_specs=[_param_spec(a, kind, n_lat) for a, kind in params] + [_row_spec(d), pl.BlockSpec(memory_space=pl.ANY)],
        out_specs=[_row_spec(d), _row_spec(n)],
        scratch_shapes=_weight_scratch((n, d), wg),
        compiler_params=_params(("arbitrary",)),
    )(*[a for a, _ in params], x, wg)


def _proj_residual(name, a, wg, off, r, x, gate, n_lat):
    rows, kdim = a.shape
    d = x.shape[1]
    assert kdim == _NDEV * r

    def body(g_ref, a_ref, x_ref, wg_ref, y_ref, o_ref, wbuf, sems):
        _load_weight(wg_ref, wbuf, sems, off, r, pl.program_id(0))
        y = jnp.dot(a_ref[...].astype(_MM), wbuf[...], preferred_element_type=_F32)
        y_ref[...] = y.astype(y_ref.dtype)
        o_ref[...] = x_ref[...] + g_ref[...] * y

    return pl.pallas_call(
        body, name=name, grid=(rows // _TM,),
        out_shape=[jax.ShapeDtypeStruct((rows, d), _ACT), jax.ShapeDtypeStruct((rows, d), _F32)],
        in_specs=[_param_spec(gate, "stream", n_lat), _row_spec(kdim), _row_spec(d), pl.BlockSpec(memory_space=pl.ANY)],
        out_specs=[_row_spec(d), _row_spec(d)],
        scratch_shapes=_weight_scratch((kdim, d), wg),
        compiler_params=_params(("arbitrary",)),
    )(gate, a, x, wg)


def _gate_proj_bwd(name, dx, y, gate, wg, off, r, n_lat):
    rows, d = dx.shape
    n = _NDEV * r
    chunk = _divisor(n, 512, 128)

    def body(g_ref, dx_ref, y_ref, wg_ref, dg_ref, dy_ref, dz_ref, wbuf, sems):
        i = pl.program_id(0)
        _load_weight(wg_ref, wbuf, sems, off, r, i)
        dxv = dx_ref[...]
        _accumulate_params([dg_ref], ["stream"], [jnp.sum(dxv * y_ref[...].astype(_F32), axis=0, keepdims=True)],
                           i, n_lat)
        dy = (g_ref[...] * dxv).astype(_MM)
        dy_ref[...] = dy.astype(dy_ref.dtype)
        for j in range(n // chunk):
            dz_ref[:, j * chunk:(j + 1) * chunk] = lax.dot_general(
                dy, wbuf[j * chunk:(j + 1) * chunk, :], (((1,), (1,)), ((), ())),
                preferred_element_type=_F32).astype(dz_ref.dtype)

    return pl.pallas_call(
        body, name=name, grid=(rows // _TM,),
        out_shape=[jax.ShapeDtypeStruct(gate.shape, _F32), jax.ShapeDtypeStruct((rows, d), _ACT),
                   jax.ShapeDtypeStruct((rows, n), _ACT)],
        in_specs=[_param_spec(gate, "stream", n_lat), _row_spec(d), _row_spec(d), pl.BlockSpec(memory_space=pl.ANY)],
        out_specs=[_param_spec(gate, "stream", n_lat), _row_spec(d), _row_spec(n)],
        scratch_shapes=_weight_scratch((n, d), wg),
        compiler_params=_params(("arbitrary",)),
    )(gate, dx, y, wg)


def _proj_norm_bwd(name, dp, wg, off, r, x, params, dx_in, n_lat):
    rows, kdim = dp.shape
    d = x.shape[1]
    assert kdim == _NDEV * r
    kinds = [kind for _, kind in params]

    def body(g_ref, sh_ref, sc_ref, dp_ref, x_ref, dxin_ref, wg_ref, dg_ref, dsh_ref, dsc_ref, dx_ref, wbuf, sems):
        i = pl.program_id(0)
        _load_weight(wg_ref, wbuf, sems, off, r, i)
        dh = jnp.dot(dp_ref[...].astype(_MM), wbuf[...], preferred_element_type=_F32)
        _, vjp = jax.vjp(lambda ps, xv: _f_norm_mod(ps, [xv]), [g_ref[...], sh_ref[...], sc_ref[...]], x_ref[...])
        dps, dxn = vjp([dh])
        _accumulate_params([dg_ref, dsh_ref, dsc_ref], kinds, dps, i, n_lat)
        dx_ref[...] = dxin_ref[...] + dxn

    specs = [_param_spec(a, kind, n_lat) for a, kind in params]
    res = pl.pallas_call(
        body, name=name, grid=(rows // _TM,),
        out_shape=[jax.ShapeDtypeStruct(a.shape, _F32) for a, _ in params] + [jax.ShapeDtypeStruct((rows, d), _F32)],
        in_specs=specs + [_row_spec(kdim), _row_spec(d), _row_spec(d), pl.BlockSpec(memory_space=pl.ANY)],
        out_specs=specs + [_row_spec(d)],
        scratch_shapes=_weight_scratch((kdim, d), wg),
        compiler_params=_params(("arbitrary",)),
    )(*[a for a, _ in params], dp, x, dx_in, wg)
    return list(res[:3]), res[3]


def _conv_halo_specs(width, cb0, n_rows):
    per = _TM // _HALO
    last = n_rows // _HALO - 1
    return [
        pl.BlockSpec((_TM, width), lambda i, j: (i, cb0 + j)),
        pl.BlockSpec((_HALO, width), lambda i, j: (jnp.maximum(i * per - 1, 0), cb0 + j)),
        pl.BlockSpec((_HALO, width), lambda i, j: (jnp.minimum((i + 1) * per, last), cb0 + j)),
    ]


def _conv_window(main_ref, prev_ref, next_ref, r0, cols, i, n_lat, nt):
    if r0 == 0:
        has_prev = (i != 0) & (i != n_lat)
        head = jnp.where(has_prev, prev_ref[:, cols].astype(_F32), 0.0)
    else:
        head = main_ref[r0 - _HALO:r0, cols].astype(_F32)
    if r0 + _CONV_ROWS == _TM:
        has_next = (i != n_lat - 1) & (i != nt - 1)
        tail = jnp.where(has_next, next_ref[:, cols].astype(_F32), 0.0)
    else:
        tail = main_ref[r0 + _CONV_ROWS:r0 + _CONV_ROWS + _HALO, cols].astype(_F32)
    return jnp.concatenate([head, main_ref[r0:r0 + _CONV_ROWS, cols].astype(_F32), tail], axis=0)


def _shifted(win, offset):
    n = win.shape[0]
    rolled = win if offset == 0 else pltpu.roll(win, (-offset) % n, 0)
    return rolled[_HALO:_HALO + _CONV_ROWS]


def _dwconv(name, x, cb0, channels, taps, out_dtype, n_lat):
    rows = x.shape[0]
    ktaps = taps.shape[0]
    half = ktaps // 2
    width = _divisor(channels, 1536, 128)
    assert (cb0 * channels) % width == 0
    cb0 = cb0 * channels // width
    nt = rows // _TM

    def body(main_ref, prev_ref, next_ref, taps_ref, o_ref):
        i = pl.program_id(0)

        def chunk(j, carry):
            cols = pl.ds(pl.multiple_of(j * _LANES, _LANES), _LANES)
            for r0 in range(0, _TM, _CONV_ROWS):
                win = _conv_window(main_ref, prev_ref, next_ref, r0, cols, i, n_lat, nt)
                acc = taps_ref[0:1, cols] * _shifted(win, -half)
                for k in range(1, ktaps):
                    acc = acc + taps_ref[k:k + 1, cols] * _shifted(win, k - half)
                o_ref[r0:r0 + _CONV_ROWS, cols] = acc.astype(out_dtype)
            return carry

        lax.fori_loop(0, width // _LANES, chunk, 0)

    return pl.pallas_call(
        body, name=name, grid=(nt, channels // width),
        out_shape=jax.ShapeDtypeStruct((rows, channels), out_dtype),
        in_specs=_conv_halo_specs(width, cb0, rows) + [pl.BlockSpec((ktaps, width), lambda i, j: (0, j))],
        out_specs=pl.BlockSpec((_TM, width), lambda i, j: (i, j)),
        compiler_params=_params(("arbitrary", "arbitrary")),
    )(x, x, x, taps)


def _dwconv_wgrad(name, dy, x, cb0, channels, ktaps, n_lat):
    rows = x.shape[0]
    half = ktaps // 2
    width = _divisor(channels, 1536, 128)
    cb0 = cb0 * channels // width
    nt = rows // _TM

    def body(dy_ref, main_ref, prev_ref, next_ref, o_ref):
        i = pl.program_id(1)

        @pl.when(i == 0)
        def _():
            o_ref[...] = jnp.zeros_like(o_ref)

        def chunk(j, carry):
            cols = pl.ds(pl.multiple_of(j * _LANES, _LANES), _LANES)
            for r0 in range(0, _TM, _CONV_ROWS):
                dyv = dy_ref[r0:r0 + _CONV_ROWS, cols].astype(_F32)
                win = _conv_window(main_ref, prev_ref, next_ref, r0, cols, i, n_lat, nt)
                for k in range(ktaps):
                    o_ref[k:k + 1, cols] += jnp.sum(dyv * _shifted(win, k - half), axis=0, keepdims=True)
            return carry

        lax.fori_loop(0, width // _LANES, chunk, 0)

    per = _TM // _HALO
    last = rows // _HALO - 1
    return pl.pallas_call(
        body, name=name, grid=(channels // width, nt),
        out_shape=jax.ShapeDtypeStruct((ktaps, channels), _F32),
        in_specs=[
            pl.BlockSpec((_TM, width), lambda j, i: (i, j)),
            pl.BlockSpec((_TM, width), lambda j, i: (i, cb0 + j)),
            pl.BlockSpec((_HALO, width), lambda j, i: (jnp.maximum(i * per - 1, 0), cb0 + j)),
            pl.BlockSpec((_HALO, width), lambda j, i: (jnp.minimum((i + 1) * per, last), cb0 + j)),
        ],
        out_specs=pl.BlockSpec((ktaps, width), lambda j, i: (0, j)),
        compiler_params=_params(("arbitrary", "arbitrary")),
    )(dy, x, x, x)


def _ffn_halo_specs(width, n_rows):
    per = _TM // _HALO
    last = n_rows // _HALO - 1
    return [pl.BlockSpec((_TM, width), lambda i: (i, 0)),
            pl.BlockSpec((_HALO, width), lambda i: (jnp.maximum(i * per - 1, 0), 0)),
            pl.BlockSpec((_HALO, width), lambda i: (jnp.minimum((i + 1) * per, last), 0))]


def _ffn_act(name, pu, taps, n_lat):
    rows, c2 = pu.shape
    ff = c2 // 2
    ktaps = taps.shape[0]
    half = ktaps // 2
    nt = rows // _TM

    def body(main_ref, prev_ref, next_ref, taps_ref, o_ref):
        i = pl.program_id(0)

        def conv(cols, r0):
            win = _conv_window(main_ref, prev_ref, next_ref, r0, cols, i, n_lat, nt)
            acc = taps_ref[0:1, cols] * _shifted(win, -half)
            for k in range(1, ktaps):
                acc = acc + taps_ref[k:k + 1, cols] * _shifted(win, k - half)
            return acc

        def chunk(j, carry):
            c0 = pl.multiple_of(j * _LANES, _LANES)
            cols_a, cols_g = pl.ds(c0, _LANES), pl.ds(pl.multiple_of(ff + c0, _LANES), _LANES)
            for r0 in range(0, _TM, _CONV_ROWS):
                ua, ug = conv(cols_a, r0), conv(cols_g, r0)
                o_ref[r0:r0 + _CONV_ROWS, cols_a] = (ug * jax.nn.sigmoid(ug) * ua).astype(o_ref.dtype)
            return carry

        lax.fori_loop(0, ff // _LANES, chunk, 0)

    return pl.pallas_call(
        body, name=name, grid=(nt,),
        out_shape=jax.ShapeDtypeStruct((rows, ff), _ACT),
        in_specs=_ffn_halo_specs(c2, rows) + [pl.BlockSpec((ktaps, c2), lambda i: (0, 0))],
        out_specs=pl.BlockSpec((_TM, ff), lambda i: (i, 0)),
        compiler_params=_params(("arbitrary",)),
    )(pu, pu, pu, taps)


def _ffn_act_bwd(name, pu, df, taps, n_lat):
    rows, c2 = pu.shape
    ff = c2 // 2
    ktaps = taps.shape[0]
    half = ktaps // 2
    nt = rows // _TM
    inner = slice(_HALO, _HALO + _CONV_ROWS)

    def body(main_ref, prev_ref, next_ref, dmain_ref, dprev_ref, dnext_ref, taps_ref, dpu_ref, dt_ref):
        i = pl.program_id(0)

        @pl.when(i == 0)
        def _():
            dt_ref[...] = jnp.zeros_like(dt_ref)

        def chunk(j, carry):
            c0 = pl.multiple_of(j * _LANES, _LANES)
            cols_a, cols_g = pl.ds(c0, _LANES), pl.ds(pl.multiple_of(ff + c0, _LANES), _LANES)
            for r0 in range(0, _TM, _CONV_ROWS):
                wins = [_conv_window(main_ref, prev_ref, next_ref, r0, cols, i, n_lat, nt) for cols in (cols_a, cols_g)]
                n = wins[0].shape[0]
                shifted = [[w if k == half else pltpu.roll(w, (half - k) % n, 0) for k in range(ktaps)] for w in wins]
                ua, ug = [sum(taps_ref[k:k + 1, cols] * sh[k] for k in range(ktaps))
                          for cols, sh in zip((cols_a, cols_g), shifted)]
                dfw = _conv_window(dmain_ref, dprev_ref, dnext_ref, r0, cols_a, i, n_lat, nt)
                sig = jax.nn.sigmoid(ug)
                d_a = dfw * (ug * sig)
                d_g = dfw * ua * (sig * (1.0 + ug * (1.0 - sig)))
                for du, cols, sh in ((d_a, cols_a, shifted[0]), (d_g, cols_g, shifted[1])):
                    acc = taps_ref[0:1, cols] * _shifted(du, half)
                    for k in range(1, ktaps):
                        acc = acc + taps_ref[k:k + 1, cols] * _shifted(du, half - k)
                    dpu_ref[r0:r0 + _CONV_ROWS, cols] = acc.astype(dpu_ref.dtype)
                    for k in range(ktaps):
                        dt_ref[k:k + 1, cols] += jnp.sum(du[inner] * sh[k][inner], axis=0, keepdims=True)
            return carry

        lax.fori_loop(0, ff // _LANES, chunk, 0)

    return pl.pallas_call(
        body, name=name, grid=(nt,),
        out_shape=[jax.ShapeDtypeStruct((rows, c2), _ACT), jax.ShapeDtypeStruct((ktaps, c2), _F32)],
        in_specs=_ffn_halo_specs(c2, rows) + _ffn_halo_specs(ff, rows) + [pl.BlockSpec((ktaps, c2), lambda i: (0, 0))],
        out_specs=[pl.BlockSpec((_TM, c2), lambda i: (i, 0)), pl.BlockSpec((ktaps, c2), lambda i: (0, 0))],
        compiler_params=_params(("arbitrary",)),
    )(pu, pu, pu, df, df, df, taps)


def _rope_tables(length, ctx_len):
    t = jnp.arange(length)
    row = (t // _GRID_W).astype(_F32)
    col = (t % _GRID_W).astype(_F32)
    n_freq = _HEAD_DIM // 4
    inv_freq = _ROPE_THETA ** (-jnp.arange(n_freq, dtype=_F32) / n_freq)
    ang = jnp.concatenate([row[:, None] * inv_freq, col[:, None] * inv_freq], axis=-1)
    cos, sin = jnp.cos(ang), jnp.sin(ang)
    cos = jnp.concatenate([cos, jnp.ones((ctx_len, _HEAD_DIM // 2), _F32)], axis=0)
    sin = jnp.concatenate([sin, jnp.zeros((ctx_len, _HEAD_DIM // 2), _F32)], axis=0)
    return jnp.tile(cos, (1, 4)), jnp.tile(jnp.concatenate([-sin, sin], axis=-1), (1, 2))


def _rotate(v, cos_ref, sin_ref):
    width = v.shape[1]
    reps = width // 128
    cos = jnp.tile(cos_ref[...], (1, reps))
    sin = jnp.tile(sin_ref[...], (1, reps))
    return v * cos, sin, width


def _partner(v):
    width = v.shape[1]
    half = _HEAD_DIM // 2
    lane = lax.broadcasted_iota(jnp.int32, v.shape, 1)
    return jnp.where(lane % _HEAD_DIM < half, pltpu.roll(v, width - half, 1), pltpu.roll(v, half, 1))


def _rope_fwd(name, p, cos, sin, q_w, kv_w):
    rows, width = p.shape
    scale = _HEAD_DIM ** -0.5

    def body(p_ref, cos_ref, sin_ref, q_ref, k_ref):
        v = p_ref[:, :q_w + kv_w].astype(_F32)
        vc, s, _ = _rotate(v, cos_ref, sin_ref)
        y = vc + _partner(v) * s
        q_ref[...] = (y[:, :q_w] * scale).astype(q_ref.dtype)
        k_ref[...] = y[:, q_w:].astype(k_ref.dtype)

    return pl.pallas_call(
        body, name=name, grid=(rows // _TM,),
        out_shape=[jax.ShapeDtypeStruct((rows, q_w), _ACT), jax.ShapeDtypeStruct((rows, kv_w), _ACT)],
        in_specs=[pl.BlockSpec((_TM, width), lambda i: (i, 0)), pl.BlockSpec((_TM, 128), lambda i: (i, 0)),
                  pl.BlockSpec((_TM, 128), lambda i: (i, 0))],
        out_specs=[pl.BlockSpec((_TM, q_w), lambda i: (i, 0)), pl.BlockSpec((_TM, kv_w), lambda i: (i, 0))],
        compiler_params=_params(("arbitrary",)),
    )(p, cos, sin)


def _rope_bwd(name, dq, dk, dv, cos, sin):
    rows, q_w = dq.shape
    kv_w = dk.shape[1]
    scale = _HEAD_DIM ** -0.5

    def body(dq_ref, dk_ref, dv_ref, cos_ref, sin_ref, o_ref):
        dy = jnp.concatenate([dq_ref[...].astype(_F32) * scale, dk_ref[...].astype(_F32)], axis=1)
        dyc, s, _ = _rotate(dy, cos_ref, sin_ref)
        o_ref[:, :q_w + kv_w] = (dyc + _partner(dy * s)).astype(o_ref.dtype)
        o_ref[:, q_w + kv_w:] = dv_ref[...].astype(o_ref.dtype)

    return pl.pallas_call(
        body, name=name, grid=(rows // _TM,),
        out_shape=jax.ShapeDtypeStruct((rows, q_w + 2 * kv_w), _ACT),
        in_specs=[pl.BlockSpec((_TM, q_w), lambda i: (i, 0)), pl.BlockSpec((_TM, kv_w), lambda i: (i, 0)),
                  pl.BlockSpec((_TM, kv_w), lambda i: (i, 0)), pl.BlockSpec((_TM, 128), lambda i: (i, 0)),
                  pl.BlockSpec((_TM, 128), lambda i: (i, 0))],
        out_specs=pl.BlockSpec((_TM, q_w + 2 * kv_w), lambda i: (i, 0)),
        compiler_params=_params(("arbitrary",)),
    )(dq, dk, dv, cos, sin)


def _attn_window(i, n_lat, length):
    wk = _TM + 2 * _WINDOW
    start = pl.multiple_of(jnp.clip(i * _TM - _WINDOW, 0, length - wk), _WINDOW)
    q_pos = i * _TM + lax.broadcasted_iota(jnp.int32, (_TM, wk), 0)
    k_pos = start + lax.broadcasted_iota(jnp.int32, (_TM, wk), 1)
    mask = (jnp.abs(q_pos - k_pos) <= _WINDOW) & (i < n_lat)
    return start, wk, mask


def _softmax_parts(q, k_loc, k_ctx, mask, sink):
    nt = (((1,), (1,)), ((), ()))
    s_loc = jnp.where(mask, lax.dot_general(q, k_loc, nt, preferred_element_type=_F32), _NEG_INF)
    s_ctx = lax.dot_general(q, k_ctx, nt, preferred_element_type=_F32)
    m = jnp.maximum(jnp.maximum(jnp.max(s_loc, axis=-1, keepdims=True), jnp.max(s_ctx, axis=-1, keepdims=True)),
                    sink)
    e_loc = jnp.exp(s_loc - m)
    e_ctx = jnp.exp(s_ctx - m)
    e_sink = jnp.exp(sink - m)
    inv = 1.0 / (jnp.sum(e_loc, axis=-1, keepdims=True) + jnp.sum(e_ctx, axis=-1, keepdims=True) + e_sink)
    return e_loc * inv, e_ctx * inv, e_sink * inv


def _attn_fwd(name, q, k, p, sinks, n_lat, length, kv_w):
    rows, q_w = q.shape
    ctx_len = rows - length
    n_heads = q_w // _HEAD_DIM
    n_kv = kv_w // _HEAD_DIM
    group = n_heads // n_kv
    v_cb = p.shape[1] // kv_w - 1
    hd = _HEAD_DIM

    def body(q_ref, k_ref, v_ref, sink_ref, o_ref):
        i = pl.program_id(0)
        start, wk, mask = _attn_window(i, n_lat, length)
        for h in range(n_kv):
            k_loc = k_ref[pl.ds(start, wk), h * hd:(h + 1) * hd]
            v_loc = v_ref[pl.ds(start, wk), h * hd:(h + 1) * hd]
            k_ctx = k_ref[length:length + ctx_len, h * hd:(h + 1) * hd]
            v_ctx = v_ref[length:length + ctx_len, h * hd:(h + 1) * hd]
            for g in range(group):
                n = h * group + g
                p_loc, p_ctx, _ = _softmax_parts(q_ref[:, n * hd:(n + 1) * hd], k_loc, k_ctx, mask,
                                                 sink_ref[:, n:n + 1])
                o = (jnp.dot(p_loc.astype(_MM), v_loc, preferred_element_type=_F32)
                     + jnp.dot(p_ctx.astype(_MM), v_ctx, preferred_element_type=_F32))
                o_ref[:, n * hd:(n + 1) * hd] = o.astype(o_ref.dtype)

    return pl.pallas_call(
        body, name=name, grid=(rows // _TM,),
        out_shape=jax.ShapeDtypeStruct((rows, q_w), _ACT),
        in_specs=[pl.BlockSpec((_TM, q_w), lambda i: (i, 0)), pl.BlockSpec((rows, kv_w), lambda i: (0, 0)),
                  pl.BlockSpec((rows, kv_w), lambda i: (0, v_cb)), pl.BlockSpec((1, n_heads), lambda i: (0, 0))],
        out_specs=pl.BlockSpec((_TM, q_w), lambda i: (i, 0)),
        compiler_params=_params(("arbitrary",)),
    )(q, k, p, sinks)


def _attn_bwd(name, q, k, p, sinks, do, n_lat, length, kv_w):
    rows, q_w = q.shape
    ctx_len = rows - length
    n_heads = q_w // _HEAD_DIM
    n_kv = kv_w // _HEAD_DIM
    group = n_heads // n_kv
    v_cb = p.shape[1] // kv_w - 1
    hd = _HEAD_DIM
    nt_dims = (((1,), (1,)), ((), ()))
    tn_dims = (((0,), (0,)), ((), ()))

    def body(q_ref, k_ref, v_ref, sink_ref, do_ref, dq_ref, dk_out, dv_out, ds_ref, dk_ref, dv_ref, out_sems):
        i = pl.program_id(0)

        @pl.when(i == 0)
        def _():
            dk_ref[...] = jnp.zeros_like(dk_ref)
            dv_ref[...] = jnp.zeros_like(dv_ref)
            ds_ref[...] = jnp.zeros_like(ds_ref)

        start, wk, mask = _attn_window(i, n_lat, length)
        head_lane = lax.broadcasted_iota(jnp.int32, (1, n_heads), 1)
        dsink = jnp.zeros((1, n_heads), _F32)
        for h in range(n_kv):
            cols = slice(h * hd, (h + 1) * hd)
            k_loc = k_ref[pl.ds(start, wk), cols]
            v_loc = v_ref[pl.ds(start, wk), cols]
            k_ctx = k_ref[length:length + ctx_len, cols]
            v_ctx = v_ref[length:length + ctx_len, cols]
            dk_loc = jnp.zeros((wk, hd), _F32)
            dv_loc = jnp.zeros((wk, hd), _F32)
            dk_ctx = jnp.zeros((ctx_len, hd), _F32)
            dv_ctx = jnp.zeros((ctx_len, hd), _F32)
            for g in range(group):
                n = h * group + g
                qh = q_ref[:, n * hd:(n + 1) * hd]
                doh = do_ref[:, n * hd:(n + 1) * hd].astype(_MM)
                p_loc, p_ctx, p_sink = _softmax_parts(qh, k_loc, k_ctx, mask, sink_ref[:, n:n + 1])
                dp_loc = lax.dot_general(doh, v_loc, nt_dims, preferred_element_type=_F32)
                dp_ctx = lax.dot_general(doh, v_ctx, nt_dims, preferred_element_type=_F32)
                dsum = (jnp.sum(p_loc * dp_loc, axis=-1, keepdims=True)
                        + jnp.sum(p_ctx * dp_ctx, axis=-1, keepdims=True))
                ds_loc = (p_loc * (dp_loc - dsum)).astype(_MM)
                ds_ctx = (p_ctx * (dp_ctx - dsum)).astype(_MM)
                dsink = dsink + jnp.where(head_lane == n, -jnp.sum(p_sink * dsum), 0.0)
                dq = (jnp.dot(ds_loc, k_loc, preferred_element_type=_F32)
                      + jnp.dot(ds_ctx, k_ctx, preferred_element_type=_F32))
                dq_ref[:, n * hd:(n + 1) * hd] = dq.astype(dq_ref.dtype)
                dk_loc += lax.dot_general(ds_loc, qh, tn_dims, preferred_element_type=_F32)
                dk_ctx += lax.dot_general(ds_ctx, qh, tn_dims, preferred_element_type=_F32)
                dv_loc += lax.dot_general(p_loc.astype(_MM), doh, tn_dims, preferred_element_type=_F32)
                dv_ctx += lax.dot_general(p_ctx.astype(_MM), doh, tn_dims, preferred_element_type=_F32)
            dk_ref[pl.ds(start, wk), cols] += dk_loc
            dv_ref[pl.ds(start, wk), cols] += dv_loc
            dk_ref[length:length + ctx_len, cols] += dk_ctx
            dv_ref[length:length + ctx_len, cols] += dv_ctx
        ds_ref[...] += dsink

        @pl.when(i == rows // _TM - 1)
        def _():
            copies = [pltpu.make_async_copy(dk_ref, dk_out, out_sems.at[0]),
                      pltpu.make_async_copy(dv_ref, dv_out, out_sems.at[1])]
            for cp in copies:
                cp.start()
            for cp in copies:
                cp.wait()

    return pl.pallas_call(
        body, name=name, grid=(rows // _TM,),
        out_shape=[jax.ShapeDtypeStruct((rows, q_w), _ACT), jax.ShapeDtypeStruct((rows, kv_w), _F32),
                   jax.ShapeDtypeStruct((rows, kv_w), _F32), jax.ShapeDtypeStruct((1, n_heads), _F32)],
        in_specs=[pl.BlockSpec((_TM, q_w), lambda i: (i, 0)), pl.BlockSpec((rows, kv_w), lambda i: (0, 0)),
                  pl.BlockSpec((rows, kv_w), lambda i: (0, v_cb)), pl.BlockSpec((1, n_heads), lambda i: (0, 0)),
                  pl.BlockSpec((_TM, q_w), lambda i: (i, 0))],
        out_specs=[pl.BlockSpec((_TM, q_w), lambda i: (i, 0)), pl.BlockSpec(memory_space=pl.ANY),
                   pl.BlockSpec(memory_space=pl.ANY), pl.BlockSpec((1, n_heads), lambda i: (0, 0))],
        scratch_shapes=[pltpu.VMEM((rows, kv_w), _F32), pltpu.VMEM((rows, kv_w), _F32),
                        pltpu.SemaphoreType.DMA((2,))],
        compiler_params=_params(("arbitrary",)),
    )(q, k, p, sinks, do)


def _loss_head(name, xs, gain, target, n_lat):
    rows, d = xs.shape

    def body(x_ref, g_ref, t_ref, loss_ref, dg_ref, dx_ref):
        i = pl.program_id(0)

        @pl.when(i == 0)
        def _():
            loss_ref[...] = jnp.zeros_like(loss_ref)
            dg_ref[...] = jnp.zeros_like(dg_ref)

        @pl.when(i < n_lat)
        def _():
            tv = t_ref[...]

            def f(gain_, x):
                y = x * lax.rsqrt(jnp.mean(x * x, axis=-1, keepdims=True) + _EPS) * gain_
                return 0.5 * jnp.sum(jnp.mean(jnp.square(y - tv), axis=-1))

            val, (dg, dx) = jax.value_and_grad(f, argnums=(0, 1))(g_ref[...], x_ref[...])
            loss_ref[...] += val
            dg_ref[...] += dg
            dx_ref[...] = dx

        @pl.when(i >= n_lat)
        def _():
            dx_ref[...] = jnp.zeros_like(dx_ref)

    return pl.pallas_call(
        body, name=name, grid=(rows // _TM,),
        out_shape=[jax.ShapeDtypeStruct((1, 128), _F32), jax.ShapeDtypeStruct((1, d), _F32),
                   jax.ShapeDtypeStruct((rows, d), _F32)],
        in_specs=[pl.BlockSpec((_TM, d), lambda i: (i, 0)), pl.BlockSpec((1, d), lambda i: (0, 0)),
                  pl.BlockSpec((_TM, d), lambda i: (jnp.minimum(i, n_lat - 1), 0))],
        out_specs=[pl.BlockSpec((1, 128), lambda i: (0, 0)), pl.BlockSpec((1, d), lambda i: (0, 0)),
                   pl.BlockSpec((_TM, d), lambda i: (i, 0))],
        compiler_params=_params(("arbitrary",)),
    )(xs, gain, target)


def _adamw(name, w, g, m, v):
    rows, cols = w.shape
    tr = _divisor(rows, 512, 8)
    b1, b2 = _ADAM["b1"], _ADAM["b2"]
    c1 = 1.0 - b1 ** _ADAM["step"]
    c2 = 1.0 - b2 ** _ADAM["step"]

    def body(w_ref, g_ref, m_ref, v_ref, d_ref, nm_ref, nv_ref):
        gv = g_ref[...]
        nm = b1 * m_ref[...] + (1.0 - b1) * gv
        nv = b2 * v_ref[...] + (1.0 - b2) * jnp.square(gv)
        d_ref[...] = -_ADAM["lr"] * ((nm / c1) / (jnp.sqrt(nv / c2) + _ADAM["eps"]) + _ADAM["wd"] * w_ref[...])
        nm_ref[...] = nm
        nv_ref[...] = nv

    spec = pl.BlockSpec((tr, cols), lambda i: (i, 0))
    return pl.pallas_call(
        body, name=name, grid=(rows // tr,),
        out_shape=[jax.ShapeDtypeStruct((rows, cols), _F32)] * 3,
        in_specs=[spec] * 4, out_specs=[spec] * 3,
        compiler_params=_params(("arbitrary",)),
    )(w, g, m, v)


def _pack(arrays, cols=128):
    flat = jnp.concatenate([a.reshape(-1).astype(_F32) for a in arrays])
    pad = (-flat.shape[0]) % (64 * cols)
    return jnp.pad(flat, (0, pad)).reshape(-1, cols)


def _unpack(flat, shapes):
    out, off = [], 0
    for s in shapes:
        n = 1
        for d in s:
            n *= d
        out.append(flat[..., off:off + n].reshape(flat.shape[:-1] + tuple(s)))
        off += n
    return out


def _gather_channels(parts):
    moved = jnp.moveaxis(parts, 0, -2)
    return moved.reshape(moved.shape[:-2] + (moved.shape[-2] * moved.shape[-1],))


def kernel(x, c, ctx, c_ctx, w_mod, b_mod, norm_mix, norm_ffn, w_in_ab, conv_a, conv_b, conv_b_bias, ln_b_gain, ln_b_bias, w_out_ab, w_qkv, w_o, sinks, w_up, w_conv_ffn, w_down, final_norm, loss_target, m_c_ctx, m_w_mod, m_b_mod, m_norm_mix, m_norm_ffn, m_w_in_ab, m_conv_a, m_conv_b, m_conv_b_bias, m_ln_b_gain, m_ln_b_bias, m_w_out_ab, m_w_qkv, m_w_o, m_sinks, m_w_up, m_w_conv_ffn, m_w_down, m_final_norm, v_c_ctx, v_w_mod, v_b_mod, v_norm_mix, v_norm_ffn, v_w_in_ab, v_conv_a, v_conv_b, v_conv_b_bias, v_ln_b_gain, v_ln_b_bias, v_w_out_ab, v_w_qkv, v_w_o, v_sinks, v_w_up, v_w_conv_ffn, v_w_down, v_final_norm):
    args = dict(locals())
    weight_names = ["c_ctx", "w_mod", "b_mod", "norm_mix", "norm_ffn", "w_in_ab", "conv_a", "conv_b", "conv_b_bias",
                    "ln_b_gain", "ln_b_bias", "w_out_ab", "w_qkv", "w_o", "sinks", "w_up", "w_conv_ffn", "w_down",
                    "final_norm"]
    length, d = x.shape[1], x.shape[2]
    ctx_len = ctx.shape[1]
    assert ctx_len == _TM and length % _TM == 0 and x.shape[0] == 1
    n_lat = length // _TM
    depth = w_mod.shape[0]
    n_even, n_odd = w_in_ab.shape[0], w_qkv.shape[0]
    a_w = conv_a.shape[2] * _NDEV
    b_w = conv_b.shape[2] * _NDEV
    assert a_w == b_w
    q_w = w_o.shape[1] * _NDEV
    kv_w = (w_qkv.shape[2] * _NDEV - q_w) // 2
    d_ff = w_down.shape[1] * _NDEV
    dev = 4 * lax.axis_index("x") + 2 * lax.axis_index("y") + lax.axis_index("c")

    small_shapes = [c.shape[1:], conv_a.shape, conv_b.shape, w_conv_ffn.shape]
    g0 = _all_gather_small("gather_small_params", _pack([c, conv_a, conv_b, w_conv_ffn]))
    c_parts, ca_parts, cb_parts, cf_parts = _unpack(g0.reshape(_NDEV, -1), small_shapes)
    conv_a_full = _gather_channels(ca_parts)
    conv_b_full = _gather_channels(cb_parts)
    conv_f_full = _gather_channels(cf_parts)

    cond = jnp.concatenate([c_parts, c_ctx[None], jnp.zeros((16 - _NDEV - 1, d), _F32)], axis=0)
    mod_cols = w_mod.shape[2]
    m_shard = _mod_forward("mod_forward", cond, w_mod)
    m_all = _all_gather_small("gather_mod", m_shard.reshape(depth * 16, mod_cols))
    m_all = jnp.moveaxis(m_all.reshape(_NDEV, depth, 16, mod_cols), 0, 2).reshape(depth, 16, _NDEV * mod_cols)
    m_all = m_all + b_mod[:, None, :]
    m_lat = lax.dynamic_index_in_dim(m_all, dev, axis=1, keepdims=False)
    m_ctx = m_all[:, _NDEV]

    def mod_vec(l, j):
        return jnp.stack([m_lat[l, j * d:(j + 1) * d], m_ctx[l, j * d:(j + 1) * d]])[:, None, :]

    def layer_mats(l):
        if l % 2 == 0:
            first = [("in", l // 2, w_in_ab[l // 2].T), ("out", l // 2, w_out_ab[l // 2])]
        else:
            first = [("qkv", l // 2, w_qkv[l // 2].T), ("o", l // 2, w_o[l // 2])]
        return first + [("up", l, w_up[l].T), ("down", l, w_down[l])]

    late = 0.0 * m_all[0, 0, 0]
    piece_mats = [layer_mats(0)[:2], layer_mats(0)[2:]] + [layer_mats(l) for l in range(1, depth)]
    n_pieces = len(piece_mats)
    slab_off, slab_r, piece_of, piece_keys, piece_rows, slabs = {}, {}, {}, [], [], []
    for p, mats in enumerate(piece_mats):
        off, keys = 0, []
        for fam, idx, mat in mats:
            slab_off[fam, idx], slab_r[fam], piece_of[fam, idx] = off, mat.shape[0], p
            off += mat.shape[0]
            keys.append((fam, idx))
        piece_keys.append(keys)
        piece_rows.append(off)
        slabs.append(jnp.concatenate([(mat + late).astype(_MM) for _, _, mat in mats], axis=0))
    wgs, gathers, start_token = [None] * n_pieces, [None] * n_pieces, jnp.zeros((), _F32)
    for p in range(n_pieces):
        land = _fill_own_slot(f"gather_fill_{p}", [slabs[p]], [0], piece_rows[p])
        gathers[p] = _exchange_start(f"gather_start_{p}", [slabs[p]], land, [0])
        start_token = start_token + gathers[p][-1][0, 0]

    def wref(fam, idx):
        return wgs[piece_of[fam, idx]], slab_off[fam, idx], slab_r[fam]

    cos, sin = _rope_tables(length, ctx_len)
    xs = jnp.concatenate([x[0], ctx[0]], axis=0)

    def full(a):
        return (a.reshape(1, -1), "full")

    wgs[0] = _exchange_wait("gather_wait_0", gathers[0], [0], m_all)
    saved = []
    for l in range(depth):
        sv = {"x_in": xs}
        if l > 0:
            wgs[1 + l] = _exchange_wait(f"gather_wait_{1 + l}", gathers[1 + l], [0], xs)
        gain1 = full(norm_mix[l] + start_token) if l == 0 else full(norm_mix[l])
        norm1 = [gain1, (mod_vec(l, 0), "stream"), (mod_vec(l, 1), "stream")]
        if l % 2 == 0:
            e = l // 2
            h1, p = _norm_proj(f"proj_in_{l}", xs, norm1, *wref("in", e), n_lat)
            (qm,) = _rowfn(f"premix_{l}", _f_premix, [], [(p, a_w, j) for j in range(5)],
                           [(None, [a_w]), (_ACT, [a_w, b_w])], n_lat)
            cv_a = _dwconv(f"conv_a_{l}", qm, 0, a_w, conv_a_full[e], _ACT, n_lat)
            cv_b = _dwconv(f"conv_b_{l}", qm, 1, b_w, conv_b_full[e], _ACT, n_lat)
            post_params = [full(conv_b_bias[e]), full(ln_b_gain[e]), full(ln_b_bias[e])]
            (z,) = _rowfn(f"postmix_{l}", _f_postmix, post_params, [(p, a_w, 0), (cv_a, a_w, 0), (cv_b, b_w, 0)],
                          [(_ACT, [a_w, b_w])], n_lat)
            y1, xs = _proj_residual(f"proj_out_{l}", z, *wref("out", e), xs, mod_vec(l, 2), n_lat)
            sv.update(p=p, qm=qm, cv_a=cv_a, cv_b=cv_b, z=z)
        else:
            o = l // 2
            h1, p = _norm_proj(f"proj_qkv_{l}", xs, norm1, *wref("qkv", o), n_lat)
            qr, kr = _rope_fwd(f"rope_{l}", p, cos, sin, q_w, kv_w)
            sk = sinks[o].reshape(1, -1)
            z = _attn_fwd(f"attn_{l}", qr, kr, p, sk, n_lat, length, kv_w)
            y1, xs = _proj_residual(f"proj_o_{l}", z, *wref("o", o), xs, mod_vec(l, 2), n_lat)
            sv.update(p=p, qr=qr, kr=kr, z=z)
        sv.update(h1=h1, y1=y1, x_mid=xs)
        if l == 0:
            wgs[1] = _exchange_wait("gather_wait_1", gathers[1], [0], xs)
        norm2 = [full(norm_ffn[l]), (mod_vec(l, 3), "stream"), (mod_vec(l, 4), "stream")]
        h2, pu = _norm_proj(f"proj_up_{l}", xs, norm2, *wref("up", l), n_lat)
        f = _ffn_act(f"ffn_act_{l}", pu, conv_f_full[l], n_lat)
        y2, xs = _proj_residual(f"proj_down_{l}", f, *wref("down", l), xs, mod_vec(l, 5), n_lat)
        sv.update(h2=h2, pu=pu, f=f, y2=y2)
        saved.append(sv)

    loss_part, d_final_norm, dxs = _loss_head("loss_head", xs, final_norm.reshape(1, -1), loss_target[0], n_lat)
    loss = lax.psum(loss_part[0, 0], ("x", "y", "c"))

    wgrads = {}
    d_mod = [[None] * 6 for _ in range(depth)]
    d_norm_mix, d_norm_ffn = [None] * depth, [None] * depth
    d_conv_a, d_conv_b = [None] * n_even, [None] * n_even
    d_bias, d_ln_g, d_ln_b = [None] * n_even, [None] * n_even, [None] * n_even
    d_sinks = [None] * n_odd
    d_conv_f = [None] * depth
    exchanges, recvs, exchange_token = [None] * n_pieces, [None] * n_pieces, jnp.zeros((), _F32)

    def piece_parts(p):
        return ([wgrads[key].reshape(_NDEV, slab_r[key[0]], d) for key in piece_keys[p]],
                [slab_off[key] for key in piece_keys[p]])

    def start_exchange(p):
        parts, offsets = piece_parts(p)
        land = _fill_own_slot(f"exchange_fill_{p}", parts, offsets, piece_rows[p])
        exchanges[p] = _exchange_start(f"exchange_start_{p}", parts, land, offsets)
        return exchanges[p][-1][0, 0]

    for l in reversed(range(depth)):
        sv = saved[l]
        d_mod[l][5], dy2, df = _gate_proj_bwd(f"bwd_down_{l}", dxs, sv["y2"], mod_vec(l, 5) + exchange_token,
                                              *wref("down", l), n_lat)
        wgrads["down", l] = _mm_tn(f"wgrad_down_{l}", sv["f"], dy2, _ACT)
        dpu, d_conv_f[l] = _ffn_act_bwd(f"ffn_act_bwd_{l}", sv["pu"], df, conv_f_full[l], n_lat)
        wgrads["up", l] = _mm_tn(f"wgrad_up_{l}", dpu, sv["h2"], _ACT)
        norm2 = [full(norm_ffn[l]), (mod_vec(l, 3), "stream"), (mod_vec(l, 4), "stream")]
        (dgain, dsh, dsc), dxs = _proj_norm_bwd(f"bwd_up_{l}", dpu, *wref("up", l), sv["x_mid"], norm2, dxs, n_lat)
        d_norm_ffn[l], d_mod[l][3], d_mod[l][4] = dgain, dsh, dsc
        mix_token = start_exchange(1) if l == 0 else jnp.zeros((), _F32)
        gate1 = mod_vec(l, 2) + mix_token
        norm1 = [full(norm_mix[l]), (mod_vec(l, 0), "stream"), (mod_vec(l, 1), "stream")]
        if l % 2 == 0:
            e = l // 2
            d_mod[l][2], dy1, dz = _gate_proj_bwd(f"bwd_out_{l}", dxs, sv["y1"], gate1, *wref("out", e), n_lat)
            wgrads["out", e] = _mm_tn(f"wgrad_out_{l}", sv["z"], dy1, _ACT)
            post_params = [full(conv_b_bias[e]), full(ln_b_gain[e]), full(ln_b_bias[e])]
            (dbias, dlg, dlb), (dgb, dcv_a, dcv_b) = _rowfn_bwd(
                f"postmix_bwd_{l}", _f_postmix, post_params,
                [(sv["p"], a_w, 0), (sv["cv_a"], a_w, 0), (sv["cv_b"], b_w, 0)], [0, 1, 2], [(dz, [a_w, b_w])],
                [(_ACT, [0]), (_ACT, [1]), (_ACT, [2])], n_lat)
            d_bias[e], d_ln_g[e], d_ln_b[e] = dbias, dlg, dlb
            d_conv_a[e] = _dwconv_wgrad(f"conv_a_wgrad_{l}", dcv_a, sv["qm"], 0, a_w, conv_a_full.shape[1], n_lat)
            d_conv_b[e] = _dwconv_wgrad(f"conv_b_wgrad_{l}", dcv_b, sv["qm"], 1, b_w, conv_b_full.shape[1], n_lat)
            dq_a = _dwconv(f"conv_a_bwd_{l}", dcv_a, 0, a_w, conv_a_full[e][::-1], _ACT, n_lat)
            dq_b = _dwconv(f"conv_b_bwd_{l}", dcv_b, 0, b_w, conv_b_full[e][::-1], _ACT, n_lat)
            _, (dp,) = _rowfn_bwd(f"premix_bwd_{l}", _f_premix, [], [(sv["p"], a_w, j) for j in range(5)],
                                  [0, 1, 2, 3, 4], [(dgb, [a_w]), (dq_a, [a_w]), (dq_b, [b_w])],
                                  [(_ACT, [0, 1, 2, 3, 4])], n_lat)
            wgrads["in", e] = _mm_tn(f"wgrad_in_{l}", dp, sv["h1"], _ACT)
            if l == 0:
                norm1 = [full(norm_mix[l] + start_exchange(0))] + norm1[1:]
            (dgain, dsh, dsc), dxs = _proj_norm_bwd(f"bwd_in_{l}", dp, *wref("in", e), sv["x_in"], norm1, dxs, n_lat)
        else:
            o = l // 2
            d_mod[l][2], dy1, dz = _gate_proj_bwd(f"bwd_o_{l}", dxs, sv["y1"], gate1, *wref("o", o), n_lat)
            wgrads["o", o] = _mm_tn(f"wgrad_o_{l}", sv["z"], dy1, _ACT)
            sk = sinks[o].reshape(1, -1)
            dqr, dkr, dv, dsk = _attn_bwd(f"attn_bwd_{l}", sv["qr"], sv["kr"], sv["p"], sk, dz, n_lat, length, kv_w)
            d_sinks[o] = dsk
            dp = _rope_bwd(f"rope_bwd_{l}", dqr, dkr, dv, cos, sin)
            wgrads["qkv", o] = _mm_tn(f"wgrad_qkv_{l}", dp, sv["h1"], _ACT)
            (dgain, dsh, dsc), dxs = _proj_norm_bwd(f"bwd_qkv_{l}", dp, *wref("qkv", o), sv["x_in"], norm1, dxs, n_lat)
        d_norm_mix[l], d_mod[l][0], d_mod[l][1] = dgain, dsh, dsc
        if l > 0:
            exchange_token = start_exchange(1 + l)
    grad_x = dxs[:length][None]

    gsums = []
    for p in range(n_pieces):
        recvs[p] = _exchange_wait(f"exchange_wait_{p}", exchanges[p], piece_parts(p)[1], dxs)
        gsums.append(_sum_slots(f"sum_weight_grads_{p}", recvs[p]))

    def slab_grad(fam, count, transposed):
        mats = [gsums[piece_of[fam, i]][slab_off[fam, i]:slab_off[fam, i] + slab_r[fam]] for i in range(count)]
        return jnp.stack([m_.T if transposed else m_ for m_ in mats])

    grads = {
        "w_in_ab": slab_grad("in", n_even, True), "w_qkv": slab_grad("qkv", n_odd, True),
        "w_up": slab_grad("up", depth, True), "w_out_ab": slab_grad("out", n_even, False),
        "w_o": slab_grad("o", n_odd, False), "w_down": slab_grad("down", depth, False),
    }

    weight_names_big = ["w_in_ab", "w_out_ab", "w_qkv", "w_o", "w_up", "w_down"]
    delta, new_m, new_v = {}, {}, {}

    def adam_big(n):
        w = args[n]
        flat = [a.reshape(-1, w.shape[-1]) for a in (w, grads[n], args["m_" + n], args["v_" + n])]
        dl, nm, nv = _adamw(f"adamw_{n}", *flat)
        delta[n], new_m[n], new_v[n] = dl.reshape(w.shape), nm.reshape(w.shape), nv.reshape(w.shape)

    for n in weight_names_big:
        adam_big(n)

    dm_dev = jnp.stack([jnp.concatenate([d_mod[l][j][:, 0, :] for j in range(6)], axis=-1)
                        for l in range(depth)])
    small_grads = [dm_dev, jnp.stack(d_norm_mix), jnp.stack(d_norm_ffn), jnp.stack(d_conv_a), jnp.stack(d_conv_b),
                   jnp.stack(d_bias), jnp.stack(d_ln_g), jnp.stack(d_ln_b), jnp.stack(d_sinks), jnp.stack(d_conv_f),
                   d_final_norm]
    sg_shapes = [a.shape for a in small_grads]
    sg_all = _all_gather_small("gather_small_grads", _pack(small_grads))
    sg_sum = _sum_slots("sum_small_grads", sg_all)
    (dm_sum, g_norm_mix, g_norm_ffn, g_conv_a, g_conv_b, g_bias, g_ln_g, g_ln_b, g_sinks, g_conv_f,
     g_final_norm) = _unpack(sg_sum.reshape(-1), sg_shapes)
    dm_each = _unpack(sg_all.reshape(_NDEV, -1), sg_shapes[:1])[0]

    def my_channels(a):
        width = a.shape[-1] // _NDEV
        return lax.dynamic_slice_in_dim(a, dev * width, width, axis=a.ndim - 1)

    grads["b_mod"] = dm_sum[:, 0] + dm_sum[:, 1]
    grads["norm_mix"] = g_norm_mix.reshape(depth, d)
    grads["norm_ffn"] = g_norm_ffn.reshape(depth, d)
    grads["conv_a"] = my_channels(g_conv_a)
    grads["conv_b"] = my_channels(g_conv_b)
    grads["conv_b_bias"] = g_bias.reshape(n_even, b_w)
    grads["ln_b_gain"] = g_ln_g.reshape(n_even, b_w)
    grads["ln_b_bias"] = g_ln_b.reshape(n_even, b_w)
    grads["sinks"] = g_sinks.reshape(n_odd, -1)
    grads["w_conv_ffn"] = my_channels(g_conv_f)
    grads["final_norm"] = g_final_norm.reshape(d)

    dm_rows = jnp.concatenate([jnp.moveaxis(dm_each[:, :, 0], 0, 1), dm_sum[:, 1:2],
                               jnp.zeros((depth, 16 - _NDEV - 1, 6 * d), _F32)], axis=1)
    dm_mine = my_channels(dm_rows)
    grads["w_mod"] = jnp.stack([_mm_tn(f"wgrad_mod_{l}", cond, dm_mine[l], _F32, silu_a=True) for l in range(depth)])
    dcond = _mod_backward_cond("mod_backward_cond", dm_mine, w_mod)
    dcond_all = _all_gather_small("gather_dcond", dcond)
    dcond_sum = _sum_slots("sum_dcond", dcond_all)[_NDEV]
    sg = jax.nn.sigmoid(c_ctx)
    grads["c_ctx"] = dcond_sum * (sg * (1.0 + c_ctx * (1.0 - sg)))

    adam_big("w_mod")
    small = [n for n in weight_names if n != "w_mod" and n not in weight_names_big]
    shapes = [args[n].shape for n in small]
    grads = {n: grads[n].reshape(args[n].shape) for n in grads}
    dl, nm, nv = _adamw("adamw_small", _pack([args[n] for n in small]), _pack([grads[n] for n in small]),
                        _pack([args["m_" + n] for n in small]), _pack([args["v_" + n] for n in small]))
    for res, packed in ((delta, dl), (new_m, nm), (new_v, nv)):
        for n, a in zip(small, _unpack(packed.reshape(-1), shapes)):
            res[n] = a

    return (loss, grad_x, *[grads[n] for n in weight_names], *[delta[n] for n in weight_names],
            *[new_m[n] for n in weight_names], *[new_v[n] for n in weight_names])
```

```python
import functools

import jax
import jax.numpy as jnp
from jax import lax
from jax.experimental import pallas as pl
from jax.experimental.pallas import tpu as pltpu

_F32 = jnp.float32
_MM = jnp.bfloat16
_ACT = jnp.bfloat16
_TM = 256
_HALO = 16
_LANES = 128
_CONV_ROWS = 128
_NDEV = 8
_HEAD_DIM = 64
_WINDOW = 128
_GRID_W = 64
_ROPE_THETA = 10000.0
_EPS = 1e-6
_NEG_INF = -1e30
_VMEM_LIMIT = 56 * 1024 * 1024
_ADAM = dict(lr=0.001, b1=0.9, b2=0.999, eps=1e-08, wd=0.01, step=10)
_MESH = pl.DeviceIdType.MESH


def _params(sem=None):
    return pltpu.CompilerParams(dimension_semantics=sem, vmem_limit_bytes=_VMEM_LIMIT)


def _divisor(n, cap, mult):
    if n <= cap:
        return n
    for d in range(cap - cap % mult, 0, -mult):
        if n % d == 0:
            return d
    raise ValueError(f"no tile for {n}")


def _my_coords():
    return lax.axis_index("x"), lax.axis_index("y"), lax.axis_index("c")


def _peer(k):
    x, y, c = _my_coords()
    px = 1 - x if k & 4 else x
    py = 1 - y if k & 2 else y
    pc = 1 - c if k & 1 else c
    return (px, py, pc), 4 * px + 2 * py + pc


def _all_gather_small(name, v):
    rows, cols = v.shape

    def body(v_ref, out_ref, send_sems, recv_sems):
        x, y, c = _my_coords()
        me = 4 * x + 2 * y + c
        out_ref[me] = v_ref[...]
        sends = []
        for k in range(1, _NDEV):
            peer, _ = _peer(k)
            cp = pltpu.make_async_remote_copy(
                src_ref=v_ref, dst_ref=out_ref.at[me], send_sem=send_sems.at[k - 1], recv_sem=recv_sems.at[k - 1],
                device_id=peer, device_id_type=_MESH)
            cp.start()
            sends.append(cp)
        for k in range(1, _NDEV):
            peer, pid = _peer(k)
            pltpu.make_async_remote_copy(
                src_ref=v_ref, dst_ref=out_ref.at[pid], send_sem=send_sems.at[k - 1], recv_sem=recv_sems.at[k - 1],
                device_id=peer, device_id_type=_MESH).wait_recv()
        for cp in sends:
            cp.wait_send()

    return pl.pallas_call(
        body, name=name,
        out_shape=jax.ShapeDtypeStruct((_NDEV, rows, cols), v.dtype),
        in_specs=[pl.BlockSpec(memory_space=pltpu.VMEM)],
        out_specs=pl.BlockSpec(memory_space=pltpu.VMEM),
        scratch_shapes=[pltpu.SemaphoreType.DMA((_NDEV - 1,)), pltpu.SemaphoreType.DMA((_NDEV - 1,))],
        compiler_params=pltpu.CompilerParams(vmem_limit_bytes=_VMEM_LIMIT),
    )(v)


def _sum_slots(name, v):
    _, rows, cols = v.shape
    tr = _divisor(rows, 1024, 16)

    def body(v_ref, o_ref):
        acc = v_ref[0].astype(_F32)
        for e in range(1, _NDEV):
            acc = acc + v_ref[e].astype(_F32)
        o_ref[...] = acc

    return pl.pallas_call(
        body, name=name, grid=(rows // tr,),
        out_shape=jax.ShapeDtypeStruct((rows, cols), _F32),
        in_specs=[pl.BlockSpec((_NDEV, tr, cols), lambda i: (0, i, 0))],
        out_specs=pl.BlockSpec((tr, cols), lambda i: (i, 0)),
        compiler_params=_params(("arbitrary",)),
    )(v)


_HBM_SPEC = pl.BlockSpec(memory_space=pltpu.HBM)
_SEM_SPEC = pl.BlockSpec(memory_space=pltpu.SEMAPHORE)
_EFFECT = pltpu.SideEffectType.DATAFLOW_SIDE_EFFECTING


def _in_hbm(a):
    return pltpu.with_memory_space_constraint(a, pltpu.HBM)


def _block_for(ref, device):
    return ref if len(ref.shape) == 2 else ref.at[device]


def _fill_own_slot(name, srcs, offsets, total_rows):
    n = len(srcs)
    cols = srcs[0].shape[-1]

    def body(*refs):
        src_refs, out_ref, bufs, sems = refs[:n], refs[n], refs[n + 1:2 * n + 1], refs[2 * n + 1]
        x, y, c = _my_coords()
        me = 4 * x + 2 * y + c
        loads = [pltpu.make_async_copy(_block_for(src_refs[m], me), bufs[m], sems.at[0, m]) for m in range(n)]
        stores = [pltpu.make_async_copy(bufs[m], out_ref.at[me, pl.ds(offsets[m], srcs[m].shape[-2]), :],
                                        sems.at[1, m]) for m in range(n)]
        for copies in (loads, stores):
            for cp in copies:
                cp.start()
            for cp in copies:
                cp.wait()

    return pl.pallas_call(
        body, name=name,
        out_shape=jax.ShapeDtypeStruct((_NDEV, total_rows, cols), srcs[0].dtype),
        in_specs=[pl.BlockSpec(memory_space=pl.ANY)] * n,
        out_specs=pl.BlockSpec(memory_space=pl.ANY),
        scratch_shapes=[pltpu.VMEM(s.shape[-2:], s.dtype) for s in srcs] + [pltpu.SemaphoreType.DMA((2, n))],
        compiler_params=pltpu.CompilerParams(vmem_limit_bytes=_VMEM_LIMIT),
    )(*srcs)


def _exchange_start(name, srcs, land, offsets):
    n = len(srcs)

    def body(*refs):
        src_refs, land_ref = refs[:n], refs[n]
        send_sems, recv_sems, token = refs[n + 1], refs[n + 2], refs[-1]
        x, y, c = _my_coords()
        me = 4 * x + 2 * y + c
        for k in range(1, _NDEV):
            peer, pid = _peer(k)
            for m in range(n):
                pltpu.make_async_remote_copy(
                    src_ref=_block_for(src_refs[m], pid),
                    dst_ref=land_ref.at[me, pl.ds(offsets[m], srcs[m].shape[-2]), :],
                    send_sem=send_sems, recv_sem=recv_sems, device_id=peer, device_id_type=_MESH).start()
        token[...] = jnp.zeros_like(token)

    sems = pltpu.SemaphoreType.DMA(())
    return pl.pallas_call(
        body, name=name,
        out_shape=(sems, sems, *[pltpu.HBM(s.shape, s.dtype) for s in srcs], pltpu.HBM(land.shape, land.dtype),
                   jax.ShapeDtypeStruct((8, 128), _F32)),
        in_specs=[_HBM_SPEC] * (n + 1),
        out_specs=(_SEM_SPEC, _SEM_SPEC, *[_HBM_SPEC] * (n + 1), pl.BlockSpec(memory_space=pltpu.VMEM)),
        input_output_aliases={i: 2 + i for i in range(n + 1)},
        compiler_params=pltpu.CompilerParams(has_side_effects=_EFFECT),
    )(*[_in_hbm(s) for s in srcs], _in_hbm(land))


def _exchange_wait(name, started, offsets, after):
    send_sems, recv_sems = started[0], started[1]
    srcs, land = list(started[2:-2]), started[-2]
    n = len(srcs)

    def body(*refs):
        src_refs, land_ref = refs[:n], refs[n]
        send_sems_, recv_sems_ = refs[n + 1], refs[n + 2]
        others = land_ref.at[pl.ds(0, _NDEV - 1)]
        cp = pltpu.make_async_remote_copy(src_ref=others, dst_ref=others, send_sem=send_sems_, recv_sem=recv_sems_,
                                          device_id=_peer(1)[0], device_id_type=_MESH)
        cp.wait_send()
        cp.wait_recv()

    res = pl.pallas_call(
        body, name=name,
        out_shape=(*[pltpu.HBM(s.shape, s.dtype) for s in srcs], pltpu.HBM(land.shape, land.dtype)),
        in_specs=[_HBM_SPEC] * (n + 1) + [_SEM_SPEC, _SEM_SPEC, pl.BlockSpec(memory_space=pl.ANY)],
        out_specs=tuple([_HBM_SPEC] * (n + 1)),
        input_output_aliases={i: i for i in range(n + 1)},
        compiler_params=pltpu.CompilerParams(has_side_effects=_EFFECT),
    )(*srcs, land, send_sems, recv_sems, after)
    return res[n]


def _load_weight(wg_ref, wbuf, sems, off, r, step):
    @pl.when(step == 0)
    def _():
        copies = [pltpu.make_async_copy(wg_ref.at[e, pl.ds(off, r), :], wbuf.at[pl.ds(e * r, r), :], sems.at[e])
                  for e in range(_NDEV)]
        for cp in copies:
            cp.start()
        for cp in copies:
            cp.wait()


def _mm_tn(name, a, b, out_dtype, silu_a=False):
    rows, na = a.shape
    nb = b.shape[1]
    tr = _divisor(rows, 3072, 16)
    tn = _divisor(na, 1536, 128)
    steps = rows // tr

    def body(a_ref, b_ref, o_ref, acc):
        t = pl.program_id(1)

        @pl.when(t == 0)
        def _():
            acc[...] = jnp.zeros_like(acc)

        av = a_ref[...]
        if silu_a:
            av = av.astype(_F32)
            av = av * jax.nn.sigmoid(av)
        acc[...] += lax.dot_general(av.astype(_MM), b_ref[...].astype(_MM), (((0,), (0,)), ((), ())),
                                    preferred_element_type=_F32)

        @pl.when(t == steps - 1)
        def _():
            o_ref[...] = acc[...].astype(out_dtype)

    return pl.pallas_call(
        body, name=name, grid=(na // tn, steps),
        out_shape=jax.ShapeDtypeStruct((na, nb), out_dtype),
        in_specs=[pl.BlockSpec((tr, tn), lambda j, t: (t, j)), pl.BlockSpec((tr, nb), lambda j, t: (t, 0))],
        out_specs=pl.BlockSpec((tn, nb), lambda j, t: (j, 0)),
        scratch_shapes=[pltpu.VMEM((tn, nb), _F32)],
        compiler_params=_params(("arbitrary", "arbitrary")),
    )(a, b)


def _mod_forward(name, cond, w_mod):
    depth, d, n = w_mod.shape
    rows = cond.shape[0]

    def body(c_ref, w_ref, o_ref):
        cv = c_ref[...]
        a = (cv * jax.nn.sigmoid(cv)).astype(_MM)
        o_ref[...] = jnp.dot(a, w_ref[...].astype(_MM), preferred_element_type=_F32)

    return pl.pallas_call(
        body, name=name, grid=(depth,),
        out_shape=jax.ShapeDtypeStruct((depth, rows, n), _F32),
        in_specs=[pl.BlockSpec((rows, d), lambda l: (0, 0)), pl.BlockSpec((None, d, n), lambda l: (l, 0, 0))],
        out_specs=pl.BlockSpec((None, rows, n), lambda l: (l, 0, 0)),
        compiler_params=_params(("arbitrary",)),
    )(cond, w_mod)


def _mod_backward_cond(name, dm, w_mod):
    depth, d, n = w_mod.shape
    rows = dm.shape[1]

    def body(g_ref, w_ref, o_ref):
        @pl.when(pl.program_id(0) == 0)
        def _():
            o_ref[...] = jnp.zeros_like(o_ref)

        o_ref[...] += lax.dot_general(g_ref[...].astype(_MM), w_ref[...].astype(_MM), (((1,), (1,)), ((), ())),
                                      preferred_element_type=_F32)

    return pl.pallas_call(
        body, name=name, grid=(depth,),
        out_shape=jax.ShapeDtypeStruct((rows, d), _F32),
        in_specs=[pl.BlockSpec((None, rows, n), lambda l: (l, 0, 0)), pl.BlockSpec((None, d, n), lambda l: (l, 0, 0))],
        out_specs=pl.BlockSpec((rows, d), lambda l: (0, 0)),
        compiler_params=_params(("arbitrary",)),
    )(dm, w_mod)


def _param_spec(arr, kind, n_lat):
    if kind == "stream":
        return pl.BlockSpec((None,) + arr.shape[1:], lambda i: (i // n_lat, 0, 0))
    return pl.BlockSpec(arr.shape, lambda i: (0,) * arr.ndim)


def _rowfn(name, fn, params, xs, outs, n_lat):
    rows = xs[0][0].shape[0]
    np_, nx = len(params), len(xs)
    stored = [(dt, ws) for dt, ws in outs if dt is not None]

    def body(*refs):
        ps = [r[...].astype(_F32) for r in refs[:np_]]
        xv = [r[...].astype(_F32) for r in refs[np_:np_ + nx]]
        pieces = fn(ps, xv)
        o_refs = iter(refs[np_ + nx:])
        k = 0
        for dt, ws in outs:
            o_ref = next(o_refs) if dt is not None else None
            off = 0
            for w in ws:
                if o_ref is not None:
                    o_ref[:, off:off + w] = pieces[k].astype(dt)
                off += w
                k += 1

    return pl.pallas_call(
        body, name=name, grid=(rows // _TM,),
        out_shape=[jax.ShapeDtypeStruct((rows, sum(ws)), dt) for dt, ws in stored],
        in_specs=[_param_spec(a, kind, n_lat) for a, kind in params]
        + [pl.BlockSpec((_TM, w), lambda i, cb=cb: (i, cb)) for _, w, cb in xs],
        out_specs=[pl.BlockSpec((_TM, sum(ws)), lambda i: (i, 0)) for _, ws in stored],
        compiler_params=_params(("arbitrary",)),
    )(*[a for a, _ in params], *[a for a, _, _ in xs])


def _rowfn_bwd(name, fn, params, xs, diff, douts, dx_outs, n_lat):
    rows = xs[0][0].shape[0]
    np_, nx, nd = len(params), len(xs), len(douts)
    nt = rows // _TM

    def body(*refs):
        i = pl.program_id(0)
        ps = [r[...].astype(_F32) for r in refs[:np_]]
        xv = [r[...].astype(_F32) for r in refs[np_:np_ + nx]]
        d_refs = refs[np_ + nx:np_ + nx + nd]
        dp_refs = refs[np_ + nx + nd:np_ + nx + nd + np_]
        dx_refs = refs[np_ + nx + nd + np_:]

        def f(ps_, xd):
            full = list(xv)
            for j, v in zip(diff, xd):
                full[j] = v
            return fn(ps_, full)

        _, vjp = jax.vjp(f, ps, [xv[j] for j in diff])
        cts = []
        for d_ref, (_, ws) in zip(d_refs, douts):
            off = 0
            for w in ws:
                cts.append(d_ref[:, off:off + w].astype(_F32))
                off += w
        dps, dxd = vjp(cts)
        grads = dict(zip(diff, dxd))
        _accumulate_params(dp_refs, [kind for _, kind in params], dps, i, n_lat)
        for dx_ref, (dt, idxs) in zip(dx_refs, dx_outs):
            off = 0
            for j in idxs:
                w = xs[j][1]
                dx_ref[:, off:off + w] = grads[j].astype(dt)
                off += w

    dp_shapes = [jax.ShapeDtypeStruct(a.shape, _F32) for a, _ in params]
    dx_shapes = [jax.ShapeDtypeStruct((rows, sum(xs[j][1] for j in idxs)), dt) for dt, idxs in dx_outs]
    in_specs = ([_param_spec(a, kind, n_lat) for a, kind in params]
                + [pl.BlockSpec((_TM, w), lambda i, cb=cb: (i, cb)) for _, w, cb in xs]
                + [pl.BlockSpec((_TM, sum(ws)), lambda i: (i, 0)) for _, ws in douts])
    operands = [a for a, _ in params] + [a for a, _, _ in xs] + [a for a, _ in douts]
    res = pl.pallas_call(
        body, name=name, grid=(nt,),
        out_shape=dp_shapes + dx_shapes,
        in_specs=in_specs,
        out_specs=[_param_spec(a, kind, n_lat) for a, kind in params]
        + [pl.BlockSpec((_TM, s.shape[1]), lambda i: (i, 0)) for s in dx_shapes],
        compiler_params=_params(("arbitrary",)),
    )(*operands)
    return list(res[:np_]), list(res[np_:])


def _f_norm_mod(ps, xs):
    gain, shift, scale = ps
    (x,) = xs
    y = x * lax.rsqrt(jnp.mean(x * x, axis=-1, keepdims=True) + _EPS) * gain
    return [y * (1.0 + scale) + shift]


def _f_premix(ps, xs):
    g_b, g_c, u_a, v_b, gate_b = xs
    return [g_b, g_c * u_a, v_b * jax.nn.sigmoid(gate_b)]


def _f_postmix(ps, xs):
    bias, ln_g, ln_b = ps
    g_b, cv_a, cv_b = xs
    u = cv_b + bias
    mu = jnp.mean(u, axis=-1, keepdims=True)
    var = jnp.mean(jnp.square(u - mu), axis=-1, keepdims=True)
    y = (u - mu) * lax.rsqrt(var + _EPS) * ln_g + ln_b
    return [g_b * cv_a, y * jax.nn.sigmoid(y)]


def _accumulate_params(dp_refs, kinds, dps, i, n_lat):
    for dp_ref, kind, dp in zip(dp_refs, kinds, dps):
        first = (i == 0) | (i == n_lat) if kind == "stream" else i == 0

        @pl.when(first)
        def _(dp_ref=dp_ref):
            dp_ref[...] = jnp.zeros_like(dp_ref)

        dp_ref[...] += dp


def _row_spec(width):
    return pl.BlockSpec((_TM, width), lambda i: (i, 0))


def _weight_scratch(shape, wg):
    return [pltpu.VMEM(shape, wg.dtype), pltpu.SemaphoreType.DMA((_NDEV,))]


def _norm_proj(name, x, params, wg, off, r, n_lat):
    rows, d = x.shape
    n = _NDEV * r
    chunk = _divisor(n, 512, 128)

    def body(g_ref, sh_ref, sc_ref, x_ref, wg_ref, h_ref, p_ref, wbuf, sems):
        _load_weight(wg_ref, wbuf, sems, off, r, pl.program_id(0))
        (h,) = _f_norm_mod([g_ref[...], sh_ref[...], sc_ref[...]], [x_ref[...]])
        hb = h.astype(_MM)
        h_ref[...] = hb.astype(h_ref.dtype)
        for j in range(n // chunk):
            p_ref[:, j * chunk:(j + 1) * chunk] = lax.dot_general(
                hb, wbuf[j * chunk:(j + 1) * chunk, :], (((1,), (1,)), ((), ())),
                preferred_element_type=_F32).astype(p_ref.dtype)

    return pl.pallas_call(
        body, name=name, grid=(rows // _TM,),
        out_shape=[jax.ShapeDtypeStruct((rows, d), _ACT), jax.ShapeDtypeStruct((rows, n), _ACT)],
        in_specs=[_param_spec(a, kind, n_lat) for a, kind in params] + [_row_spec(d), pl.BlockSpec(memory_space=pl.ANY)],
        out_specs=[_row_spec(d), _row_spec(n)],
        scratch_shapes=_weight_scratch((n, d), wg),
        compiler_params=_params(("arbitrary",)),
    )(*[a for a, _ in params], x, wg)


def _proj_residual(name, a, wg, off, r, x, gate, n_lat):
    rows, kdim = a.shape
    d = x.shape[1]
    assert kdim == _NDEV * r

    def body(g_ref, a_ref, x_ref, wg_ref, y_ref, o_ref, wbuf, sems):
        _load_weight(wg_ref, wbuf, sems, off, r, pl.program_id(0))
        y = jnp.dot(a_ref[...].astype(_MM), wbuf[...], preferred_element_type=_F32)
        y_ref[...] = y.astype(y_ref.dtype)
        o_ref[...] = x_ref[...] + g_ref[...] * y

    return pl.pallas_call(
        body, name=name, grid=(rows // _TM,),
        out_shape=[jax.ShapeDtypeStruct((rows, d), _ACT), jax.ShapeDtypeStruct((rows, d), _F32)],
        in_specs=[_param_spec(gate, "stream", n_lat), _row_spec(kdim), _row_spec(d), pl.BlockSpec(memory_space=pl.ANY)],
        out_specs=[_row_spec(d), _row_spec(d)],
        scratch_shapes=_weight_scratch((kdim, d), wg),
        compiler_params=_params(("arbitrary",)),
    )(gate, a, x, wg)


def _gate_proj_bwd(name, dx, y, gate, wg, off, r, n_lat):
    rows, d = dx.shape
    n = _NDEV * r
    chunk = _divisor(n, 512, 128)

    def body(g_ref, dx_ref, y_ref, wg_ref, dg_ref, dy_ref, dz_ref, wbuf, sems):
        i = pl.program_id(0)
        _load_weight(wg_ref, wbuf, sems, off, r, i)
        dxv = dx_ref[...]
        _accumulate_params([dg_ref], ["stream"], [jnp.sum(dxv * y_ref[...].astype(_F32), axis=0, keepdims=True)],
                           i, n_lat)
        dy = (g_ref[...] * dxv).astype(_MM)
        dy_ref[...] = dy.astype(dy_ref.dtype)
        for j in range(n // chunk):
            dz_ref[:, j * chunk:(j + 1) * chunk] = lax.dot_general(
                dy, wbuf[j * chunk:(j + 1) * chunk, :], (((1,), (1,)), ((), ())),
                preferred_element_type=_F32).astype(dz_ref.dtype)

    return pl.pallas_call(
        body, name=name, grid=(rows // _TM,),
        out_shape=[jax.ShapeDtypeStruct(gate.shape, _F32), jax.ShapeDtypeStruct((rows, d), _ACT),
                   jax.ShapeDtypeStruct((rows, n), _ACT)],
        in_specs=[_param_spec(gate, "stream", n_lat), _row_spec(d), _row_spec(d), pl.BlockSpec(memory_space=pl.ANY)],
        out_specs=[_param_spec(gate, "stream", n_lat), _row_spec(d), _row_spec(n)],
        scratch_shapes=_weight_scratch((n, d), wg),
        compiler_params=_params(("arbitrary",)),
    )(gate, dx, y, wg)


def _proj_norm_bwd(name, dp, wg, off, r, x, params, dx_in, n_lat):
    rows, kdim = dp.shape
    d = x.shape[1]
    assert kdim == _NDEV * r
    kinds = [kind for _, kind in params]

    def body(g_ref, sh_ref, sc_ref, dp_ref, x_ref, dxin_ref, wg_ref, dg_ref, dsh_ref, dsc_ref, dx_ref, wbuf, sems):
        i = pl.program_id(0)
        _load_weight(wg_ref, wbuf, sems, off, r, i)
        dh = jnp.dot(dp_ref[...].astype(_MM), wbuf[...], preferred_element_type=_F32)
        _, vjp = jax.vjp(lambda ps, xv: _f_norm_mod(ps, [xv]), [g_ref[...], sh_ref[...], sc_ref[...]], x_ref[...])
        dps, dxn = vjp([dh])
        _accumulate_params([dg_ref, dsh_ref, dsc_ref], kinds, dps, i, n_lat)
        dx_ref[...] = dxin_ref[...] + dxn

    specs = [_param_spec(a, kind, n_lat) for a, kind in params]
    res = pl.pallas_call(
        body, name=name, grid=(rows // _TM,),
        out_shape=[jax.ShapeDtypeStruct(a.shape, _F32) for a, _ in params] + [jax.ShapeDtypeStruct((rows, d), _F32)],
        in_specs=specs + [_row_spec(kdim), _row_spec(d), _row_spec(d), pl.BlockSpec(memory_space=pl.ANY)],
        out_specs=specs + [_row_spec(d)],
        scratch_shapes=_weight_scratch((kdim, d), wg),
        compiler_params=_params(("arbitrary",)),
    )(*[a for a, _ in params], dp, x, dx_in, wg)
    return list(res[:3]), res[3]


def _conv_halo_specs(width, cb0, n_rows):
    per = _TM // _HALO
    last = n_rows // _HALO - 1
    return [
        pl.BlockSpec((_TM, width), lambda i, j: (i, cb0 + j)),
        pl.BlockSpec((_HALO, width), lambda i, j: (jnp.maximum(i * per - 1, 0), cb0 + j)),
        pl.BlockSpec((_HALO, width), lambda i, j: (jnp.minimum((i + 1) * per, last), cb0 + j)),
    ]


def _conv_window(main_ref, prev_ref, next_ref, r0, cols, i, n_lat, nt):
    if r0 == 0:
        has_prev = (i != 0) & (i != n_lat)
        head = jnp.where(has_prev, prev_ref[:, cols].astype(_F32), 0.0)
    else:
        head = main_ref[r0 - _HALO:r0, cols].astype(_F32)
    if r0 + _CONV_ROWS == _TM:
        has_next = (i != n_lat - 1) & (i != nt - 1)
        tail = jnp.where(has_next, next_ref[:, cols].astype(_F32), 0.0)
    else:
        tail = main_ref[r0 + _CONV_ROWS:r0 + _CONV_ROWS + _HALO, cols].astype(_F32)
    return jnp.concatenate([head, main_ref[r0:r0 + _CONV_ROWS, cols].astype(_F32), tail], axis=0)


def _shifted(win, offset):
    n = win.shape[0]
    rolled = win if offset == 0 else pltpu.roll(win, (-offset) % n, 0)
    return rolled[_HALO:_HALO + _CONV_ROWS]


def _dwconv(name, x, cb0, channels, taps, out_dtype, n_lat):
    rows = x.shape[0]
    ktaps = taps.shape[0]
    half = ktaps // 2
    width = _divisor(channels, 1536, 128)
    assert (cb0 * channels) % width == 0
    cb0 = cb0 * channels // width
    nt = rows // _TM

    def body(main_ref, prev_ref, next_ref, taps_ref, o_ref):
        i = pl.program_id(0)

        def chunk(j, carry):
            cols = pl.ds(pl.multiple_of(j * _LANES, _LANES), _LANES)
            for r0 in range(0, _TM, _CONV_ROWS):
                win = _conv_window(main_ref, prev_ref, next_ref, r0, cols, i, n_lat, nt)
                acc = taps_ref[0:1, cols] * _shifted(win, -half)
                for k in range(1, ktaps):
                    acc = acc + taps_ref[k:k + 1, cols] * _shifted(win, k - half)
                o_ref[r0:r0 + _CONV_ROWS, cols] = acc.astype(out_dtype)
            return carry

        lax.fori_loop(0, width // _LANES, chunk, 0)

    return pl.pallas_call(
        body, name=name, grid=(nt, channels // width),
        out_shape=jax.ShapeDtypeStruct((rows, channels), out_dtype),
        in_specs=_conv_halo_specs(width, cb0, rows) + [pl.BlockSpec((ktaps, width), lambda i, j: (0, j))],
        out_specs=pl.BlockSpec((_TM, width), lambda i, j: (i, j)),
        compiler_params=_params(("arbitrary", "arbitrary")),
    )(x, x, x, taps)


def _dwconv_wgrad(name, dy, x, cb0, channels, ktaps, n_lat):
    rows = x.shape[0]
    half = ktaps // 2
    width = _divisor(channels, 1536, 128)
    cb0 = cb0 * channels // width
    nt = rows // _TM

    def body(dy_ref, main_ref, prev_ref, next_ref, o_ref):
        i = pl.program_id(1)

        @pl.when(i == 0)
        def _():
            o_ref[...] = jnp.zeros_like(o_ref)

        def chunk(j, carry):
            cols = pl.ds(pl.multiple_of(j * _LANES, _LANES), _LANES)
            for r0 in range(0, _TM, _CONV_ROWS):
                dyv = dy_ref[r0:r0 + _CONV_ROWS, cols].astype(_F32)
                win = _conv_window(main_ref, prev_ref, next_ref, r0, cols, i, n_lat, nt)
                for k in range(ktaps):
                    o_ref[k:k + 1, cols] += jnp.sum(dyv * _shifted(win, k - half), axis=0, keepdims=True)
            return carry

        lax.fori_loop(0, width // _LANES, chunk, 0)

    per = _TM // _HALO
    last = rows // _HALO - 1
    return pl.pallas_call(
        body, name=name, grid=(channels // width, nt),
        out_shape=jax.ShapeDtypeStruct((ktaps, channels), _F32),
        in_specs=[
            pl.BlockSpec((_TM, width), lambda j, i: (i, j)),
            pl.BlockSpec((_TM, width), lambda j, i: (i, cb0 + j)),
            pl.BlockSpec((_HALO, width), lambda j, i: (jnp.maximum(i * per - 1, 0), cb0 + j)),
            pl.BlockSpec((_HALO, width), lambda j, i: (jnp.minimum((i + 1) * per, last), cb0 + j)),
        ],
        out_specs=pl.BlockSpec((ktaps, width), lambda j, i: (0, j)),
        compiler_params=_params(("arbitrary", "arbitrary")),
    )(dy, x, x, x)


def _ffn_halo_specs(width, n_rows):
    per = _TM // _HALO
    last = n_rows // _HALO - 1
    return [pl.BlockSpec((_TM, width), lambda i: (i, 0)),
            pl.BlockSpec((_HALO, width), lambda i: (jnp.maximum(i * per - 1, 0), 0)),
            pl.BlockSpec((_HALO, width), lambda i: (jnp.minimum((i + 1) * per, last), 0))]


def _ffn_act(name, pu, taps, n_lat):
    rows, c2 = pu.shape
    ff = c2 // 2
    ktaps = taps.shape[0]
    half = ktaps // 2
    nt = rows // _TM

    def body(main_ref, prev_ref, next_ref, taps_ref, o_ref):
        i = pl.program_id(0)

        def conv(cols, r0):
            win = _conv_window(main_ref, prev_ref, next_ref, r0, cols, i, n_lat, nt)
            acc = taps_ref[0:1, cols] * _shifted(win, -half)
            for k in range(1, ktaps):
                acc = acc + taps_ref[k:k + 1, cols] * _shifted(win, k - half)
            return acc

        def chunk(j, carry):
            c0 = pl.multiple_of(j * _LANES, _LANES)
            cols_a, cols_g = pl.ds(c0, _LANES), pl.ds(pl.multiple_of(ff + c0, _LANES), _LANES)
            for r0 in range(0, _TM, _CONV_ROWS):
                ua, ug = conv(cols_a, r0), conv(cols_g, r0)
                o_ref[r0:r0 + _CONV_ROWS, cols_a] = (ug * jax.nn.sigmoid(ug) * ua).astype(o_ref.dtype)
            return carry

        lax.fori_loop(0, ff // _LANES, chunk, 0)

    return pl.pallas_call(
        body, name=name, grid=(nt,),
        out_shape=jax.ShapeDtypeStruct((rows, ff), _ACT),
        in_specs=_ffn_halo_specs(c2, rows) + [pl.BlockSpec((ktaps, c2), lambda i: (0, 0))],
        out_specs=pl.BlockSpec((_TM, ff), lambda i: (i, 0)),
        compiler_params=_params(("arbitrary",)),
    )(pu, pu, pu, taps)


def _ffn_act_bwd(name, pu, df, taps, n_lat):
    rows, c2 = pu.shape
    ff = c2 // 2
    ktaps = taps.shape[0]
    half = ktaps // 2
    nt = rows // _TM
    inner = slice(_HALO, _HALO + _CONV_ROWS)

    def body(main_ref, prev_ref, next_ref, dmain_ref, dprev_ref, dnext_ref, taps_ref, dpu_ref, dt_ref):
        i = pl.program_id(0)

        @pl.when(i == 0)
        def _():
            dt_ref[...] = jnp.zeros_like(dt_ref)

        def chunk(j, carry):
            c0 = pl.multiple_of(j * _LANES, _LANES)
            cols_a, cols_g = pl.ds(c0, _LANES), pl.ds(pl.multiple_of(ff + c0, _LANES), _LANES)
            for r0 in range(0, _TM, _CONV_ROWS):
                wins = [_conv_window(main_ref, prev_ref, next_ref, r0, cols, i, n_lat, nt) for cols in (cols_a, cols_g)]
                n = wins[0].shape[0]
                shifted = [[w if k == half else pltpu.roll(w, (half - k) % n, 0) for k in range(ktaps)] for w in wins]
                ua, ug = [sum(taps_ref[k:k + 1, cols] * sh[k] for k in range(ktaps))
                          for cols, sh in zip((cols_a, cols_g), shifted)]
                dfw = _conv_window(dmain_ref, dprev_ref, dnext_ref, r0, cols_a, i, n_lat, nt)
                sig = jax.nn.sigmoid(ug)
                d_a = dfw * (ug * sig)
                d_g = dfw * ua * (sig * (1.0 + ug * (1.0 - sig)))
                for du, cols, sh in ((d_a, cols_a, shifted[0]), (d_g, cols_g, shifted[1])):
                    acc = taps_ref[0:1, cols] * _shifted(du, half)
                    for k in range(1, ktaps):
                        acc = acc + taps_ref[k:k + 1, cols] * _shifted(du, half - k)
                    dpu_ref[r0:r0 + _CONV_ROWS, cols] = acc.astype(dpu_ref.dtype)
                    for k in range(ktaps):
                        dt_ref[k:k + 1, cols] += jnp.sum(du[inner] * sh[k][inner], axis=0, keepdims=True)
            return carry

        lax.fori_loop(0, ff // _LANES, chunk, 0)

    return pl.pallas_call(
        body, name=name, grid=(nt,),
        out_shape=[jax.ShapeDtypeStruct((rows, c2), _ACT), jax.ShapeDtypeStruct((ktaps, c2), _F32)],
        in_specs=_ffn_halo_specs(c2, rows) + _ffn_halo_specs(ff, rows) + [pl.BlockSpec((ktaps, c2), lambda i: (0, 0))],
        out_specs=[pl.BlockSpec((_TM, c2), lambda i: (i, 0)), pl.BlockSpec((ktaps, c2), lambda i: (0, 0))],
        compiler_params=_params(("arbitrary",)),
    )(pu, pu, pu, df, df, df, taps)


def _rope_tables(length, ctx_len):
    t = jnp.arange(length)
    row = (t // _GRID_W).astype(_F32)
    col = (t % _GRID_W).astype(_F32)
    n_freq = _HEAD_DIM // 4
    inv_freq = _ROPE_THETA ** (-jnp.arange(n_freq, dtype=_F32) / n_freq)
    ang = jnp.concatenate([row[:, None] * inv_freq, col[:, None] * inv_freq], axis=-1)
    cos, sin = jnp.cos(ang), jnp.sin(ang)
    cos = jnp.concatenate([cos, jnp.ones((ctx_len, _HEAD_DIM // 2), _F32)], axis=0)
    sin = jnp.concatenate([sin, jnp.zeros((ctx_len, _HEAD_DIM // 2), _F32)], axis=0)
    return jnp.tile(cos, (1, 4)), jnp.tile(jnp.concatenate([-sin, sin], axis=-1), (1, 2))


def _rotate(v, cos_ref, sin_ref):
    width = v.shape[1]
    reps = width // 128
    cos = jnp.tile(cos_ref[...], (1, reps))
    sin = jnp.tile(sin_ref[...], (1, reps))
    return v * cos, sin, width


def _partner(v):
    width = v.shape[1]
    half = _HEAD_DIM // 2
    lane = lax.broadcasted_iota(jnp.int32, v.shape, 1)
    return jnp.where(lane % _HEAD_DIM < half, pltpu.roll(v, width - half, 1), pltpu.roll(v, half, 1))


def _rope_fwd(name, p, cos, sin, q_w, kv_w):
    rows, width = p.shape
    scale = _HEAD_DIM ** -0.5

    def body(p_ref, cos_ref, sin_ref, q_ref, k_ref):
        v = p_ref[:, :q_w + kv_w].astype(_F32)
        vc, s, _ = _rotate(v, cos_ref, sin_ref)
        y = vc + _partner(v) * s
        q_ref[...] = (y[:, :q_w] * scale).astype(q_ref.dtype)
        k_ref[...] = y[:, q_w:].astype(k_ref.dtype)

    return pl.pallas_call(
        body, name=name, grid=(rows // _TM,),
        out_shape=[jax.ShapeDtypeStruct((rows, q_w), _ACT), jax.ShapeDtypeStruct((rows, kv_w), _ACT)],
        in_specs=[pl.BlockSpec((_TM, width), lambda i: (i, 0)), pl.BlockSpec((_TM, 128), lambda i: (i, 0)),
                  pl.BlockSpec((_TM, 128), lambda i: (i, 0))],
        out_specs=[pl.BlockSpec((_TM, q_w), lambda i: (i, 0)), pl.BlockSpec((_TM, kv_w), lambda i: (i, 0))],
        compiler_params=_params(("arbitrary",)),
    )(p, cos, sin)


def _rope_bwd(name, dq, dk, dv, cos, sin):
    rows, q_w = dq.shape
    kv_w = dk.shape[1]
    scale = _HEAD_DIM ** -0.5

    def body(dq_ref, dk_ref, dv_ref, cos_ref, sin_ref, o_ref):
        dy = jnp.concatenate([dq_ref[...].astype(_F32) * scale, dk_ref[...].astype(_F32)], axis=1)
        dyc, s, _ = _rotate(dy, cos_ref, sin_ref)
        o_ref[:, :q_w + kv_w] = (dyc + _partner(dy * s)).astype(o_ref.dtype)
        o_ref[:, q_w + kv_w:] = dv_ref[...].astype(o_ref.dtype)

    return pl.pallas_call(
        body, name=name, grid=(rows // _TM,),
        out_shape=jax.ShapeDtypeStruct((rows, q_w + 2 * kv_w), _ACT),
        in_specs=[pl.BlockSpec((_TM, q_w), lambda i: (i, 0)), pl.BlockSpec((_TM, kv_w), lambda i: (i, 0)),
                  pl.BlockSpec((_TM, kv_w), lambda i: (i, 0)), pl.BlockSpec((_TM, 128), lambda i: (i, 0)),
                  pl.BlockSpec((_TM, 128), lambda i: (i, 0))],
        out_specs=pl.BlockSpec((_TM, q_w + 2 * kv_w), lambda i: (i, 0)),
        compiler_params=_params(("arbitrary",)),
    )(dq, dk, dv, cos, sin)


def _attn_window(i, n_lat, length):
    wk = _TM + 2 * _WINDOW
    start = pl.multiple_of(jnp.clip(i * _TM - _WINDOW, 0, length - wk), _WINDOW)
    q_pos = i * _TM + lax.broadcasted_iota(jnp.int32, (_TM, wk), 0)
    k_pos = start + lax.broadcasted_iota(jnp.int32, (_TM, wk), 1)
    mask = (jnp.abs(q_pos - k_pos) <= _WINDOW) & (i < n_lat)
    return start, wk, mask


def _softmax_parts(q, k_loc, k_ctx, mask, sink):
    nt = (((1,), (1,)), ((), ()))
    s_loc = jnp.where(mask, lax.dot_general(q, k_loc, nt, preferred_element_type=_F32), _NEG_INF)
    s_ctx = lax.dot_general(q, k_ctx, nt, preferred_element_type=_F32)
    m = jnp.maximum(jnp.maximum(jnp.max(s_loc, axis=-1, keepdims=True), jnp.max(s_ctx, axis=-1, keepdims=True)),
                    sink)
    e_loc = jnp.exp(s_loc - m)
    e_ctx = jnp.exp(s_ctx - m)
    e_sink = jnp.exp(sink - m)
    inv = 1.0 / (jnp.sum(e_loc, axis=-1, keepdims=True) + jnp.sum(e_ctx, axis=-1, keepdims=True) + e_sink)
    return e_loc * inv, e_ctx * inv, e_sink * inv


def _attn_fwd(name, q, k, p, sinks, n_lat, length, kv_w):
    rows, q_w = q.shape
    ctx_len = rows - length
    n_heads = q_w // _HEAD_DIM
    n_kv = kv_w // _HEAD_DIM
    group = n_heads // n_kv
    v_cb = p.shape[1] // kv_w - 1
    hd = _HEAD_DIM

    def body(q_ref, k_ref, v_ref, sink_ref, o_ref):
        i = pl.program_id(0)
        start, wk, mask = _attn_window(i, n_lat, length)
        for h in range(n_kv):
            k_loc = k_ref[pl.ds(start, wk), h * hd:(h + 1) * hd]
            v_loc = v_ref[pl.ds(start, wk), h * hd:(h + 1) * hd]
            k_ctx = k_ref[length:length + ctx_len, h * hd:(h + 1) * hd]
            v_ctx = v_ref[length:length + ctx_len, h * hd:(h + 1) * hd]
            for g in range(group):
                n = h * group + g
                p_loc, p_ctx, _ = _softmax_parts(q_ref[:, n * hd:(n + 1) * hd], k_loc, k_ctx, mask,
                                                 sink_ref[:, n:n + 1])
                o = (jnp.dot(p_loc.astype(_MM), v_loc, preferred_element_type=_F32)
                     + jnp.dot(p_ctx.astype(_MM), v_ctx, preferred_element_type=_F32))
                o_ref[:, n * hd:(n + 1) * hd] = o.astype(o_ref.dtype)

    return pl.pallas_call(
        body, name=name, grid=(rows // _TM,),
        out_shape=jax.ShapeDtypeStruct((rows, q_w), _ACT),
        in_specs=[pl.BlockSpec((_TM, q_w), lambda i: (i, 0)), pl.BlockSpec((rows, kv_w), lambda i: (0, 0)),
                  pl.BlockSpec((rows, kv_w), lambda i: (0, v_cb)), pl.BlockSpec((1, n_heads), lambda i: (0, 0))],
        out_specs=pl.BlockSpec((_TM, q_w), lambda i: (i, 0)),
        compiler_params=_params(("arbitrary",)),
    )(q, k, p, sinks)


def _attn_bwd(name, q, k, p, sinks, do, n_lat, length, kv_w):
    rows, q_w = q.shape
    ctx_len = rows - length
    n_heads = q_w // _HEAD_DIM
    n_kv = kv_w // _HEAD_DIM
    group = n_heads // n_kv
    v_cb = p.shape[1] // kv_w - 1
    hd = _HEAD_DIM
    nt_dims = (((1,), (1,)), ((), ()))
    tn_dims = (((0,), (0,)), ((), ()))

    def body(q_ref, k_ref, v_ref, sink_ref, do_ref, dq_ref, dk_out, dv_out, ds_ref, dk_ref, dv_ref, out_sems):
        i = pl.program_id(0)

        @pl.when(i == 0)
        def _():
            dk_ref[...] = jnp.zeros_like(dk_ref)
            dv_ref[...] = jnp.zeros_like(dv_ref)
            ds_ref[...] = jnp.zeros_like(ds_ref)

        start, wk, mask = _attn_window(i, n_lat, length)
        head_lane = lax.broadcasted_iota(jnp.int32, (1, n_heads), 1)
        dsink = jnp.zeros((1, n_heads), _F32)
        for h in range(n_kv):
            cols = slice(h * hd, (h + 1) * hd)
            k_loc = k_ref[pl.ds(start, wk), cols]
            v_loc = v_ref[pl.ds(start, wk), cols]
            k_ctx = k_ref[length:length + ctx_len, cols]
            v_ctx = v_ref[length:length + ctx_len, cols]
            ds_rows, p_rows, q_rows, do_rows = [], [], [], []
            for g in range(group):
                n = h * group + g
                qh = q_ref[:, n * hd:(n + 1) * hd]
                doh = do_ref[:, n * hd:(n + 1) * hd].astype(_MM)
                p_loc, p_ctx, p_sink = _softmax_parts(qh, k_loc, k_ctx, mask, sink_ref[:, n:n + 1])
                dp_loc = lax.dot_general(doh, v_loc, nt_dims, preferred_element_type=_F32)
                dp_ctx = lax.dot_general(doh, v_ctx, nt_dims, preferred_element_type=_F32)
                dsum = (jnp.sum(p_loc * dp_loc, axis=-1, keepdims=True)
                        + jnp.sum(p_ctx * dp_ctx, axis=-1, keepdims=True))
                ds_loc = (p_loc * (dp_loc - dsum)).astype(_MM)
                ds_ctx = (p_ctx * (dp_ctx - dsum)).astype(_MM)
                dsink = dsink + jnp.where(head_lane == n, -jnp.sum(p_sink * dsum), 0.0)
                dq = (jnp.dot(ds_loc, k_loc, preferred_element_type=_F32)
                      + jnp.dot(ds_ctx, k_ctx, preferred_element_type=_F32))
                dq_ref[:, n * hd:(n + 1) * hd] = dq.astype(dq_ref.dtype)
                ds_rows.append(jnp.concatenate([ds_loc, ds_ctx], axis=1))
                p_rows.append(jnp.concatenate([p_loc.astype(_MM), p_ctx.astype(_MM)], axis=1))
                q_rows.append(qh)
                do_rows.append(doh)
            dk_all = lax.dot_general(jnp.concatenate(ds_rows, axis=0), jnp.concatenate(q_rows, axis=0), tn_dims,
                                     preferred_element_type=_F32)
            dv_all = lax.dot_general(jnp.concatenate(p_rows, axis=0), jnp.concatenate(do_rows, axis=0), tn_dims,
                                     preferred_element_type=_F32)
            dk_ref[pl.ds(start, wk), cols] += dk_all[:wk]
            dv_ref[pl.ds(start, wk), cols] += dv_all[:wk]
            dk_ref[length:length + ctx_len, cols] += dk_all[wk:]
            dv_ref[length:length + ctx_len, cols] += dv_all[wk:]
        ds_ref[...] += dsink

        @pl.when(i == rows // _TM - 1)
        def _():
            copies = [pltpu.make_async_copy(dk_ref, dk_out, out_sems.at[0]),
                      pltpu.make_async_copy(dv_ref, dv_out, out_sems.at[1])]
            for cp in copies:
                cp.start()
            for cp in copies:
                cp.wait()

    return pl.pallas_call(
        body, name=name, grid=(rows // _TM,),
        out_shape=[jax.ShapeDtypeStruct((rows, q_w), _ACT), jax.ShapeDtypeStruct((rows, kv_w), _F32),
                   jax.ShapeDtypeStruct((rows, kv_w), _F32), jax.ShapeDtypeStruct((1, n_heads), _F32)],
        in_specs=[pl.BlockSpec((_TM, q_w), lambda i: (i, 0)), pl.BlockSpec((rows, kv_w), lambda i: (0, 0)),
                  pl.BlockSpec((rows, kv_w), lambda i: (0, v_cb)), pl.BlockSpec((1, n_heads), lambda i: (0, 0)),
                  pl.BlockSpec((_TM, q_w), lambda i: (i, 0))],
        out_specs=[pl.BlockSpec((_TM, q_w), lambda i: (i, 0)), pl.BlockSpec(memory_space=pl.ANY),
                   pl.BlockSpec(memory_space=pl.ANY), pl.BlockSpec((1, n_heads), lambda i: (0, 0))],
        scratch_shapes=[pltpu.VMEM((rows, kv_w), _F32), pltpu.VMEM((rows, kv_w), _F32),
                        pltpu.SemaphoreType.DMA((2,))],
        compiler_params=_params(("arbitrary",)),
    )(q, k, p, sinks, do)


def _loss_head(name, xs, gain, target, n_lat):
    rows, d = xs.shape

    def body(x_ref, g_ref, t_ref, loss_ref, dg_ref, dx_ref):
        i = pl.program_id(0)

        @pl.when(i == 0)
        def _():
            loss_ref[...] = jnp.zeros_like(loss_ref)
            dg_ref[...] = jnp.zeros_like(dg_ref)

        @pl.when(i < n_lat)
        def _():
            tv = t_ref[...]

            def f(gain_, x):
                y = x * lax.rsqrt(jnp.mean(x * x, axis=-1, keepdims=True) + _EPS) * gain_
                return 0.5 * jnp.sum(jnp.mean(jnp.square(y - tv), axis=-1))

            val, (dg, dx) = jax.value_and_grad(f, argnums=(0, 1))(g_ref[...], x_ref[...])
            loss_ref[...] += val
            dg_ref[...] += dg
            dx_ref[...] = dx

        @pl.when(i >= n_lat)
        def _():
            dx_ref[...] = jnp.zeros_like(dx_ref)

    return pl.pallas_call(
        body, name=name, grid=(rows // _TM,),
        out_shape=[jax.ShapeDtypeStruct((1, 128), _F32), jax.ShapeDtypeStruct((1, d), _F32),
                   jax.ShapeDtypeStruct((rows, d), _F32)],
        in_specs=[pl.BlockSpec((_TM, d), lambda i: (i, 0)), pl.BlockSpec((1, d), lambda i: (0, 0)),
                  pl.BlockSpec((_TM, d), lambda i: (jnp.minimum(i, n_lat - 1), 0))],
        out_specs=[pl.BlockSpec((1, 128), lambda i: (0, 0)), pl.BlockSpec((1, d), lambda i: (0, 0)),
                   pl.BlockSpec((_TM, d), lambda i: (i, 0))],
        compiler_params=_params(("arbitrary",)),
    )(xs, gain, target)


def _adamw(name, w, g, m, v):
    rows, cols = w.shape
    tr = _divisor(rows, 512, 8)
    b1, b2 = _ADAM["b1"], _ADAM["b2"]
    c1 = 1.0 - b1 ** _ADAM["step"]
    c2 = 1.0 - b2 ** _ADAM["step"]

    def body(w_ref, g_ref, m_ref, v_ref, d_ref, nm_ref, nv_ref):
        gv = g_ref[...]
        nm = b1 * m_ref[...] + (1.0 - b1) * gv
        nv = b2 * v_ref[...] + (1.0 - b2) * jnp.square(gv)
        d_ref[...] = -_ADAM["lr"] * ((nm / c1) / (jnp.sqrt(nv / c2) + _ADAM["eps"]) + _ADAM["wd"] * w_ref[...])
        nm_ref[...] = nm
        nv_ref[...] = nv

    spec = pl.BlockSpec((tr, cols), lambda i: (i, 0))
    return pl.pallas_call(
        body, name=name, grid=(rows // tr,),
        out_shape=[jax.ShapeDtypeStruct((rows, cols), _F32)] * 3,
        in_specs=[spec] * 4, out_specs=[spec] * 3,
        compiler_params=_params(("arbitrary",)),
    )(w, g, m, v)


def _pack(arrays, cols=128):
    flat = jnp.concatenate([a.reshape(-1).astype(_F32) for a in arrays])
    pad = (-flat.shape[0]) % (64 * cols)
    return jnp.pad(flat, (0, pad)).reshape(-1, cols)


def _unpack(flat, shapes):
    out, off = [], 0
    for s in shapes:
        n = 1
        for d in s:
            n *= d
        out.append(flat[..., off:off + n].reshape(flat.shape[:-1] + tuple(s)))
        off += n
    return out


def _gather_channels(parts):
    moved = jnp.moveaxis(parts, 0, -2)
    return moved.reshape(moved.shape[:-2] + (moved.shape[-2] * moved.shape[-1],))


def kernel(x, c, ctx, c_ctx, w_mod, b_mod, norm_mix, norm_ffn, w_in_ab, conv_a, conv_b, conv_b_bias, ln_b_gain, ln_b_bias, w_out_ab, w_qkv, w_o, sinks, w_up, w_conv_ffn, w_down, final_norm, loss_target, m_c_ctx, m_w_mod, m_b_mod, m_norm_mix, m_norm_ffn, m_w_in_ab, m_conv_a, m_conv_b, m_conv_b_bias, m_ln_b_gain, m_ln_b_bias, m_w_out_ab, m_w_qkv, m_w_o, m_sinks, m_w_up, m_w_conv_ffn, m_w_down, m_final_norm, v_c_ctx, v_w_mod, v_b_mod, v_norm_mix, v_norm_ffn, v_w_in_ab, v_conv_a, v_conv_b, v_conv_b_bias, v_ln_b_gain, v_ln_b_bias, v_w_out_ab, v_w_qkv, v_w_o, v_sinks, v_w_up, v_w_conv_ffn, v_w_down, v_final_norm):
    args = dict(locals())
    weight_names = ["c_ctx", "w_mod", "b_mod", "norm_mix", "norm_ffn", "w_in_ab", "conv_a", "conv_b", "conv_b_bias",
                    "ln_b_gain", "ln_b_bias", "w_out_ab", "w_qkv", "w_o", "sinks", "w_up", "w_conv_ffn", "w_down",
                    "final_norm"]
    length, d = x.shape[1], x.shape[2]
    ctx_len = ctx.shape[1]
    assert ctx_len == _TM and length % _TM == 0 and x.shape[0] == 1
    n_lat = length // _TM
    depth = w_mod.shape[0]
    n_even, n_odd = w_in_ab.shape[0], w_qkv.shape[0]
    a_w = conv_a.shape[2] * _NDEV
    b_w = conv_b.shape[2] * _NDEV
    assert a_w == b_w
    q_w = w_o.shape[1] * _NDEV
    kv_w = (w_qkv.shape[2] * _NDEV - q_w) // 2
    d_ff = w_down.shape[1] * _NDEV
    dev = 4 * lax.axis_index("x") + 2 * lax.axis_index("y") + lax.axis_index("c")

    small_shapes = [c.shape[1:], conv_a.shape, conv_b.shape, w_conv_ffn.shape]
    g0 = _all_gather_small("gather_small_params", _pack([c, conv_a, conv_b, w_conv_ffn]))
    c_parts, ca_parts, cb_parts, cf_parts = _unpack(g0.reshape(_NDEV, -1), small_shapes)
    conv_a_full = _gather_channels(ca_parts)
    conv_b_full = _gather_channels(cb_parts)
    conv_f_full = _gather_channels(cf_parts)

    cond = jnp.concatenate([c_parts, c_ctx[None], jnp.zeros((16 - _NDEV - 1, d), _F32)], axis=0)
    mod_cols = w_mod.shape[2]
    m_shard = _mod_forward("mod_forward", cond, w_mod)
    m_all = _all_gather_small("gather_mod", m_shard.reshape(depth * 16, mod_cols))
    m_all = jnp.moveaxis(m_all.reshape(_NDEV, depth, 16, mod_cols), 0, 2).reshape(depth, 16, _NDEV * mod_cols)
    m_all = m_all + b_mod[:, None, :]
    m_lat = lax.dynamic_index_in_dim(m_all, dev, axis=1, keepdims=False)
    m_ctx = m_all[:, _NDEV]

    def mod_vec(l, j):
        return jnp.stack([m_lat[l, j * d:(j + 1) * d], m_ctx[l, j * d:(j + 1) * d]])[:, None, :]

    def layer_mats(l):
        if l % 2 == 0:
            first = [("in", l // 2, w_in_ab[l // 2].T), ("out", l // 2, w_out_ab[l // 2])]
        else:
            first = [("qkv", l // 2, w_qkv[l // 2].T), ("o", l // 2, w_o[l // 2])]
        return first + [("up", l, w_up[l].T), ("down", l, w_down[l])]

    late = 0.0 * m_all[0, 0, 0]
    piece_mats = [layer_mats(0)[:2], layer_mats(0)[2:]] + [layer_mats(l) for l in range(1, depth)]
    n_pieces = len(piece_mats)
    slab_off, slab_r, piece_of, piece_keys, piece_rows, slabs = {}, {}, {}, [], [], []
    for p, mats in enumerate(piece_mats):
        off, keys = 0, []
        for fam, idx, mat in mats:
            slab_off[fam, idx], slab_r[fam], piece_of[fam, idx] = off, mat.shape[0], p
            off += mat.shape[0]
            keys.append((fam, idx))
        piece_keys.append(keys)
        piece_rows.append(off)
        slabs.append(jnp.concatenate([(mat + late).astype(_MM) for _, _, mat in mats], axis=0))
    wgs, gathers, start_token = [None] * n_pieces, [None] * n_pieces, jnp.zeros((), _F32)
    for p in range(n_pieces):
        land = _fill_own_slot(f"gather_fill_{p}", [slabs[p]], [0], piece_rows[p])
        gathers[p] = _exchange_start(f"gather_start_{p}", [slabs[p]], land, [0])
        start_token = start_token + gathers[p][-1][0, 0]

    def wref(fam, idx):
        return wgs[piece_of[fam, idx]], slab_off[fam, idx], slab_r[fam]

    cos, sin = _rope_tables(length, ctx_len)
    xs = jnp.concatenate([x[0], ctx[0]], axis=0)

    def full(a):
        return (a.reshape(1, -1), "full")

    wgs[0] = _exchange_wait("gather_wait_0", gathers[0], [0], m_all)
    saved = []
    for l in range(depth):
        sv = {"x_in": xs}
        if l > 0:
            wgs[1 + l] = _exchange_wait(f"gather_wait_{1 + l}", gathers[1 + l], [0], xs)
        gain1 = full(norm_mix[l] + start_token) if l == 0 else full(norm_mix[l])
        norm1 = [gain1, (mod_vec(l, 0), "stream"), (mod_vec(l, 1), "stream")]
        if l % 2 == 0:
            e = l // 2
            h1, p = _norm_proj(f"proj_in_{l}", xs, norm1, *wref("in", e), n_lat)
            (qm,) = _rowfn(f"premix_{l}", _f_premix, [], [(p, a_w, j) for j in range(5)],
                           [(None, [a_w]), (_ACT, [a_w, b_w])], n_lat)
            cv_a = _dwconv(f"conv_a_{l}", qm, 0, a_w, conv_a_full[e], _ACT, n_lat)
            cv_b = _dwconv(f"conv_b_{l}", qm, 1, b_w, conv_b_full[e], _ACT, n_lat)
            post_params = [full(conv_b_bias[e]), full(ln_b_gain[e]), full(ln_b_bias[e])]
            (z,) = _rowfn(f"postmix_{l}", _f_postmix, post_params, [(p, a_w, 0), (cv_a, a_w, 0), (cv_b, b_w, 0)],
                          [(_ACT, [a_w, b_w])], n_lat)
            y1, xs = _proj_residual(f"proj_out_{l}", z, *wref("out", e), xs, mod_vec(l, 2), n_lat)
            sv.update(p=p, qm=qm, cv_a=cv_a, cv_b=cv_b, z=z)
        else:
            o = l // 2
            h1, p = _norm_proj(f"proj_qkv_{l}", xs, norm1, *wref("qkv", o), n_lat)
            qr, kr = _rope_fwd(f"rope_{l}", p, cos, sin, q_w, kv_w)
            sk = sinks[o].reshape(1, -1)
            z = _attn_fwd(f"attn_{l}", qr, kr, p, sk, n_lat, length, kv_w)
            y1, xs = _proj_residual(f"proj_o_{l}", z, *wref("o", o), xs, mod_vec(l, 2), n_lat)
            sv.update(p=p, qr=qr, kr=kr, z=z)
        sv.update(h1=h1, y1=y1, x_mid=xs)
        if l == 0:
            wgs[1] = _exchange_wait("gather_wait_1", gathers[1], [0], xs)
        norm2 = [full(norm_ffn[l]), (mod_vec(l, 3), "stream"), (mod_vec(l, 4), "stream")]
        h2, pu = _norm_proj(f"proj_up_{l}", xs, norm2, *wref("up", l), n_lat)
        f = _ffn_act(f"ffn_act_{l}", pu, conv_f_full[l], n_lat)
        y2, xs = _proj_residual(f"proj_down_{l}", f, *wref("down", l), xs, mod_vec(l, 5), n_lat)
        sv.update(h2=h2, pu=pu, f=f, y2=y2)
        saved.append(sv)

    loss_part, d_final_norm, dxs = _loss_head("loss_head", xs, final_norm.reshape(1, -1), loss_target[0], n_lat)
    loss = lax.psum(loss_part[0, 0], ("x", "y", "c"))

    wgrads = {}
    d_mod = [[None] * 6 for _ in range(depth)]
    d_norm_mix, d_norm_ffn = [None] * depth, [None] * depth
    d_conv_a, d_conv_b = [None] * n_even, [None] * n_even
    d_bias, d_ln_g, d_ln_b = [None] * n_even, [None] * n_even, [None] * n_even
    d_sinks = [None] * n_odd
    d_conv_f = [None] * depth
    exchanges, recvs, exchange_token = [None] * n_pieces, [None] * n_pieces, jnp.zeros((), _F32)

    def piece_parts(p):
        return ([wgrads[key].reshape(_NDEV, slab_r[key[0]], d) for key in piece_keys[p]],
                [slab_off[key] for key in piece_keys[p]])

    def start_exchange(p):
        parts, offsets = piece_parts(p)
        land = _fill_own_slot(f"exchange_fill_{p}", parts, offsets, piece_rows[p])
        exchanges[p] = _exchange_start(f"exchange_start_{p}", parts, land, offsets)
        return exchanges[p][-1][0, 0]

    for l in reversed(range(depth)):
        sv = saved[l]
        d_mod[l][5], dy2, df = _gate_proj_bwd(f"bwd_down_{l}", dxs, sv["y2"], mod_vec(l, 5) + exchange_token,
                                              *wref("down", l), n_lat)
        wgrads["down", l] = _mm_tn(f"wgrad_down_{l}", sv["f"], dy2, _ACT)
        dpu, d_conv_f[l] = _ffn_act_bwd(f"ffn_act_bwd_{l}", sv["pu"], df, conv_f_full[l], n_lat)
        wgrads["up", l] = _mm_tn(f"wgrad_up_{l}", dpu, sv["h2"], _ACT)
        norm2 = [full(norm_ffn[l]), (mod_vec(l, 3), "stream"), (mod_vec(l, 4), "stream")]
        (dgain, dsh, dsc), dxs = _proj_norm_bwd(f"bwd_up_{l}", dpu, *wref("up", l), sv["x_mid"], norm2, dxs, n_lat)
        d_norm_ffn[l], d_mod[l][3], d_mod[l][4] = dgain, dsh, dsc
        mix_token = start_exchange(1) if l == 0 else jnp.zeros((), _F32)
        gate1 = mod_vec(l, 2) + mix_token
        norm1 = [full(norm_mix[l]), (mod_vec(l, 0), "stream"), (mod_vec(l, 1), "stream")]
        if l % 2 == 0:
            e = l // 2
            d_mod[l][2], dy1, dz = _gate_proj_bwd(f"bwd_out_{l}", dxs, sv["y1"], gate1, *wref("out", e), n_lat)
            wgrads["out", e] = _mm_tn(f"wgrad_out_{l}", sv["z"], dy1, _ACT)
            post_params = [full(conv_b_bias[e]), full(ln_b_gain[e]), full(ln_b_bias[e])]
            (dbias, dlg, dlb), (dgb, dcv_a, dcv_b) = _rowfn_bwd(
                f"postmix_bwd_{l}", _f_postmix, post_params,
                [(sv["p"], a_w, 0), (sv["cv_a"], a_w, 0), (sv["cv_b"], b_w, 0)], [0, 1, 2], [(dz, [a_w, b_w])],
                [(_ACT, [0]), (_ACT, [1]), (_ACT, [2])], n_lat)
            d_bias[e], d_ln_g[e], d_ln_b[e] = dbias, dlg, dlb
            d_conv_a[e] = _dwconv_wgrad(f"conv_a_wgrad_{l}", dcv_a, sv["qm"], 0, a_w, conv_a_full.shape[1], n_lat)
            d_conv_b[e] = _dwconv_wgrad(f"conv_b_wgrad_{l}", dcv_b, sv["qm"], 1, b_w, conv_b_full.shape[1], n_lat)
            dq_a = _dwconv(f"conv_a_bwd_{l}", dcv_a, 0, a_w, conv_a_full[e][::-1], _ACT, n_lat)
            dq_b = _dwconv(f"conv_b_bwd_{l}", dcv_b, 0, b_w, conv_b_full[e][::-1], _ACT, n_lat)
            _, (dp,) = _rowfn_bwd(f"premix_bwd_{l}", _f_premix, [], [(sv["p"], a_w, j) for j in range(5)],
                                  [0, 1, 2, 3, 4], [(dgb, [a_w]), (dq_a, [a_w]), (dq_b, [b_w])],
                                  [(_ACT, [0, 1, 2, 3, 4])], n_lat)
            wgrads["in", e] = _mm_tn(f"wgrad_in_{l}", dp, sv["h1"], _ACT)
            if l == 0:
                norm1 = [full(norm_mix[l] + start_exchange(0))] + norm1[1:]
            (dgain, dsh, dsc), dxs = _proj_norm_bwd(f"bwd_in_{l}", dp, *wref("in", e), sv["x_in"], norm1, dxs, n_lat)
        else:
            o = l // 2
            d_mod[l][2], dy1, dz = _gate_proj_bwd(f"bwd_o_{l}", dxs, sv["y1"], gate1, *wref("o", o), n_lat)
            wgrads["o", o] = _mm_tn(f"wgrad_o_{l}", sv["z"], dy1, _ACT)
            sk = sinks[o].reshape(1, -1)
            dqr, dkr, dv, dsk = _attn_bwd(f"attn_bwd_{l}", sv["qr"], sv["kr"], sv["p"], sk, dz, n_lat, length, kv_w)
            d_sinks[o] = dsk
            dp = _rope_bwd(f"rope_bwd_{l}", dqr, dkr, dv, cos, sin)
            wgrads["qkv", o] = _mm_tn(f"wgrad_qkv_{l}", dp, sv["h1"], _ACT)
            (dgain, dsh, dsc), dxs = _proj_norm_bwd(f"bwd_qkv_{l}", dp, *wref("qkv", o), sv["x_in"], norm1, dxs, n_lat)
        d_norm_mix[l], d_mod[l][0], d_mod[l][1] = dgain, dsh, dsc
        if l > 0:
            exchange_token = start_exchange(1 + l)
    grad_x = dxs[:length][None]

    gsums = []
    for p in range(n_pieces):
        recvs[p] = _exchange_wait(f"exchange_wait_{p}", exchanges[p], piece_parts(p)[1], dxs)
        gsums.append(_sum_slots(f"sum_weight_grads_{p}", recvs[p]))

    def slab_grad(fam, count, transposed):
        mats = [gsums[piece_of[fam, i]][slab_off[fam, i]:slab_off[fam, i] + slab_r[fam]] for i in range(count)]
        return jnp.stack([m_.T if transposed else m_ for m_ in mats])

    grads = {
        "w_in_ab": slab_grad("in", n_even, True), "w_qkv": slab_grad("qkv", n_odd, True),
        "w_up": slab_grad("up", depth, True), "w_out_ab": slab_grad("out", n_even, False),
        "w_o": slab_grad("o", n_odd, False), "w_down": slab_grad("down", depth, False),
    }

    weight_names_big = ["w_in_ab", "w_out_ab", "w_qkv", "w_o", "w_up", "w_down"]
    delta, new_m, new_v = {}, {}, {}

    def adam_big(n):
        w = args[n]
        flat = [a.reshape(-1, w.shape[-1]) for a in (w, grads[n], args["m_" + n], args["v_" + n])]
        dl, nm, nv = _adamw(f"adamw_{n}", *flat)
        delta[n], new_m[n], new_v[n] = dl.reshape(w.shape), nm.reshape(w.shape), nv.reshape(w.shape)

    for n in weight_names_big:
        adam_big(n)

    dm_dev = jnp.stack([jnp.concatenate([d_mod[l][j][:, 0, :] for j in range(6)], axis=-1)
                        for l in range(depth)])
    small_grads = [dm_dev, jnp.stack(d_norm_mix), jnp.stack(d_norm_ffn), jnp.stack(d_conv_a), jnp.stack(d_conv_b),
                   jnp.stack(d_bias), jnp.stack(d_ln_g), jnp.stack(d_ln_b), jnp.stack(d_sinks), jnp.stack(d_conv_f),
                   d_final_norm]
    sg_shapes = [a.shape for a in small_grads]
    sg_all = _all_gather_small("gather_small_grads", _pack(small_grads))
    sg_sum = _sum_slots("sum_small_grads", sg_all)
    (dm_sum, g_norm_mix, g_norm_ffn, g_conv_a, g_conv_b, g_bias, g_ln_g, g_ln_b, g_sinks, g_conv_f,
     g_final_norm) = _unpack(sg_sum.reshape(-1), sg_shapes)
    dm_each = _unpack(sg_all.reshape(_NDEV, -1), sg_shapes[:1])[0]

    def my_channels(a):
        width = a.shape[-1] // _NDEV
        return lax.dynamic_slice_in_dim(a, dev * width, width, axis=a.ndim - 1)

    grads["b_mod"] = dm_sum[:, 0] + dm_sum[:, 1]
    grads["norm_mix"] = g_norm_mix.reshape(depth, d)
    grads["norm_ffn"] = g_norm_ffn.reshape(depth, d)
    grads["conv_a"] = my_channels(g_conv_a)
    grads["conv_b"] = my_channels(g_conv_b)
    grads["conv_b_bias"] = g_bias.reshape(n_even, b_w)
    grads["ln_b_gain"] = g_ln_g.reshape(n_even, b_w)
    grads["ln_b_bias"] = g_ln_b.reshape(n_even, b_w)
    grads["sinks"] = g_sinks.reshape(n_odd, -1)
    grads["w_conv_ffn"] = my_channels(g_conv_f)
    grads["final_norm"] = g_final_norm.reshape(d)

    dm_rows = jnp.concatenate([jnp.moveaxis(dm_each[:, :, 0], 0, 1), dm_sum[:, 1:2],
                               jnp.zeros((depth, 16 - _NDEV - 1, 6 * d), _F32)], axis=1)
    dm_mine = my_channels(dm_rows)
    grads["w_mod"] = jnp.stack([_mm_tn(f"wgrad_mod_{l}", cond, dm_mine[l], _F32, silu_a=True) for l in range(depth)])
    dcond = _mod_backward_cond("mod_backward_cond", dm_mine, w_mod)
    dcond_all = _all_gather_small("gather_dcond", dcond)
    dcond_sum = _sum_slots("sum_dcond", dcond_all)[_NDEV]
    sg = jax.nn.sigmoid(c_ctx)
    grads["c_ctx"] = dcond_sum * (sg * (1.0 + c_ctx * (1.0 - sg)))

    adam_big("w_mod")
    small = [n for n in weight_names if n != "w_mod" and n not in weight_names_big]
    shapes = [args[n].shape for n in small]
    grads = {n: grads[n].reshape(args[n].shape) for n in grads}
    dl, nm, nv = _adamw("adamw_small", _pack([args[n] for n in small]), _pack([grads[n] for n in small]),
                        _pack([args["m_" + n] for n in small]), _pack([args["v_" + n] for n in small]))
    for res, packed in ((delta, dl), (new_m, nm), (new_v, nv)):
        for n, a in zip(small, _unpack(packed.reshape(-1), shapes)):
            res[n] = a

    return (loss, grad_x, *[grads[n] for n in weight_names], *[delta[n] for n in weight_names],
            *[new_m[n] for n in weight_names], *[new_v[n] for n in weight_names])
```
